```python
import jax, jax.numpy as jnp
from jax import lax
import numpy as np

D_MODEL = 1024
BATCH = 8
SEQ = 2048
DEPTH = 1
DEC_BATCH = 128
DEC_SEQ = 4
PAST_LEN = 16384
PAGE_SIZE = 128

SSD_EXPAND = 2
D_SSD = SSD_EXPAND * D_MODEL
SSD_HEADDIM = 64
SSD_HEADS = D_SSD // SSD_HEADDIM
SSD_GROUPS = 4
SSD_HPG = SSD_HEADS // SSD_GROUPS
D_STATE = 128
SSD_CONV = 4
SSD_CHUNK = 128
D_XBC = D_SSD + 2 * SSD_GROUPS * D_STATE
D_LRU = D_MODEL
LRU_BLOCKS = 8
LRU_BW = D_LRU // LRU_BLOCKS
LRU_CONV = 4
LRU_C = 8.0
D_FF = 3 * D_MODEL
FFN_CONV = 3
ALPHA = (2.0 * DEPTH) ** 0.25
BETA = (8.0 * DEPTH) ** -0.25
LN_EPS = 1e-5
RMS_EPS = 1e-6
D_IN = D_SSD + D_XBC + SSD_HEADS + 2 * D_LRU + 2 * D_MODEL

kernel_name = "hybrid_ssd_rglru_convffn_step"


def layer_norm(x, g, b):
    xf = x.astype(jnp.float32)
    mu = jnp.mean(xf, -1, keepdims=True)
    var = jnp.mean(jnp.square(xf - mu), -1, keepdims=True)
    return ((xf - mu) * lax.rsqrt(var + LN_EPS) * g + b).astype(x.dtype)


def rms_norm(x, g, dtype):
    xf = x.astype(jnp.float32)
    return (xf * lax.rsqrt(jnp.mean(jnp.square(xf), -1, keepdims=True) + RMS_EPS) * g).astype(dtype)


def causal_dwconv(u, buf, w, b):
    K = w.shape[0]
    L = u.shape[1]
    xp = jnp.concatenate([buf.astype(u.dtype), u], axis=1)
    y = b + sum(w[k] * xp[:, k:k + L] for k in range(K))
    return y, xp[:, L:]


def ssd_scan(x, dt, A, Bm, Cm, h0):
    f32 = jnp.float32
    b, L = x.shape[:2]
    Q = SSD_CHUNK if L % SSD_CHUNK == 0 else L
    nc = L // Q
    G, Hg, P, N = SSD_GROUPS, SSD_HPG, SSD_HEADDIM, D_STATE
    x = x.astype(f32).reshape(b, nc, Q, G, Hg, P)
    dt = dt.astype(f32).reshape(b, nc, Q, G, Hg)
    Bm = Bm.astype(f32).reshape(b, nc, Q, G, N)
    Cm = Cm.astype(f32).reshape(b, nc, Q, G, N)
    dA = dt * A.astype(f32).reshape(G, Hg)
    a_cs = jnp.cumsum(dA, axis=2)
    xdt = x * dt[..., None]
    seg = a_cs[:, :, :, None] - a_cs[:, :, None, :]
    causal = jnp.tril(jnp.ones((Q, Q), bool))[:, :, None, None]
    decay = jnp.exp(jnp.where(causal, seg, -jnp.inf))
    cb = jnp.einsum('bclgn,bcsgn->bclsg', Cm, Bm)
    y_diag = jnp.einsum('bclsg,bclsgh,bcsghp->bclghp', cb, decay, xdt)
    decay_to_end = jnp.exp(a_cs[:, :, -1:] - a_cs)
    states = jnp.einsum('bclgn,bclgh,bclghp->bcghpn', Bm, decay_to_end, xdt)
    chunk_decay = jnp.exp(a_cs[:, :, -1])

    def step(h, inp):
        s, d = inp
        return d[..., None, None] * h + s, h

    h_init = h0.astype(f32).reshape(b, G, Hg, P, N)
    h_last, h_prev = lax.scan(step, h_init,
                              (jnp.moveaxis(states, 1, 0), jnp.moveaxis(chunk_decay, 1, 0)))
    h_prev = jnp.moveaxis(h_prev, 0, 1)
    y_off = jnp.einsum('bclgn,bcghpn,bclgh->bclghp', Cm, h_prev, jnp.exp(a_cs))
    y = (y_diag + y_off).reshape(b, L, SSD_HEADS, P)
    return y, h_last.reshape(b, SSD_HEADS, P, N)


def ssd_branch(z, xbc, dt_raw, conv_buf, h0, conv_w, conv_b, dt_bias, a_log, d_skip, norm_g):
    xbc, new_buf = causal_dwconv(xbc, conv_buf, conv_w, conv_b)
    xbc = jax.nn.silu(xbc)
    b, L = xbc.shape[:2]
    GN = SSD_GROUPS * D_STATE
    xs = xbc[..., :D_SSD].reshape(b, L, SSD_HEADS, SSD_HEADDIM)
    Bm = xbc[..., D_SSD:D_SSD + GN].reshape(b, L, SSD_GROUPS, D_STATE)
    Cm = xbc[..., D_SSD + GN:].reshape(b, L, SSD_GROUPS, D_STATE)
    dt = jax.nn.softplus(dt_raw.astype(jnp.float32) + dt_bias)
    A = -jnp.exp(a_log.astype(jnp.float32))
    y, h_last = ssd_scan(xs, dt, A, Bm, Cm, h0)
    y = y + d_skip[:, None] * xs
    y = y.reshape(b, L, D_SSD) * jax.nn.silu(z)
    return rms_norm(y, norm_g, z.dtype), h_last.astype(h0.dtype), new_buf


def rglru_branch(xr, gy, conv_buf, h0, start_pos, conv_w, conv_b, wa, ba, wx, bx, lam):
    f32 = jnp.float32
    xr, new_buf = causal_dwconv(xr, conv_buf, conv_w, conv_b)
    b, L = xr.shape[:2]
    xb = xr.reshape(b, L, LRU_BLOCKS, LRU_BW)
    r = jax.nn.sigmoid(jnp.einsum('blnc,ncd->blnd', xb, wa) + ba).reshape(b, L, D_LRU).astype(f32)
    i = jax.nn.sigmoid(jnp.einsum('blnc,ncd->blnd', xb, wx) + bx).reshape(b, L, D_LRU).astype(f32)
    log_a = -LRU_C * r * jax.nn.softplus(-lam.astype(f32))
    a = jnp.exp(log_a)
    mult = jnp.sqrt(1.0 - jnp.exp(2.0 * log_a))
    first = (start_pos + jnp.arange(L)) == 0
    mult = jnp.where(first[None, :, None], 1.0, mult)
    u = mult * i * xr.astype(f32)
    u = u.at[:, 0].add(a[:, 0] * h0.astype(f32))

    def combine(c1, c2):
        a1, b1 = c1
        a2, b2 = c2
        return a1 * a2, a2 * b1 + b2

    _, h = lax.associative_scan(combine, (a, u), axis=1)
    y = h * jax.nn.gelu(gy.astype(f32))
    return y.astype(xr.dtype), h[:, -1].astype(h0.dtype), new_buf


def conv_ffn(x, buf, w_gate, w_up, conv_w, conv_b, w_down):
    g, new_buf = causal_dwconv(x @ w_gate, buf, conv_w, conv_b)
    h = jax.nn.gelu(g) * (x @ w_up)
    return h @ w_down, new_buf


def block(x, start_pos, ssd_h, ssd_buf, lru_h, lru_buf, ffn_buf, p):
    proj = x @ p['w_in']
    sizes = (D_SSD, D_XBC, SSD_HEADS, D_LRU, D_LRU, D_MODEL, D_MODEL)
    cuts = tuple(int(c) for c in np.cumsum(sizes)[:-1])
    z, xbc, dt_raw, lru_x, lru_y, g_ssd, g_lru = jnp.split(proj, cuts, axis=-1)
    y_ssd, ssd_h_new, ssd_buf_new = ssd_branch(
        z, xbc, dt_raw, ssd_buf, ssd_h, p['ssd_conv_w'], p['ssd_conv_b'], p['ssd_dt_bias'],
        p['ssd_a_log'], p['ssd_d'], p['ssd_norm_g'])
    y_lru, lru_h_new, lru_buf_new = rglru_branch(
        lru_x, lru_y, lru_buf, lru_h, start_pos, p['lru_conv_w'], p['lru_conv_b'],
        p['lru_wa'], p['lru_ba'], p['lru_wx'], p['lru_bx'], p['lru_lambda'])
    gate_ssd = jax.nn.sigmoid(g_ssd + p['b_gate'][:D_MODEL])
    gate_lru = jax.nn.sigmoid(g_lru + p['b_gate'][D_MODEL:])
    merged = gate_ssd * (y_ssd @ p['w_ssd_out']) + gate_lru * (y_lru @ p['w_lru_out'])
    x1 = layer_norm(ALPHA * x + merged @ p['w_o'], p['ln1_g'], p['ln1_b'])
    f, ffn_buf_new = conv_ffn(x1, ffn_buf, p['ffn_w_gate'], p['ffn_w_up'],
                              p['ffn_conv_w'], p['ffn_conv_b'], p['ffn_w_down'])
    y = layer_norm(ALPHA * x1 + f, p['ln2_g'], p['ln2_b'])
    return y, (ssd_h_new, ssd_buf_new, lru_h_new, lru_buf_new, ffn_buf_new)


def trunk(x, start_pos, ssd_h, ssd_buf, lru_h, lru_buf, ffn_buf, w):
    outs = [[] for _ in range(5)]
    for l in range(DEPTH):
        p = {k: v[l] for k, v in w.items()}
        x, new = block(x, start_pos, ssd_h[l], ssd_buf[l], lru_h[l], lru_buf[l], ffn_buf[l], p)
        for o, s in zip(outs, new):
            o.append(s)
    return x, tuple(jnp.stack(o) for o in outs)


def setup_inputs(seed: int = 0) -> dict:
    key = jax.random.key(seed)
    ks = iter(jax.random.split(key, 40))
    nrm = lambda shape, s: jax.random.normal(next(ks), shape, jnp.float32) * s
    unif = lambda shape, lo, hi: jax.random.uniform(next(ks), shape, jnp.float32, lo, hi)
    Dp = DEPTH
    dt0 = jnp.exp(unif((Dp, SSD_HEADS), np.log(1e-3), np.log(1e-1)))
    a0 = unif((Dp, D_LRU), 0.9, 0.999)
    return {
        'x_prompt': nrm((BATCH, SEQ, D_MODEL), 1.0),
        'x_sample': nrm((DEC_BATCH, DEC_SEQ, D_MODEL), 1.0),
        'state_ssd': nrm((Dp, DEC_BATCH, SSD_HEADS, SSD_HEADDIM, D_STATE), 0.1),
        'cache_ssd_conv': nrm((Dp, DEC_BATCH, SSD_CONV - 1, D_XBC), 1.0),
        'state_lru': nrm((Dp, DEC_BATCH, D_LRU), 0.5),
        'cache_lru_conv': nrm((Dp, DEC_BATCH, LRU_CONV - 1, D_LRU), 1.0),
        'cache_ffn_conv': nrm((Dp, DEC_BATCH, FFN_CONV - 1, D_FF), 1.0),
        'w_in': nrm((Dp, D_MODEL, D_IN), D_MODEL ** -0.5),
        'b_gate': nrm((Dp, 2 * D_MODEL), 0.01),
        'ssd_conv_w': nrm((Dp, SSD_CONV, D_XBC), SSD_CONV ** -0.5),
        'ssd_conv_b': nrm((Dp, D_XBC), 0.01),
        'ssd_dt_bias': dt0 + jnp.log(-jnp.expm1(-dt0)),
        'ssd_a_log': jnp.log(unif((Dp, SSD_HEADS), 1.0, 16.0)),
        'ssd_d': 1.0 + nrm((Dp, SSD_HEADS), 0.01),
        'ssd_norm_g': 1.0 + nrm((Dp, D_SSD), 0.01),
        'w_ssd_out': nrm((Dp, D_SSD, D_MODEL), BETA * D_SSD ** -0.5),
        'lru_conv_w': nrm((Dp, LRU_CONV, D_LRU), LRU_CONV ** -0.5),
        'lru_conv_b': nrm((Dp, D_LRU), 0.01),
        'lru_wa': nrm((Dp, LRU_BLOCKS, LRU_BW, LRU_BW), LRU_BW ** -0.5),
        'lru_ba': nrm((Dp, LRU_BLOCKS, LRU_BW), 0.01),
        'lru_wx': nrm((Dp, LRU_BLOCKS, LRU_BW, LRU_BW), LRU_BW ** -0.5),
        'lru_bx': nrm((Dp, LRU_BLOCKS, LRU_BW), 0.01),
        'lru_lambda': jnp.log(a0) - jnp.log1p(-a0),
        'w_lru_out': nrm((Dp, D_LRU, D_MODEL), BETA * D_LRU ** -0.5),
        'w_o': nrm((Dp, D_MODEL, D_MODEL), BETA * D_MODEL ** -0.5),
        'ln1_g': 1.0 + nrm((Dp, D_MODEL), 0.01),
        'ln1_b': nrm((Dp, D_MODEL), 0.01),
        'ffn_w_gate': nrm((Dp, D_MODEL, D_FF), D_MODEL ** -0.5),
        'ffn_w_up': nrm((Dp, D_MODEL, D_FF), BETA * D_MODEL ** -0.5),
        'ffn_conv_w': nrm((Dp, FFN_CONV, D_FF), FFN_CONV ** -0.5),
        'ffn_conv_b': nrm((Dp, D_FF), 0.01),
        'ffn_w_down': nrm((Dp, D_FF, D_MODEL), BETA * D_FF ** -0.5),
        'ln2_g': 1.0 + nrm((Dp, D_MODEL), 0.01),
        'ln2_b': nrm((Dp, D_MODEL), 0.01),
    }


def reference(x_prompt, x_sample, state_ssd, cache_ssd_conv, state_lru, cache_lru_conv,
              cache_ffn_conv, w_in, b_gate, ssd_conv_w, ssd_conv_b, ssd_dt_bias, ssd_a_log,
              ssd_d, ssd_norm_g, w_ssd_out, lru_conv_w, lru_conv_b, lru_wa, lru_ba, lru_wx,
              lru_bx, lru_lambda, w_lru_out, w_o, ln1_g, ln1_b, ffn_w_gate, ffn_w_up,
              ffn_conv_w, ffn_conv_b, ffn_w_down, ln2_g, ln2_b):
    w = dict(w_in=w_in, b_gate=b_gate, ssd_conv_w=ssd_conv_w, ssd_conv_b=ssd_conv_b,
             ssd_dt_bias=ssd_dt_bias, ssd_a_log=ssd_a_log, ssd_d=ssd_d, ssd_norm_g=ssd_norm_g,
             w_ssd_out=w_ssd_out, lru_conv_w=lru_conv_w, lru_conv_b=lru_conv_b, lru_wa=lru_wa,
             lru_ba=lru_ba, lru_wx=lru_wx, lru_bx=lru_bx, lru_lambda=lru_lambda,
             w_lru_out=w_lru_out, w_o=w_o, ln1_g=ln1_g, ln1_b=ln1_b, ffn_w_gate=ffn_w_gate,
             ffn_w_up=ffn_w_up, ffn_conv_w=ffn_conv_w, ffn_conv_b=ffn_conv_b,
             ffn_w_down=ffn_w_down, ln2_g=ln2_g, ln2_b=ln2_b)
    bp = x_prompt.shape[0]
    z_ssd = jnp.zeros((DEPTH, bp) + state_ssd.shape[2:], state_ssd.dtype)
    z_ssd_buf = jnp.zeros((DEPTH, bp) + cache_ssd_conv.shape[2:], cache_ssd_conv.dtype)
    z_lru = jnp.zeros((DEPTH, bp) + state_lru.shape[2:], state_lru.dtype)
    z_lru_buf = jnp.zeros((DEPTH, bp) + cache_lru_conv.shape[2:], cache_lru_conv.dtype)
    z_ffn_buf = jnp.zeros((DEPTH, bp) + cache_ffn_conv.shape[2:], cache_ffn_conv.dtype)
    y_prompt, (p_ssd, p_ssd_buf, p_lru, p_lru_buf, p_ffn_buf) = trunk(
        x_prompt, 0, z_ssd, z_ssd_buf, z_lru, z_lru_buf, z_ffn_buf, w)
    y_sample, (s_ssd, s_ssd_buf, s_lru, s_lru_buf, s_ffn_buf) = trunk(
        x_sample, PAST_LEN, state_ssd, cache_ssd_conv, state_lru, cache_lru_conv,
        cache_ffn_conv, w)
    return (y_prompt, y_sample, p_ssd, p_ssd_buf, p_lru, p_lru_buf, p_ffn_buf,
            s_ssd, s_ssd_buf, s_lru, s_lru_buf, s_ffn_buf)
```

```python
import functools

import numpy as np
import jax
import jax.numpy as jnp
from jax import lax
from jax.experimental import pallas as pl
from jax.experimental.pallas import tpu as pltpu

F32 = jnp.float32
BF16 = jnp.bfloat16

SSD_GROUPS = 4
SSD_CHUNK = 128
LRU_BLOCKS = 8
LRU_C = 8.0
LN_EPS = 1e-5
RMS_EPS = 1e-6
PAST_LEN = 16384

LANES = 128
SUBLANES = 8
VMEM_LIMIT_BYTES = 60 * 1024 * 1024

HIST = 8
ROW_BLK = 32


def _dot(a, b):
    return jnp.dot(a, b, preferred_element_type=F32)


def _dot_nt(a, b):
    return lax.dot_general(a, b, (((1,), (1,)), ((), ())), preferred_element_type=F32)


def _dot_tn(a, b):
    return lax.dot_general(a, b, (((0,), (0,)), ((), ())), preferred_element_type=F32)


def _split3(v):
    hi = v.astype(BF16)
    r1 = v - hi.astype(F32)
    mid = r1.astype(BF16)
    lo = (r1 - mid.astype(F32)).astype(BF16)
    return hi, mid, lo


def _dot_exact_rhs(v, m):
    hi, mid, lo = _split3(v)
    return _dot(hi, m) + _dot(mid, m) + _dot(lo, m)


def _dot_exact_lhs(m, v):
    hi, mid, lo = _split3(v)
    return _dot(m, hi) + _dot(m, mid) + _dot(m, lo)


def _softplus(x):
    return jnp.maximum(x, 0.0) + jnp.log1p(jnp.exp(-jnp.abs(x)))


def _sigmoid(x):
    return 1.0 / (1.0 + jnp.exp(-x))


def _silu(x):
    return x * _sigmoid(x)


def _gelu(x):
    c = np.sqrt(2.0 / np.pi).astype(np.float32)
    return 0.5 * x * (1.0 + jnp.tanh(c * (x + 0.044715 * (x * x * x))))


def _layer_norm(v, g, b):
    mu = jnp.mean(v, axis=-1, keepdims=True)
    d = v - mu
    var = jnp.mean(d * d, axis=-1, keepdims=True)
    return d * lax.rsqrt(var + LN_EPS) * g + b


def _conv_block(buf_ref, w_ref, b_ref, hist, stride, r0, rows, c0, cw):
    taps = w_ref.shape[0]
    acc = b_ref[:, c0:c0 + cw]
    for k in range(taps):
        off = hist + r0 - (taps - 1 - k) * stride
        if not isinstance(off, int):
            off = pl.multiple_of(off, SUBLANES)
        acc = acc + w_ref[k:k + 1, c0:c0 + cw] * buf_ref[pl.ds(off, rows), c0:c0 + cw]
    return acc


def _prompt_mixer_kernel(
        x_ref, wz_ref, wxbc_ref, wdt_ref, wlx_ref, wly_ref, wgs_ref, wgl_ref,
        scw_ref, scb_ref, dtb_ref, aneg_ref, dexp_ref, ng_ref, wso_ref,
        lcw_ref, lcb_ref, wa_ref, ba_ref, wx_ref, bx_ref, lam_ref, wlo_ref,
        bg_ref, wo_ref, l1g_ref, l1b_ref, e_ref,
        x1_ref, st_ref, sconv_ref, lst_ref, lconv_ref,
        xb_s, xbc_s, xc_s, lx_s, xr_s, ly_s, y_s, z_s, ysb_s, ylb_s, ht_s, hl_s,
        *, tl, alpha, n_heads, headdim, d_state):
    t = pl.program_id(1)
    nt = pl.num_programs(1)
    d_ssd = n_heads * headdim
    gn = SSD_GROUPS * d_state
    hpg = n_heads // SSD_GROUPS
    gw = hpg * headdim
    d_lru = lx_s.shape[1]
    d_model = x_ref.shape[2]
    q = SSD_CHUNK

    @pl.when(t == 0)
    def _():
        xbc_s[0:HIST, :] = jnp.zeros((HIST, xbc_s.shape[1]), F32)
        lx_s[0:HIST, :] = jnp.zeros((HIST, d_lru), F32)
        ht_s[...] = jnp.zeros(ht_s.shape, F32)
        hl_s[...] = jnp.zeros(hl_s.shape, F32)

    xb_s[...] = x_ref[0].astype(BF16)

    xbc_s[HIST:HIST + tl, :] = _dot(xb_s[...], wxbc_ref[...])
    cwid = 512
    for r0 in range(0, tl, q):
        for c0 in range(0, xc_s.shape[1], cwid):
            xc_s[r0:r0 + q, c0:c0 + cwid] = _silu(
                _conv_block(xbc_s, scw_ref, scb_ref, HIST, 1, r0, q, c0, cwid))
    xbc_s[0:HIST, :] = xbc_s[tl:tl + HIST, :]

    row_i = lax.broadcasted_iota(jnp.int32, (q, q), 0)
    col_i = lax.broadcasted_iota(jnp.int32, (q, q), 1)
    causal = row_i >= col_i
    tri = jnp.where(causal, 1.0, 0.0).astype(BF16)
    lane_i = lax.broadcasted_iota(jnp.int32, (q, LANES), 1)
    left = lane_i < headdim

    def chunk_body(c, carry):
        r0 = pl.multiple_of(c * q, q)
        rows = pl.ds(r0, q)
        dt = _softplus(_dot(xb_s[rows, :], wdt_ref[...]) + dtb_ref[...])
        d_a = dt * aneg_ref[...]
        a_cs = _dot_exact_lhs(tri, d_a)
        a_last = a_cs[q - 1:q, :]
        wgt = dt * jnp.exp(a_last - a_cs)
        ea = jnp.exp(a_cs)
        w_exp = _dot_exact_rhs(wgt, e_ref[...])
        ea_exp = _dot_exact_rhs(ea, e_ref[...])
        a_cs_t = a_cs.T
        dt_t = dt.T
        for g in range(SSD_GROUPS):
            b_g = xc_s[rows, d_ssd + g * d_state:d_ssd + (g + 1) * d_state]
            c_g = xc_s[rows, d_ssd + gn + g * d_state:d_ssd + gn + (g + 1) * d_state]
            b_gb = b_g.astype(BF16)
            c_gb = c_g.astype(BF16)
            cb = _dot_nt(c_gb, b_gb)
            for hp in range(hpg // 2):
                c0 = g * gw + hp * 2 * headdim
                xs_pair = xc_s[rows, c0:c0 + 2 * headdim]
                lmats = []
                for j in range(2):
                    h = g * hpg + hp * 2 + j
                    seg = (jnp.broadcast_to(a_cs[:, h:h + 1], (q, q))
                           - jnp.broadcast_to(a_cs_t[h:h + 1, :], (q, q)))
                    dec = jnp.exp(jnp.where(causal, seg, -jnp.inf))
                    lmats.append((cb * dec * jnp.broadcast_to(dt_t[h:h + 1, :], (q, q))).astype(BF16))
                lpair = jnp.concatenate(lmats, axis=1)
                rhs = jnp.concatenate([jnp.where(left, xs_pair, 0.0),
                                       jnp.where(left, 0.0, xs_pair)], axis=0).astype(BF16)
                y_s[rows, c0:c0 + 2 * headdim] = _dot(lpair, rhs)
            gcols = slice(g * gw, (g + 1) * gw)
            h_g = ht_s[:, gcols]
            y_off = _dot(c_gb, h_g.astype(BF16)) * ea_exp[:, gcols]
            y_s[rows, gcols] = y_s[rows, gcols] + y_off
            xw = (xc_s[rows, gcols] * w_exp[:, gcols]).astype(BF16)
            ht_s[:, gcols] = h_g * ea_exp[q - 1:q, gcols] + _dot_tn(b_gb, xw)
        return carry

    lax.fori_loop(0, tl // q, chunk_body, 0)

    z_s[...] = _dot(xb_s[...], wz_ref[...])

    def gate_body(i, carry):
        rows = pl.ds(pl.multiple_of(i * ROW_BLK, ROW_BLK), ROW_BLK)
        y = y_s[rows, :] + dexp_ref[...] * xc_s[rows, 0:d_ssd]
        y = y * _silu(z_s[rows, :])
        ms = jnp.mean(y * y, axis=-1, keepdims=True)
        ysb_s[rows, :] = (y * lax.rsqrt(ms + RMS_EPS) * ng_ref[...]).astype(BF16)
        return carry

    lax.fori_loop(0, tl // ROW_BLK, gate_body, 0)

    lx_s[HIST:HIST + tl, :] = _dot(xb_s[...], wlx_ref[...])
    for r0 in range(0, tl, q):
        for c0 in range(0, d_lru, cwid):
            xr_s[r0:r0 + q, c0:c0 + cwid] = _conv_block(lx_s, lcw_ref, lcb_ref, HIST, 1, r0, q, c0, cwid)
    lx_s[0:HIST, :] = lx_s[tl:tl + HIST, :]
    ly_s[...] = _dot(xb_s[...], wly_ref[...])

    trow = lax.broadcasted_iota(jnp.int32, (tl, LANES), 0)
    is_row0 = trow == 0
    first_tok = jnp.logical_and(is_row0, t == 0)
    bw = d_lru // LRU_BLOCKS
    for n in range(LRU_BLOCKS):
        cols = slice(n * bw, (n + 1) * bw)
        xr = xr_s[:, cols]
        xrb = xr.astype(BF16)
        r = _sigmoid(_dot(xrb, wa_ref[n]) + ba_ref[:, cols])
        gi = _sigmoid(_dot(xrb, wx_ref[n]) + bx_ref[:, cols])
        log_a = (-LRU_C) * r * _softplus(-lam_ref[:, cols])
        a = jnp.exp(log_a)
        mult = jnp.sqrt(1.0 - jnp.exp(2.0 * log_a))
        mult = jnp.where(first_tok, 1.0, mult)
        u = mult * gi * xr
        u = u + jnp.where(is_row0, a * hl_s[0:1, cols], 0.0)
        d = 1
        while d < tl:
            keep = trow >= d
            u = jnp.where(keep, a * pltpu.roll(u, d, 0) + u, u)
            a = jnp.where(keep, a * pltpu.roll(a, d, 0), a)
            d *= 2
        hl_s[0:1, cols] = u[tl - 1:tl, :]
        ylb_s[:, cols] = (u * _gelu(ly_s[:, cols])).astype(BF16)

    half = d_model
    z_s[:, 0:half] = _dot(ysb_s[...], wso_ref[...])
    z_s[:, half:2 * half] = _dot(ylb_s[...], wlo_ref[...])
    y_s[:, 0:half] = _dot(xb_s[...], wgs_ref[...])
    y_s[:, half:2 * half] = _dot(xb_s[...], wgl_ref[...])

    def merge_body(i, carry):
        rows = pl.ds(pl.multiple_of(i * ROW_BLK, ROW_BLK), ROW_BLK)
        g_ssd = _sigmoid(y_s[rows, 0:half] + bg_ref[:, 0:half])
        g_lru = _sigmoid(y_s[rows, half:2 * half] + bg_ref[:, half:2 * half])
        ylb_s[rows, :] = (g_ssd * z_s[rows, 0:half] + g_lru * z_s[rows, half:2 * half]).astype(BF16)
        return carry

    lax.fori_loop(0, tl // ROW_BLK, merge_body, 0)
    z_s[:, 0:half] = _dot(ylb_s[...], wo_ref[...])

    def ln_body(i, carry):
        rows = pl.ds(pl.multiple_of(i * ROW_BLK, ROW_BLK), ROW_BLK)
        v = alpha * x_ref[0, rows, :] + z_s[rows, 0:half]
        x1_ref[0, rows, :] = _layer_norm(v, l1g_ref[...], l1b_ref[...])
        return carry

    lax.fori_loop(0, tl // ROW_BLK, ln_body, 0)

    @pl.when(t == nt - 1)
    def _():
        st_ref[0] = ht_s[...].T
        sconv_ref[0] = xbc_s[HIST - 3:HIST, :]
        lst_ref[0] = hl_s[0:1, :]
        lconv_ref[0] = lx_s[HIST - 3:HIST, :]


def _const_spec(shape):
    nd = len(shape)
    return pl.BlockSpec(shape, lambda *_: (0,) * nd, pipeline_mode=pl.Buffered(1))


def _prompt_mixer(x, wts, tl):
    nb, seq, d_model = x.shape
    n_heads, headdim, d_state = wts['n_heads'], wts['headdim'], wts['d_state']
    d_ssd = n_heads * headdim
    d_xbc = d_ssd + 2 * SSD_GROUPS * d_state
    d_lru = wts['wlx'].shape[1]
    names = ['wz', 'wxbc', 'wdt', 'wlx', 'wly', 'wgs', 'wgl', 'scw', 'scb', 'dtb', 'aneg', 'dexp', 'ng',
             'wso', 'lcw', 'lcb', 'wa', 'ba', 'wx', 'bx', 'lam', 'wlo', 'bg', 'wo', 'l1g', 'l1b', 'e']
    consts = [wts[k] for k in names]
    kern = functools.partial(_prompt_mixer_kernel, tl=tl, alpha=wts['alpha'], n_heads=n_heads,
                             headdim=headdim, d_state=d_state)
    out_shape = (
        jax.ShapeDtypeStruct((nb, seq, d_model), F32),
        jax.ShapeDtypeStruct((nb, d_ssd, d_state), F32),
        jax.ShapeDtypeStruct((nb, 3, d_xbc), F32),
        jax.ShapeDtypeStruct((nb, 1, d_lru), F32),
        jax.ShapeDtypeStruct((nb, 3, d_lru), F32),
    )
    out_specs = (
        pl.BlockSpec((1, tl, d_model), lambda b, t: (b, t, 0)),
        pl.BlockSpec((1, d_ssd, d_state), lambda b, t: (b, 0, 0)),
        pl.BlockSpec((1, 3, d_xbc), lambda b, t: (b, 0, 0)),
        pl.BlockSpec((1, 1, d_lru), lambda b, t: (b, 0, 0)),
        pl.BlockSpec((1, 3, d_lru), lambda b, t: (b, 0, 0)),
    )
    scratch = [
        pltpu.VMEM((tl, d_model), BF16),
        pltpu.VMEM((HIST + tl, d_xbc), F32),
        pltpu.VMEM((tl, d_xbc), F32),
        pltpu.VMEM((HIST + tl, d_lru), F32),
        pltpu.VMEM((tl, d_lru), F32),
        pltpu.VMEM((tl, d_lru), F32),
        pltpu.VMEM((tl, d_ssd), F32),
        pltpu.VMEM((tl, d_ssd), F32),
        pltpu.VMEM((tl, d_ssd), BF16),
        pltpu.VMEM((tl, d_lru), BF16),
        pltpu.VMEM((d_state, d_ssd), F32),
        pltpu.VMEM((SUBLANES, d_lru), F32),
    ]
    return pl.pallas_call(
        kern,
        grid=(nb, seq // tl),
        in_specs=[pl.BlockSpec((1, tl, d_model), lambda b, t: (b, t, 0))]
        + [_const_spec(c.shape) for c in consts],
        out_specs=out_specs,
        out_shape=out_shape,
        scratch_shapes=scratch,
        compiler_params=pltpu.CompilerParams(
            dimension_semantics=("arbitrary", "arbitrary"),
            vmem_limit_bytes=VMEM_LIMIT_BYTES),
        name="prompt_mixer",
    )(x, *consts)


def _ffn_kernel(x_ref, h0_ref, wg_ref, wu_ref, cw_ref, cb_ref, wd_ref, g_ref, b_ref,
                y_ref, tail_ref, xb_s, gb_s, hb_s, *, tm, stride, hist, alpha, fchunk):
    t = pl.program_id(1)
    nt = pl.num_programs(1)
    taps = cw_ref.shape[0]
    nh = (taps - 1) * stride
    d_ff = gb_s.shape[1]

    @pl.when(t == 0)
    def _():
        gb_s[hist - nh:hist, :] = h0_ref[0]

    xb_s[...] = x_ref[0].astype(BF16)
    acc = None
    for c0 in range(0, d_ff, fchunk):
        cols = slice(c0, c0 + fchunk)
        gb_s[hist:hist + tm, cols] = _dot(xb_s[...], wg_ref[:, cols])
        up = _dot(xb_s[...], wu_ref[:, cols])
        gc = _conv_block(gb_s, cw_ref, cb_ref, hist, stride, 0, tm, c0, fchunk)
        hb_s[...] = (_gelu(gc) * up).astype(BF16)
        part = _dot(hb_s[...], wd_ref[cols, :])
        acc = part if acc is None else acc + part
    v = alpha * x_ref[0] + acc
    y_ref[0] = _layer_norm(v, g_ref[...], b_ref[...])
    gb_s[hist - nh:hist, :] = gb_s[hist + tm - nh:hist + tm, :]

    @pl.when(t == nt - 1)
    def _():
        tail_ref[0] = gb_s[hist - nh:hist, :]


def _ffn(x, hist0, wts, tm, stride):
    nb, seq, d_model = x.shape
    d_ff = wts['wg'].shape[1]
    taps = wts['fcw'].shape[0]
    nh = (taps - 1) * stride
    hist = -(-nh // SUBLANES) * SUBLANES
    consts = [wts[k] for k in ['wg', 'wu', 'fcw', 'fcb', 'wd', 'l2g', 'l2b']]
    kern = functools.partial(_ffn_kernel, tm=tm, stride=stride, hist=hist, alpha=wts['alpha'], fchunk=1024)
    return pl.pallas_call(
        kern,
        grid=(nb, seq // tm),
        in_specs=[pl.BlockSpec((1, tm, d_model), lambda b, t: (b, t, 0)),
                  pl.BlockSpec((1, nh, d_ff), lambda b, t: (b, 0, 0))]
        + [_const_spec(c.shape) for c in consts],
        out_specs=(pl.BlockSpec((1, tm, d_model), lambda b, t: (b, t, 0)),
                   pl.BlockSpec((1, nh, d_ff), lambda b, t: (b, 0, 0))),
        out_shape=(jax.ShapeDtypeStruct((nb, seq, d_model), F32),
                   jax.ShapeDtypeStruct((nb, nh, d_ff), F32)),
        scratch_shapes=[pltpu.VMEM((tm, d_model), BF16),
                        pltpu.VMEM((hist + tm, d_ff), F32),
                        pltpu.VMEM((tm, 1024), BF16)],
        compiler_params=pltpu.CompilerParams(
            dimension_semantics=("arbitrary", "arbitrary"),
            vmem_limit_bytes=VMEM_LIMIT_BYTES),
        name="conv_ffn",
    )(x, hist0, *consts)


def _sample_ssd_kernel(
        x_ref, xall_ref, cssd_ref, wxbc_ref, wdt_ref, scw_ref, scb_ref, dtb_ref, aneg_ref, dexp_ref,
        e_ref, e2_ref,
        pre_ref, c_ref, b_ref, ea_ref, ydg_ref, xw_ref,
        xbc_s, xs_s, bs_s, acs_s, dts_s,
        *, nseq, steps, n_heads, headdim, d_state):
    l = pl.program_id(0)
    d_ssd = n_heads * headdim
    gn = SSD_GROUPS * d_state
    hist = (scw_ref.shape[0] - 1) * nseq
    r0 = l * nseq

    def blk(i):
        return pl.ds(pl.multiple_of(i * nseq, nseq), nseq)

    def sblk(i):
        return slice(i * nseq, (i + 1) * nseq)

    @pl.when(l == 0)
    def _():
        xbc_s[0:hist, :] = cssd_ref[...]
        dts_s[...] = _softplus(_dot(xall_ref[...].astype(BF16), wdt_ref[...]) + dtb_ref[...])
        acc = jnp.zeros((nseq, LANES), F32)
        for s in range(steps):
            acc = acc + dts_s[sblk(s), :] * aneg_ref[...]
            acs_s[sblk(s), :] = acc

    xb = x_ref[...].astype(BF16)

    pre = _dot(xb, wxbc_ref[...])
    pre_ref[...] = pre
    xbc_s[pl.ds(pl.multiple_of(hist + r0, nseq), nseq), :] = pre
    cwid = 512
    for c0 in range(0, d_ssd, cwid):
        xs_s[blk(l), c0:c0 + cwid] = _silu(
            _conv_block(xbc_s, scw_ref, scb_ref, hist, nseq, r0, nseq, c0, cwid))
    b_l = _silu(_conv_block(xbc_s, scw_ref, scb_ref, hist, nseq, r0, nseq, d_ssd, gn))
    c_l = _silu(_conv_block(xbc_s, scw_ref, scb_ref, hist, nseq, r0, nseq, d_ssd + gn, gn))
    bs_s[blk(l), :] = b_l
    b_ref[...] = b_l
    c_ref[...] = c_l

    a_cs = acs_s[blk(l), :]
    dt = dts_s[blk(l), :]
    a_end = acs_s[sblk(steps - 1), :]
    ea_ref[...] = _dot_exact_rhs(jnp.exp(a_cs), e_ref[...])
    xw_ref[...] = xs_s[blk(l), :] * _dot_exact_rhs(dt * jnp.exp(a_end - a_cs), e_ref[...])

    ydg_ref[...] = dexp_ref[...] * xs_s[blk(l), :]
    for s in range(steps):
        @pl.when(s <= l)
        def _(s=s):
            coef = _dot_exact_rhs(jnp.exp(a_cs - acs_s[sblk(s), :]) * dts_s[sblk(s), :], e_ref[...])
            cbx = _dot_exact_rhs(bs_s[sblk(s), :] * c_l, e2_ref[...])
            ydg_ref[...] += cbx * coef * xs_s[sblk(s), :]


def _sample_lru_kernel(
        x_ref, clru_ref, slru_ref, wlx_ref, wly_ref, lcw_ref, lcb_ref, wa_ref, ba_ref, wx_ref, bx_ref,
        lam_ref, prelx_ref, ylru_ref, lst_ref, lx_s, hl_s, *, nseq, steps, start_pos):
    l = pl.program_id(0)
    hist = (lcw_ref.shape[0] - 1) * nseq
    d_lru = lx_s.shape[1]
    r0 = l * nseq

    @pl.when(l == 0)
    def _():
        lx_s[0:hist, :] = clru_ref[...]
        hl_s[...] = slru_ref[...]

    xb = x_ref[...].astype(BF16)
    prelx = _dot(xb, wlx_ref[...])
    prelx_ref[...] = prelx
    lx_s[pl.ds(pl.multiple_of(hist + r0, nseq), nseq), :] = prelx
    ly = _dot(xb, wly_ref[...])
    bw = d_lru // LRU_BLOCKS
    first = (l + start_pos) == 0
    for n in range(LRU_BLOCKS):
        cols = slice(n * bw, (n + 1) * bw)
        xr = _conv_block(lx_s, lcw_ref, lcb_ref, hist, nseq, r0, nseq, n * bw, bw)
        xrb = xr.astype(BF16)
        r = _sigmoid(_dot(xrb, wa_ref[n]) + ba_ref[:, cols])
        gi = _sigmoid(_dot(xrb, wx_ref[n]) + bx_ref[:, cols])
        log_a = (-LRU_C) * r * _softplus(-lam_ref[:, cols])
        a = jnp.exp(log_a)
        mult = jnp.where(first, 1.0, jnp.sqrt(1.0 - jnp.exp(2.0 * log_a)))
        h = a * hl_s[:, cols] + mult * gi * xr
        hl_s[:, cols] = h
        ylru_ref[:, cols] = (h * _gelu(ly[:, cols])).astype(BF16)

    @pl.when(l == steps - 1)
    def _():
        lst_ref[...] = hl_s[...]


def _sample_state_kernel(st_ref, c_ref, b_ref, xw_ref, ea_ref, nst_ref, yoff_ref,
                         *, nseq, steps, n_heads, headdim, d_state):
    j = pl.program_id(0)
    half = nseq // 2
    nrow = 2 * steps
    hpg = n_heads // SSD_GROUPS
    gw = hpg * headdim

    def gather(ref, cols):
        return jnp.concatenate([ref[pl.ds(j + k * half, 1), cols] for k in range(nrow)], axis=0)

    c8 = gather(c_ref, slice(None))
    b8 = gather(b_ref, slice(None))
    par = lax.broadcasted_iota(jnp.int32, (nrow, gw), 0) % 2
    assert 2 * headdim == LANES and d_state == LANES
    low = lax.broadcasted_iota(jnp.int32, (nrow, LANES), 1) < headdim
    for g in range(SSD_GROUPS):
        gcols = slice(g * gw, (g + 1) * gw)
        c8g = c8[:, g * d_state:(g + 1) * d_state].astype(BF16)
        b8g = b8[:, g * d_state:(g + 1) * d_state].astype(BF16)
        xw8 = gather(xw_ref, gcols)
        ea8 = gather(ea_ref, gcols)
        cds = []
        for hp in range(hpg // 2):
            pair = ea8[:, hp * LANES:(hp + 1) * LANES]
            swapped = pltpu.roll(pair, headdim, 1)
            cds.append(jnp.where(low, pair, swapped))
            cds.append(jnp.where(low, swapped, pair))
        yo = None
        for e in range(2):
            sg = st_ref[e, gcols, :]
            yo_e = _dot_nt(c8g, sg.astype(BF16))
            yo = yo_e if e == 0 else jnp.where(par == e, yo_e, yo)
            xw_e = jnp.where(par == e, xw8, 0.0).astype(BF16)
            upd = _dot_tn(xw_e, b8g)
            k_last = 2 * (steps - 1) + e
            for hh in range(hpg):
                cd = cds[hh][k_last:k_last + 1, :]
                hr = slice(hh * headdim, (hh + 1) * headdim)
                nst_ref[e, g * gw + hh * headdim:g * gw + (hh + 1) * headdim, :] = sg[hr, :] * cd + upd[hr, :]
        yo = yo * ea8
        for k in range(nrow):
            yoff_ref[pl.ds(j + k * half, 1), gcols] = yo[k:k + 1, :]


def _sample_post_kernel(x_ref, ydg_ref, yoff_ref, ylru_ref, wz_ref, ng_ref, wso_ref, wlo_ref,
                        wgs_ref, wgl_ref, bg_ref, wo_ref, l1g_ref, l1b_ref, x1_ref, *, alpha):
    d_model = x_ref.shape[1]
    xb = x_ref[...].astype(BF16)
    y = (ydg_ref[...] + yoff_ref[...]) * _silu(_dot(xb, wz_ref[...]))
    ms = jnp.mean(y * y, axis=-1, keepdims=True)
    ysb = (y * lax.rsqrt(ms + RMS_EPS) * ng_ref[...]).astype(BF16)
    g_ssd = _sigmoid(_dot(xb, wgs_ref[...]) + bg_ref[:, 0:d_model])
    g_lru = _sigmoid(_dot(xb, wgl_ref[...]) + bg_ref[:, d_model:2 * d_model])
    merged = g_ssd * _dot(ysb, wso_ref[...]) + g_lru * _dot(ylru_ref[...], wlo_ref[...])
    o = _dot(merged.astype(BF16), wo_ref[...])
    x1_ref[...] = _layer_norm(alpha * x_ref[...] + o, l1g_ref[...], l1b_ref[...])


def _sample_mixer(x_lm, cssd_lm, clru_lm, slru_lm, state, wts, nseq, steps, start_pos):
    n_heads, headdim, d_state = wts['n_heads'], wts['headdim'], wts['d_state']
    d_model = x_lm.shape[1]
    d_ssd = n_heads * headdim
    gn = SSD_GROUPS * d_state
    d_xbc = d_ssd + 2 * gn
    d_lru = wts['wlx'].shape[1]
    ntok = steps * nseq
    dims = dict(nseq=nseq, steps=steps, n_heads=n_heads, headdim=headdim, d_state=d_state)
    params = pltpu.CompilerParams(dimension_semantics=("arbitrary",), vmem_limit_bytes=VMEM_LIMIT_BYTES)
    step_blk = lambda w: pl.BlockSpec((nseq, w), lambda l: (l, 0))

    sds = jax.ShapeDtypeStruct
    ssd_names = ['wxbc', 'wdt', 'scw', 'scb', 'dtb', 'aneg', 'dexp', 'e', 'e2']
    ssd_consts = [x_lm, cssd_lm] + [wts[k] for k in ssd_names]
    pre, c_lm, b_lm, ea_lm, ydg_lm, xw_lm = pl.pallas_call(
        functools.partial(_sample_ssd_kernel, **dims),
        grid=(steps,),
        in_specs=[step_blk(d_model)] + [_const_spec(c.shape) for c in ssd_consts],
        out_specs=(step_blk(d_xbc), step_blk(gn), step_blk(gn), step_blk(d_ssd), step_blk(d_ssd),
                   step_blk(d_ssd)),
        out_shape=(sds((ntok, d_xbc), F32), sds((ntok, gn), F32), sds((ntok, gn), F32),
                   sds((ntok, d_ssd), F32), sds((ntok, d_ssd), F32), sds((ntok, d_ssd), F32)),
        scratch_shapes=[
            pltpu.VMEM((cssd_lm.shape[0] + ntok, d_xbc), F32),
            pltpu.VMEM((ntok, d_ssd), F32),
            pltpu.VMEM((ntok, gn), F32),
            pltpu.VMEM((ntok, LANES), F32),
            pltpu.VMEM((ntok, LANES), F32),
        ],
        compiler_params=params,
        name="sample_ssd",
    )(x_lm, *ssd_consts)

    lru_names = ['wlx', 'wly', 'lcw', 'lcb', 'wa', 'ba', 'wx', 'bx', 'lam']
    lru_consts = [clru_lm, slru_lm] + [wts[k] for k in lru_names]
    prelx, ylru_lm, lst = pl.pallas_call(
        functools.partial(_sample_lru_kernel, nseq=nseq, steps=steps, start_pos=start_pos),
        grid=(steps,),
        in_specs=[step_blk(d_model)] + [_const_spec(c.shape) for c in lru_consts],
        out_specs=(step_blk(d_lru), step_blk(d_lru), pl.BlockSpec((nseq, d_lru), lambda l: (0, 0))),
        out_shape=(sds((ntok, d_lru), F32), sds((ntok, d_lru), BF16), sds((nseq, d_lru), F32)),
        scratch_shapes=[
            pltpu.VMEM((clru_lm.shape[0] + ntok, d_lru), F32),
            pltpu.VMEM((nseq, d_lru), F32),
        ],
        compiler_params=params,
        name="sample_lru",
    )(x_lm, *lru_consts)

    half = nseq // 2
    full = lambda a: _const_spec(a.shape)
    new_state, yoff_lm = pl.pallas_call(
        functools.partial(_sample_state_kernel, **dims),
        grid=(half,),
        in_specs=[pl.BlockSpec((None, 2, d_ssd, d_state), lambda j: (j, 0, 0, 0)),
                  full(c_lm), full(b_lm), full(xw_lm), full(ea_lm)],
        out_specs=(pl.BlockSpec((None, 2, d_ssd, d_state), lambda j: (j, 0, 0, 0)),
                   pl.BlockSpec((ntok, d_ssd), lambda j: (0, 0))),
        out_shape=(sds(state.shape, F32), sds((ntok, d_ssd), F32)),
        compiler_params=params,
        name="sample_state",
    )(state, c_lm, b_lm, xw_lm, ea_lm)

    post_names = ['wz', 'ng', 'wso', 'wlo', 'wgs', 'wgl', 'bg', 'wo', 'l1g', 'l1b']
    post_consts = [wts[k] for k in post_names]
    x1_lm = pl.pallas_call(
        functools.partial(_sample_post_kernel, alpha=wts['alpha']),
        grid=(steps,),
        in_specs=[step_blk(d_model), step_blk(d_ssd), step_blk(d_ssd), step_blk(d_lru)]
        + [_const_spec(c.shape) for c in post_consts],
        out_specs=step_blk(d_model),
        out_shape=sds((ntok, d_model), F32),
        compiler_params=params,
        name="sample_post",
    )(x_lm, ydg_lm, yoff_lm, ylru_lm, *post_consts)
    return x1_lm, new_state, pre, lst, prelx


def _prep_weights(w_in, b_gate, ssd_conv_w, ssd_conv_b, ssd_dt_bias, ssd_a_log, ssd_d, ssd_norm_g,
                  w_ssd_out, lru_conv_w, lru_conv_b, lru_wa, lru_ba, lru_wx, lru_bx, lru_lambda,
                  w_lru_out, w_o, ln1_g, ln1_b, ffn_w_gate, ffn_w_up, ffn_conv_w, ffn_conv_b,
                  ffn_w_down, ln2_g, ln2_b, n_heads, headdim, d_state):
    depth = w_in.shape[0]
    d_model = w_in.shape[1]
    d_ssd = n_heads * headdim
    d_xbc = d_ssd + 2 * SSD_GROUPS * d_state
    d_lru = lru_lambda.shape[1]
    sizes = (d_ssd, d_xbc, n_heads, d_lru, d_lru, d_model, d_model)
    cuts = np.cumsum((0,) + sizes)
    wi = w_in[0]
    parts = [wi[:, cuts[i]:cuts[i + 1]] for i in range(len(sizes))]
    row = lambda v: v.reshape(1, -1).astype(F32)
    pad_heads = lambda v: jnp.pad(v.reshape(1, -1).astype(F32), ((0, 0), (0, LANES - n_heads)))
    head_of_col = np.arange(d_ssd) // headdim
    expand = (np.arange(LANES)[:, None] == head_of_col[None, :]).astype(np.float32)
    return dict(
        n_heads=n_heads, headdim=headdim, d_state=d_state,
        alpha=float((2.0 * depth) ** 0.25),
        wz=parts[0].astype(BF16), wxbc=parts[1].astype(BF16),
        wdt=jnp.pad(parts[2], ((0, 0), (0, LANES - n_heads))).astype(BF16),
        wlx=parts[3].astype(BF16), wly=parts[4].astype(BF16),
        wgs=parts[5].astype(BF16), wgl=parts[6].astype(BF16),
        scw=ssd_conv_w[0].astype(F32), scb=row(ssd_conv_b[0]),
        dtb=pad_heads(ssd_dt_bias[0]), aneg=pad_heads(-jnp.exp(ssd_a_log[0].astype(F32))),
        dexp=row(jnp.repeat(ssd_d[0], headdim)), ng=row(ssd_norm_g[0]),
        wso=w_ssd_out[0].astype(BF16),
        lcw=lru_conv_w[0].astype(F32), lcb=row(lru_conv_b[0]),
        wa=lru_wa[0].astype(BF16), ba=row(lru_ba[0]), wx=lru_wx[0].astype(BF16), bx=row(lru_bx[0]),
        lam=row(lru_lambda[0]), wlo=w_lru_out[0].astype(BF16),
        bg=row(b_gate[0]), wo=w_o[0].astype(BF16), l1g=row(ln1_g[0]), l1b=row(ln1_b[0]),
        e=jnp.asarray(expand, BF16),
        e2=jnp.asarray(np.arange(SSD_GROUPS * d_state)[:, None] // d_state
                       == (head_of_col // (n_heads // SSD_GROUPS))[None, :], BF16),
        wg=ffn_w_gate[0].astype(BF16), wu=ffn_w_up[0].astype(BF16),
        fcw=ffn_conv_w[0].astype(F32), fcb=row(ffn_conv_b[0]), wd=ffn_w_down[0].astype(BF16),
        l2g=row(ln2_g[0]), l2b=row(ln2_b[0]),
    )


def kernel(x_prompt, x_sample, state_ssd, cache_ssd_conv, state_lru, cache_lru_conv, cache_ffn_conv, w_in, b_gate, ssd_conv_w, ssd_conv_b, ssd_dt_bias, ssd_a_log, ssd_d, ssd_norm_g, w_ssd_out, lru_conv_w, lru_conv_b, lru_wa, lru_ba, lru_wx, lru_bx, lru_lambda, w_lru_out, w_o, ln1_g, ln1_b, ffn_w_gate, ffn_w_up, ffn_conv_w, ffn_conv_b, ffn_w_down, ln2_g, ln2_b):
    assert w_in.shape[0] == 1, "single-layer trunk"
    _, _, n_heads, headdim, d_state = state_ssd.shape
    wts = _prep_weights(w_in, b_gate, ssd_conv_w, ssd_conv_b, ssd_dt_bias, ssd_a_log, ssd_d, ssd_norm_g,
                        w_ssd_out, lru_conv_w, lru_conv_b, lru_wa, lru_ba, lru_wx, lru_bx, lru_lambda,
                        w_lru_out, w_o, ln1_g, ln1_b, ffn_w_gate, ffn_w_up, ffn_conv_w, ffn_conv_b,
                        ffn_w_down, ln2_g, ln2_b, n_heads, headdim, d_state)
    bp = x_prompt.shape[0]
    d_ff = ffn_w_gate.shape[2]

    x1_p, p_ssd, p_ssd_buf, p_lru, p_lru_buf = _prompt_mixer(x_prompt, wts, tl=256)
    y_prompt, p_ffn_buf = _ffn(x1_p, jnp.zeros((bp, ffn_conv_w.shape[1] - 1, d_ff), F32), wts, tm=512, stride=1)
    p_ssd = p_ssd.reshape(1, bp, n_heads, headdim, d_state)

    nb_s, steps, _ = x_sample.shape
    half = nb_s // 2

    def to_lm(a):
        k, c = a.shape[1], a.shape[2]
        return a.reshape(half, 2, k, c).transpose(2, 1, 0, 3).reshape(k * nb_s, c)

    def from_lm(a, k):
        c = a.shape[1]
        return a.reshape(k, 2, half, c).transpose(2, 1, 0, 3).reshape(nb_s, k, c)

    d_ssd = n_heads * headdim
    x1_lm, new_state, pre, lst, prelx = _sample_mixer(
        to_lm(x_sample), to_lm(cache_ssd_conv[0]), to_lm(cache_lru_conv[0]), to_lm(state_lru[0][:, None, :]),
        state_ssd[0].reshape(half, 2, d_ssd, d_state), wts, nb_s, steps, PAST_LEN)
    y_lm, tail = _ffn(x1_lm[None], to_lm(cache_ffn_conv[0])[None], wts, tm=steps * nb_s, stride=nb_s)
    k_ssd = ssd_conv_w.shape[1] - 1
    k_lru = lru_conv_w.shape[1] - 1
    k_ffn = ffn_conv_w.shape[1] - 1
    assert steps >= max(k_ssd, k_lru, k_ffn)
    return (y_prompt, from_lm(y_lm[0], steps), p_ssd, p_ssd_buf[None], p_lru.reshape(1, bp, -1), p_lru_buf[None],
            p_ffn_buf[None],
            new_state.reshape(1, nb_s, n_heads, headdim, d_state),
            from_lm(pre[(steps - k_ssd) * nb_s:], k_ssd)[None],
            from_lm(lst, 1).reshape(1, nb_s, -1),
            from_lm(prelx[(steps - k_lru) * nb_s:], k_lru)[None],
            from_lm(tail[0], k_ffn)[None])
```

```python
import functools

import numpy as np
import jax
import jax.numpy as jnp
from jax import lax
from jax.experimental import pallas as pl
from jax.experimental.pallas import tpu as pltpu

F32 = jnp.float32
BF16 = jnp.bfloat16

SSD_GROUPS = 4
SSD_CHUNK = 128
LRU_BLOCKS = 8
LRU_C = 8.0
LN_EPS = 1e-5
RMS_EPS = 1e-6
PAST_LEN = 16384

LANES = 128
SUBLANES = 8
VMEM_LIMIT_BYTES = 60 * 1024 * 1024

HIST = 8
ROW_BLK = 32
UNROLL = True


def _dot(a, b):
    return jnp.dot(a, b, preferred_element_type=F32)


def _dot_nt(a, b):
    return lax.dot_general(a, b, (((1,), (1,)), ((), ())), preferred_element_type=F32)


def _dot_tn(a, b):
    return lax.dot_general(a, b, (((0,), (0,)), ((), ())), preferred_element_type=F32)


def _split3(v):
    hi = v.astype(BF16)
    r1 = v - hi.astype(F32)
    mid = r1.astype(BF16)
    lo = (r1 - mid.astype(F32)).astype(BF16)
    return hi, mid, lo


def _dot_exact_rhs(v, m):
    hi, mid, lo = _split3(v)
    return _dot(hi, m) + _dot(mid, m) + _dot(lo, m)


def _dot_exact_lhs(m, v):
    hi, mid, lo = _split3(v)
    return _dot(m, hi) + _dot(m, mid) + _dot(m, lo)


def _softplus(x):
    return jnp.maximum(x, 0.0) + jnp.log1p(jnp.exp(-jnp.abs(x)))


def _sigmoid(x):
    return 1.0 / (1.0 + jnp.exp(-x))


def _silu(x):
    return x * _sigmoid(x)


def _gelu(x):
    c = np.sqrt(2.0 / np.pi).astype(np.float32)
    return 0.5 * x * (1.0 + jnp.tanh(c * (x + 0.044715 * (x * x * x))))


def _layer_norm(v, g, b):
    mu = jnp.mean(v, axis=-1, keepdims=True)
    d = v - mu
    var = jnp.mean(d * d, axis=-1, keepdims=True)
    return d * lax.rsqrt(var + LN_EPS) * g + b


def _rows(i, n):
    if isinstance(i, int):
        return slice(i * n, (i + 1) * n)
    return pl.ds(pl.multiple_of(i * n, n), n)


def _loop(n, body, unroll):
    if unroll:
        for i in range(n):
            body(i, None)
    else:
        lax.fori_loop(0, n, lambda i, c: (body(i, c), c)[1], 0)


def _conv_block(buf_ref, w_ref, b_ref, hist, stride, r0, rows, c0, cw):
    taps = w_ref.shape[0]
    acc = b_ref[:, c0:c0 + cw]
    for k in range(taps):
        off = hist + r0 - (taps - 1 - k) * stride
        if not isinstance(off, int):
            off = pl.multiple_of(off, SUBLANES)
        acc = acc + w_ref[k:k + 1, c0:c0 + cw] * buf_ref[pl.ds(off, rows), c0:c0 + cw]
    return acc


def _prompt_mixer_kernel(
        x_ref, wz_ref, wxbc_ref, wdt_ref, wlx_ref, wly_ref, wgs_ref, wgl_ref,
        scw_ref, scb_ref, dtb_ref, aneg_ref, dexp_ref, ng_ref, wso_ref,
        lcw_ref, lcb_ref, wa_ref, ba_ref, wx_ref, bx_ref, lam_ref, wlo_ref,
        bg_ref, wo_ref, l1g_ref, l1b_ref, e_ref,
        x1_ref, st_ref, sconv_ref, lst_ref, lconv_ref,
        xb_s, xbc_s, xc_s, lx_s, xr_s, ly_s, y_s, z_s, ysb_s, ylb_s, ht_s, hl_s,
        *, tl, alpha, n_heads, headdim, d_state):
    t = pl.program_id(1)
    nt = pl.num_programs(1)
    d_ssd = n_heads * headdim
    gn = SSD_GROUPS * d_state
    hpg = n_heads // SSD_GROUPS
    gw = hpg * headdim
    d_lru = lx_s.shape[1]
    d_model = x_ref.shape[2]
    q = SSD_CHUNK

    @pl.when(t == 0)
    def _():
        xbc_s[0:HIST, :] = jnp.zeros((HIST, xbc_s.shape[1]), F32)
        lx_s[0:HIST, :] = jnp.zeros((HIST, d_lru), F32)
        ht_s[...] = jnp.zeros(ht_s.shape, F32)
        hl_s[...] = jnp.zeros(hl_s.shape, F32)

    xb_s[...] = x_ref[0].astype(BF16)

    xbc_s[HIST:HIST + tl, :] = _dot(xb_s[...], wxbc_ref[...])
    cwid = 512
    for r0 in range(0, tl, q):
        for c0 in range(0, xc_s.shape[1], cwid):
            xc_s[r0:r0 + q, c0:c0 + cwid] = _silu(
                _conv_block(xbc_s, scw_ref, scb_ref, HIST, 1, r0, q, c0, cwid))
    xbc_s[0:HIST, :] = xbc_s[tl:tl + HIST, :]

    row_i = lax.broadcasted_iota(jnp.int32, (q, q), 0)
    col_i = lax.broadcasted_iota(jnp.int32, (q, q), 1)
    causal = row_i >= col_i
    tri = jnp.where(causal, 1.0, 0.0).astype(BF16)
    lane_i = lax.broadcasted_iota(jnp.int32, (q, LANES), 1)
    left = lane_i < headdim

    def chunk_body(c, carry):
        rows = _rows(c, q)
        dt = _softplus(_dot(xb_s[rows, :], wdt_ref[...]) + dtb_ref[...])
        d_a = dt * aneg_ref[...]
        a_cs = _dot_exact_lhs(tri, d_a)
        a_last = a_cs[q - 1:q, :]
        wgt = dt * jnp.exp(a_last - a_cs)
        ea = jnp.exp(a_cs)
        w_exp = _dot_exact_rhs(wgt, e_ref[...])
        ea_exp = _dot_exact_rhs(ea, e_ref[...])
        a_cs_t = a_cs.T
        dt_t = dt.T
        for g in range(SSD_GROUPS):
            b_g = xc_s[rows, d_ssd + g * d_state:d_ssd + (g + 1) * d_state]
            c_g = xc_s[rows, d_ssd + gn + g * d_state:d_ssd + gn + (g + 1) * d_state]
            b_gb = b_g.astype(BF16)
            c_gb = c_g.astype(BF16)
            cb = _dot_nt(c_gb, b_gb)
            for hp in range(hpg // 2):
                c0 = g * gw + hp * 2 * headdim
                xs_pair = xc_s[rows, c0:c0 + 2 * headdim]
                lmats = []
                for j in range(2):
                    h = g * hpg + hp * 2 + j
                    seg = (jnp.broadcast_to(a_cs[:, h:h + 1], (q, q))
                           - jnp.broadcast_to(a_cs_t[h:h + 1, :], (q, q)))
                    dec = jnp.exp(jnp.where(causal, seg, -jnp.inf))
                    lmats.append((cb * dec * jnp.broadcast_to(dt_t[h:h + 1, :], (q, q))).astype(BF16))
                lpair = jnp.concatenate(lmats, axis=1)
                rhs = jnp.concatenate([jnp.where(left, xs_pair, 0.0),
                                       jnp.where(left, 0.0, xs_pair)], axis=0).astype(BF16)
                y_s[rows, c0:c0 + 2 * headdim] = _dot(lpair, rhs)
            gcols = slice(g * gw, (g + 1) * gw)
            h_g = ht_s[:, gcols]
            y_off = _dot(c_gb, h_g.astype(BF16)) * ea_exp[:, gcols]
            y_s[rows, gcols] = y_s[rows, gcols] + y_off
            xw = (xc_s[rows, gcols] * w_exp[:, gcols]).astype(BF16)
            ht_s[:, gcols] = h_g * ea_exp[q - 1:q, gcols] + _dot_tn(b_gb, xw)
        return carry

    _loop(tl // q, chunk_body, UNROLL)

    z_s[...] = _dot(xb_s[...], wz_ref[...])

    def gate_body(i, carry):
        rows = _rows(i, ROW_BLK)
        y = y_s[rows, :] + dexp_ref[...] * xc_s[rows, 0:d_ssd]
        y = y * _silu(z_s[rows, :])
        ms = jnp.mean(y * y, axis=-1, keepdims=True)
        ysb_s[rows, :] = (y * lax.rsqrt(ms + RMS_EPS) * ng_ref[...]).astype(BF16)
        return carry

    _loop(tl // ROW_BLK, gate_body, UNROLL)

    lx_s[HIST:HIST + tl, :] = _dot(xb_s[...], wlx_ref[...])
    for r0 in range(0, tl, q):
        for c0 in range(0, d_lru, cwid):
            xr_s[r0:r0 + q, c0:c0 + cwid] = _conv_block(lx_s, lcw_ref, lcb_ref, HIST, 1, r0, q, c0, cwid)
    lx_s[0:HIST, :] = lx_s[tl:tl + HIST, :]
    ly_s[...] = _dot(xb_s[...], wly_ref[...])

    trow = lax.broadcasted_iota(jnp.int32, (tl, LANES), 0)
    is_row0 = trow == 0
    first_tok = jnp.logical_and(is_row0, t == 0)
    bw = d_lru // LRU_BLOCKS
    for n in range(LRU_BLOCKS):
        cols = slice(n * bw, (n + 1) * bw)
        xr = xr_s[:, cols]
        xrb = xr.astype(BF16)
        r = _sigmoid(_dot(xrb, wa_ref[n]) + ba_ref[:, cols])
        gi = _sigmoid(_dot(xrb, wx_ref[n]) + bx_ref[:, cols])
        log_a = (-LRU_C) * r * _softplus(-lam_ref[:, cols])
        a = jnp.exp(log_a)
        mult = jnp.sqrt(1.0 - jnp.exp(2.0 * log_a))
        mult = jnp.where(first_tok, 1.0, mult)
        u = mult * gi * xr
        u = u + jnp.where(is_row0, a * hl_s[0:1, cols], 0.0)
        d = 1
        while d < tl:
            keep = trow >= d
            u = jnp.where(keep, a * pltpu.roll(u, d, 0) + u, u)
            a = jnp.where(keep, a * pltpu.roll(a, d, 0), a)
            d *= 2
        hl_s[0:1, cols] = u[tl - 1:tl, :]
        ylb_s[:, cols] = (u * _gelu(ly_s[:, cols])).astype(BF16)

    half = d_model
    z_s[:, 0:half] = _dot(ysb_s[...], wso_ref[...])
    z_s[:, half:2 * half] = _dot(ylb_s[...], wlo_ref[...])
    y_s[:, 0:half] = _dot(xb_s[...], wgs_ref[...])
    y_s[:, half:2 * half] = _dot(xb_s[...], wgl_ref[...])

    def merge_body(i, carry):
        rows = _rows(i, ROW_BLK)
        g_ssd = _sigmoid(y_s[rows, 0:half] + bg_ref[:, 0:half])
        g_lru = _sigmoid(y_s[rows, half:2 * half] + bg_ref[:, half:2 * half])
        ylb_s[rows, :] = (g_ssd * z_s[rows, 0:half] + g_lru * z_s[rows, half:2 * half]).astype(BF16)
        return carry

    _loop(tl // ROW_BLK, merge_body, UNROLL)
    z_s[:, 0:half] = _dot(ylb_s[...], wo_ref[...])

    def ln_body(i, carry):
        rows = _rows(i, ROW_BLK)
        v = alpha * x_ref[0, rows, :] + z_s[rows, 0:half]
        x1_ref[0, rows, :] = _layer_norm(v, l1g_ref[...], l1b_ref[...])
        return carry

    _loop(tl // ROW_BLK, ln_body, UNROLL)

    @pl.when(t == nt - 1)
    def _():
        st_ref[0] = ht_s[...].T
        sconv_ref[0] = xbc_s[HIST - 3:HIST, :]
        lst_ref[0] = hl_s[0:1, :]
        lconv_ref[0] = lx_s[HIST - 3:HIST, :]


def _const_spec(shape):
    nd = len(shape)
    return pl.BlockSpec(shape, lambda *_: (0,) * nd, pipeline_mode=pl.Buffered(1))


def _prompt_mixer(x, wts, tl):
    nb, seq, d_model = x.shape
    n_heads, headdim, d_state = wts['n_heads'], wts['headdim'], wts['d_state']
    d_ssd = n_heads * headdim
    d_xbc = d_ssd + 2 * SSD_GROUPS * d_state
    d_lru = wts['wlx'].shape[1]
    names = ['wz', 'wxbc', 'wdt', 'wlx', 'wly', 'wgs', 'wgl', 'scw', 'scb', 'dtb', 'aneg', 'dexp', 'ng',
             'wso', 'lcw', 'lcb', 'wa', 'ba', 'wx', 'bx', 'lam', 'wlo', 'bg', 'wo', 'l1g', 'l1b', 'e']
    consts = [wts[k] for k in names]
    kern = functools.partial(_prompt_mixer_kernel, tl=tl, alpha=wts['alpha'], n_heads=n_heads,
                             headdim=headdim, d_state=d_state)
    out_shape = (
        jax.ShapeDtypeStruct((nb, seq, d_model), F32),
        jax.ShapeDtypeStruct((nb, d_ssd, d_state), F32),
        jax.ShapeDtypeStruct((nb, 3, d_xbc), F32),
        jax.ShapeDtypeStruct((nb, 1, d_lru), F32),
        jax.ShapeDtypeStruct((nb, 3, d_lru), F32),
    )
    out_specs = (
        pl.BlockSpec((1, tl, d_model), lambda b, t: (b, t, 0)),
        pl.BlockSpec((1, d_ssd, d_state), lambda b, t: (b, 0, 0)),
        pl.BlockSpec((1, 3, d_xbc), lambda b, t: (b, 0, 0)),
        pl.BlockSpec((1, 1, d_lru), lambda b, t: (b, 0, 0)),
        pl.BlockSpec((1, 3, d_lru), lambda b, t: (b, 0, 0)),
    )
    scratch = [
        pltpu.VMEM((tl, d_model), BF16),
        pltpu.VMEM((HIST + tl, d_xbc), F32),
        pltpu.VMEM((tl, d_xbc), F32),
        pltpu.VMEM((HIST + tl, d_lru), F32),
        pltpu.VMEM((tl, d_lru), F32),
        pltpu.VMEM((tl, d_lru), F32),
        pltpu.VMEM((tl, d_ssd), F32),
        pltpu.VMEM((tl, d_ssd), F32),
        pltpu.VMEM((tl, d_ssd), BF16),
        pltpu.VMEM((tl, d_lru), BF16),
        pltpu.VMEM((d_state, d_ssd), F32),
        pltpu.VMEM((SUBLANES, d_lru), F32),
    ]
    return pl.pallas_call(
        kern,
        grid=(nb, seq // tl),
        in_specs=[pl.BlockSpec((1, tl, d_model), lambda b, t: (b, t, 0))]
        + [_const_spec(c.shape) for c in consts],
        out_specs=out_specs,
        out_shape=out_shape,
        scratch_shapes=scratch,
        compiler_params=pltpu.CompilerParams(
            dimension_semantics=("arbitrary", "arbitrary"),
            vmem_limit_bytes=VMEM_LIMIT_BYTES),
        name="prompt_mixer",
    )(x, *consts)


def _ffn_kernel(x_ref, h0_ref, wg_ref, wu_ref, cw_ref, cb_ref, wd_ref, g_ref, b_ref,
                y_ref, tail_ref, xb_s, gb_s, hb_s, *, tm, stride, hist, alpha, fchunk):
    t = pl.program_id(1)
    nt = pl.num_programs(1)
    taps = cw_ref.shape[0]
    nh = (taps - 1) * stride
    d_ff = gb_s.shape[1]

    @pl.when(t == 0)
    def _():
        gb_s[hist - nh:hist, :] = h0_ref[0]

    xb_s[...] = x_ref[0].astype(BF16)
    acc = None
    for c0 in range(0, d_ff, fchunk):
        cols = slice(c0, c0 + fchunk)
        gb_s[hist:hist + tm, cols] = _dot(xb_s[...], wg_ref[:, cols])
        up = _dot(xb_s[...], wu_ref[:, cols])
        gc = _conv_block(gb_s, cw_ref, cb_ref, hist, stride, 0, tm, c0, fchunk)
        hb_s[...] = (_gelu(gc) * up).astype(BF16)
        part = _dot(hb_s[...], wd_ref[cols, :])
        acc = part if acc is None else acc + part
    v = alpha * x_ref[0] + acc
    y_ref[0] = _layer_norm(v, g_ref[...], b_ref[...])
    gb_s[hist - nh:hist, :] = gb_s[hist + tm - nh:hist + tm, :]

    @pl.when(t == nt - 1)
    def _():
        tail_ref[0] = gb_s[hist - nh:hist, :]


def _ffn(x, hist0, wts, tm, stride):
    nb, seq, d_model = x.shape
    d_ff = wts['wg'].shape[1]
    taps = wts['fcw'].shape[0]
    nh = (taps - 1) * stride
    hist = -(-nh // SUBLANES) * SUBLANES
    consts = [wts[k] for k in ['wg', 'wu', 'fcw', 'fcb', 'wd', 'l2g', 'l2b']]
    kern = functools.partial(_ffn_kernel, tm=tm, stride=stride, hist=hist, alpha=wts['alpha'], fchunk=1024)
    return pl.pallas_call(
        kern,
        grid=(nb, seq // tm),
        in_specs=[pl.BlockSpec((1, tm, d_model), lambda b, t: (b, t, 0)),
                  pl.BlockSpec((1, nh, d_ff), lambda b, t: (b, 0, 0))]
        + [_const_spec(c.shape) for c in consts],
        out_specs=(pl.BlockSpec((1, tm, d_model), lambda b, t: (b, t, 0)),
                   pl.BlockSpec((1, nh, d_ff), lambda b, t: (b, 0, 0))),
        out_shape=(jax.ShapeDtypeStruct((nb, seq, d_model), F32),
                   jax.ShapeDtypeStruct((nb, nh, d_ff), F32)),
        scratch_shapes=[pltpu.VMEM((tm, d_model), BF16),
                        pltpu.VMEM((hist + tm, d_ff), F32),
                        pltpu.VMEM((tm, 1024), BF16)],
        compiler_params=pltpu.CompilerParams(
            dimension_semantics=("arbitrary", "arbitrary"),
            vmem_limit_bytes=VMEM_LIMIT_BYTES),
        name="conv_ffn",
    )(x, hist0, *consts)


def _sample_ssd_kernel(
        x_ref, xall_ref, cssd_ref, wxbc_ref, wdt_ref, scw_ref, scb_ref, dtb_ref, aneg_ref, dexp_ref,
        e_ref, e2_ref,
        pre_ref, c_ref, b_ref, ea_ref, ydg_ref, xw_ref,
        xbc_s, xs_s, bs_s, acs_s, dts_s,
        *, nseq, steps, n_heads, headdim, d_state):
    l = pl.program_id(0)
    d_ssd = n_heads * headdim
    gn = SSD_GROUPS * d_state
    hist = (scw_ref.shape[0] - 1) * nseq
    r0 = l * nseq

    def blk(i):
        return pl.ds(pl.multiple_of(i * nseq, nseq), nseq)

    def sblk(i):
        return slice(i * nseq, (i + 1) * nseq)

    @pl.when(l == 0)
    def _():
        xbc_s[0:hist, :] = cssd_ref[...]
        dts_s[...] = _softplus(_dot(xall_ref[...].astype(BF16), wdt_ref[...]) + dtb_ref[...])
        acc = jnp.zeros((nseq, LANES), F32)
        for s in range(steps):
            acc = acc + dts_s[sblk(s), :] * aneg_ref[...]
            acs_s[sblk(s), :] = acc

    xb = x_ref[...].astype(BF16)

    pre = _dot(xb, wxbc_ref[...])
    pre_ref[...] = pre
    xbc_s[pl.ds(pl.multiple_of(hist + r0, nseq), nseq), :] = pre
    cwid = 512
    for c0 in range(0, d_ssd, cwid):
        xs_s[blk(l), c0:c0 + cwid] = _silu(
            _conv_block(xbc_s, scw_ref, scb_ref, hist, nseq, r0, nseq, c0, cwid))
    b_l = _silu(_conv_block(xbc_s, scw_ref, scb_ref, hist, nseq, r0, nseq, d_ssd, gn))
    c_l = _silu(_conv_block(xbc_s, scw_ref, scb_ref, hist, nseq, r0, nseq, d_ssd + gn, gn))
    bs_s[blk(l), :] = b_l
    b_ref[...] = b_l
    c_ref[...] = c_l

    a_cs = acs_s[blk(l), :]
    dt = dts_s[blk(l), :]
    a_end = acs_s[sblk(steps - 1), :]
    ea_ref[...] = _dot_exact_rhs(jnp.exp(a_cs), e_ref[...])
    xw_ref[...] = xs_s[blk(l), :] * _dot_exact_rhs(dt * jnp.exp(a_end - a_cs), e_ref[...])

    ydg_ref[...] = dexp_ref[...] * xs_s[blk(l), :]
    for s in range(steps):
        @pl.when(s <= l)
        def _(s=s):
            coef = _dot_exact_rhs(jnp.exp(a_cs - acs_s[sblk(s), :]) * dts_s[sblk(s), :], e_ref[...])
            cbx = _dot_exact_rhs(bs_s[sblk(s), :] * c_l, e2_ref[...])
            ydg_ref[...] += cbx * coef * xs_s[sblk(s), :]


def _sample_lru_kernel(
        x_ref, clru_ref, slru_ref, wlx_ref, wly_ref, lcw_ref, lcb_ref, wa_ref, ba_ref, wx_ref, bx_ref,
        lam_ref, prelx_ref, ylru_ref, lst_ref, lx_s, hl_s, *, nseq, steps, start_pos):
    l = pl.program_id(0)
    hist = (lcw_ref.shape[0] - 1) * nseq
    d_lru = lx_s.shape[1]
    r0 = l * nseq

    @pl.when(l == 0)
    def _():
        lx_s[0:hist, :] = clru_ref[...]
        hl_s[...] = slru_ref[...]

    xb = x_ref[...].astype(BF16)
    prelx = _dot(xb, wlx_ref[...])
    prelx_ref[...] = prelx
    lx_s[pl.ds(pl.multiple_of(hist + r0, nseq), nseq), :] = prelx
    ly = _dot(xb, wly_ref[...])
    bw = d_lru // LRU_BLOCKS
    first = (l + start_pos) == 0
    for n in range(LRU_BLOCKS):
        cols = slice(n * bw, (n + 1) * bw)
        xr = _conv_block(lx_s, lcw_ref, lcb_ref, hist, nseq, r0, nseq, n * bw, bw)
        xrb = xr.astype(BF16)
        r = _sigmoid(_dot(xrb, wa_ref[n]) + ba_ref[:, cols])
        gi = _sigmoid(_dot(xrb, wx_ref[n]) + bx_ref[:, cols])
        log_a = (-LRU_C) * r * _softplus(-lam_ref[:, cols])
        a = jnp.exp(log_a)
        mult = jnp.where(first, 1.0, jnp.sqrt(1.0 - jnp.exp(2.0 * log_a)))
        h = a * hl_s[:, cols] + mult * gi * xr
        hl_s[:, cols] = h
        ylru_ref[:, cols] = (h * _gelu(ly[:, cols])).astype(BF16)

    @pl.when(l == steps - 1)
    def _():
        lst_ref[...] = hl_s[...]


def _sample_state_kernel(st_ref, c_ref, b_ref, xw_ref, ea_ref, nst_ref, yoff_ref,
                         *, nseq, steps, n_heads, headdim, d_state):
    j = pl.program_id(0)
    half = nseq // 2
    nrow = 2 * steps
    hpg = n_heads // SSD_GROUPS
    gw = hpg * headdim

    def gather(ref, cols):
        return jnp.concatenate([ref[pl.ds(j + k * half, 1), cols] for k in range(nrow)], axis=0)

    c8 = gather(c_ref, slice(None))
    b8 = gather(b_ref, slice(None))
    par = lax.broadcasted_iota(jnp.int32, (nrow, gw), 0) % 2
    assert 2 * headdim == LANES and d_state == LANES
    low = lax.broadcasted_iota(jnp.int32, (nrow, LANES), 1) < headdim
    for g in range(SSD_GROUPS):
        gcols = slice(g * gw, (g + 1) * gw)
        c8g = c8[:, g * d_state:(g + 1) * d_state].astype(BF16)
        b8g = b8[:, g * d_state:(g + 1) * d_state].astype(BF16)
        xw8 = gather(xw_ref, gcols)
        ea8 = gather(ea_ref, gcols)
        cds = []
        for hp in range(hpg // 2):
            pair = ea8[:, hp * LANES:(hp + 1) * LANES]
            swapped = pltpu.roll(pair, headdim, 1)
            cds.append(jnp.where(low, pair, swapped))
            cds.append(jnp.where(low, swapped, pair))
        yo = None
        for e in range(2):
            sg = st_ref[e, gcols, :]
            yo_e = _dot_nt(c8g, sg.astype(BF16))
            yo = yo_e if e == 0 else jnp.where(par == e, yo_e, yo)
            xw_e = jnp.where(par == e, xw8, 0.0).astype(BF16)
            upd = _dot_tn(xw_e, b8g)
            k_last = 2 * (steps - 1) + e
            for hh in range(hpg):
                cd = cds[hh][k_last:k_last + 1, :]
                hr = slice(hh * headdim, (hh + 1) * headdim)
                nst_ref[e, g * gw + hh * headdim:g * gw + (hh + 1) * headdim, :] = sg[hr, :] * cd + upd[hr, :]
        yo = yo * ea8
        for k in range(nrow):
            yoff_ref[pl.ds(j + k * half, 1), gcols] = yo[k:k + 1, :]


def _sample_post_kernel(x_ref, ydg_ref, yoff_ref, ylru_ref, wz_ref, ng_ref, wso_ref, wlo_ref,
                        wgs_ref, wgl_ref, bg_ref, wo_ref, l1g_ref, l1b_ref, x1_ref, *, alpha):
    d_model = x_ref.shape[1]
    xb = x_ref[...].astype(BF16)
    y = (ydg_ref[...] + yoff_ref[...]) * _silu(_dot(xb, wz_ref[...]))
    ms = jnp.mean(y * y, axis=-1, keepdims=True)
    ysb = (y * lax.rsqrt(ms + RMS_EPS) * ng_ref[...]).astype(BF16)
    g_ssd = _sigmoid(_dot(xb, wgs_ref[...]) + bg_ref[:, 0:d_model])
    g_lru = _sigmoid(_dot(xb, wgl_ref[...]) + bg_ref[:, d_model:2 * d_model])
    merged = g_ssd * _dot(ysb, wso_ref[...]) + g_lru * _dot(ylru_ref[...], wlo_ref[...])
    o = _dot(merged.astype(BF16), wo_ref[...])
    x1_ref[...] = _layer_norm(alpha * x_ref[...] + o, l1g_ref[...], l1b_ref[...])


def _sample_mixer(x_lm, cssd_lm, clru_lm, slru_lm, state, wts, nseq, steps, start_pos):
    n_heads, headdim, d_state = wts['n_heads'], wts['headdim'], wts['d_state']
    d_model = x_lm.shape[1]
    d_ssd = n_heads * headdim
    gn = SSD_GROUPS * d_state
    d_xbc = d_ssd + 2 * gn
    d_lru = wts['wlx'].shape[1]
    ntok = steps * nseq
    dims = dict(nseq=nseq, steps=steps, n_heads=n_heads, headdim=headdim, d_state=d_state)
    params = pltpu.CompilerParams(dimension_semantics=("arbitrary",), vmem_limit_bytes=VMEM_LIMIT_BYTES)
    step_blk = lambda w: pl.BlockSpec((nseq, w), lambda l: (l, 0))

    sds = jax.ShapeDtypeStruct
    ssd_names = ['wxbc', 'wdt', 'scw', 'scb', 'dtb', 'aneg', 'dexp', 'e', 'e2']
    ssd_consts = [x_lm, cssd_lm] + [wts[k] for k in ssd_names]
    pre, c_lm, b_lm, ea_lm, ydg_lm, xw_lm = pl.pallas_call(
        functools.partial(_sample_ssd_kernel, **dims),
        grid=(steps,),
        in_specs=[step_blk(d_model)] + [_const_spec(c.shape) for c in ssd_consts],
        out_specs=(step_blk(d_xbc), step_blk(gn), step_blk(gn), step_blk(d_ssd), step_blk(d_ssd),
                   step_blk(d_ssd)),
        out_shape=(sds((ntok, d_xbc), F32), sds((ntok, gn), F32), sds((ntok, gn), F32),
                   sds((ntok, d_ssd), F32), sds((ntok, d_ssd), F32), sds((ntok, d_ssd), F32)),
        scratch_shapes=[
            pltpu.VMEM((cssd_lm.shape[0] + ntok, d_xbc), F32),
            pltpu.VMEM((ntok, d_ssd), F32),
            pltpu.VMEM((ntok, gn), F32),
            pltpu.VMEM((ntok, LANES), F32),
            pltpu.VMEM((ntok, LANES), F32),
        ],
        compiler_params=params,
        name="sample_ssd",
    )(x_lm, *ssd_consts)

    lru_names = ['wlx', 'wly', 'lcw', 'lcb', 'wa', 'ba', 'wx', 'bx', 'lam']
    lru_consts = [clru_lm, slru_lm] + [wts[k] for k in lru_names]
    prelx, ylru_lm, lst = pl.pallas_call(
        functools.partial(_sample_lru_kernel, nseq=nseq, steps=steps, start_pos=start_pos),
        grid=(steps,),
        in_specs=[step_blk(d_model)] + [_const_spec(c.shape) for c in lru_consts],
        out_specs=(step_blk(d_lru), step_blk(d_lru), pl.BlockSpec((nseq, d_lru), lambda l: (0, 0))),
        out_shape=(sds((ntok, d_lru), F32), sds((ntok, d_lru), BF16), sds((nseq, d_lru), F32)),
        scratch_shapes=[
            pltpu.VMEM((clru_lm.shape[0] + ntok, d_lru), F32),
            pltpu.VMEM((nseq, d_lru), F32),
        ],
        compiler_params=params,
        name="sample_lru",
    )(x_lm, *lru_consts)

    half = nseq // 2
    full = lambda a: _const_spec(a.shape)
    new_state, yoff_lm = pl.pallas_call(
        functools.partial(_sample_state_kernel, **dims),
        grid=(half,),
        in_specs=[pl.BlockSpec((None, 2, d_ssd, d_state), lambda j: (j, 0, 0, 0)),
                  full(c_lm), full(b_lm), full(xw_lm), full(ea_lm)],
        out_specs=(pl.BlockSpec((None, 2, d_ssd, d_state), lambda j: (j, 0, 0, 0)),
                   pl.BlockSpec((ntok, d_ssd), lambda j: (0, 0))),
        out_shape=(sds(state.shape, F32), sds((ntok, d_ssd), F32)),
        compiler_params=params,
        name="sample_state",
    )(state, c_lm, b_lm, xw_lm, ea_lm)

    post_names = ['wz', 'ng', 'wso', 'wlo', 'wgs', 'wgl', 'bg', 'wo', 'l1g', 'l1b']
    post_consts = [wts[k] for k in post_names]
    x1_lm = pl.pallas_call(
        functools.partial(_sample_post_kernel, alpha=wts['alpha']),
        grid=(steps,),
        in_specs=[step_blk(d_model), step_blk(d_ssd), step_blk(d_ssd), step_blk(d_lru)]
        + [_const_spec(c.shape) for c in post_consts],
        out_specs=step_blk(d_model),
        out_shape=sds((ntok, d_model), F32),
        compiler_params=params,
        name="sample_post",
    )(x_lm, ydg_lm, yoff_lm, ylru_lm, *post_consts)
    return x1_lm, new_state, pre, lst, prelx


def _prep_weights(w_in, b_gate, ssd_conv_w, ssd_conv_b, ssd_dt_bias, ssd_a_log, ssd_d, ssd_norm_g,
                  w_ssd_out, lru_conv_w, lru_conv_b, lru_wa, lru_ba, lru_wx, lru_bx, lru_lambda,
                  w_lru_out, w_o, ln1_g, ln1_b, ffn_w_gate, ffn_w_up, ffn_conv_w, ffn_conv_b,
                  ffn_w_down, ln2_g, ln2_b, n_heads, headdim, d_state):
    depth = w_in.shape[0]
    d_model = w_in.shape[1]
    d_ssd = n_heads * headdim
    d_xbc = d_ssd + 2 * SSD_GROUPS * d_state
    d_lru = lru_lambda.shape[1]
    sizes = (d_ssd, d_xbc, n_heads, d_lru, d_lru, d_model, d_model)
    cuts = np.cumsum((0,) + sizes)
    wi = w_in[0]
    parts = [wi[:, cuts[i]:cuts[i + 1]] for i in range(len(sizes))]
    row = lambda v: v.reshape(1, -1).astype(F32)
    pad_heads = lambda v: jnp.pad(v.reshape(1, -1).astype(F32), ((0, 0), (0, LANES - n_heads)))
    head_of_col = np.arange(d_ssd) // headdim
    expand = (np.arange(LANES)[:, None] == head_of_col[None, :]).astype(np.float32)
    return dict(
        n_heads=n_heads, headdim=headdim, d_state=d_state,
        alpha=float((2.0 * depth) ** 0.25),
        wz=parts[0].astype(BF16), wxbc=parts[1].astype(BF16),
        wdt=jnp.pad(parts[2], ((0, 0), (0, LANES - n_heads))).astype(BF16),
        wlx=parts[3].astype(BF16), wly=parts[4].astype(BF16),
        wgs=parts[5].astype(BF16), wgl=parts[6].astype(BF16),
        scw=ssd_conv_w[0].astype(F32), scb=row(ssd_conv_b[0]),
        dtb=pad_heads(ssd_dt_bias[0]), aneg=pad_heads(-jnp.exp(ssd_a_log[0].astype(F32))),
        dexp=row(jnp.repeat(ssd_d[0], headdim)), ng=row(ssd_norm_g[0]),
        wso=w_ssd_out[0].astype(BF16),
        lcw=lru_conv_w[0].astype(F32), lcb=row(lru_conv_b[0]),
        wa=lru_wa[0].astype(BF16), ba=row(lru_ba[0]), wx=lru_wx[0].astype(BF16), bx=row(lru_bx[0]),
        lam=row(lru_lambda[0]), wlo=w_lru_out[0].astype(BF16),
        bg=row(b_gate[0]), wo=w_o[0].astype(BF16), l1g=row(ln1_g[0]), l1b=row(ln1_b[0]),
        e=jnp.asarray(expand, BF16),
        e2=jnp.asarray(np.arange(SSD_GROUPS * d_state)[:, None] // d_state
                       == (head_of_col // (n_heads // SSD_GROUPS))[None, :], BF16),
        wg=ffn_w_gate[0].astype(BF16), wu=ffn_w_up[0].astype(BF16),
        fcw=ffn_conv_w[0].astype(F32), fcb=row(ffn_conv_b[0]), wd=ffn_w_down[0].astype(BF16),
        l2g=row(ln2_g[0]), l2b=row(ln2_b[0]),
    )


def kernel(x_prompt, x_sample, state_ssd, cache_ssd_conv, state_lru, cache_lru_conv, cache_ffn_conv, w_in, b_gate, ssd_conv_w, ssd_conv_b, ssd_dt_bias, ssd_a_log, ssd_d, ssd_norm_g, w_ssd_out, lru_conv_w, lru_conv_b, lru_wa, lru_ba, lru_wx, lru_bx, lru_lambda, w_lru_out, w_o, ln1_g, ln1_b, ffn_w_gate, ffn_w_up, ffn_conv_w, ffn_conv_b, ffn_w_down, ln2_g, ln2_b):
    assert w_in.shape[0] == 1, "single-layer trunk"
    _, _, n_heads, headdim, d_state = state_ssd.shape
    wts = _prep_weights(w_in, b_gate, ssd_conv_w, ssd_conv_b, ssd_dt_bias, ssd_a_log, ssd_d, ssd_norm_g,
                        w_ssd_out, lru_conv_w, lru_conv_b, lru_wa, lru_ba, lru_wx, lru_bx, lru_lambda,
                        w_lru_out, w_o, ln1_g, ln1_b, ffn_w_gate, ffn_w_up, ffn_conv_w, ffn_conv_b,
                        ffn_w_down, ln2_g, ln2_b, n_heads, headdim, d_state)
    bp = x_prompt.shape[0]
    d_ff = ffn_w_gate.shape[2]

    x1_p, p_ssd, p_ssd_buf, p_lru, p_lru_buf = _prompt_mixer(x_prompt, wts, tl=256)
    y_prompt, p_ffn_buf = _ffn(x1_p, jnp.zeros((bp, ffn_conv_w.shape[1] - 1, d_ff), F32), wts, tm=512, stride=1)
    p_ssd = p_ssd.reshape(1, bp, n_heads, headdim, d_state)

    nb_s, steps, _ = x_sample.shape
    half = nb_s // 2

    def to_lm(a):
        k, c = a.shape[1], a.shape[2]
        return a.reshape(half, 2, k, c).transpose(2, 1, 0, 3).reshape(k * nb_s, c)

    def from_lm(a, k):
        c = a.shape[1]
        return a.reshape(k, 2, half, c).transpose(2, 1, 0, 3).reshape(nb_s, k, c)

    d_ssd = n_heads * headdim
    x1_lm, new_state, pre, lst, prelx = _sample_mixer(
        to_lm(x_sample), to_lm(cache_ssd_conv[0]), to_lm(cache_lru_conv[0]), to_lm(state_lru[0][:, None, :]),
        state_ssd[0].reshape(half, 2, d_ssd, d_state), wts, nb_s, steps, PAST_LEN)
    y_lm, tail = _ffn(x1_lm[None], to_lm(cache_ffn_conv[0])[None], wts, tm=steps * nb_s, stride=nb_s)
    k_ssd = ssd_conv_w.shape[1] - 1
    k_lru = lru_conv_w.shape[1] - 1
    k_ffn = ffn_conv_w.shape[1] - 1
    assert steps >= max(k_ssd, k_lru, k_ffn)
    return (y_prompt, from_lm(y_lm[0], steps), p_ssd, p_ssd_buf[None], p_lru.reshape(1, bp, -1), p_lru_buf[None],
            p_ffn_buf[None],
            new_state.reshape(1, nb_s, n_heads, headdim, d_state),
            from_lm(pre[(steps - k_ssd) * nb_s:], k_ssd)[None],
            from_lm(lst, 1).reshape(1, nb_s, -1),
            from_lm(prelx[(steps - k_lru) * nb_s:], k_lru)[None],
            from_lm(tail[0], k_ffn)[None])
```

```python
import functools

import numpy as np
import jax
import jax.numpy as jnp
from jax import lax
from jax.experimental import pallas as pl
from jax.experimental.pallas import tpu as pltpu

F32 = jnp.float32
BF16 = jnp.bfloat16

SSD_GROUPS = 4
SSD_CHUNK = 128
LRU_BLOCKS = 8
LRU_C = 8.0
LN_EPS = 1e-5
RMS_EPS = 1e-6
PAST_LEN = 16384

LANES = 128
SUBLANES = 8
VMEM_LIMIT_BYTES = 60 * 1024 * 1024

HIST = 8
ROW_BLK = 32
UNROLL = True


def _dot(a, b):
    return jnp.dot(a, b, preferred_element_type=F32)


def _dot_nt(a, b):
    return lax.dot_general(a, b, (((1,), (1,)), ((), ())), preferred_element_type=F32)


def _dot_tn(a, b):
    return lax.dot_general(a, b, (((0,), (0,)), ((), ())), preferred_element_type=F32)


def _split3(v):
    hi = v.astype(BF16)
    r1 = v - hi.astype(F32)
    mid = r1.astype(BF16)
    lo = (r1 - mid.astype(F32)).astype(BF16)
    return hi, mid, lo


def _dot_exact_rhs(v, m):
    hi, mid, lo = _split3(v)
    return _dot(hi, m) + _dot(mid, m) + _dot(lo, m)


def _dot_exact_lhs(m, v):
    hi, mid, lo = _split3(v)
    return _dot(m, hi) + _dot(m, mid) + _dot(m, lo)


def _split2(v):
    hi = v.astype(BF16)
    lo = (v - hi.astype(F32)).astype(BF16)
    return hi, lo


def _dot_2way_rhs(v, m2):
    return _dot(jnp.concatenate(_split2(v), axis=1), m2)


def _dot_2way_lhs(m2, v):
    return _dot(m2, jnp.concatenate(_split2(v), axis=0))


def _softplus(x):
    return jnp.maximum(x, 0.0) + jnp.log1p(jnp.exp(-jnp.abs(x)))


def _sigmoid(x):
    return 0.5 * jnp.tanh(0.5 * x) + 0.5


def _silu(x):
    h = 0.5 * x
    return h + h * jnp.tanh(h)


def _gelu(x):
    c = np.sqrt(2.0 / np.pi).astype(np.float32)
    return 0.5 * x * (1.0 + jnp.tanh(c * (x + 0.044715 * (x * x * x))))


def _layer_norm(v, g, b):
    mu = jnp.mean(v, axis=-1, keepdims=True)
    d = v - mu
    var = jnp.mean(d * d, axis=-1, keepdims=True)
    return d * lax.rsqrt(var + LN_EPS) * g + b


def _rows(i, n):
    if isinstance(i, int):
        return slice(i * n, (i + 1) * n)
    return pl.ds(pl.multiple_of(i * n, n), n)


def _loop(n, body, unroll):
    if unroll:
        for i in range(n):
            body(i, None)
    else:
        lax.fori_loop(0, n, lambda i, c: (body(i, c), c)[1], 0)


def _conv_block(buf_ref, w_ref, b_ref, hist, stride, r0, rows, c0, cw):
    taps = w_ref.shape[0]
    acc = b_ref[:, c0:c0 + cw]
    for k in range(taps):
        off = hist + r0 - (taps - 1 - k) * stride
        if not isinstance(off, int):
            off = pl.multiple_of(off, SUBLANES)
        acc = acc + w_ref[k:k + 1, c0:c0 + cw] * buf_ref[pl.ds(off, rows), c0:c0 + cw]
    return acc


LOG2_SUBLANES = 3
NPOS = SSD_CHUNK // SUBLANES
LOG2_NPOS = 4
assert 1 << LOG2_SUBLANES == SUBLANES and 1 << LOG2_NPOS == NPOS


def _tok_of_row(r):
    return (r & (SUBLANES - 1)) * NPOS + lax.shift_right_logical(r, LOG2_SUBLANES)


def _row_of_tok(t):
    return (t & (NPOS - 1)) * SUBLANES + lax.shift_right_logical(t, LOG2_NPOS)


def _perm_matrix():
    q = SSD_CHUNK
    r = lax.broadcasted_iota(jnp.int32, (q, q), 0)
    c = lax.broadcasted_iota(jnp.int32, (q, q), 1)
    return jnp.where(c == _tok_of_row(r), 1.0, 0.0).astype(BF16)


def _unperm_matrix():
    q = SSD_CHUNK
    t = lax.broadcasted_iota(jnp.int32, (q, q), 0)
    r = lax.broadcasted_iota(jnp.int32, (q, q), 1)
    return jnp.where(r == _row_of_tok(t), 1.0, 0.0).astype(BF16)


def _fill_wrap(buf_ref, wrap_ref, c, wrap, tail_ref):
    q = SSD_CHUNK
    ncol = buf_ref.shape[1]
    sub0 = lax.broadcasted_iota(jnp.int32, (SUBLANES, ncol), 0) == 0
    for m in range(wrap // SUBLANES):
        r_cur = (c + 1) * q - wrap + m * SUBLANES
        cur = buf_ref[r_cur:r_cur + SUBLANES, :]
        if c == 0:
            prv = tail_ref[m * SUBLANES:(m + 1) * SUBLANES, :]
        else:
            prv = buf_ref[r_cur - q:r_cur - q + SUBLANES, :]
        wrap_ref[c * wrap + m * SUBLANES:c * wrap + (m + 1) * SUBLANES, :] = jnp.where(
            sub0, pltpu.roll(prv, 1, 0), pltpu.roll(cur, 1, 0))


def _conv_seg(buf_ref, wrap_ref, w_ref, b_ref, c, wrap, c0, cw):
    q = SSD_CHUNK
    taps = w_ref.shape[0]
    cols = slice(c0, c0 + cw)
    acc = b_ref[:, cols] + w_ref[taps - 1:taps, cols] * buf_ref[c * q:(c + 1) * q, cols]
    for k in range(taps - 1):
        back = (taps - 1 - k) * SUBLANES
        shifted = jnp.concatenate(
            [wrap_ref[(c + 1) * wrap - back:(c + 1) * wrap, cols], buf_ref[c * q:(c + 1) * q - back, cols]],
            axis=0)
        acc = acc + w_ref[k:k + 1, cols] * shifted
    return acc


def _seg_tail_rows(wrap):
    n = wrap // SUBLANES
    return [(NPOS - n + m) * SUBLANES + SUBLANES - 1 for m in range(n)]


def _prompt_mixer_kernel(
        x_ref, wz_ref, wxbc_ref, wdt_ref, wlx_ref, wly_ref, wgs_ref, wgl_ref,
        scw_ref, scb_ref, dtb_ref, aneg_ref, dexp_ref, ng_ref, wso_ref,
        lcw_ref, lcb_ref, wax_ref, ba_ref, bx_ref, lam_ref, wlo_ref,
        bg_ref, wo_ref, l1g_ref, l1b_ref, e_ref,
        x1_ref, st_ref, sconv_ref, lst_ref, lconv_ref,
        xb_s, xp_s, xbc_s, swrap_s, stail_s, xc_s, lx_s, lwrap_s, ltail_s, xr_s, ly_s, y_s, z_s, ysb_s, ylb_s,
        ht_s, hl_s,
        *, tl, alpha, n_heads, headdim, d_state):
    t = pl.program_id(1)
    nt = pl.num_programs(1)
    d_ssd = n_heads * headdim
    gn = SSD_GROUPS * d_state
    hpg = n_heads // SSD_GROUPS
    gw = hpg * headdim
    d_lru = lx_s.shape[1]
    d_model = x_ref.shape[2]
    q = SSD_CHUNK

    nch = tl // q
    wrap_s = (scw_ref.shape[0] - 1) * SUBLANES
    wrap_l = (lcw_ref.shape[0] - 1) * SUBLANES

    @pl.when(t == 0)
    def _():
        stail_s[...] = jnp.zeros(stail_s.shape, F32)
        ltail_s[...] = jnp.zeros(ltail_s.shape, F32)
        ht_s[...] = jnp.zeros(ht_s.shape, F32)
        hl_s[...] = jnp.zeros(hl_s.shape, F32)

    perm = _perm_matrix()
    perm2 = jnp.concatenate([perm, perm], axis=1)
    for c in range(nch):
        rows = _rows(c, q)
        xp = _dot_2way_lhs(perm2, x_ref[0, rows, :])
        xp_s[rows, :] = xp
        xb_s[rows, :] = xp.astype(BF16)

    xbc_s[...] = _dot(xb_s[...], wxbc_ref[...])
    cwid = 512
    for c in range(nch):
        _fill_wrap(xbc_s, swrap_s, c, wrap_s, stail_s)
        for c0 in range(0, xc_s.shape[1], cwid):
            xc_s[c * q:(c + 1) * q, c0:c0 + cwid] = _silu(
                _conv_seg(xbc_s, swrap_s, scw_ref, scb_ref, c, wrap_s, c0, cwid))
    stail_s[...] = xbc_s[tl - wrap_s:tl, :]

    tok_r = _tok_of_row(lax.broadcasted_iota(jnp.int32, (q, q), 0))
    tok_c = _tok_of_row(lax.broadcasted_iota(jnp.int32, (q, q), 1))
    causal = tok_r >= tok_c
    tri = jnp.where(causal, 1.0, 0.0).astype(BF16)
    lane_i = lax.broadcasted_iota(jnp.int32, (q, LANES), 1)
    left = lane_i < headdim

    def chunk_body(c, carry):
        rows = _rows(c, q)
        dt = _softplus(_dot(xb_s[rows, :], wdt_ref[...]) + dtb_ref[...])
        d_a = dt * aneg_ref[...]
        a_cs = _dot_exact_lhs(tri, d_a)
        a_last = a_cs[q - 1:q, :]
        wgt = dt * jnp.exp(a_last - a_cs)
        ea = jnp.exp(a_cs)
        w_exp = _dot_2way_rhs(wgt, e_ref[...])
        ea_exp = _dot_2way_rhs(ea, e_ref[...])
        a_cs_t = a_cs.T
        dt_t = dt.T
        for g in range(SSD_GROUPS):
            b_g = xc_s[rows, d_ssd + g * d_state:d_ssd + (g + 1) * d_state]
            c_g = xc_s[rows, d_ssd + gn + g * d_state:d_ssd + gn + (g + 1) * d_state]
            b_gb = b_g.astype(BF16)
            c_gb = c_g.astype(BF16)
            cb = _dot_nt(c_gb, b_gb)
            for hp in range(hpg // 2):
                c0 = g * gw + hp * 2 * headdim
                xs_pair = xc_s[rows, c0:c0 + 2 * headdim]
                lmats = []
                for j in range(2):
                    h = g * hpg + hp * 2 + j
                    seg = (jnp.broadcast_to(a_cs[:, h:h + 1], (q, q))
                           - jnp.broadcast_to(a_cs_t[h:h + 1, :], (q, q)))
                    dec = jnp.exp(jnp.where(causal, seg, -jnp.inf))
                    lmats.append((cb * dec * jnp.broadcast_to(dt_t[h:h + 1, :], (q, q))).astype(BF16))
                lpair = jnp.concatenate(lmats, axis=1)
                rhs = jnp.concatenate([jnp.where(left, xs_pair, 0.0),
                                       jnp.where(left, 0.0, xs_pair)], axis=0).astype(BF16)
                y_s[rows, c0:c0 + 2 * headdim] = _dot(lpair, rhs)
            gcols = slice(g * gw, (g + 1) * gw)
            h_g = ht_s[:, gcols]
            y_off = _dot(c_gb, h_g.astype(BF16)) * ea_exp[:, gcols]
            y_s[rows, gcols] = y_s[rows, gcols] + y_off
            xw = (xc_s[rows, gcols] * w_exp[:, gcols]).astype(BF16)
            ht_s[:, gcols] = h_g * ea_exp[q - 1:q, gcols] + _dot_tn(b_gb, xw)
        return carry

    _loop(tl // q, chunk_body, UNROLL)

    z_s[...] = _dot(xb_s[...], wz_ref[...])

    def gate_body(i, carry):
        rows = _rows(i, ROW_BLK)
        y = y_s[rows, :] + dexp_ref[...] * xc_s[rows, 0:d_ssd]
        y = y * _silu(z_s[rows, :])
        ms = jnp.mean(y * y, axis=-1, keepdims=True)
        ysb_s[rows, :] = (y * lax.rsqrt(ms + RMS_EPS) * ng_ref[...]).astype(BF16)
        return carry

    _loop(tl // ROW_BLK, gate_body, UNROLL)

    lx_s[...] = _dot(xb_s[...], wlx_ref[...])
    for c in range(nch):
        _fill_wrap(lx_s, lwrap_s, c, wrap_l, ltail_s)
        for c0 in range(0, d_lru, cwid):
            xr_s[c * q:(c + 1) * q, c0:c0 + cwid] = _conv_seg(lx_s, lwrap_s, lcw_ref, lcb_ref, c, wrap_l, c0, cwid)
    ltail_s[...] = lx_s[tl - wrap_l:tl, :]
    ly_s[...] = _dot(xb_s[...], wly_ref[...])

    bw = d_lru // LRU_BLOCKS
    sub = lax.broadcasted_iota(jnp.int32, (SUBLANES, bw), 0)
    crow = lax.broadcasted_iota(jnp.int32, (q, bw), 0)
    for n in range(LRU_BLOCKS):
        cols = slice(n * bw, (n + 1) * bw)
        xr = xr_s[:, cols]
        xrb = xr.astype(BF16)
        rg = _dot(xrb, wax_ref[n])
        r = _sigmoid(rg[:, 0:bw] + ba_ref[:, cols])
        gi = _sigmoid(rg[:, bw:2 * bw] + bx_ref[:, cols])
        log_a = (-LRU_C) * r * _softplus(-lam_ref[:, cols])
        a_all = jnp.exp(log_a)
        mult_all = jnp.sqrt(1.0 - jnp.exp(2.0 * log_a))
        for c in range(nch):
            a = a_all[c * q:(c + 1) * q, :]
            mult = mult_all[c * q:(c + 1) * q, :]
            if c == 0:
                mult = jnp.where(jnp.logical_and(crow == 0, t == 0), 1.0, mult)
            u = mult * gi[c * q:(c + 1) * q, :] * xr[c * q:(c + 1) * q, :]
            a_p = [a[i * SUBLANES:(i + 1) * SUBLANES, :] for i in range(NPOS)]
            u_p = [u[i * SUBLANES:(i + 1) * SUBLANES, :] for i in range(NPOS)]
            h = u_p[0]
            g = a_p[0]
            for i in range(1, NPOS):
                h = a_p[i] * h + u_p[i]
                g = a_p[i] * g
            gs = jnp.where(sub == 0, 0.0, pltpu.roll(g, 1, 0))
            hs = jnp.where(sub == 0, hl_s[0:1, cols], pltpu.roll(h, 1, 0))
            d = 1
            while d < SUBLANES:
                keep = sub >= d
                hs = jnp.where(keep, gs * pltpu.roll(hs, d, 0) + hs, hs)
                gs = jnp.where(keep, gs * pltpu.roll(gs, d, 0), gs)
                d *= 2
            h = hs
            out = []
            for i in range(NPOS):
                h = a_p[i] * h + u_p[i]
                out.append(h)
            hl_s[0:1, cols] = h[SUBLANES - 1:SUBLANES, :]
            hseq = jnp.concatenate(out, axis=0)
            ylb_s[c * q:(c + 1) * q, cols] = (hseq * _gelu(ly_s[c * q:(c + 1) * q, cols])).astype(BF16)

    half = d_model
    z_s[:, 0:half] = _dot(ysb_s[...], wso_ref[...])
    z_s[:, half:2 * half] = _dot(ylb_s[...], wlo_ref[...])
    y_s[:, 0:half] = _dot(xb_s[...], wgs_ref[...])
    y_s[:, half:2 * half] = _dot(xb_s[...], wgl_ref[...])

    def merge_body(i, carry):
        rows = _rows(i, ROW_BLK)
        g_ssd = _sigmoid(y_s[rows, 0:half] + bg_ref[:, 0:half])
        g_lru = _sigmoid(y_s[rows, half:2 * half] + bg_ref[:, half:2 * half])
        ylb_s[rows, :] = (g_ssd * z_s[rows, 0:half] + g_lru * z_s[rows, half:2 * half]).astype(BF16)
        return carry

    _loop(tl // ROW_BLK, merge_body, UNROLL)
    z_s[:, 0:half] = _dot(ylb_s[...], wo_ref[...])

    def ln_body(i, carry):
        rows = _rows(i, ROW_BLK)
        v = alpha * xp_s[rows, :] + z_s[rows, 0:half]
        x1_ref[0, rows, :] = _layer_norm(v, l1g_ref[...], l1b_ref[...])
        return carry

    _loop(tl // ROW_BLK, ln_body, UNROLL)

    @pl.when(t == nt - 1)
    def _():
        st_ref[0] = ht_s[...].T
        lst_ref[0] = hl_s[0:1, :]
        for m, r in enumerate(_seg_tail_rows(wrap_s)):
            sconv_ref[0, m:m + 1, :] = xbc_s[tl - q + r:tl - q + r + 1, :]
        for m, r in enumerate(_seg_tail_rows(wrap_l)):
            lconv_ref[0, m:m + 1, :] = lx_s[tl - q + r:tl - q + r + 1, :]


def _const_spec(shape):
    nd = len(shape)
    return pl.BlockSpec(shape, lambda *_: (0,) * nd, pipeline_mode=pl.Buffered(1))


def _prompt_mixer(x, wts, tl):
    nb, seq, d_model = x.shape
    n_heads, headdim, d_state = wts['n_heads'], wts['headdim'], wts['d_state']
    d_ssd = n_heads * headdim
    d_xbc = d_ssd + 2 * SSD_GROUPS * d_state
    d_lru = wts['wlx'].shape[1]
    names = ['wz', 'wxbc', 'wdt', 'wlx', 'wly', 'wgs', 'wgl', 'scw', 'scb', 'dtb', 'aneg', 'dexp', 'ng',
             'wso', 'lcw', 'lcb', 'wax', 'ba', 'bx', 'lam', 'wlo', 'bg', 'wo', 'l1g', 'l1b', 'ee']
    consts = [wts[k] for k in names]
    kern = functools.partial(_prompt_mixer_kernel, tl=tl, alpha=wts['alpha'], n_heads=n_heads,
                             headdim=headdim, d_state=d_state)
    out_shape = (
        jax.ShapeDtypeStruct((nb, seq, d_model), F32),
        jax.ShapeDtypeStruct((nb, d_ssd, d_state), F32),
        jax.ShapeDtypeStruct((nb, 3, d_xbc), F32),
        jax.ShapeDtypeStruct((nb, 1, d_lru), F32),
        jax.ShapeDtypeStruct((nb, 3, d_lru), F32),
    )
    out_specs = (
        pl.BlockSpec((1, tl, d_model), lambda b, t: (b, t, 0)),
        pl.BlockSpec((1, d_ssd, d_state), lambda b, t: (b, 0, 0)),
        pl.BlockSpec((1, 3, d_xbc), lambda b, t: (b, 0, 0)),
        pl.BlockSpec((1, 1, d_lru), lambda b, t: (b, 0, 0)),
        pl.BlockSpec((1, 3, d_lru), lambda b, t: (b, 0, 0)),
    )
    nch = tl // SSD_CHUNK
    wrap_s = (wts['scw'].shape[0] - 1) * SUBLANES
    wrap_l = (wts['lcw'].shape[0] - 1) * SUBLANES
    scratch = [
        pltpu.VMEM((tl, d_model), BF16),
        pltpu.VMEM((tl, d_model), F32),
        pltpu.VMEM((tl, d_xbc), F32),
        pltpu.VMEM((nch * wrap_s, d_xbc), F32),
        pltpu.VMEM((wrap_s, d_xbc), F32),
        pltpu.VMEM((tl, d_xbc), F32),
        pltpu.VMEM((tl, d_lru), F32),
        pltpu.VMEM((nch * wrap_l, d_lru), F32),
        pltpu.VMEM((wrap_l, d_lru), F32),
        pltpu.VMEM((tl, d_lru), F32),
        pltpu.VMEM((tl, d_lru), F32),
        pltpu.VMEM((tl, d_ssd), F32),
        pltpu.VMEM((tl, d_ssd), F32),
        pltpu.VMEM((tl, d_ssd), BF16),
        pltpu.VMEM((tl, d_lru), BF16),
        pltpu.VMEM((d_state, d_ssd), F32),
        pltpu.VMEM((SUBLANES, d_lru), F32),
    ]
    return pl.pallas_call(
        kern,
        grid=(nb, seq // tl),
        in_specs=[pl.BlockSpec((1, tl, d_model), lambda b, t: (b, t, 0))]
        + [_const_spec(c.shape) for c in consts],
        out_specs=out_specs,
        out_shape=out_shape,
        scratch_shapes=scratch,
        compiler_params=pltpu.CompilerParams(
            dimension_semantics=("arbitrary", "arbitrary"),
            vmem_limit_bytes=VMEM_LIMIT_BYTES),
        name="prompt_mixer",
    )(x, *consts)


def _ffn_kernel(x_ref, h0_ref, wg_ref, wu_ref, cw_ref, cb_ref, wd_ref, g_ref, b_ref,
                y_ref, tail_ref, xb_s, gb_s, hb_s, *, tm, stride, hist, alpha, fchunk):
    t = pl.program_id(1)
    nt = pl.num_programs(1)
    taps = cw_ref.shape[0]
    nh = (taps - 1) * stride
    d_ff = gb_s.shape[1]

    @pl.when(t == 0)
    def _():
        gb_s[hist - nh:hist, :] = h0_ref[0]

    xb_s[...] = x_ref[0].astype(BF16)
    acc = None
    for c0 in range(0, d_ff, fchunk):
        cols = slice(c0, c0 + fchunk)
        gb_s[hist:hist + tm, cols] = _dot(xb_s[...], wg_ref[:, cols])
        up = _dot(xb_s[...], wu_ref[:, cols])
        gc = _conv_block(gb_s, cw_ref, cb_ref, hist, stride, 0, tm, c0, fchunk)
        hb_s[...] = (_gelu(gc) * up).astype(BF16)
        part = _dot(hb_s[...], wd_ref[cols, :])
        acc = part if acc is None else acc + part
    v = alpha * x_ref[0] + acc
    y_ref[0] = _layer_norm(v, g_ref[...], b_ref[...])
    gb_s[hist - nh:hist, :] = gb_s[hist + tm - nh:hist + tm, :]

    @pl.when(t == nt - 1)
    def _():
        tail_ref[0] = gb_s[hist - nh:hist, :]


def _ffn_seg_kernel(x_ref, h0_ref, wg_ref, wu_ref, cw_ref, cb_ref, wd_ref, g_ref, b_ref,
                    y_ref, tail_ref, xb_s, gb_s, gwrap_s, gtail_s, hb_s, *, tm, alpha, fchunk):
    t = pl.program_id(1)
    nt = pl.num_programs(1)
    q = SSD_CHUNK
    nch = tm // q
    wrap = (cw_ref.shape[0] - 1) * SUBLANES
    d_ff = gb_s.shape[1]
    tail_rows = _seg_tail_rows(wrap)

    @pl.when(t == 0)
    def _():
        gtail_s[...] = jnp.zeros(gtail_s.shape, F32)
        for m in range(len(tail_rows)):
            r = m * SUBLANES + SUBLANES - 1
            gtail_s[r:r + 1, :] = h0_ref[0, m:m + 1, :]

    xb_s[...] = x_ref[0].astype(BF16)
    gb_s[...] = _dot(xb_s[...], wg_ref[...])
    for c in range(nch):
        _fill_wrap(gb_s, gwrap_s, c, wrap, gtail_s)
    acc = None
    for c0 in range(0, d_ff, fchunk):
        cols = slice(c0, c0 + fchunk)
        up = _dot(xb_s[...], wu_ref[:, cols])
        for c in range(nch):
            gc = _conv_seg(gb_s, gwrap_s, cw_ref, cb_ref, c, wrap, c0, fchunk)
            hb_s[c * q:(c + 1) * q, :] = (_gelu(gc) * up[c * q:(c + 1) * q, :]).astype(BF16)
        part = _dot(hb_s[...], wd_ref[cols, :])
        acc = part if acc is None else acc + part
    gtail_s[...] = gb_s[tm - wrap:tm, :]

    unperm = _unperm_matrix()
    unperm2 = jnp.concatenate([unperm, unperm], axis=1)
    for c in range(nch):
        rows = slice(c * q, (c + 1) * q)
        y = _layer_norm(alpha * x_ref[0, rows, :] + acc[rows, :], g_ref[...], b_ref[...])
        y_ref[0, rows, :] = _dot_2way_lhs(unperm2, y)

    @pl.when(t == nt - 1)
    def _():
        for m, r in enumerate(tail_rows):
            tail_ref[0, m:m + 1, :] = gb_s[tm - q + r:tm - q + r + 1, :]


def _ffn_seg(x, hist0, wts, tm):
    nb, seq, d_model = x.shape
    d_ff = wts['wg'].shape[1]
    taps = wts['fcw'].shape[0]
    wrap = (taps - 1) * SUBLANES
    fchunk = 1024
    consts = [wts[k] for k in ['wg', 'wu', 'fcw', 'fcb', 'wd', 'l2g', 'l2b']]
    kern = functools.partial(_ffn_seg_kernel, tm=tm, alpha=wts['alpha'], fchunk=fchunk)
    return pl.pallas_call(
        kern,
        grid=(nb, seq // tm),
        in_specs=[pl.BlockSpec((1, tm, d_model), lambda b, t: (b, t, 0)),
                  pl.BlockSpec((1, taps - 1, d_ff), lambda b, t: (b, 0, 0))]
        + [_const_spec(c.shape) for c in consts],
        out_specs=(pl.BlockSpec((1, tm, d_model), lambda b, t: (b, t, 0)),
                   pl.BlockSpec((1, taps - 1, d_ff), lambda b, t: (b, 0, 0))),
        out_shape=(jax.ShapeDtypeStruct((nb, seq, d_model), F32),
                   jax.ShapeDtypeStruct((nb, taps - 1, d_ff), F32)),
        scratch_shapes=[pltpu.VMEM((tm, d_model), BF16),
                        pltpu.VMEM((tm, d_ff), F32),
                        pltpu.VMEM((tm // SSD_CHUNK * wrap, d_ff), F32),
                        pltpu.VMEM((wrap, d_ff), F32),
                        pltpu.VMEM((tm, fchunk), BF16)],
        compiler_params=pltpu.CompilerParams(
            dimension_semantics=("arbitrary", "arbitrary"),
            vmem_limit_bytes=VMEM_LIMIT_BYTES),
        name="conv_ffn_seg",
    )(x, hist0, *consts)


def _ffn(x, hist0, wts, tm, stride):
    nb, seq, d_model = x.shape
    d_ff = wts['wg'].shape[1]
    taps = wts['fcw'].shape[0]
    nh = (taps - 1) * stride
    hist = -(-nh // SUBLANES) * SUBLANES
    consts = [wts[k] for k in ['wg', 'wu', 'fcw', 'fcb', 'wd', 'l2g', 'l2b']]
    kern = functools.partial(_ffn_kernel, tm=tm, stride=stride, hist=hist, alpha=wts['alpha'], fchunk=1024)
    return pl.pallas_call(
        kern,
        grid=(nb, seq // tm),
        in_specs=[pl.BlockSpec((1, tm, d_model), lambda b, t: (b, t, 0)),
                  pl.BlockSpec((1, nh, d_ff), lambda b, t: (b, 0, 0))]
        + [_const_spec(c.shape) for c in consts],
        out_specs=(pl.BlockSpec((1, tm, d_model), lambda b, t: (b, t, 0)),
                   pl.BlockSpec((1, nh, d_ff), lambda b, t: (b, 0, 0))),
        out_shape=(jax.ShapeDtypeStruct((nb, seq, d_model), F32),
                   jax.ShapeDtypeStruct((nb, nh, d_ff), F32)),
        scratch_shapes=[pltpu.VMEM((tm, d_model), BF16),
                        pltpu.VMEM((hist + tm, d_ff), F32),
                        pltpu.VMEM((tm, 1024), BF16)],
        compiler_params=pltpu.CompilerParams(
            dimension_semantics=("arbitrary", "arbitrary"),
            vmem_limit_bytes=VMEM_LIMIT_BYTES),
        name="conv_ffn",
    )(x, hist0, *consts)


def _sample_ssd_kernel(
        x_ref, xall_ref, cssd_ref, wxbc_ref, wdt_ref, scw_ref, scb_ref, dtb_ref, aneg_ref, dexp_ref,
        e_ref, e2_ref,
        pre_ref, c_ref, b_ref, ea_ref, ydg_ref, xw_ref,
        xbc_s, xs_s, bs_s, acs_s, dts_s,
        *, nseq, steps, n_heads, headdim, d_state):
    l = pl.program_id(0)
    d_ssd = n_heads * headdim
    gn = SSD_GROUPS * d_state
    hist = (scw_ref.shape[0] - 1) * nseq
    r0 = l * nseq

    def blk(i):
        return pl.ds(pl.multiple_of(i * nseq, nseq), nseq)

    def sblk(i):
        return slice(i * nseq, (i + 1) * nseq)

    @pl.when(l == 0)
    def _():
        xbc_s[0:hist, :] = cssd_ref[...]
        dts_s[...] = _softplus(_dot(xall_ref[...].astype(BF16), wdt_ref[...]) + dtb_ref[...])
        acc = jnp.zeros((nseq, LANES), F32)
        for s in range(steps):
            acc = acc + dts_s[sblk(s), :] * aneg_ref[...]
            acs_s[sblk(s), :] = acc

    xb = x_ref[...].astype(BF16)

    pre = _dot(xb, wxbc_ref[...])
    pre_ref[...] = pre
    xbc_s[pl.ds(pl.multiple_of(hist + r0, nseq), nseq), :] = pre
    cwid = 512
    for c0 in range(0, d_ssd, cwid):
        xs_s[blk(l), c0:c0 + cwid] = _silu(
            _conv_block(xbc_s, scw_ref, scb_ref, hist, nseq, r0, nseq, c0, cwid))
    b_l = _silu(_conv_block(xbc_s, scw_ref, scb_ref, hist, nseq, r0, nseq, d_ssd, gn))
    c_l = _silu(_conv_block(xbc_s, scw_ref, scb_ref, hist, nseq, r0, nseq, d_ssd + gn, gn))
    bs_s[blk(l), :] = b_l
    b_ref[...] = b_l
    c_ref[...] = c_l

    a_cs = acs_s[blk(l), :]
    dt = dts_s[blk(l), :]
    a_end = acs_s[sblk(steps - 1), :]
    ea_ref[...] = _dot_exact_rhs(jnp.exp(a_cs), e_ref[...])
    xw_ref[...] = xs_s[blk(l), :] * _dot_exact_rhs(dt * jnp.exp(a_end - a_cs), e_ref[...])

    ydg_ref[...] = dexp_ref[...] * xs_s[blk(l), :]
    for s in range(steps):
        @pl.when(s <= l)
        def _(s=s):
            coef = _dot_exact_rhs(jnp.exp(a_cs - acs_s[sblk(s), :]) * dts_s[sblk(s), :], e_ref[...])
            cbx = _dot_exact_rhs(bs_s[sblk(s), :] * c_l, e2_ref[...])
            ydg_ref[...] += cbx * coef * xs_s[sblk(s), :]


def _sample_lru_kernel(
        x_ref, clru_ref, slru_ref, wlx_ref, wly_ref, lcw_ref, lcb_ref, wa_ref, ba_ref, wx_ref, bx_ref,
        lam_ref, prelx_ref, ylru_ref, lst_ref, lx_s, hl_s, *, nseq, steps, start_pos):
    l = pl.program_id(0)
    hist = (lcw_ref.shape[0] - 1) * nseq
    d_lru = lx_s.shape[1]
    r0 = l * nseq

    @pl.when(l == 0)
    def _():
        lx_s[0:hist, :] = clru_ref[...]
        hl_s[...] = slru_ref[...]

    xb = x_ref[...].astype(BF16)
    prelx = _dot(xb, wlx_ref[...])
    prelx_ref[...] = prelx
    lx_s[pl.ds(pl.multiple_of(hist + r0, nseq), nseq), :] = prelx
    ly = _dot(xb, wly_ref[...])
    bw = d_lru // LRU_BLOCKS
    first = (l + start_pos) == 0
    for n in range(LRU_BLOCKS):
        cols = slice(n * bw, (n + 1) * bw)
        xr = _conv_block(lx_s, lcw_ref, lcb_ref, hist, nseq, r0, nseq, n * bw, bw)
        xrb = xr.astype(BF16)
        r = _sigmoid(_dot(xrb, wa_ref[n]) + ba_ref[:, cols])
        gi = _sigmoid(_dot(xrb, wx_ref[n]) + bx_ref[:, cols])
        log_a = (-LRU_C) * r * _softplus(-lam_ref[:, cols])
        a = jnp.exp(log_a)
        mult = jnp.where(first, 1.0, jnp.sqrt(1.0 - jnp.exp(2.0 * log_a)))
        h = a * hl_s[:, cols] + mult * gi * xr
        hl_s[:, cols] = h
        ylru_ref[:, cols] = (h * _gelu(ly[:, cols])).astype(BF16)

    @pl.when(l == steps - 1)
    def _():
        lst_ref[...] = hl_s[...]


def _sample_state_kernel(st_ref, c_ref, b_ref, xw_ref, ea_ref, nst_ref, yoff_ref,
                         *, nseq, steps, n_heads, headdim, d_state):
    j = pl.program_id(0)
    half = nseq // 2
    nrow = 2 * steps
    hpg = n_heads // SSD_GROUPS
    gw = hpg * headdim

    def gather(ref, cols):
        return jnp.concatenate([ref[pl.ds(j + k * half, 1), cols] for k in range(nrow)], axis=0)

    c8 = gather(c_ref, slice(None))
    b8 = gather(b_ref, slice(None))
    par = lax.broadcasted_iota(jnp.int32, (nrow, gw), 0) % 2
    assert 2 * headdim == LANES and d_state == LANES
    low = lax.broadcasted_iota(jnp.int32, (nrow, LANES), 1) < headdim
    for g in range(SSD_GROUPS):
        gcols = slice(g * gw, (g + 1) * gw)
        c8g = c8[:, g * d_state:(g + 1) * d_state].astype(BF16)
        b8g = b8[:, g * d_state:(g + 1) * d_state].astype(BF16)
        xw8 = gather(xw_ref, gcols)
        ea8 = gather(ea_ref, gcols)
        cds = []
        for hp in range(hpg // 2):
            pair = ea8[:, hp * LANES:(hp + 1) * LANES]
            swapped = pltpu.roll(pair, headdim, 1)
            cds.append(jnp.where(low, pair, swapped))
            cds.append(jnp.where(low, swapped, pair))
        yo = None
        for e in range(2):
            sg = st_ref[e, gcols, :]
            yo_e = _dot_nt(c8g, sg.astype(BF16))
            yo = yo_e if e == 0 else jnp.where(par == e, yo_e, yo)
            xw_e = jnp.where(par == e, xw8, 0.0).astype(BF16)
            upd = _dot_tn(xw_e, b8g)
            k_last = 2 * (steps - 1) + e
            for hh in range(hpg):
                cd = cds[hh][k_last:k_last + 1, :]
                hr = slice(hh * headdim, (hh + 1) * headdim)
                nst_ref[e, g * gw + hh * headdim:g * gw + (hh + 1) * headdim, :] = sg[hr, :] * cd + upd[hr, :]
        yo = yo * ea8
        for k in range(nrow):
            yoff_ref[pl.ds(j + k * half, 1), gcols] = yo[k:k + 1, :]


def _sample_post_kernel(x_ref, ydg_ref, yoff_ref, ylru_ref, wz_ref, ng_ref, wso_ref, wlo_ref,
                        wgs_ref, wgl_ref, bg_ref, wo_ref, l1g_ref, l1b_ref, x1_ref, *, alpha):
    d_model = x_ref.shape[1]
    xb = x_ref[...].astype(BF16)
    y = (ydg_ref[...] + yoff_ref[...]) * _silu(_dot(xb, wz_ref[...]))
    ms = jnp.mean(y * y, axis=-1, keepdims=True)
    ysb = (y * lax.rsqrt(ms + RMS_EPS) * ng_ref[...]).astype(BF16)
    g_ssd = _sigmoid(_dot(xb, wgs_ref[...]) + bg_ref[:, 0:d_model])
    g_lru = _sigmoid(_dot(xb, wgl_ref[...]) + bg_ref[:, d_model:2 * d_model])
    merged = g_ssd * _dot(ysb, wso_ref[...]) + g_lru * _dot(ylru_ref[...], wlo_ref[...])
    o = _dot(merged.astype(BF16), wo_ref[...])
    x1_ref[...] = _layer_norm(alpha * x_ref[...] + o, l1g_ref[...], l1b_ref[...])


def _sample_mixer(x_lm, cssd_lm, clru_lm, slru_lm, state, wts, nseq, steps, start_pos):
    n_heads, headdim, d_state = wts['n_heads'], wts['headdim'], wts['d_state']
    d_model = x_lm.shape[1]
    d_ssd = n_heads * headdim
    gn = SSD_GROUPS * d_state
    d_xbc = d_ssd + 2 * gn
    d_lru = wts['wlx'].shape[1]
    ntok = steps * nseq
    dims = dict(nseq=nseq, steps=steps, n_heads=n_heads, headdim=headdim, d_state=d_state)
    params = pltpu.CompilerParams(dimension_semantics=("arbitrary",), vmem_limit_bytes=VMEM_LIMIT_BYTES)
    step_blk = lambda w: pl.BlockSpec((nseq, w), lambda l: (l, 0))

    sds = jax.ShapeDtypeStruct
    ssd_names = ['wxbc', 'wdt', 'scw', 'scb', 'dtb', 'aneg', 'dexp', 'e', 'e2']
    ssd_consts = [x_lm, cssd_lm] + [wts[k] for k in ssd_names]
    pre, c_lm, b_lm, ea_lm, ydg_lm, xw_lm = pl.pallas_call(
        functools.partial(_sample_ssd_kernel, **dims),
        grid=(steps,),
        in_specs=[step_blk(d_model)] + [_const_spec(c.shape) for c in ssd_consts],
        out_specs=(step_blk(d_xbc), step_blk(gn), step_blk(gn), step_blk(d_ssd), step_blk(d_ssd),
                   step_blk(d_ssd)),
        out_shape=(sds((ntok, d_xbc), F32), sds((ntok, gn), F32), sds((ntok, gn), F32),
                   sds((ntok, d_ssd), F32), sds((ntok, d_ssd), F32), sds((ntok, d_ssd), F32)),
        scratch_shapes=[
            pltpu.VMEM((cssd_lm.shape[0] + ntok, d_xbc), F32),
            pltpu.VMEM((ntok, d_ssd), F32),
            pltpu.VMEM((ntok, gn), F32),
            pltpu.VMEM((ntok, LANES), F32),
            pltpu.VMEM((ntok, LANES), F32),
        ],
        compiler_params=params,
        name="sample_ssd",
    )(x_lm, *ssd_consts)

    lru_names = ['wlx', 'wly', 'lcw', 'lcb', 'wa', 'ba', 'wx', 'bx', 'lam']
    lru_consts = [clru_lm, slru_lm] + [wts[k] for k in lru_names]
    prelx, ylru_lm, lst = pl.pallas_call(
        functools.partial(_sample_lru_kernel, nseq=nseq, steps=steps, start_pos=start_pos),
        grid=(steps,),
        in_specs=[step_blk(d_model)] + [_const_spec(c.shape) for c in lru_consts],
        out_specs=(step_blk(d_lru), step_blk(d_lru), pl.BlockSpec((nseq, d_lru), lambda l: (0, 0))),
        out_shape=(sds((ntok, d_lru), F32), sds((ntok, d_lru), BF16), sds((nseq, d_lru), F32)),
        scratch_shapes=[
            pltpu.VMEM((clru_lm.shape[0] + ntok, d_lru), F32),
            pltpu.VMEM((nseq, d_lru), F32),
        ],
        compiler_params=params,
        name="sample_lru",
    )(x_lm, *lru_consts)

    half = nseq // 2
    full = lambda a: _const_spec(a.shape)
    new_state, yoff_lm = pl.pallas_call(
        functools.partial(_sample_state_kernel, **dims),
        grid=(half,),
        in_specs=[pl.BlockSpec((None, 2, d_ssd, d_state), lambda j: (j, 0, 0, 0)),
                  full(c_lm), full(b_lm), full(xw_lm), full(ea_lm)],
        out_specs=(pl.BlockSpec((None, 2, d_ssd, d_state), lambda j: (j, 0, 0, 0)),
                   pl.BlockSpec((ntok, d_ssd), lambda j: (0, 0))),
        out_shape=(sds(state.shape, F32), sds((ntok, d_ssd), F32)),
        compiler_params=params,
        name="sample_state",
    )(state, c_lm, b_lm, xw_lm, ea_lm)

    post_names = ['wz', 'ng', 'wso', 'wlo', 'wgs', 'wgl', 'bg', 'wo', 'l1g', 'l1b']
    post_consts = [wts[k] for k in post_names]
    x1_lm = pl.pallas_call(
        functools.partial(_sample_post_kernel, alpha=wts['alpha']),
        grid=(steps,),
        in_specs=[step_blk(d_model), step_blk(d_ssd), step_blk(d_ssd), step_blk(d_lru)]
        + [_const_spec(c.shape) for c in post_consts],
        out_specs=step_blk(d_model),
        out_shape=sds((ntok, d_model), F32),
        compiler_params=params,
        name="sample_post",
    )(x_lm, ydg_lm, yoff_lm, ylru_lm, *post_consts)
    return x1_lm, new_state, pre, lst, prelx


def _prep_weights(w_in, b_gate, ssd_conv_w, ssd_conv_b, ssd_dt_bias, ssd_a_log, ssd_d, ssd_norm_g,
                  w_ssd_out, lru_conv_w, lru_conv_b, lru_wa, lru_ba, lru_wx, lru_bx, lru_lambda,
                  w_lru_out, w_o, ln1_g, ln1_b, ffn_w_gate, ffn_w_up, ffn_conv_w, ffn_conv_b,
                  ffn_w_down, ln2_g, ln2_b, n_heads, headdim, d_state):
    depth = w_in.shape[0]
    d_model = w_in.shape[1]
    d_ssd = n_heads * headdim
    d_xbc = d_ssd + 2 * SSD_GROUPS * d_state
    d_lru = lru_lambda.shape[1]
    sizes = (d_ssd, d_xbc, n_heads, d_lru, d_lru, d_model, d_model)
    cuts = np.cumsum((0,) + sizes)
    wi = w_in[0]
    parts = [wi[:, cuts[i]:cuts[i + 1]] for i in range(len(sizes))]
    row = lambda v: v.reshape(1, -1).astype(F32)
    pad_heads = lambda v: jnp.pad(v.reshape(1, -1).astype(F32), ((0, 0), (0, LANES - n_heads)))
    head_of_col = np.arange(d_ssd) // headdim
    expand = (np.arange(LANES)[:, None] == head_of_col[None, :]).astype(np.float32)
    return dict(
        n_heads=n_heads, headdim=headdim, d_state=d_state,
        alpha=float((2.0 * depth) ** 0.25),
        wz=parts[0].astype(BF16), wxbc=parts[1].astype(BF16),
        wdt=jnp.pad(parts[2], ((0, 0), (0, LANES - n_heads))).astype(BF16),
        wlx=parts[3].astype(BF16), wly=parts[4].astype(BF16),
        wgs=parts[5].astype(BF16), wgl=parts[6].astype(BF16),
        scw=ssd_conv_w[0].astype(F32), scb=row(ssd_conv_b[0]),
        dtb=pad_heads(ssd_dt_bias[0]), aneg=pad_heads(-jnp.exp(ssd_a_log[0].astype(F32))),
        dexp=row(jnp.repeat(ssd_d[0], headdim)), ng=row(ssd_norm_g[0]),
        wso=w_ssd_out[0].astype(BF16),
        lcw=lru_conv_w[0].astype(F32), lcb=row(lru_conv_b[0]),
        wa=lru_wa[0].astype(BF16), ba=row(lru_ba[0]), wx=lru_wx[0].astype(BF16), bx=row(lru_bx[0]),
        wax=jnp.concatenate([lru_wa[0], lru_wx[0]], axis=-1).astype(BF16),
        lam=row(lru_lambda[0]), wlo=w_lru_out[0].astype(BF16),
        bg=row(b_gate[0]), wo=w_o[0].astype(BF16), l1g=row(ln1_g[0]), l1b=row(ln1_b[0]),
        e=jnp.asarray(expand, BF16),
        ee=jnp.asarray(np.concatenate([expand, expand], axis=0), BF16),
        e2=jnp.asarray(np.arange(SSD_GROUPS * d_state)[:, None] // d_state
                       == (head_of_col // (n_heads // SSD_GROUPS))[None, :], BF16),
        wg=ffn_w_gate[0].astype(BF16), wu=ffn_w_up[0].astype(BF16),
        fcw=ffn_conv_w[0].astype(F32), fcb=row(ffn_conv_b[0]), wd=ffn_w_down[0].astype(BF16),
        l2g=row(ln2_g[0]), l2b=row(ln2_b[0]),
    )


def kernel(x_prompt, x_sample, state_ssd, cache_ssd_conv, state_lru, cache_lru_conv, cache_ffn_conv, w_in, b_gate, ssd_conv_w, ssd_conv_b, ssd_dt_bias, ssd_a_log, ssd_d, ssd_norm_g, w_ssd_out, lru_conv_w, lru_conv_b, lru_wa, lru_ba, lru_wx, lru_bx, lru_lambda, w_lru_out, w_o, ln1_g, ln1_b, ffn_w_gate, ffn_w_up, ffn_conv_w, ffn_conv_b, ffn_w_down, ln2_g, ln2_b):
    assert w_in.shape[0] == 1, "single-layer trunk"
    _, _, n_heads, headdim, d_state = state_ssd.shape
    wts = _prep_weights(w_in, b_gate, ssd_conv_w, ssd_conv_b, ssd_dt_bias, ssd_a_log, ssd_d, ssd_norm_g,
                        w_ssd_out, lru_conv_w, lru_conv_b, lru_wa, lru_ba, lru_wx, lru_bx, lru_lambda,
                        w_lru_out, w_o, ln1_g, ln1_b, ffn_w_gate, ffn_w_up, ffn_conv_w, ffn_conv_b,
                        ffn_w_down, ln2_g, ln2_b, n_heads, headdim, d_state)
    bp = x_prompt.shape[0]
    d_ff = ffn_w_gate.shape[2]

    x1_p, p_ssd, p_ssd_buf, p_lru, p_lru_buf = _prompt_mixer(x_prompt, wts, tl=256)
    y_prompt, p_ffn_buf = _ffn_seg(x1_p, jnp.zeros((bp, ffn_conv_w.shape[1] - 1, d_ff), F32), wts, tm=512)
    p_ssd = p_ssd.reshape(1, bp, n_heads, headdim, d_state)

    nb_s, steps, _ = x_sample.shape
    half = nb_s // 2

    def to_lm(a):
        k, c = a.shape[1], a.shape[2]
        return a.reshape(half, 2, k, c).transpose(2, 1, 0, 3).reshape(k * nb_s, c)

    def from_lm(a, k):
        c = a.shape[1]
        return a.reshape(k, 2, half, c).transpose(2, 1, 0, 3).reshape(nb_s, k, c)

    d_ssd = n_heads * headdim
    x1_lm, new_state, pre, lst, prelx = _sample_mixer(
        to_lm(x_sample), to_lm(cache_ssd_conv[0]), to_lm(cache_lru_conv[0]), to_lm(state_lru[0][:, None, :]),
        state_ssd[0].reshape(half, 2, d_ssd, d_state), wts, nb_s, steps, PAST_LEN)
    y_lm, tail = _ffn(x1_lm[None], to_lm(cache_ffn_conv[0])[None], wts, tm=steps * nb_s, stride=nb_s)
    k_ssd = ssd_conv_w.shape[1] - 1
    k_lru = lru_conv_w.shape[1] - 1
    k_ffn = ffn_conv_w.shape[1] - 1
    assert steps >= max(k_ssd, k_lru, k_ffn)
    return (y_prompt, from_lm(y_lm[0], steps), p_ssd, p_ssd_buf[None], p_lru.reshape(1, bp, -1), p_lru_buf[None],
            p_ffn_buf[None],
            new_state.reshape(1, nb_s, n_heads, headdim, d_state),
            from_lm(pre[(steps - k_ssd) * nb_s:], k_ssd)[None],
            from_lm(lst, 1).reshape(1, nb_s, -1),
            from_lm(prelx[(steps - k_lru) * nb_s:], k_lru)[None],
            from_lm(tail[0], k_ffn)[None])
```

```python
import functools

import numpy as np
import jax
import jax.numpy as jnp
from jax import lax
from jax.experimental import pallas as pl
from jax.experimental.pallas import tpu as pltpu

F32 = jnp.float32
BF16 = jnp.bfloat16

SSD_GROUPS = 4
SSD_CHUNK = 128
LRU_BLOCKS = 8
LRU_C = 8.0
LN_EPS = 1e-5
RMS_EPS = 1e-6
PAST_LEN = 16384

LANES = 128
SUBLANES = 8
VMEM_LIMIT_BYTES = 60 * 1024 * 1024

HIST = 8
ROW_BLK = 32
UNROLL = True


def _dot(a, b):
    if b.dtype == jnp.uint32:
        b = pltpu.bitcast(b, BF16)
    return jnp.dot(a, b, preferred_element_type=F32)


def _pack_rows(w):
    *lead, k, n = w.shape
    w2 = jnp.swapaxes(w.reshape(*lead, k // 2, 2, n), -1, -2)
    return lax.bitcast_convert_type(w2, jnp.uint32)


def _pack_rows_01(m):
    bits = np.ascontiguousarray(m, np.float32).view(np.uint32) >> 16
    return jnp.asarray(bits[0::2] | (bits[1::2] << 16), jnp.uint32)


def _dot_nt(a, b):
    return lax.dot_general(a, b, (((1,), (1,)), ((), ())), preferred_element_type=F32)


def _dot_tn(a, b):
    return lax.dot_general(a, b, (((0,), (0,)), ((), ())), preferred_element_type=F32)


def _split3(v):
    hi = v.astype(BF16)
    r1 = v - hi.astype(F32)
    mid = r1.astype(BF16)
    lo = (r1 - mid.astype(F32)).astype(BF16)
    return hi, mid, lo


def _dot_exact_rhs(v, m):
    hi, mid, lo = _split3(v)
    return _dot(hi, m) + _dot(mid, m) + _dot(lo, m)


def _dot_exact_lhs(m, v):
    hi, mid, lo = _split3(v)
    return _dot(m, hi) + _dot(m, mid) + _dot(m, lo)


def _split2(v):
    hi = v.astype(BF16)
    lo = (v - hi.astype(F32)).astype(BF16)
    return hi, lo


def _dot_2way_rhs(v, m2):
    return _dot(jnp.concatenate(_split2(v), axis=1), m2)


def _dot_2way_lhs(m2, v):
    return _dot(m2, jnp.concatenate(_split2(v), axis=0))


def _softplus(x):
    return jnp.maximum(x, 0.0) + jnp.log1p(jnp.exp(-jnp.abs(x)))


def _sigmoid(x):
    return 0.5 * jnp.tanh(0.5 * x) + 0.5


def _silu(x):
    h = 0.5 * x
    return h + h * jnp.tanh(h)


def _gelu(x):
    c = np.sqrt(2.0 / np.pi).astype(np.float32)
    return 0.5 * x * (1.0 + jnp.tanh(c * (x + 0.044715 * (x * x * x))))


def _layer_norm(v, g, b):
    mu = jnp.mean(v, axis=-1, keepdims=True)
    d = v - mu
    var = jnp.mean(d * d, axis=-1, keepdims=True)
    return d * lax.rsqrt(var + LN_EPS) * g + b


def _rows(i, n):
    if isinstance(i, int):
        return slice(i * n, (i + 1) * n)
    return pl.ds(pl.multiple_of(i * n, n), n)


def _loop(n, body, unroll):
    if unroll:
        for i in range(n):
            body(i, None)
    else:
        lax.fori_loop(0, n, lambda i, c: (body(i, c), c)[1], 0)


def _conv_block(buf_ref, w_ref, b_ref, hist, stride, r0, rows, c0, cw):
    taps = w_ref.shape[0]
    acc = b_ref[:, c0:c0 + cw]
    for k in range(taps):
        off = hist + r0 - (taps - 1 - k) * stride
        if not isinstance(off, int):
            off = pl.multiple_of(off, SUBLANES)
        acc = acc + w_ref[k:k + 1, c0:c0 + cw] * buf_ref[pl.ds(off, rows), c0:c0 + cw]
    return acc


LOG2_SUBLANES = 3
NPOS = SSD_CHUNK // SUBLANES
LOG2_NPOS = 4
assert 1 << LOG2_SUBLANES == SUBLANES and 1 << LOG2_NPOS == NPOS


def _tok_of_row(r):
    return (r & (SUBLANES - 1)) * NPOS + lax.shift_right_logical(r, LOG2_SUBLANES)


def _row_of_tok(t):
    return (t & (NPOS - 1)) * SUBLANES + lax.shift_right_logical(t, LOG2_NPOS)


def _perm_matrix():
    q = SSD_CHUNK
    r = lax.broadcasted_iota(jnp.int32, (q, q), 0)
    c = lax.broadcasted_iota(jnp.int32, (q, q), 1)
    return jnp.where(c == _tok_of_row(r), 1.0, 0.0).astype(BF16)


def _unperm_matrix():
    q = SSD_CHUNK
    t = lax.broadcasted_iota(jnp.int32, (q, q), 0)
    r = lax.broadcasted_iota(jnp.int32, (q, q), 1)
    return jnp.where(r == _row_of_tok(t), 1.0, 0.0).astype(BF16)


def _fill_wrap(buf_ref, wrap_ref, c, wrap, tail_ref):
    q = SSD_CHUNK
    ncol = buf_ref.shape[1]
    sub0 = lax.broadcasted_iota(jnp.int32, (SUBLANES, ncol), 0) == 0
    for m in range(wrap // SUBLANES):
        r_cur = (c + 1) * q - wrap + m * SUBLANES
        cur = buf_ref[r_cur:r_cur + SUBLANES, :]
        if c == 0:
            prv = tail_ref[m * SUBLANES:(m + 1) * SUBLANES, :]
        else:
            prv = buf_ref[r_cur - q:r_cur - q + SUBLANES, :]
        wrap_ref[c * wrap + m * SUBLANES:c * wrap + (m + 1) * SUBLANES, :] = jnp.where(
            sub0, pltpu.roll(prv, 1, 0), pltpu.roll(cur, 1, 0))


def _conv_seg(buf_ref, wrap_ref, w_ref, b_ref, c, wrap, c0, cw):
    q = SSD_CHUNK
    taps = w_ref.shape[0]
    cols = slice(c0, c0 + cw)
    acc = b_ref[:, cols] + w_ref[taps - 1:taps, cols] * buf_ref[c * q:(c + 1) * q, cols]
    for k in range(taps - 1):
        back = (taps - 1 - k) * SUBLANES
        shifted = jnp.concatenate(
            [wrap_ref[(c + 1) * wrap - back:(c + 1) * wrap, cols], buf_ref[c * q:(c + 1) * q - back, cols]],
            axis=0)
        acc = acc + w_ref[k:k + 1, cols] * shifted
    return acc


def _seg_tail_rows(wrap):
    n = wrap // SUBLANES
    return [(NPOS - n + m) * SUBLANES + SUBLANES - 1 for m in range(n)]


def _prompt_mixer_kernel(
        x_ref, wz_ref, wxbc_ref, wdt_ref, wlx_ref, wly_ref, wgs_ref, wgl_ref,
        scw_ref, scb_ref, dtb_ref, aneg_ref, dexp_ref, ng_ref, wso_ref,
        lcw_ref, lcb_ref, wax_ref, ba_ref, bx_ref, lam_ref, wlo_ref,
        bg_ref, wo_ref, l1g_ref, l1b_ref, e_ref,
        x1_ref, st_ref, sconv_ref, lst_ref, lconv_ref,
        xb_s, xp_s, xbc_s, swrap_s, stail_s, xc_s, lx_s, lwrap_s, ltail_s, xr_s, ly_s, y_s, z_s, ysb_s, ylb_s,
        ht_s, hl_s,
        *, tl, alpha, n_heads, headdim, d_state):
    t = pl.program_id(1)
    nt = pl.num_programs(1)
    d_ssd = n_heads * headdim
    gn = SSD_GROUPS * d_state
    hpg = n_heads // SSD_GROUPS
    gw = hpg * headdim
    d_lru = lx_s.shape[1]
    d_model = x_ref.shape[2]
    q = SSD_CHUNK

    nch = tl // q
    wrap_s = (scw_ref.shape[0] - 1) * SUBLANES
    wrap_l = (lcw_ref.shape[0] - 1) * SUBLANES

    @pl.when(t == 0)
    def _():
        stail_s[...] = jnp.zeros(stail_s.shape, F32)
        ltail_s[...] = jnp.zeros(ltail_s.shape, F32)
        ht_s[...] = jnp.zeros(ht_s.shape, F32)
        hl_s[...] = jnp.zeros(hl_s.shape, F32)

    perm = _perm_matrix()
    perm2 = jnp.concatenate([perm, perm], axis=1)
    for c in range(nch):
        rows = _rows(c, q)
        xp = _dot_2way_lhs(perm2, x_ref[0, rows, :])
        xp_s[rows, :] = xp
        xb_s[rows, :] = xp.astype(BF16)

    xbc_s[...] = _dot(xb_s[...], wxbc_ref[...])
    cwid = 512
    for c in range(nch):
        _fill_wrap(xbc_s, swrap_s, c, wrap_s, stail_s)
        for c0 in range(0, xc_s.shape[1], cwid):
            xc_s[c * q:(c + 1) * q, c0:c0 + cwid] = _silu(
                _conv_seg(xbc_s, swrap_s, scw_ref, scb_ref, c, wrap_s, c0, cwid))
    stail_s[...] = xbc_s[tl - wrap_s:tl, :]

    tok_r = _tok_of_row(lax.broadcasted_iota(jnp.int32, (q, q), 0))
    tok_c = _tok_of_row(lax.broadcasted_iota(jnp.int32, (q, q), 1))
    causal = tok_r >= tok_c
    tri = jnp.where(causal, 1.0, 0.0).astype(BF16)
    lane_i = lax.broadcasted_iota(jnp.int32, (q, LANES), 1)
    left = lane_i < headdim

    def chunk_body(c, carry):
        rows = _rows(c, q)
        dt = _softplus(_dot(xb_s[rows, :], wdt_ref[...]) + dtb_ref[...])
        d_a = dt * aneg_ref[...]
        a_cs = _dot_exact_lhs(tri, d_a)
        a_last = a_cs[q - 1:q, :]
        wgt = dt * jnp.exp(a_last - a_cs)
        ea = jnp.exp(a_cs)
        w_exp = _dot_2way_rhs(wgt, e_ref[...])
        ea_exp = _dot_2way_rhs(ea, e_ref[...])
        a_cs_t = a_cs.T
        dt_t = dt.T
        for g in range(SSD_GROUPS):
            b_g = xc_s[rows, d_ssd + g * d_state:d_ssd + (g + 1) * d_state]
            c_g = xc_s[rows, d_ssd + gn + g * d_state:d_ssd + gn + (g + 1) * d_state]
            b_gb = b_g.astype(BF16)
            c_gb = c_g.astype(BF16)
            cb = _dot_nt(c_gb, b_gb)
            for hp in range(hpg // 2):
                c0 = g * gw + hp * 2 * headdim
                xs_pair = xc_s[rows, c0:c0 + 2 * headdim]
                lmats = []
                for j in range(2):
                    h = g * hpg + hp * 2 + j
                    seg = (jnp.broadcast_to(a_cs[:, h:h + 1], (q, q))
                           - jnp.broadcast_to(a_cs_t[h:h + 1, :], (q, q)))
                    dec = jnp.exp(jnp.where(causal, seg, -jnp.inf))
                    lmats.append((cb * dec * jnp.broadcast_to(dt_t[h:h + 1, :], (q, q))).astype(BF16))
                lpair = jnp.concatenate(lmats, axis=1)
                rhs = jnp.concatenate([jnp.where(left, xs_pair, 0.0),
                                       jnp.where(left, 0.0, xs_pair)], axis=0).astype(BF16)
                y_s[rows, c0:c0 + 2 * headdim] = _dot(lpair, rhs)
            gcols = slice(g * gw, (g + 1) * gw)
            h_g = ht_s[:, gcols]
            y_off = _dot(c_gb, h_g.astype(BF16)) * ea_exp[:, gcols]
            y_s[rows, gcols] = y_s[rows, gcols] + y_off
            xw = (xc_s[rows, gcols] * w_exp[:, gcols]).astype(BF16)
            ht_s[:, gcols] = h_g * ea_exp[q - 1:q, gcols] + _dot_tn(b_gb, xw)
        return carry

    _loop(tl // q, chunk_body, UNROLL)

    z_s[...] = _dot(xb_s[...], wz_ref[...])

    def gate_body(i, carry):
        rows = _rows(i, ROW_BLK)
        y = y_s[rows, :] + dexp_ref[...] * xc_s[rows, 0:d_ssd]
        y = y * _silu(z_s[rows, :])
        ms = jnp.mean(y * y, axis=-1, keepdims=True)
        ysb_s[rows, :] = (y * lax.rsqrt(ms + RMS_EPS) * ng_ref[...]).astype(BF16)
        return carry

    _loop(tl // ROW_BLK, gate_body, UNROLL)

    lx_s[...] = _dot(xb_s[...], wlx_ref[...])
    for c in range(nch):
        _fill_wrap(lx_s, lwrap_s, c, wrap_l, ltail_s)
        for c0 in range(0, d_lru, cwid):
            xr_s[c * q:(c + 1) * q, c0:c0 + cwid] = _conv_seg(lx_s, lwrap_s, lcw_ref, lcb_ref, c, wrap_l, c0, cwid)
    ltail_s[...] = lx_s[tl - wrap_l:tl, :]
    ly_s[...] = _dot(xb_s[...], wly_ref[...])

    bw = d_lru // LRU_BLOCKS
    sub = lax.broadcasted_iota(jnp.int32, (SUBLANES, bw), 0)
    crow = lax.broadcasted_iota(jnp.int32, (q, bw), 0)
    for n in range(LRU_BLOCKS):
        cols = slice(n * bw, (n + 1) * bw)
        xr = xr_s[:, cols]
        xrb = xr.astype(BF16)
        rg = _dot(xrb, wax_ref[n])
        r = _sigmoid(rg[:, 0:bw] + ba_ref[:, cols])
        gi = _sigmoid(rg[:, bw:2 * bw] + bx_ref[:, cols])
        log_a = (-LRU_C) * r * _softplus(-lam_ref[:, cols])
        a_all = jnp.exp(log_a)
        mult_all = jnp.sqrt(1.0 - jnp.exp(2.0 * log_a))
        for c in range(nch):
            a = a_all[c * q:(c + 1) * q, :]
            mult = mult_all[c * q:(c + 1) * q, :]
            if c == 0:
                mult = jnp.where(jnp.logical_and(crow == 0, t == 0), 1.0, mult)
            u = mult * gi[c * q:(c + 1) * q, :] * xr[c * q:(c + 1) * q, :]
            a_p = [a[i * SUBLANES:(i + 1) * SUBLANES, :] for i in range(NPOS)]
            u_p = [u[i * SUBLANES:(i + 1) * SUBLANES, :] for i in range(NPOS)]
            h = u_p[0]
            g = a_p[0]
            for i in range(1, NPOS):
                h = a_p[i] * h + u_p[i]
                g = a_p[i] * g
            gs = jnp.where(sub == 0, 0.0, pltpu.roll(g, 1, 0))
            hs = jnp.where(sub == 0, hl_s[0:1, cols], pltpu.roll(h, 1, 0))
            d = 1
            while d < SUBLANES:
                keep = sub >= d
                hs = jnp.where(keep, gs * pltpu.roll(hs, d, 0) + hs, hs)
                gs = jnp.where(keep, gs * pltpu.roll(gs, d, 0), gs)
                d *= 2
            h = hs
            out = []
            for i in range(NPOS):
                h = a_p[i] * h + u_p[i]
                out.append(h)
            hl_s[0:1, cols] = h[SUBLANES - 1:SUBLANES, :]
            hseq = jnp.concatenate(out, axis=0)
            ylb_s[c * q:(c + 1) * q, cols] = (hseq * _gelu(ly_s[c * q:(c + 1) * q, cols])).astype(BF16)

    half = d_model
    z_s[:, 0:half] = _dot(ysb_s[...], wso_ref[...])
    z_s[:, half:2 * half] = _dot(ylb_s[...], wlo_ref[...])
    y_s[:, 0:half] = _dot(xb_s[...], wgs_ref[...])
    y_s[:, half:2 * half] = _dot(xb_s[...], wgl_ref[...])

    def merge_body(i, carry):
        rows = _rows(i, ROW_BLK)
        g_ssd = _sigmoid(y_s[rows, 0:half] + bg_ref[:, 0:half])
        g_lru = _sigmoid(y_s[rows, half:2 * half] + bg_ref[:, half:2 * half])
        ylb_s[rows, :] = (g_ssd * z_s[rows, 0:half] + g_lru * z_s[rows, half:2 * half]).astype(BF16)
        return carry

    _loop(tl // ROW_BLK, merge_body, UNROLL)
    z_s[:, 0:half] = _dot(ylb_s[...], wo_ref[...])

    def ln_body(i, carry):
        rows = _rows(i, ROW_BLK)
        v = alpha * xp_s[rows, :] + z_s[rows, 0:half]
        x1_ref[0, rows, :] = _layer_norm(v, l1g_ref[...], l1b_ref[...])
        return carry

    _loop(tl // ROW_BLK, ln_body, UNROLL)

    @pl.when(t == nt - 1)
    def _():
        st_ref[0] = ht_s[...].T
        lst_ref[0] = hl_s[0:1, :]
        for m, r in enumerate(_seg_tail_rows(wrap_s)):
            sconv_ref[0, m:m + 1, :] = xbc_s[tl - q + r:tl - q + r + 1, :]
        for m, r in enumerate(_seg_tail_rows(wrap_l)):
            lconv_ref[0, m:m + 1, :] = lx_s[tl - q + r:tl - q + r + 1, :]


def _const_spec(shape):
    nd = len(shape)
    return pl.BlockSpec(shape, lambda *_: (0,) * nd, pipeline_mode=pl.Buffered(1))


def _prompt_mixer(x, wts, tl):
    nb, seq, d_model = x.shape
    n_heads, headdim, d_state = wts['n_heads'], wts['headdim'], wts['d_state']
    d_ssd = n_heads * headdim
    d_xbc = d_ssd + 2 * SSD_GROUPS * d_state
    d_lru = wts['wlx'].shape[1]
    names = ['wz', 'wxbc', 'wdt', 'wlx', 'wly', 'wgs', 'wgl', 'scw', 'scb', 'dtb', 'aneg', 'dexp', 'ng',
             'wso', 'lcw', 'lcb', 'wax', 'ba', 'bx', 'lam', 'wlo', 'bg', 'wo', 'l1g', 'l1b', 'ee']
    consts = [wts[k] for k in names]
    kern = functools.partial(_prompt_mixer_kernel, tl=tl, alpha=wts['alpha'], n_heads=n_heads,
                             headdim=headdim, d_state=d_state)
    out_shape = (
        jax.ShapeDtypeStruct((nb, seq, d_model), F32),
        jax.ShapeDtypeStruct((nb, d_ssd, d_state), F32),
        jax.ShapeDtypeStruct((nb, 3, d_xbc), F32),
        jax.ShapeDtypeStruct((nb, 1, d_lru), F32),
        jax.ShapeDtypeStruct((nb, 3, d_lru), F32),
    )
    out_specs = (
        pl.BlockSpec((1, tl, d_model), lambda b, t: (b, t, 0)),
        pl.BlockSpec((1, d_ssd, d_state), lambda b, t: (b, 0, 0)),
        pl.BlockSpec((1, 3, d_xbc), lambda b, t: (b, 0, 0)),
        pl.BlockSpec((1, 1, d_lru), lambda b, t: (b, 0, 0)),
        pl.BlockSpec((1, 3, d_lru), lambda b, t: (b, 0, 0)),
    )
    nch = tl // SSD_CHUNK
    wrap_s = (wts['scw'].shape[0] - 1) * SUBLANES
    wrap_l = (wts['lcw'].shape[0] - 1) * SUBLANES
    scratch = [
        pltpu.VMEM((tl, d_model), BF16),
        pltpu.VMEM((tl, d_model), F32),
        pltpu.VMEM((tl, d_xbc), F32),
        pltpu.VMEM((nch * wrap_s, d_xbc), F32),
        pltpu.VMEM((wrap_s, d_xbc), F32),
        pltpu.VMEM((tl, d_xbc), F32),
        pltpu.VMEM((tl, d_lru), F32),
        pltpu.VMEM((nch * wrap_l, d_lru), F32),
        pltpu.VMEM((wrap_l, d_lru), F32),
        pltpu.VMEM((tl, d_lru), F32),
        pltpu.VMEM((tl, d_lru), F32),
        pltpu.VMEM((tl, d_ssd), F32),
        pltpu.VMEM((tl, d_ssd), F32),
        pltpu.VMEM((tl, d_ssd), BF16),
        pltpu.VMEM((tl, d_lru), BF16),
        pltpu.VMEM((d_state, d_ssd), F32),
        pltpu.VMEM((SUBLANES, d_lru), F32),
    ]
    return pl.pallas_call(
        kern,
        grid=(nb, seq // tl),
        in_specs=[pl.BlockSpec((1, tl, d_model), lambda b, t: (b, t, 0))]
        + [_const_spec(c.shape) for c in consts],
        out_specs=out_specs,
        out_shape=out_shape,
        scratch_shapes=scratch,
        compiler_params=pltpu.CompilerParams(
            dimension_semantics=("arbitrary", "arbitrary"),
            vmem_limit_bytes=VMEM_LIMIT_BYTES),
        name="prompt_mixer",
    )(x, *consts)


def _ffn_kernel(x_ref, h0_ref, wg_ref, wu_ref, cw_ref, cb_ref, wd_ref, g_ref, b_ref,
                y_ref, tail_ref, xb_s, gb_s, hb_s, *, tm, stride, hist, alpha, fchunk):
    t = pl.program_id(1)
    nt = pl.num_programs(1)
    taps = cw_ref.shape[0]
    nh = (taps - 1) * stride
    d_ff = gb_s.shape[1]

    @pl.when(t == 0)
    def _():
        gb_s[hist - nh:hist, :] = h0_ref[0]

    xb_s[...] = x_ref[0].astype(BF16)
    acc = None
    for c0 in range(0, d_ff, fchunk):
        cols = slice(c0, c0 + fchunk)
        gb_s[hist:hist + tm, cols] = _dot(xb_s[...], wg_ref[:, cols])
        up = _dot(xb_s[...], wu_ref[:, cols])
        gc = _conv_block(gb_s, cw_ref, cb_ref, hist, stride, 0, tm, c0, fchunk)
        hb_s[...] = (_gelu(gc) * up).astype(BF16)
        part = _dot(hb_s[...], wd_ref[c0 // 2:(c0 + fchunk) // 2, :])
        acc = part if acc is None else acc + part
    v = alpha * x_ref[0] + acc
    y_ref[0] = _layer_norm(v, g_ref[...], b_ref[...])
    gb_s[hist - nh:hist, :] = gb_s[hist + tm - nh:hist + tm, :]

    @pl.when(t == nt - 1)
    def _():
        tail_ref[0] = gb_s[hist - nh:hist, :]


def _ffn_seg_kernel(x_ref, h0_ref, wg_ref, wu_ref, cw_ref, cb_ref, wd_ref, g_ref, b_ref,
                    y_ref, tail_ref, xb_s, gb_s, gwrap_s, gtail_s, hb_s, *, tm, alpha, fchunk):
    t = pl.program_id(1)
    nt = pl.num_programs(1)
    q = SSD_CHUNK
    nch = tm // q
    wrap = (cw_ref.shape[0] - 1) * SUBLANES
    d_ff = gb_s.shape[1]
    tail_rows = _seg_tail_rows(wrap)

    @pl.when(t == 0)
    def _():
        gtail_s[...] = jnp.zeros(gtail_s.shape, F32)
        for m in range(len(tail_rows)):
            r = m * SUBLANES + SUBLANES - 1
            gtail_s[r:r + 1, :] = h0_ref[0, m:m + 1, :]

    xb_s[...] = x_ref[0].astype(BF16)
    gb_s[...] = _dot(xb_s[...], wg_ref[...])
    for c in range(nch):
        _fill_wrap(gb_s, gwrap_s, c, wrap, gtail_s)
    acc = None
    for c0 in range(0, d_ff, fchunk):
        cols = slice(c0, c0 + fchunk)
        up = _dot(xb_s[...], wu_ref[:, cols])
        for c in range(nch):
            gc = _conv_seg(gb_s, gwrap_s, cw_ref, cb_ref, c, wrap, c0, fchunk)
            hb_s[c * q:(c + 1) * q, :] = (_gelu(gc) * up[c * q:(c + 1) * q, :]).astype(BF16)
        part = _dot(hb_s[...], wd_ref[c0 // 2:(c0 + fchunk) // 2, :])
        acc = part if acc is None else acc + part
    gtail_s[...] = gb_s[tm - wrap:tm, :]

    unperm = _unperm_matrix()
    unperm2 = jnp.concatenate([unperm, unperm], axis=1)
    for c in range(nch):
        rows = slice(c * q, (c + 1) * q)
        y = _layer_norm(alpha * x_ref[0, rows, :] + acc[rows, :], g_ref[...], b_ref[...])
        y_ref[0, rows, :] = _dot_2way_lhs(unperm2, y)

    @pl.when(t == nt - 1)
    def _():
        for m, r in enumerate(tail_rows):
            tail_ref[0, m:m + 1, :] = gb_s[tm - q + r:tm - q + r + 1, :]


def _ffn_seg(x, hist0, wts, tm):
    nb, seq, d_model = x.shape
    d_ff = wts['wg'].shape[1]
    taps = wts['fcw'].shape[0]
    wrap = (taps - 1) * SUBLANES
    fchunk = 1024
    consts = [wts[k] for k in ['wg', 'wu', 'fcw', 'fcb', 'wd', 'l2g', 'l2b']]
    kern = functools.partial(_ffn_seg_kernel, tm=tm, alpha=wts['alpha'], fchunk=fchunk)
    return pl.pallas_call(
        kern,
        grid=(nb, seq // tm),
        in_specs=[pl.BlockSpec((1, tm, d_model), lambda b, t: (b, t, 0)),
                  pl.BlockSpec((1, taps - 1, d_ff), lambda b, t: (b, 0, 0))]
        + [_const_spec(c.shape) for c in consts],
        out_specs=(pl.BlockSpec((1, tm, d_model), lambda b, t: (b, t, 0)),
                   pl.BlockSpec((1, taps - 1, d_ff), lambda b, t: (b, 0, 0))),
        out_shape=(jax.ShapeDtypeStruct((nb, seq, d_model), F32),
                   jax.ShapeDtypeStruct((nb, taps - 1, d_ff), F32)),
        scratch_shapes=[pltpu.VMEM((tm, d_model), BF16),
                        pltpu.VMEM((tm, d_ff), F32),
                        pltpu.VMEM((tm // SSD_CHUNK * wrap, d_ff), F32),
                        pltpu.VMEM((wrap, d_ff), F32),
                        pltpu.VMEM((tm, fchunk), BF16)],
        compiler_params=pltpu.CompilerParams(
            dimension_semantics=("arbitrary", "arbitrary"),
            vmem_limit_bytes=VMEM_LIMIT_BYTES),
        name="conv_ffn_seg",
    )(x, hist0, *consts)


def _ffn(x, hist0, wts, tm, stride):
    nb, seq, d_model = x.shape
    d_ff = wts['wg'].shape[1]
    taps = wts['fcw'].shape[0]
    nh = (taps - 1) * stride
    hist = -(-nh // SUBLANES) * SUBLANES
    consts = [wts[k] for k in ['wg', 'wu', 'fcw', 'fcb', 'wd', 'l2g', 'l2b']]
    kern = functools.partial(_ffn_kernel, tm=tm, stride=stride, hist=hist, alpha=wts['alpha'], fchunk=1024)
    return pl.pallas_call(
        kern,
        grid=(nb, seq // tm),
        in_specs=[pl.BlockSpec((1, tm, d_model), lambda b, t: (b, t, 0)),
                  pl.BlockSpec((1, nh, d_ff), lambda b, t: (b, 0, 0))]
        + [_const_spec(c.shape) for c in consts],
        out_specs=(pl.BlockSpec((1, tm, d_model), lambda b, t: (b, t, 0)),
                   pl.BlockSpec((1, nh, d_ff), lambda b, t: (b, 0, 0))),
        out_shape=(jax.ShapeDtypeStruct((nb, seq, d_model), F32),
                   jax.ShapeDtypeStruct((nb, nh, d_ff), F32)),
        scratch_shapes=[pltpu.VMEM((tm, d_model), BF16),
                        pltpu.VMEM((hist + tm, d_ff), F32),
                        pltpu.VMEM((tm, 1024), BF16)],
        compiler_params=pltpu.CompilerParams(
            dimension_semantics=("arbitrary", "arbitrary"),
            vmem_limit_bytes=VMEM_LIMIT_BYTES),
        name="conv_ffn",
    )(x, hist0, *consts)


def _sample_ssd_kernel(
        x_ref, xall_ref, cssd_ref, wxbc_ref, wdt_ref, scw_ref, scb_ref, dtb_ref, aneg_ref, dexp_ref,
        e_ref, e2_ref,
        pre_ref, c_ref, b_ref, ea_ref, ydg_ref, xw_ref,
        xbc_s, xs_s, bs_s, acs_s, dts_s,
        *, nseq, steps, n_heads, headdim, d_state):
    l = pl.program_id(0)
    d_ssd = n_heads * headdim
    gn = SSD_GROUPS * d_state
    hist = (scw_ref.shape[0] - 1) * nseq
    r0 = l * nseq

    def blk(i):
        return pl.ds(pl.multiple_of(i * nseq, nseq), nseq)

    def sblk(i):
        return slice(i * nseq, (i + 1) * nseq)

    @pl.when(l == 0)
    def _():
        xbc_s[0:hist, :] = cssd_ref[...]
        dts_s[...] = _softplus(_dot(xall_ref[...].astype(BF16), wdt_ref[...]) + dtb_ref[...])
        acc = jnp.zeros((nseq, LANES), F32)
        for s in range(steps):
            acc = acc + dts_s[sblk(s), :] * aneg_ref[...]
            acs_s[sblk(s), :] = acc

    xb = x_ref[...].astype(BF16)

    pre = _dot(xb, wxbc_ref[...])
    pre_ref[...] = pre
    xbc_s[pl.ds(pl.multiple_of(hist + r0, nseq), nseq), :] = pre
    cwid = 512
    for c0 in range(0, d_ssd, cwid):
        xs_s[blk(l), c0:c0 + cwid] = _silu(
            _conv_block(xbc_s, scw_ref, scb_ref, hist, nseq, r0, nseq, c0, cwid))
    b_l = _silu(_conv_block(xbc_s, scw_ref, scb_ref, hist, nseq, r0, nseq, d_ssd, gn))
    c_l = _silu(_conv_block(xbc_s, scw_ref, scb_ref, hist, nseq, r0, nseq, d_ssd + gn, gn))
    bs_s[blk(l), :] = b_l
    b_ref[...] = b_l
    c_ref[...] = c_l

    a_cs = acs_s[blk(l), :]
    dt = dts_s[blk(l), :]
    a_end = acs_s[sblk(steps - 1), :]
    ea_ref[...] = _dot_exact_rhs(jnp.exp(a_cs), e_ref[...])
    xw_ref[...] = xs_s[blk(l), :] * _dot_exact_rhs(dt * jnp.exp(a_end - a_cs), e_ref[...])

    ydg_ref[...] = dexp_ref[...] * xs_s[blk(l), :]
    for s in range(steps):
        @pl.when(s <= l)
        def _(s=s):
            coef = _dot_exact_rhs(jnp.exp(a_cs - acs_s[sblk(s), :]) * dts_s[sblk(s), :], e_ref[...])
            cbx = _dot_exact_rhs(bs_s[sblk(s), :] * c_l, e2_ref[...])
            ydg_ref[...] += cbx * coef * xs_s[sblk(s), :]


def _sample_lru_kernel(
        x_ref, clru_ref, slru_ref, wlx_ref, wly_ref, lcw_ref, lcb_ref, wa_ref, ba_ref, wx_ref, bx_ref,
        lam_ref, prelx_ref, ylru_ref, lst_ref, lx_s, hl_s, *, nseq, steps, start_pos):
    l = pl.program_id(0)
    hist = (lcw_ref.shape[0] - 1) * nseq
    d_lru = lx_s.shape[1]
    r0 = l * nseq

    @pl.when(l == 0)
    def _():
        lx_s[0:hist, :] = clru_ref[...]
        hl_s[...] = slru_ref[...]

    xb = x_ref[...].astype(BF16)
    prelx = _dot(xb, wlx_ref[...])
    prelx_ref[...] = prelx
    lx_s[pl.ds(pl.multiple_of(hist + r0, nseq), nseq), :] = prelx
    ly = _dot(xb, wly_ref[...])
    bw = d_lru // LRU_BLOCKS
    first = (l + start_pos) == 0
    for n in range(LRU_BLOCKS):
        cols = slice(n * bw, (n + 1) * bw)
        xr = _conv_block(lx_s, lcw_ref, lcb_ref, hist, nseq, r0, nseq, n * bw, bw)
        xrb = xr.astype(BF16)
        r = _sigmoid(_dot(xrb, wa_ref[n]) + ba_ref[:, cols])
        gi = _sigmoid(_dot(xrb, wx_ref[n]) + bx_ref[:, cols])
        log_a = (-LRU_C) * r * _softplus(-lam_ref[:, cols])
        a = jnp.exp(log_a)
        mult = jnp.where(first, 1.0, jnp.sqrt(1.0 - jnp.exp(2.0 * log_a)))
        h = a * hl_s[:, cols] + mult * gi * xr
        hl_s[:, cols] = h
        ylru_ref[:, cols] = (h * _gelu(ly[:, cols])).astype(BF16)

    @pl.when(l == steps - 1)
    def _():
        lst_ref[...] = hl_s[...]


def _sample_state_kernel(st_ref, c_ref, b_ref, xw_ref, ea_ref, nst_ref, yoff_ref,
                         *, nseq, steps, n_heads, headdim, d_state):
    j = pl.program_id(0)
    half = nseq // 2
    nrow = 2 * steps
    hpg = n_heads // SSD_GROUPS
    gw = hpg * headdim

    def gather(ref, cols):
        return jnp.concatenate([ref[pl.ds(j + k * half, 1), cols] for k in range(nrow)], axis=0)

    c8 = gather(c_ref, slice(None))
    b8 = gather(b_ref, slice(None))
    par = lax.broadcasted_iota(jnp.int32, (nrow, gw), 0) % 2
    assert 2 * headdim == LANES and d_state == LANES
    low = lax.broadcasted_iota(jnp.int32, (nrow, LANES), 1) < headdim
    for g in range(SSD_GROUPS):
        gcols = slice(g * gw, (g + 1) * gw)
        c8g = c8[:, g * d_state:(g + 1) * d_state].astype(BF16)
        b8g = b8[:, g * d_state:(g + 1) * d_state].astype(BF16)
        xw8 = gather(xw_ref, gcols)
        ea8 = gather(ea_ref, gcols)
        cds = []
        for hp in range(hpg // 2):
            pair = ea8[:, hp * LANES:(hp + 1) * LANES]
            swapped = pltpu.roll(pair, headdim, 1)
            cds.append(jnp.where(low, pair, swapped))
            cds.append(jnp.where(low, swapped, pair))
        yo = None
        for e in range(2):
            sg = st_ref[e, gcols, :]
            yo_e = _dot_nt(c8g, sg.astype(BF16))
            yo = yo_e if e == 0 else jnp.where(par == e, yo_e, yo)
            xw_e = jnp.where(par == e, xw8, 0.0).astype(BF16)
            upd = _dot_tn(xw_e, b8g)
            k_last = 2 * (steps - 1) + e
            for hh in range(hpg):
                cd = cds[hh][k_last:k_last + 1, :]
                hr = slice(hh * headdim, (hh + 1) * headdim)
                nst_ref[e, g * gw + hh * headdim:g * gw + (hh + 1) * headdim, :] = sg[hr, :] * cd + upd[hr, :]
        yo = yo * ea8
        for k in range(nrow):
            yoff_ref[pl.ds(j + k * half, 1), gcols] = yo[k:k + 1, :]


def _sample_post_kernel(x_ref, ydg_ref, yoff_ref, ylru_ref, wz_ref, ng_ref, wso_ref, wlo_ref,
                        wgs_ref, wgl_ref, bg_ref, wo_ref, l1g_ref, l1b_ref, x1_ref, *, alpha):
    d_model = x_ref.shape[1]
    xb = x_ref[...].astype(BF16)
    y = (ydg_ref[...] + yoff_ref[...]) * _silu(_dot(xb, wz_ref[...]))
    ms = jnp.mean(y * y, axis=-1, keepdims=True)
    ysb = (y * lax.rsqrt(ms + RMS_EPS) * ng_ref[...]).astype(BF16)
    g_ssd = _sigmoid(_dot(xb, wgs_ref[...]) + bg_ref[:, 0:d_model])
    g_lru = _sigmoid(_dot(xb, wgl_ref[...]) + bg_ref[:, d_model:2 * d_model])
    merged = g_ssd * _dot(ysb, wso_ref[...]) + g_lru * _dot(ylru_ref[...], wlo_ref[...])
    o = _dot(merged.astype(BF16), wo_ref[...])
    x1_ref[...] = _layer_norm(alpha * x_ref[...] + o, l1g_ref[...], l1b_ref[...])


def _sample_mixer(x_lm, cssd_lm, clru_lm, slru_lm, state, wts, nseq, steps, start_pos):
    n_heads, headdim, d_state = wts['n_heads'], wts['headdim'], wts['d_state']
    d_model = x_lm.shape[1]
    d_ssd = n_heads * headdim
    gn = SSD_GROUPS * d_state
    d_xbc = d_ssd + 2 * gn
    d_lru = wts['wlx'].shape[1]
    ntok = steps * nseq
    dims = dict(nseq=nseq, steps=steps, n_heads=n_heads, headdim=headdim, d_state=d_state)
    params = pltpu.CompilerParams(dimension_semantics=("arbitrary",), vmem_limit_bytes=VMEM_LIMIT_BYTES)
    step_blk = lambda w: pl.BlockSpec((nseq, w), lambda l: (l, 0))

    sds = jax.ShapeDtypeStruct
    ssd_names = ['wxbc', 'wdt', 'scw', 'scb', 'dtb', 'aneg', 'dexp', 'e', 'e2']
    ssd_consts = [x_lm, cssd_lm] + [wts[k] for k in ssd_names]
    pre, c_lm, b_lm, ea_lm, ydg_lm, xw_lm = pl.pallas_call(
        functools.partial(_sample_ssd_kernel, **dims),
        grid=(steps,),
        in_specs=[step_blk(d_model)] + [_const_spec(c.shape) for c in ssd_consts],
        out_specs=(step_blk(d_xbc), step_blk(gn), step_blk(gn), step_blk(d_ssd), step_blk(d_ssd),
                   step_blk(d_ssd)),
        out_shape=(sds((ntok, d_xbc), F32), sds((ntok, gn), F32), sds((ntok, gn), F32),
                   sds((ntok, d_ssd), F32), sds((ntok, d_ssd), F32), sds((ntok, d_ssd), F32)),
        scratch_shapes=[
            pltpu.VMEM((cssd_lm.shape[0] + ntok, d_xbc), F32),
            pltpu.VMEM((ntok, d_ssd), F32),
            pltpu.VMEM((ntok, gn), F32),
            pltpu.VMEM((ntok, LANES), F32),
            pltpu.VMEM((ntok, LANES), F32),
        ],
        compiler_params=params,
        name="sample_ssd",
    )(x_lm, *ssd_consts)

    lru_names = ['wlx', 'wly', 'lcw', 'lcb', 'wa', 'ba', 'wx', 'bx', 'lam']
    lru_consts = [clru_lm, slru_lm] + [wts[k] for k in lru_names]
    prelx, ylru_lm, lst = pl.pallas_call(
        functools.partial(_sample_lru_kernel, nseq=nseq, steps=steps, start_pos=start_pos),
        grid=(steps,),
        in_specs=[step_blk(d_model)] + [_const_spec(c.shape) for c in lru_consts],
        out_specs=(step_blk(d_lru), step_blk(d_lru), pl.BlockSpec((nseq, d_lru), lambda l: (0, 0))),
        out_shape=(sds((ntok, d_lru), F32), sds((ntok, d_lru), BF16), sds((nseq, d_lru), F32)),
        scratch_shapes=[
            pltpu.VMEM((clru_lm.shape[0] + ntok, d_lru), F32),
            pltpu.VMEM((nseq, d_lru), F32),
        ],
        compiler_params=params,
        name="sample_lru",
    )(x_lm, *lru_consts)

    half = nseq // 2
    full = lambda a: _const_spec(a.shape)
    new_state, yoff_lm = pl.pallas_call(
        functools.partial(_sample_state_kernel, **dims),
        grid=(half,),
        in_specs=[pl.BlockSpec((None, 2, d_ssd, d_state), lambda j: (j, 0, 0, 0)),
                  full(c_lm), full(b_lm), full(xw_lm), full(ea_lm)],
        out_specs=(pl.BlockSpec((None, 2, d_ssd, d_state), lambda j: (j, 0, 0, 0)),
                   pl.BlockSpec((ntok, d_ssd), lambda j: (0, 0))),
        out_shape=(sds(state.shape, F32), sds((ntok, d_ssd), F32)),
        compiler_params=params,
        name="sample_state",
    )(state, c_lm, b_lm, xw_lm, ea_lm)

    post_names = ['wz', 'ng', 'wso', 'wlo', 'wgs', 'wgl', 'bg', 'wo', 'l1g', 'l1b']
    post_consts = [wts[k] for k in post_names]
    x1_lm = pl.pallas_call(
        functools.partial(_sample_post_kernel, alpha=wts['alpha']),
        grid=(steps,),
        in_specs=[step_blk(d_model), step_blk(d_ssd), step_blk(d_ssd), step_blk(d_lru)]
        + [_const_spec(c.shape) for c in post_consts],
        out_specs=step_blk(d_model),
        out_shape=sds((ntok, d_model), F32),
        compiler_params=params,
        name="sample_post",
    )(x_lm, ydg_lm, yoff_lm, ylru_lm, *post_consts)
    return x1_lm, new_state, pre, lst, prelx


def _prep_weights(w_in, b_gate, ssd_conv_w, ssd_conv_b, ssd_dt_bias, ssd_a_log, ssd_d, ssd_norm_g,
                  w_ssd_out, lru_conv_w, lru_conv_b, lru_wa, lru_ba, lru_wx, lru_bx, lru_lambda,
                  w_lru_out, w_o, ln1_g, ln1_b, ffn_w_gate, ffn_w_up, ffn_conv_w, ffn_conv_b,
                  ffn_w_down, ln2_g, ln2_b, n_heads, headdim, d_state):
    depth = w_in.shape[0]
    d_model = w_in.shape[1]
    d_ssd = n_heads * headdim
    d_xbc = d_ssd + 2 * SSD_GROUPS * d_state
    d_lru = lru_lambda.shape[1]
    sizes = (d_ssd, d_xbc, n_heads, d_lru, d_lru, d_model, d_model)
    cuts = np.cumsum((0,) + sizes)
    wi = w_in[0]
    parts = [wi[:, cuts[i]:cuts[i + 1]] for i in range(len(sizes))]
    row = lambda v: v.reshape(1, -1).astype(F32)
    mat = lambda w: _pack_rows(w.astype(BF16))
    pad_heads = lambda v: jnp.pad(v.reshape(1, -1).astype(F32), ((0, 0), (0, LANES - n_heads)))
    head_of_col = np.arange(d_ssd) // headdim
    expand = (np.arange(LANES)[:, None] == head_of_col[None, :]).astype(np.float32)
    return dict(
        n_heads=n_heads, headdim=headdim, d_state=d_state,
        alpha=float((2.0 * depth) ** 0.25),
        wz=mat(parts[0]), wxbc=mat(parts[1]),
        wdt=mat(jnp.pad(parts[2], ((0, 0), (0, LANES - n_heads)))),
        wlx=mat(parts[3]), wly=mat(parts[4]), wgs=mat(parts[5]), wgl=mat(parts[6]),
        scw=ssd_conv_w[0].astype(F32), scb=row(ssd_conv_b[0]),
        dtb=pad_heads(ssd_dt_bias[0]), aneg=pad_heads(-jnp.exp(ssd_a_log[0].astype(F32))),
        dexp=row(jnp.repeat(ssd_d[0], headdim)), ng=row(ssd_norm_g[0]),
        wso=mat(w_ssd_out[0]),
        lcw=lru_conv_w[0].astype(F32), lcb=row(lru_conv_b[0]),
        wa=mat(lru_wa[0]), ba=row(lru_ba[0]), wx=mat(lru_wx[0]), bx=row(lru_bx[0]),
        wax=mat(jnp.concatenate([lru_wa[0], lru_wx[0]], axis=-1)),
        lam=row(lru_lambda[0]), wlo=mat(w_lru_out[0]),
        bg=row(b_gate[0]), wo=mat(w_o[0]), l1g=row(ln1_g[0]), l1b=row(ln1_b[0]),
        e=_pack_rows_01(expand),
        ee=_pack_rows_01(np.concatenate([expand, expand], axis=0)),
        e2=_pack_rows_01(np.arange(SSD_GROUPS * d_state)[:, None] // d_state
                         == (head_of_col // (n_heads // SSD_GROUPS))[None, :]),
        wg=mat(ffn_w_gate[0]), wu=mat(ffn_w_up[0]),
        fcw=ffn_conv_w[0].astype(F32), fcb=row(ffn_conv_b[0]), wd=mat(ffn_w_down[0]),
        l2g=row(ln2_g[0]), l2b=row(ln2_b[0]),
    )


def kernel(x_prompt, x_sample, state_ssd, cache_ssd_conv, state_lru, cache_lru_conv, cache_ffn_conv, w_in, b_gate, ssd_conv_w, ssd_conv_b, ssd_dt_bias, ssd_a_log, ssd_d, ssd_norm_g, w_ssd_out, lru_conv_w, lru_conv_b, lru_wa, lru_ba, lru_wx, lru_bx, lru_lambda, w_lru_out, w_o, ln1_g, ln1_b, ffn_w_gate, ffn_w_up, ffn_conv_w, ffn_conv_b, ffn_w_down, ln2_g, ln2_b):
    assert w_in.shape[0] == 1, "single-layer trunk"
    _, _, n_heads, headdim, d_state = state_ssd.shape
    wts = _prep_weights(w_in, b_gate, ssd_conv_w, ssd_conv_b, ssd_dt_bias, ssd_a_log, ssd_d, ssd_norm_g,
                        w_ssd_out, lru_conv_w, lru_conv_b, lru_wa, lru_ba, lru_wx, lru_bx, lru_lambda,
                        w_lru_out, w_o, ln1_g, ln1_b, ffn_w_gate, ffn_w_up, ffn_conv_w, ffn_conv_b,
                        ffn_w_down, ln2_g, ln2_b, n_heads, headdim, d_state)
    bp = x_prompt.shape[0]
    d_ff = ffn_w_gate.shape[2]

    x1_p, p_ssd, p_ssd_buf, p_lru, p_lru_buf = _prompt_mixer(x_prompt, wts, tl=256)
    y_prompt, p_ffn_buf = _ffn_seg(x1_p, jnp.zeros((bp, ffn_conv_w.shape[1] - 1, d_ff), F32), wts, tm=512)
    p_ssd = p_ssd.reshape(1, bp, n_heads, headdim, d_state)

    nb_s, steps, _ = x_sample.shape
    half = nb_s // 2

    def to_lm(a):
        k, c = a.shape[1], a.shape[2]
        return a.reshape(half, 2, k, c).transpose(2, 1, 0, 3).reshape(k * nb_s, c)

    def from_lm(a, k):
        c = a.shape[1]
        return a.reshape(k, 2, half, c).transpose(2, 1, 0, 3).reshape(nb_s, k, c)

    d_ssd = n_heads * headdim
    x1_lm, new_state, pre, lst, prelx = _sample_mixer(
        to_lm(x_sample), to_lm(cache_ssd_conv[0]), to_lm(cache_lru_conv[0]), to_lm(state_lru[0][:, None, :]),
        state_ssd[0].reshape(half, 2, d_ssd, d_state), wts, nb_s, steps, PAST_LEN)
    y_lm, tail = _ffn(x1_lm[None], to_lm(cache_ffn_conv[0])[None], wts, tm=steps * nb_s, stride=nb_s)
    k_ssd = ssd_conv_w.shape[1] - 1
    k_lru = lru_conv_w.shape[1] - 1
    k_ffn = ffn_conv_w.shape[1] - 1
    assert steps >= max(k_ssd, k_lru, k_ffn)
    return (y_prompt, from_lm(y_lm[0], steps), p_ssd, p_ssd_buf[None], p_lru.reshape(1, bp, -1), p_lru_buf[None],
            p_ffn_buf[None],
            new_state.reshape(1, nb_s, n_heads, headdim, d_state),
            from_lm(pre[(steps - k_ssd) * nb_s:], k_ssd)[None],
            from_lm(lst, 1).reshape(1, nb_s, -1),
            from_lm(prelx[(steps - k_lru) * nb_s:], k_lru)[None],
            from_lm(tail[0], k_ffn)[None])
```

```python
import functools

import numpy as np
import jax
import jax.numpy as jnp
from jax import lax
from jax.experimental import pallas as pl
from jax.experimental.pallas import tpu as pltpu

F32 = jnp.float32
BF16 = jnp.bfloat16

SSD_GROUPS = 4
SSD_CHUNK = 128
LRU_BLOCKS = 8
LRU_C = 8.0
LN_EPS = 1e-5
RMS_EPS = 1e-6
PAST_LEN = 16384

LANES = 128
SUBLANES = 8
VMEM_LIMIT_BYTES = 60 * 1024 * 1024

HIST = 8
ROW_BLK = 32
UNROLL = True


def _dot(a, b):
    if b.dtype == jnp.uint32:
        b = pltpu.bitcast(b, BF16)
    return jnp.dot(a, b, preferred_element_type=F32)


def _pack_rows_01(m):
    bits = np.ascontiguousarray(m, np.float32).view(np.uint32) >> 16
    return jnp.asarray(bits[0::2] | (bits[1::2] << 16), jnp.uint32)


def _dot_nt(a, b):
    return lax.dot_general(a, b, (((1,), (1,)), ((), ())), preferred_element_type=F32)


def _dot_tn(a, b):
    return lax.dot_general(a, b, (((0,), (0,)), ((), ())), preferred_element_type=F32)


def _split3(v):
    hi = v.astype(BF16)
    r1 = v - hi.astype(F32)
    mid = r1.astype(BF16)
    lo = (r1 - mid.astype(F32)).astype(BF16)
    return hi, mid, lo


def _dot_exact_rhs(v, m):
    hi, mid, lo = _split3(v)
    return _dot(hi, m) + _dot(mid, m) + _dot(lo, m)


def _dot_exact_lhs(m, v):
    hi, mid, lo = _split3(v)
    return _dot(m, hi) + _dot(m, mid) + _dot(m, lo)


def _split2(v):
    hi = v.astype(BF16)
    lo = (v - hi.astype(F32)).astype(BF16)
    return hi, lo


def _dot_2way_rhs(v, m2):
    return _dot(jnp.concatenate(_split2(v), axis=1), m2)


def _dot_2way_lhs(m2, v):
    return _dot(m2, jnp.concatenate(_split2(v), axis=0))


def _softplus(x):
    return jnp.maximum(x, 0.0) + jnp.log1p(jnp.exp(-jnp.abs(x)))


def _sigmoid(x):
    return 0.5 * jnp.tanh(0.5 * x) + 0.5


def _silu(x):
    h = 0.5 * x
    return h + h * jnp.tanh(h)


def _gelu(x):
    c = np.sqrt(2.0 / np.pi).astype(np.float32)
    return 0.5 * x * (1.0 + jnp.tanh(c * (x + 0.044715 * (x * x * x))))


def _layer_norm(v, g, b):
    mu = jnp.mean(v, axis=-1, keepdims=True)
    d = v - mu
    var = jnp.mean(d * d, axis=-1, keepdims=True)
    return d * lax.rsqrt(var + LN_EPS) * g + b


def _rows(i, n):
    if isinstance(i, int):
        return slice(i * n, (i + 1) * n)
    return pl.ds(pl.multiple_of(i * n, n), n)


def _loop(n, body, unroll):
    if unroll:
        for i in range(n):
            body(i, None)
    else:
        lax.fori_loop(0, n, lambda i, c: (body(i, c), c)[1], 0)


def _conv_block(buf_ref, w_ref, b_ref, hist, stride, r0, rows, c0, cw):
    taps = w_ref.shape[0]
    acc = b_ref[:, c0:c0 + cw]
    for k in range(taps):
        off = hist + r0 - (taps - 1 - k) * stride
        if not isinstance(off, int):
            off = pl.multiple_of(off, SUBLANES)
        acc = acc + w_ref[k:k + 1, c0:c0 + cw] * buf_ref[pl.ds(off, rows), c0:c0 + cw]
    return acc


LOG2_SUBLANES = 3
NPOS = SSD_CHUNK // SUBLANES
LOG2_NPOS = 4
assert 1 << LOG2_SUBLANES == SUBLANES and 1 << LOG2_NPOS == NPOS


def _tok_of_row(r):
    return (r & (SUBLANES - 1)) * NPOS + lax.shift_right_logical(r, LOG2_SUBLANES)


def _row_of_tok(t):
    return (t & (NPOS - 1)) * SUBLANES + lax.shift_right_logical(t, LOG2_NPOS)


def _perm_matrix():
    q = SSD_CHUNK
    r = lax.broadcasted_iota(jnp.int32, (q, q), 0)
    c = lax.broadcasted_iota(jnp.int32, (q, q), 1)
    return jnp.where(c == _tok_of_row(r), 1.0, 0.0).astype(BF16)


def _unperm_matrix():
    q = SSD_CHUNK
    t = lax.broadcasted_iota(jnp.int32, (q, q), 0)
    r = lax.broadcasted_iota(jnp.int32, (q, q), 1)
    return jnp.where(r == _row_of_tok(t), 1.0, 0.0).astype(BF16)


def _fill_wrap(buf_ref, wrap_ref, c, wrap, tail_ref):
    q = SSD_CHUNK
    ncol = buf_ref.shape[1]
    sub0 = lax.broadcasted_iota(jnp.int32, (SUBLANES, ncol), 0) == 0
    for m in range(wrap // SUBLANES):
        r_cur = (c + 1) * q - wrap + m * SUBLANES
        cur = buf_ref[r_cur:r_cur + SUBLANES, :]
        if c == 0:
            prv = tail_ref[m * SUBLANES:(m + 1) * SUBLANES, :]
        else:
            prv = buf_ref[r_cur - q:r_cur - q + SUBLANES, :]
        wrap_ref[c * wrap + m * SUBLANES:c * wrap + (m + 1) * SUBLANES, :] = jnp.where(
            sub0, pltpu.roll(prv, 1, 0), pltpu.roll(cur, 1, 0))


def _conv_seg(buf_ref, wrap_ref, w_ref, b_ref, c, wrap, c0, cw):
    q = SSD_CHUNK
    taps = w_ref.shape[0]
    cols = slice(c0, c0 + cw)
    acc = b_ref[:, cols] + w_ref[taps - 1:taps, cols] * buf_ref[c * q:(c + 1) * q, cols]
    for k in range(taps - 1):
        back = (taps - 1 - k) * SUBLANES
        shifted = jnp.concatenate(
            [wrap_ref[(c + 1) * wrap - back:(c + 1) * wrap, cols], buf_ref[c * q:(c + 1) * q - back, cols]],
            axis=0)
        acc = acc + w_ref[k:k + 1, cols] * shifted
    return acc


def _seg_tail_rows(wrap):
    n = wrap // SUBLANES
    return [(NPOS - n + m) * SUBLANES + SUBLANES - 1 for m in range(n)]


def _prompt_mixer_kernel(
        x_ref, wz_ref, wxbc_ref, wdt_ref, wlx_ref, wly_ref, wgs_ref, wgl_ref,
        scw_ref, scb_ref, dtb_ref, aneg_ref, dexp_ref, ng_ref, wso_ref,
        lcw_ref, lcb_ref, wax_ref, ba_ref, bx_ref, lam_ref, wlo_ref,
        bg_ref, wo_ref, l1g_ref, l1b_ref, e_ref,
        x1_ref, st_ref, sconv_ref, lst_ref, lconv_ref,
        xb_s, xp_s, xbc_s, swrap_s, stail_s, xc_s, lx_s, lwrap_s, ltail_s, xr_s, ly_s, y_s, z_s, ysb_s, ylb_s,
        ht_s, hl_s, gs_s, gl_s, ys_s, yl_s, o_s, mb_s,
        *, tl, alpha, n_heads, headdim, d_state):
    t = pl.program_id(1)
    nt = pl.num_programs(1)
    d_ssd = n_heads * headdim
    gn = SSD_GROUPS * d_state
    hpg = n_heads // SSD_GROUPS
    gw = hpg * headdim
    d_lru = lx_s.shape[1]
    d_model = x_ref.shape[2]
    q = SSD_CHUNK

    nch = tl // q
    wrap_s = (scw_ref.shape[0] - 1) * SUBLANES
    wrap_l = (lcw_ref.shape[0] - 1) * SUBLANES

    @pl.when(t == 0)
    def _():
        stail_s[...] = jnp.zeros(stail_s.shape, F32)
        ltail_s[...] = jnp.zeros(ltail_s.shape, F32)
        ht_s[...] = jnp.zeros(ht_s.shape, F32)
        hl_s[...] = jnp.zeros(hl_s.shape, F32)

    perm = _perm_matrix()
    perm2 = jnp.concatenate([perm, perm], axis=1)
    for c in range(nch):
        rows = _rows(c, q)
        xp = _dot_2way_lhs(perm2, x_ref[0, rows, :])
        xp_s[rows, :] = xp
        xb_s[rows, :] = xp.astype(BF16)

    cwid = 512

    def proj(dst_ref, w_ref, c0, cw):
        dst_ref[:, c0:c0 + cw] = _dot(xb_s[...], w_ref[:, c0:c0 + cw])

    def lru_conv(c):
        _fill_wrap(lx_s, lwrap_s, c, wrap_l, ltail_s)
        for c0 in range(0, d_lru, cwid):
            xr_s[c * q:(c + 1) * q, c0:c0 + cwid] = _conv_seg(lx_s, lwrap_s, lcw_ref, lcb_ref, c, wrap_l, c0, cwid)
        if c == nch - 1:
            ltail_s[...] = lx_s[tl - wrap_l:tl, :]

    bw = d_lru // LRU_BLOCKS
    sub = lax.broadcasted_iota(jnp.int32, (SUBLANES, bw), 0)
    crow = lax.broadcasted_iota(jnp.int32, (q, bw), 0)

    def lru_block(n):
        cols = slice(n * bw, (n + 1) * bw)
        xr = xr_s[:, cols]
        xrb = xr.astype(BF16)
        rg = _dot(xrb, wax_ref[n])
        r = _sigmoid(rg[:, 0:bw] + ba_ref[:, cols])
        gi = _sigmoid(rg[:, bw:2 * bw] + bx_ref[:, cols])
        log_a = (-LRU_C) * r * _softplus(-lam_ref[:, cols])
        a_all = jnp.exp(log_a)
        mult_all = jnp.sqrt(1.0 - jnp.exp(2.0 * log_a))
        for c in range(nch):
            a = a_all[c * q:(c + 1) * q, :]
            mult = mult_all[c * q:(c + 1) * q, :]
            if c == 0:
                mult = jnp.where(jnp.logical_and(crow == 0, t == 0), 1.0, mult)
            u = mult * gi[c * q:(c + 1) * q, :] * xr[c * q:(c + 1) * q, :]
            a_p = [a[i * SUBLANES:(i + 1) * SUBLANES, :] for i in range(NPOS)]
            u_p = [u[i * SUBLANES:(i + 1) * SUBLANES, :] for i in range(NPOS)]
            h = u_p[0]
            g = a_p[0]
            for i in range(1, NPOS):
                h = a_p[i] * h + u_p[i]
                g = a_p[i] * g
            gs = jnp.where(sub == 0, 0.0, pltpu.roll(g, 1, 0))
            hs = jnp.where(sub == 0, hl_s[0:1, cols], pltpu.roll(h, 1, 0))
            d = 1
            while d < SUBLANES:
                keep = sub >= d
                hs = jnp.where(keep, gs * pltpu.roll(hs, d, 0) + hs, hs)
                gs = jnp.where(keep, gs * pltpu.roll(gs, d, 0), gs)
                d *= 2
            h = hs
            out = []
            for i in range(NPOS):
                h = a_p[i] * h + u_p[i]
                out.append(h)
            hl_s[0:1, cols] = h[SUBLANES - 1:SUBLANES, :]
            hseq = jnp.concatenate(out, axis=0)
            ylb_s[c * q:(c + 1) * q, cols] = (hseq * _gelu(ly_s[c * q:(c + 1) * q, cols])).astype(BF16)

    def ssd_wrap(c):
        _fill_wrap(xbc_s, swrap_s, c, wrap_s, stail_s)
        if c == nch - 1:
            stail_s[...] = xbc_s[tl - wrap_s:tl, :]

    def ssd_conv(c, c0):
        xc_s[c * q:(c + 1) * q, c0:c0 + cwid] = _silu(
            _conv_seg(xbc_s, swrap_s, scw_ref, scb_ref, c, wrap_s, c0, cwid))

    def merge_gate(dst_ref, c0, cw, b0):
        dst_ref[:, c0:c0 + cw] = _sigmoid(dst_ref[:, c0:c0 + cw] + bg_ref[:, b0 + c0:b0 + c0 + cw])

    tok_r = _tok_of_row(lax.broadcasted_iota(jnp.int32, (q, q), 0))
    tok_c = _tok_of_row(lax.broadcasted_iota(jnp.int32, (q, q), 1))
    causal = tok_r >= tok_c
    tri = jnp.where(causal, 1.0, 0.0).astype(BF16)
    lane_i = lax.broadcasted_iota(jnp.int32, (q, LANES), 1)
    left = lane_i < headdim

    def chunk_body(c, carry):
        rows = _rows(c, q)
        dt = _softplus(_dot(xb_s[rows, :], wdt_ref[...]) + dtb_ref[...])
        d_a = dt * aneg_ref[...]
        a_cs = _dot_exact_lhs(tri, d_a)
        a_last = a_cs[q - 1:q, :]
        wgt = dt * jnp.exp(a_last - a_cs)
        ea = jnp.exp(a_cs)
        w_exp = _dot_2way_rhs(wgt, e_ref[...])
        ea_exp = _dot_2way_rhs(ea, e_ref[...])
        a_cs_t = a_cs.T
        dt_t = dt.T
        for g in range(SSD_GROUPS):
            b_g = xc_s[rows, d_ssd + g * d_state:d_ssd + (g + 1) * d_state]
            c_g = xc_s[rows, d_ssd + gn + g * d_state:d_ssd + gn + (g + 1) * d_state]
            b_gb = b_g.astype(BF16)
            c_gb = c_g.astype(BF16)
            cb = _dot_nt(c_gb, b_gb)
            for hp in range(hpg // 2):
                c0 = g * gw + hp * 2 * headdim
                xs_pair = xc_s[rows, c0:c0 + 2 * headdim]
                lmats = []
                for j in range(2):
                    h = g * hpg + hp * 2 + j
                    seg = (jnp.broadcast_to(a_cs[:, h:h + 1], (q, q))
                           - jnp.broadcast_to(a_cs_t[h:h + 1, :], (q, q)))
                    dec = jnp.exp(jnp.where(causal, seg, -jnp.inf))
                    lmats.append((cb * dec * jnp.broadcast_to(dt_t[h:h + 1, :], (q, q))).astype(BF16))
                lpair = jnp.concatenate(lmats, axis=1)
                rhs = jnp.concatenate([jnp.where(left, xs_pair, 0.0),
                                       jnp.where(left, 0.0, xs_pair)], axis=0).astype(BF16)
                y_s[rows, c0:c0 + 2 * headdim] = _dot(lpair, rhs)
            gcols = slice(g * gw, (g + 1) * gw)
            h_g = ht_s[:, gcols]
            y_off = _dot(c_gb, h_g.astype(BF16)) * ea_exp[:, gcols]
            y_s[rows, gcols] = y_s[rows, gcols] + y_off
            xw = (xc_s[rows, gcols] * w_exp[:, gcols]).astype(BF16)
            ht_s[:, gcols] = h_g * ea_exp[q - 1:q, gcols] + _dot_tn(b_gb, xw)
            if carry is not None and g < len(carry):
                carry[g]()
        return carry

    def gate_body(i, carry):
        rows = _rows(i, ROW_BLK)
        y = y_s[rows, :] + dexp_ref[...] * xc_s[rows, 0:d_ssd]
        y = y * _silu(z_s[rows, :])
        ms = jnp.mean(y * y, axis=-1, keepdims=True)
        ysb_s[rows, :] = (y * lax.rsqrt(ms + RMS_EPS) * ng_ref[...]).astype(BF16)
        return carry

    assert nch == 2 and LRU_BLOCKS == 8 and d_lru == 2 * cwid and d_model == 2 * cwid
    d_xbc = xc_s.shape[1]
    nxb = d_xbc // cwid
    proj(lx_s, wlx_ref, 0, cwid)
    proj(lx_s, wlx_ref, cwid, cwid)
    proj(ly_s, wly_ref, 0, cwid); lru_conv(0)
    proj(ly_s, wly_ref, cwid, cwid); lru_conv(1)
    P = functools.partial
    mxu_a = ([P(proj, xbc_s, wxbc_ref, j * cwid, cwid) for j in range(nxb)]
             + [P(proj, gs_s, wgs_ref, j * cwid, cwid) for j in range(2)])
    for n in range(LRU_BLOCKS):
        mxu_a[n]()
        lru_block(n)
    mxu_b = ([P(proj, gl_s, wgl_ref, j * cwid, cwid) for j in range(2)]
             + [P(proj, z_s, wz_ref, j * cwid, cwid) for j in range(d_ssd // cwid)])
    ssd_wrap(0)
    ssd_wrap(1)
    conv = [P(ssd_conv, c, c0) for c in range(nch) for c0 in range(0, d_xbc, cwid)]
    per = len(conv) // len(mxu_b)
    for i, m in enumerate(mxu_b):
        m()
        for v in conv[i * per:(i + 1) * per]:
            v()
    gates_per_chunk = q // ROW_BLK
    chunk_body(0, None)
    merge_gate(gs_s, 0, d_model, 0)
    for i in range(gates_per_chunk):
        gate_body(i, None)
    chunk_body(1, None)
    merge_gate(gl_s, 0, d_model, d_model)
    for j in range(2):
        yl_s[:, j * cwid:(j + 1) * cwid] = _dot(ylb_s[...], wlo_ref[:, j * cwid:(j + 1) * cwid])
        for i in range(gates_per_chunk + 2 * j, gates_per_chunk + 2 * j + 2):
            gate_body(i, None)

    half = d_model
    ys_s[...] = _dot(ysb_s[...], wso_ref[...])

    def merge_body(i, carry):
        rows = _rows(i, ROW_BLK)
        mb_s[rows, :] = (gs_s[rows, :] * ys_s[rows, :] + gl_s[rows, :] * yl_s[rows, :]).astype(BF16)
        return carry

    _loop(tl // ROW_BLK, merge_body, UNROLL)
    o_s[...] = _dot(mb_s[...], wo_ref[...])

    def ln_body(i, carry):
        rows = _rows(i, ROW_BLK)
        v = alpha * xp_s[rows, :] + o_s[rows, :]
        x1_ref[0, rows, :] = _layer_norm(v, l1g_ref[...], l1b_ref[...])
        return carry

    _loop(tl // ROW_BLK, ln_body, UNROLL)

    @pl.when(t == nt - 1)
    def _():
        st_ref[0] = ht_s[...].T
        lst_ref[0] = hl_s[0:1, :]
        for m, r in enumerate(_seg_tail_rows(wrap_s)):
            sconv_ref[0, m:m + 1, :] = xbc_s[tl - q + r:tl - q + r + 1, :]
        for m, r in enumerate(_seg_tail_rows(wrap_l)):
            lconv_ref[0, m:m + 1, :] = lx_s[tl - q + r:tl - q + r + 1, :]


def _const_spec(shape):
    nd = len(shape)
    return pl.BlockSpec(shape, lambda *_: (0,) * nd, pipeline_mode=pl.Buffered(1))


def _prompt_mixer(x, wts, tl):
    nb, seq, d_model = x.shape
    n_heads, headdim, d_state = wts['n_heads'], wts['headdim'], wts['d_state']
    d_ssd = n_heads * headdim
    d_xbc = d_ssd + 2 * SSD_GROUPS * d_state
    d_lru = wts['wlx'].shape[1]
    names = ['wz', 'wxbc', 'wdt', 'wlx', 'wly', 'wgs', 'wgl', 'scw', 'scb', 'dtb', 'aneg', 'dexp', 'ng',
             'wso', 'lcw', 'lcb', 'wax', 'ba', 'bx', 'lam', 'wlo', 'bg', 'wo', 'l1g', 'l1b', 'ee']
    consts = [wts[k] for k in names]
    kern = functools.partial(_prompt_mixer_kernel, tl=tl, alpha=wts['alpha'], n_heads=n_heads,
                             headdim=headdim, d_state=d_state)
    out_shape = (
        jax.ShapeDtypeStruct((nb, seq, d_model), F32),
        jax.ShapeDtypeStruct((nb, d_ssd, d_state), F32),
        jax.ShapeDtypeStruct((nb, 3, d_xbc), F32),
        jax.ShapeDtypeStruct((nb, 1, d_lru), F32),
        jax.ShapeDtypeStruct((nb, 3, d_lru), F32),
    )
    out_specs = (
        pl.BlockSpec((1, tl, d_model), lambda b, t: (b, t, 0)),
        pl.BlockSpec((1, d_ssd, d_state), lambda b, t: (b, 0, 0)),
        pl.BlockSpec((1, 3, d_xbc), lambda b, t: (b, 0, 0)),
        pl.BlockSpec((1, 1, d_lru), lambda b, t: (b, 0, 0)),
        pl.BlockSpec((1, 3, d_lru), lambda b, t: (b, 0, 0)),
    )
    nch = tl // SSD_CHUNK
    wrap_s = (wts['scw'].shape[0] - 1) * SUBLANES
    wrap_l = (wts['lcw'].shape[0] - 1) * SUBLANES
    scratch = [
        pltpu.VMEM((tl, d_model), BF16),
        pltpu.VMEM((tl, d_model), F32),
        pltpu.VMEM((tl, d_xbc), F32),
        pltpu.VMEM((nch * wrap_s, d_xbc), F32),
        pltpu.VMEM((wrap_s, d_xbc), F32),
        pltpu.VMEM((tl, d_xbc), F32),
        pltpu.VMEM((tl, d_lru), F32),
        pltpu.VMEM((nch * wrap_l, d_lru), F32),
        pltpu.VMEM((wrap_l, d_lru), F32),
        pltpu.VMEM((tl, d_lru), F32),
        pltpu.VMEM((tl, d_lru), F32),
        pltpu.VMEM((tl, d_ssd), F32),
        pltpu.VMEM((tl, d_ssd), F32),
        pltpu.VMEM((tl, d_ssd), BF16),
        pltpu.VMEM((tl, d_lru), BF16),
        pltpu.VMEM((d_state, d_ssd), F32),
        pltpu.VMEM((SUBLANES, d_lru), F32),
        pltpu.VMEM((tl, d_model), F32),
        pltpu.VMEM((tl, d_model), F32),
        pltpu.VMEM((tl, d_model), F32),
        pltpu.VMEM((tl, d_model), F32),
        pltpu.VMEM((tl, d_model), F32),
        pltpu.VMEM((tl, d_model), BF16),
    ]
    return pl.pallas_call(
        kern,
        grid=(nb, seq // tl),
        in_specs=[pl.BlockSpec((1, tl, d_model), lambda b, t: (b, t, 0))]
        + [_const_spec(c.shape) for c in consts],
        out_specs=out_specs,
        out_shape=out_shape,
        scratch_shapes=scratch,
        compiler_params=pltpu.CompilerParams(
            dimension_semantics=("arbitrary", "arbitrary"),
            vmem_limit_bytes=VMEM_LIMIT_BYTES),
        name="prompt_mixer",
    )(x, *consts)


def _ffn_kernel(x_ref, h0_ref, wg_ref, wu_ref, cw_ref, cb_ref, wd_ref, g_ref, b_ref,
                y_ref, tail_ref, xb_s, gb_s, hb_s, *, tm, stride, hist, alpha, fchunk):
    t = pl.program_id(1)
    nt = pl.num_programs(1)
    taps = cw_ref.shape[0]
    nh = (taps - 1) * stride
    d_ff = gb_s.shape[1]

    @pl.when(t == 0)
    def _():
        gb_s[hist - nh:hist, :] = h0_ref[0]

    xb_s[...] = x_ref[0].astype(BF16)
    acc = None
    for c0 in range(0, d_ff, fchunk):
        cols = slice(c0, c0 + fchunk)
        gb_s[hist:hist + tm, cols] = _dot(xb_s[...], wg_ref[:, cols])
        up = _dot(xb_s[...], wu_ref[:, cols])
        gc = _conv_block(gb_s, cw_ref, cb_ref, hist, stride, 0, tm, c0, fchunk)
        hb_s[...] = (_gelu(gc) * up).astype(BF16)
        part = _dot(hb_s[...], wd_ref[cols, :])
        acc = part if acc is None else acc + part
    v = alpha * x_ref[0] + acc
    y_ref[0] = _layer_norm(v, g_ref[...], b_ref[...])
    gb_s[hist - nh:hist, :] = gb_s[hist + tm - nh:hist + tm, :]

    @pl.when(t == nt - 1)
    def _():
        tail_ref[0] = gb_s[hist - nh:hist, :]


def _ffn_seg_kernel(x_ref, h0_ref, wg_ref, wu_ref, cw_ref, cb_ref, wd_ref, g_ref, b_ref,
                    y_ref, tail_ref, xb_s, gb_s, gwrap_s, gtail_s, hb_s, *, tm, alpha, fchunk):
    t = pl.program_id(1)
    nt = pl.num_programs(1)
    q = SSD_CHUNK
    nch = tm // q
    wrap = (cw_ref.shape[0] - 1) * SUBLANES
    d_ff = gb_s.shape[1]
    tail_rows = _seg_tail_rows(wrap)

    @pl.when(t == 0)
    def _():
        gtail_s[...] = jnp.zeros(gtail_s.shape, F32)
        for m in range(len(tail_rows)):
            r = m * SUBLANES + SUBLANES - 1
            gtail_s[r:r + 1, :] = h0_ref[0, m:m + 1, :]

    xb_s[...] = x_ref[0].astype(BF16)
    gb_s[...] = _dot(xb_s[...], wg_ref[...])
    for c in range(nch):
        _fill_wrap(gb_s, gwrap_s, c, wrap, gtail_s)
    acc = None
    for c0 in range(0, d_ff, fchunk):
        cols = slice(c0, c0 + fchunk)
        up = _dot(xb_s[...], wu_ref[:, cols])
        for c in range(nch):
            gc = _conv_seg(gb_s, gwrap_s, cw_ref, cb_ref, c, wrap, c0, fchunk)
            hb_s[c * q:(c + 1) * q, :] = (_gelu(gc) * up[c * q:(c + 1) * q, :]).astype(BF16)
        part = _dot(hb_s[...], wd_ref[cols, :])
        acc = part if acc is None else acc + part
    gtail_s[...] = gb_s[tm - wrap:tm, :]

    unperm = _unperm_matrix()
    unperm2 = jnp.concatenate([unperm, unperm], axis=1)
    for c in range(nch):
        rows = slice(c * q, (c + 1) * q)
        y = _layer_norm(alpha * x_ref[0, rows, :] + acc[rows, :], g_ref[...], b_ref[...])
        y_ref[0, rows, :] = _dot_2way_lhs(unperm2, y)

    @pl.when(t == nt - 1)
    def _():
        for m, r in enumerate(tail_rows):
            tail_ref[0, m:m + 1, :] = gb_s[tm - q + r:tm - q + r + 1, :]


def _ffn_seg(x, hist0, wts, tm):
    nb, seq, d_model = x.shape
    d_ff = wts['wg'].shape[1]
    taps = wts['fcw'].shape[0]
    wrap = (taps - 1) * SUBLANES
    fchunk = 1024
    consts = [wts[k] for k in ['wg', 'wu', 'fcw', 'fcb', 'wd', 'l2g', 'l2b']]
    kern = functools.partial(_ffn_seg_kernel, tm=tm, alpha=wts['alpha'], fchunk=fchunk)
    return pl.pallas_call(
        kern,
        grid=(nb, seq // tm),
        in_specs=[pl.BlockSpec((1, tm, d_model), lambda b, t: (b, t, 0)),
                  pl.BlockSpec((1, taps - 1, d_ff), lambda b, t: (b, 0, 0))]
        + [_const_spec(c.shape) for c in consts],
        out_specs=(pl.BlockSpec((1, tm, d_model), lambda b, t: (b, t, 0)),
                   pl.BlockSpec((1, taps - 1, d_ff), lambda b, t: (b, 0, 0))),
        out_shape=(jax.ShapeDtypeStruct((nb, seq, d_model), F32),
                   jax.ShapeDtypeStruct((nb, taps - 1, d_ff), F32)),
        scratch_shapes=[pltpu.VMEM((tm, d_model), BF16),
                        pltpu.VMEM((tm, d_ff), F32),
                        pltpu.VMEM((tm // SSD_CHUNK * wrap, d_ff), F32),
                        pltpu.VMEM((wrap, d_ff), F32),
                        pltpu.VMEM((tm, fchunk), BF16)],
        compiler_params=pltpu.CompilerParams(
            dimension_semantics=("arbitrary", "arbitrary"),
            vmem_limit_bytes=VMEM_LIMIT_BYTES),
        name="conv_ffn_seg",
    )(x, hist0, *consts)


def _ffn(x, hist0, wts, tm, stride):
    nb, seq, d_model = x.shape
    d_ff = wts['wg'].shape[1]
    taps = wts['fcw'].shape[0]
    nh = (taps - 1) * stride
    hist = -(-nh // SUBLANES) * SUBLANES
    consts = [wts[k] for k in ['wg', 'wu', 'fcw', 'fcb', 'wd', 'l2g', 'l2b']]
    kern = functools.partial(_ffn_kernel, tm=tm, stride=stride, hist=hist, alpha=wts['alpha'], fchunk=1024)
    return pl.pallas_call(
        kern,
        grid=(nb, seq // tm),
        in_specs=[pl.BlockSpec((1, tm, d_model), lambda b, t: (b, t, 0)),
                  pl.BlockSpec((1, nh, d_ff), lambda b, t: (b, 0, 0))]
        + [_const_spec(c.shape) for c in consts],
        out_specs=(pl.BlockSpec((1, tm, d_model), lambda b, t: (b, t, 0)),
                   pl.BlockSpec((1, nh, d_ff), lambda b, t: (b, 0, 0))),
        out_shape=(jax.ShapeDtypeStruct((nb, seq, d_model), F32),
                   jax.ShapeDtypeStruct((nb, nh, d_ff), F32)),
        scratch_shapes=[pltpu.VMEM((tm, d_model), BF16),
                        pltpu.VMEM((hist + tm, d_ff), F32),
                        pltpu.VMEM((tm, 1024), BF16)],
        compiler_params=pltpu.CompilerParams(
            dimension_semantics=("arbitrary", "arbitrary"),
            vmem_limit_bytes=VMEM_LIMIT_BYTES),
        name="conv_ffn",
    )(x, hist0, *consts)


def _sample_ssd_kernel(
        x_ref, xall_ref, cssd_ref, wxbc_ref, wdt_ref, scw_ref, scb_ref, dtb_ref, aneg_ref, dexp_ref,
        e_ref, e2_ref,
        pre_ref, c_ref, b_ref, ea_ref, ydg_ref, xw_ref,
        xbc_s, xs_s, bs_s, acs_s, dts_s,
        *, nseq, steps, n_heads, headdim, d_state):
    l = pl.program_id(0)
    d_ssd = n_heads * headdim
    gn = SSD_GROUPS * d_state
    hist = (scw_ref.shape[0] - 1) * nseq
    r0 = l * nseq

    def blk(i):
        return pl.ds(pl.multiple_of(i * nseq, nseq), nseq)

    def sblk(i):
        return slice(i * nseq, (i + 1) * nseq)

    @pl.when(l == 0)
    def _():
        xbc_s[0:hist, :] = cssd_ref[...]
        dts_s[...] = _softplus(_dot(xall_ref[...].astype(BF16), wdt_ref[...]) + dtb_ref[...])
        acc = jnp.zeros((nseq, LANES), F32)
        for s in range(steps):
            acc = acc + dts_s[sblk(s), :] * aneg_ref[...]
            acs_s[sblk(s), :] = acc

    xb = x_ref[...].astype(BF16)

    pre = _dot(xb, wxbc_ref[...])
    pre_ref[...] = pre
    xbc_s[pl.ds(pl.multiple_of(hist + r0, nseq), nseq), :] = pre
    cwid = 512
    for c0 in range(0, d_ssd, cwid):
        xs_s[blk(l), c0:c0 + cwid] = _silu(
            _conv_block(xbc_s, scw_ref, scb_ref, hist, nseq, r0, nseq, c0, cwid))
    b_l = _silu(_conv_block(xbc_s, scw_ref, scb_ref, hist, nseq, r0, nseq, d_ssd, gn))
    c_l = _silu(_conv_block(xbc_s, scw_ref, scb_ref, hist, nseq, r0, nseq, d_ssd + gn, gn))
    bs_s[blk(l), :] = b_l
    b_ref[...] = b_l
    c_ref[...] = c_l

    a_cs = acs_s[blk(l), :]
    dt = dts_s[blk(l), :]
    a_end = acs_s[sblk(steps - 1), :]
    ea_ref[...] = _dot_exact_rhs(jnp.exp(a_cs), e_ref[...])
    xw_ref[...] = xs_s[blk(l), :] * _dot_exact_rhs(dt * jnp.exp(a_end - a_cs), e_ref[...])

    ydg_ref[...] = dexp_ref[...] * xs_s[blk(l), :]
    for s in range(steps):
        @pl.when(s <= l)
        def _(s=s):
            coef = _dot_exact_rhs(jnp.exp(a_cs - acs_s[sblk(s), :]) * dts_s[sblk(s), :], e_ref[...])
            cbx = _dot_exact_rhs(bs_s[sblk(s), :] * c_l, e2_ref[...])
            ydg_ref[...] += cbx * coef * xs_s[sblk(s), :]


def _sample_lru_kernel(
        x_ref, clru_ref, slru_ref, wlx_ref, wly_ref, lcw_ref, lcb_ref, wa_ref, ba_ref, wx_ref, bx_ref,
        lam_ref, prelx_ref, ylru_ref, lst_ref, lx_s, hl_s, *, nseq, steps, start_pos):
    l = pl.program_id(0)
    hist = (lcw_ref.shape[0] - 1) * nseq
    d_lru = lx_s.shape[1]
    r0 = l * nseq

    @pl.when(l == 0)
    def _():
        lx_s[0:hist, :] = clru_ref[...]
        hl_s[...] = slru_ref[...]

    xb = x_ref[...].astype(BF16)
    prelx = _dot(xb, wlx_ref[...])
    prelx_ref[...] = prelx
    lx_s[pl.ds(pl.multiple_of(hist + r0, nseq), nseq), :] = prelx
    ly = _dot(xb, wly_ref[...])
    bw = d_lru // LRU_BLOCKS
    first = (l + start_pos) == 0
    for n in range(LRU_BLOCKS):
        cols = slice(n * bw, (n + 1) * bw)
        xr = _conv_block(lx_s, lcw_ref, lcb_ref, hist, nseq, r0, nseq, n * bw, bw)
        xrb = xr.astype(BF16)
        r = _sigmoid(_dot(xrb, wa_ref[n]) + ba_ref[:, cols])
        gi = _sigmoid(_dot(xrb, wx_ref[n]) + bx_ref[:, cols])
        log_a = (-LRU_C) * r * _softplus(-lam_ref[:, cols])
        a = jnp.exp(log_a)
        mult = jnp.where(first, 1.0, jnp.sqrt(1.0 - jnp.exp(2.0 * log_a)))
        h = a * hl_s[:, cols] + mult * gi * xr
        hl_s[:, cols] = h
        ylru_ref[:, cols] = (h * _gelu(ly[:, cols])).astype(BF16)

    @pl.when(l == steps - 1)
    def _():
        lst_ref[...] = hl_s[...]


def _sample_state_kernel(st_ref, c_ref, b_ref, xw_ref, ea_ref, nst_ref, yoff_ref,
                         *, nseq, steps, n_heads, headdim, d_state):
    j = pl.program_id(0)
    half = nseq // 2
    nrow = 2 * steps
    hpg = n_heads // SSD_GROUPS
    gw = hpg * headdim

    def gather(ref, cols):
        return jnp.concatenate([ref[pl.ds(j + k * half, 1), cols] for k in range(nrow)], axis=0)

    c8 = gather(c_ref, slice(None))
    b8 = gather(b_ref, slice(None))
    par = lax.broadcasted_iota(jnp.int32, (nrow, gw), 0) % 2
    assert 2 * headdim == LANES and d_state == LANES
    low = lax.broadcasted_iota(jnp.int32, (nrow, LANES), 1) < headdim
    for g in range(SSD_GROUPS):
        gcols = slice(g * gw, (g + 1) * gw)
        c8g = c8[:, g * d_state:(g + 1) * d_state].astype(BF16)
        b8g = b8[:, g * d_state:(g + 1) * d_state].astype(BF16)
        xw8 = gather(xw_ref, gcols)
        ea8 = gather(ea_ref, gcols)
        cds = []
        for hp in range(hpg // 2):
            pair = ea8[:, hp * LANES:(hp + 1) * LANES]
            swapped = pltpu.roll(pair, headdim, 1)
            cds.append(jnp.where(low, pair, swapped))
            cds.append(jnp.where(low, swapped, pair))
        yo = None
        for e in range(2):
            sg = st_ref[e, gcols, :]
            yo_e = _dot_nt(c8g, sg.astype(BF16))
            yo = yo_e if e == 0 else jnp.where(par == e, yo_e, yo)
            xw_e = jnp.where(par == e, xw8, 0.0).astype(BF16)
            upd = _dot_tn(xw_e, b8g)
            k_last = 2 * (steps - 1) + e
            for hh in range(hpg):
                cd = cds[hh][k_last:k_last + 1, :]
                hr = slice(hh * headdim, (hh + 1) * headdim)
                nst_ref[e, g * gw + hh * headdim:g * gw + (hh + 1) * headdim, :] = sg[hr, :] * cd + upd[hr, :]
        yo = yo * ea8
        for k in range(nrow):
            yoff_ref[pl.ds(j + k * half, 1), gcols] = yo[k:k + 1, :]


def _sample_post_kernel(x_ref, ydg_ref, yoff_ref, ylru_ref, wz_ref, ng_ref, wso_ref, wlo_ref,
                        wgs_ref, wgl_ref, bg_ref, wo_ref, l1g_ref, l1b_ref, x1_ref, *, alpha):
    d_model = x_ref.shape[1]
    xb = x_ref[...].astype(BF16)
    y = (ydg_ref[...] + yoff_ref[...]) * _silu(_dot(xb, wz_ref[...]))
    ms = jnp.mean(y * y, axis=-1, keepdims=True)
    ysb = (y * lax.rsqrt(ms + RMS_EPS) * ng_ref[...]).astype(BF16)
    g_ssd = _sigmoid(_dot(xb, wgs_ref[...]) + bg_ref[:, 0:d_model])
    g_lru = _sigmoid(_dot(xb, wgl_ref[...]) + bg_ref[:, d_model:2 * d_model])
    merged = g_ssd * _dot(ysb, wso_ref[...]) + g_lru * _dot(ylru_ref[...], wlo_ref[...])
    o = _dot(merged.astype(BF16), wo_ref[...])
    x1_ref[...] = _layer_norm(alpha * x_ref[...] + o, l1g_ref[...], l1b_ref[...])


def _sample_mixer(x_lm, cssd_lm, clru_lm, slru_lm, state, wts, nseq, steps, start_pos):
    n_heads, headdim, d_state = wts['n_heads'], wts['headdim'], wts['d_state']
    d_model = x_lm.shape[1]
    d_ssd = n_heads * headdim
    gn = SSD_GROUPS * d_state
    d_xbc = d_ssd + 2 * gn
    d_lru = wts['wlx'].shape[1]
    ntok = steps * nseq
    dims = dict(nseq=nseq, steps=steps, n_heads=n_heads, headdim=headdim, d_state=d_state)
    params = pltpu.CompilerParams(dimension_semantics=("arbitrary",), vmem_limit_bytes=VMEM_LIMIT_BYTES)
    step_blk = lambda w: pl.BlockSpec((nseq, w), lambda l: (l, 0))

    sds = jax.ShapeDtypeStruct
    ssd_names = ['wxbc', 'wdt', 'scw', 'scb', 'dtb', 'aneg', 'dexp', 'e', 'e2']
    ssd_consts = [x_lm, cssd_lm] + [wts[k] for k in ssd_names]
    pre, c_lm, b_lm, ea_lm, ydg_lm, xw_lm = pl.pallas_call(
        functools.partial(_sample_ssd_kernel, **dims),
        grid=(steps,),
        in_specs=[step_blk(d_model)] + [_const_spec(c.shape) for c in ssd_consts],
        out_specs=(step_blk(d_xbc), step_blk(gn), step_blk(gn), step_blk(d_ssd), step_blk(d_ssd),
                   step_blk(d_ssd)),
        out_shape=(sds((ntok, d_xbc), F32), sds((ntok, gn), F32), sds((ntok, gn), F32),
                   sds((ntok, d_ssd), F32), sds((ntok, d_ssd), F32), sds((ntok, d_ssd), F32)),
        scratch_shapes=[
            pltpu.VMEM((cssd_lm.shape[0] + ntok, d_xbc), F32),
            pltpu.VMEM((ntok, d_ssd), F32),
            pltpu.VMEM((ntok, gn), F32),
            pltpu.VMEM((ntok, LANES), F32),
            pltpu.VMEM((ntok, LANES), F32),
        ],
        compiler_params=params,
        name="sample_ssd",
    )(x_lm, *ssd_consts)

    lru_names = ['wlx', 'wly', 'lcw', 'lcb', 'wa', 'ba', 'wx', 'bx', 'lam']
    lru_consts = [clru_lm, slru_lm] + [wts[k] for k in lru_names]
    prelx, ylru_lm, lst = pl.pallas_call(
        functools.partial(_sample_lru_kernel, nseq=nseq, steps=steps, start_pos=start_pos),
        grid=(steps,),
        in_specs=[step_blk(d_model)] + [_const_spec(c.shape) for c in lru_consts],
        out_specs=(step_blk(d_lru), step_blk(d_lru), pl.BlockSpec((nseq, d_lru), lambda l: (0, 0))),
        out_shape=(sds((ntok, d_lru), F32), sds((ntok, d_lru), BF16), sds((nseq, d_lru), F32)),
        scratch_shapes=[
            pltpu.VMEM((clru_lm.shape[0] + ntok, d_lru), F32),
            pltpu.VMEM((nseq, d_lru), F32),
        ],
        compiler_params=params,
        name="sample_lru",
    )(x_lm, *lru_consts)

    half = nseq // 2
    full = lambda a: _const_spec(a.shape)
    new_state, yoff_lm = pl.pallas_call(
        functools.partial(_sample_state_kernel, **dims),
        grid=(half,),
        in_specs=[pl.BlockSpec((None, 2, d_ssd, d_state), lambda j: (j, 0, 0, 0)),
                  full(c_lm), full(b_lm), full(xw_lm), full(ea_lm)],
        out_specs=(pl.BlockSpec((None, 2, d_ssd, d_state), lambda j: (j, 0, 0, 0)),
                   pl.BlockSpec((ntok, d_ssd), lambda j: (0, 0))),
        out_shape=(sds(state.shape, F32), sds((ntok, d_ssd), F32)),
        compiler_params=params,
        name="sample_state",
    )(state, c_lm, b_lm, xw_lm, ea_lm)

    post_names = ['wz', 'ng', 'wso', 'wlo', 'wgs', 'wgl', 'bg', 'wo', 'l1g', 'l1b']
    post_consts = [wts[k] for k in post_names]
    x1_lm = pl.pallas_call(
        functools.partial(_sample_post_kernel, alpha=wts['alpha']),
        grid=(steps,),
        in_specs=[step_blk(d_model), step_blk(d_ssd), step_blk(d_ssd), step_blk(d_lru)]
        + [_const_spec(c.shape) for c in post_consts],
        out_specs=step_blk(d_model),
        out_shape=sds((ntok, d_model), F32),
        compiler_params=params,
        name="sample_post",
    )(x_lm, ydg_lm, yoff_lm, ylru_lm, *post_consts)
    return x1_lm, new_state, pre, lst, prelx


def _prep_weights(w_in, b_gate, ssd_conv_w, ssd_conv_b, ssd_dt_bias, ssd_a_log, ssd_d, ssd_norm_g,
                  w_ssd_out, lru_conv_w, lru_conv_b, lru_wa, lru_ba, lru_wx, lru_bx, lru_lambda,
                  w_lru_out, w_o, ln1_g, ln1_b, ffn_w_gate, ffn_w_up, ffn_conv_w, ffn_conv_b,
                  ffn_w_down, ln2_g, ln2_b, n_heads, headdim, d_state):
    depth = w_in.shape[0]
    d_model = w_in.shape[1]
    d_ssd = n_heads * headdim
    d_xbc = d_ssd + 2 * SSD_GROUPS * d_state
    d_lru = lru_lambda.shape[1]
    sizes = (d_ssd, d_xbc, n_heads, d_lru, d_lru, d_model, d_model)
    cuts = np.cumsum((0,) + sizes)
    wi = w_in[0]
    parts = [wi[:, cuts[i]:cuts[i + 1]] for i in range(len(sizes))]
    row = lambda v: v.reshape(1, -1).astype(F32)
    mat = lambda w: w.astype(BF16)
    pad_heads = lambda v: jnp.pad(v.reshape(1, -1).astype(F32), ((0, 0), (0, LANES - n_heads)))
    head_of_col = np.arange(d_ssd) // headdim
    expand = (np.arange(LANES)[:, None] == head_of_col[None, :]).astype(np.float32)
    return dict(
        n_heads=n_heads, headdim=headdim, d_state=d_state,
        alpha=float((2.0 * depth) ** 0.25),
        wz=mat(parts[0]), wxbc=mat(parts[1]),
        wdt=mat(jnp.pad(parts[2], ((0, 0), (0, LANES - n_heads)))),
        wlx=mat(parts[3]), wly=mat(parts[4]), wgs=mat(parts[5]), wgl=mat(parts[6]),
        scw=ssd_conv_w[0].astype(F32), scb=row(ssd_conv_b[0]),
        dtb=pad_heads(ssd_dt_bias[0]), aneg=pad_heads(-jnp.exp(ssd_a_log[0].astype(F32))),
        dexp=row(jnp.repeat(ssd_d[0], headdim)), ng=row(ssd_norm_g[0]),
        wso=mat(w_ssd_out[0]),
        lcw=lru_conv_w[0].astype(F32), lcb=row(lru_conv_b[0]),
        wa=mat(lru_wa[0]), ba=row(lru_ba[0]), wx=mat(lru_wx[0]), bx=row(lru_bx[0]),
        wax=mat(jnp.concatenate([lru_wa[0], lru_wx[0]], axis=-1)),
        lam=row(lru_lambda[0]), wlo=mat(w_lru_out[0]),
        bg=row(b_gate[0]), wo=mat(w_o[0]), l1g=row(ln1_g[0]), l1b=row(ln1_b[0]),
        e=_pack_rows_01(expand),
        ee=_pack_rows_01(np.concatenate([expand, expand], axis=0)),
        e2=_pack_rows_01(np.arange(SSD_GROUPS * d_state)[:, None] // d_state
                         == (head_of_col // (n_heads // SSD_GROUPS))[None, :]),
        wg=mat(ffn_w_gate[0]), wu=mat(ffn_w_up[0]),
        fcw=ffn_conv_w[0].astype(F32), fcb=row(ffn_conv_b[0]), wd=mat(ffn_w_down[0]),
        l2g=row(ln2_g[0]), l2b=row(ln2_b[0]),
    )


def kernel(x_prompt, x_sample, state_ssd, cache_ssd_conv, state_lru, cache_lru_conv, cache_ffn_conv, w_in, b_gate, ssd_conv_w, ssd_conv_b, ssd_dt_bias, ssd_a_log, ssd_d, ssd_norm_g, w_ssd_out, lru_conv_w, lru_conv_b, lru_wa, lru_ba, lru_wx, lru_bx, lru_lambda, w_lru_out, w_o, ln1_g, ln1_b, ffn_w_gate, ffn_w_up, ffn_conv_w, ffn_conv_b, ffn_w_down, ln2_g, ln2_b):
    assert w_in.shape[0] == 1, "single-layer trunk"
    _, _, n_heads, headdim, d_state = state_ssd.shape
    wts = _prep_weights(w_in, b_gate, ssd_conv_w, ssd_conv_b, ssd_dt_bias, ssd_a_log, ssd_d, ssd_norm_g,
                        w_ssd_out, lru_conv_w, lru_conv_b, lru_wa, lru_ba, lru_wx, lru_bx, lru_lambda,
                        w_lru_out, w_o, ln1_g, ln1_b, ffn_w_gate, ffn_w_up, ffn_conv_w, ffn_conv_b,
                        ffn_w_down, ln2_g, ln2_b, n_heads, headdim, d_state)
    bp = x_prompt.shape[0]
    d_ff = ffn_w_gate.shape[2]

    x1_p, p_ssd, p_ssd_buf, p_lru, p_lru_buf = _prompt_mixer(x_prompt, wts, tl=256)
    y_prompt, p_ffn_buf = _ffn_seg(x1_p, jnp.zeros((bp, ffn_conv_w.shape[1] - 1, d_ff), F32), wts, tm=512)
    p_ssd = p_ssd.reshape(1, bp, n_heads, headdim, d_state)

    nb_s, steps, _ = x_sample.shape
    half = nb_s // 2

    def to_lm(a):
        k, c = a.shape[1], a.shape[2]
        return a.reshape(half, 2, k, c).transpose(2, 1, 0, 3).reshape(k * nb_s, c)

    def from_lm(a, k):
        c = a.shape[1]
        return a.reshape(k, 2, half, c).transpose(2, 1, 0, 3).reshape(nb_s, k, c)

    d_ssd = n_heads * headdim
    x1_lm, new_state, pre, lst, prelx = _sample_mixer(
        to_lm(x_sample), to_lm(cache_ssd_conv[0]), to_lm(cache_lru_conv[0]), to_lm(state_lru[0][:, None, :]),
        state_ssd[0].reshape(half, 2, d_ssd, d_state), wts, nb_s, steps, PAST_LEN)
    y_lm, tail = _ffn(x1_lm[None], to_lm(cache_ffn_conv[0])[None], wts, tm=steps * nb_s, stride=nb_s)
    k_ssd = ssd_conv_w.shape[1] - 1
    k_lru = lru_conv_w.shape[1] - 1
    k_ffn = ffn_conv_w.shape[1] - 1
    assert steps >= max(k_ssd, k_lru, k_ffn)
    return (y_prompt, from_lm(y_lm[0], steps), p_ssd, p_ssd_buf[None], p_lru.reshape(1, bp, -1), p_lru_buf[None],
            p_ffn_buf[None],
            new_state.reshape(1, nb_s, n_heads, headdim, d_state),
            from_lm(pre[(steps - k_ssd) * nb_s:], k_ssd)[None],
            from_lm(lst, 1).reshape(1, nb_s, -1),
            from_lm(prelx[(steps - k_lru) * nb_s:], k_lru)[None],
            from_lm(tail[0], k_ffn)[None])
```

```python
import functools

import numpy as np
import jax
import jax.numpy as jnp
from jax import lax
from jax.experimental import pallas as pl
from jax.experimental.pallas import tpu as pltpu

F32 = jnp.float32
BF16 = jnp.bfloat16

SSD_GROUPS = 4
SSD_CHUNK = 128
LRU_BLOCKS = 8
LRU_C = 8.0
LN_EPS = 1e-5
RMS_EPS = 1e-6
PAST_LEN = 16384

LANES = 128
SUBLANES = 8
VMEM_LIMIT_BYTES = 60 * 1024 * 1024

HIST = 8
ROW_BLK = 32
UNROLL = True


def _dot(a, b):
    if b.dtype == jnp.uint32:
        b = pltpu.bitcast(b, BF16)
    return jnp.dot(a, b, preferred_element_type=F32)


def _pack_kernel(w_ref, *out_refs, ranges):
    for o_ref, (a, b, pad) in zip(out_refs, ranges):
        v = w_ref[:, a:b]
        if pad:
            v = jnp.concatenate([v, jnp.zeros((v.shape[0], pad), v.dtype)], axis=1)
        o_ref[...] = pltpu.bitcast(v.astype(BF16), jnp.uint32)


def _pack_weight(w, ranges=None):
    k, n = w.shape
    single = ranges is None
    ranges = ((0, n, 0),) if single else tuple(ranges)
    bk = 128 if n > 4096 else 256
    widths = [b - a + pad for a, b, pad in ranges]
    outs = pl.pallas_call(
        functools.partial(_pack_kernel, ranges=ranges),
        grid=(k // bk,),
        in_specs=[pl.BlockSpec((bk, n), lambda i: (i, 0))],
        out_specs=tuple(pl.BlockSpec((bk // 2, wd), lambda i: (i, 0)) for wd in widths),
        out_shape=tuple(jax.ShapeDtypeStruct((k // 2, wd), jnp.uint32) for wd in widths),
        compiler_params=pltpu.CompilerParams(dimension_semantics=("arbitrary",),
                                             vmem_limit_bytes=VMEM_LIMIT_BYTES),
        name="pack_weight",
    )(w.astype(F32))
    return outs[0] if single else outs


def _pack_rows_01(m):
    bits = np.ascontiguousarray(m, np.float32).view(np.uint32) >> 16
    return jnp.asarray(bits[0::2] | (bits[1::2] << 16), jnp.uint32)


def _dot_nt(a, b):
    return lax.dot_general(a, b, (((1,), (1,)), ((), ())), preferred_element_type=F32)


def _dot_tn(a, b):
    return lax.dot_general(a, b, (((0,), (0,)), ((), ())), preferred_element_type=F32)


def _split3(v):
    hi = v.astype(BF16)
    r1 = v - hi.astype(F32)
    mid = r1.astype(BF16)
    lo = (r1 - mid.astype(F32)).astype(BF16)
    return hi, mid, lo


def _dot_exact_rhs(v, m):
    hi, mid, lo = _split3(v)
    return _dot(hi, m) + _dot(mid, m) + _dot(lo, m)


def _dot_exact_lhs(m, v):
    hi, mid, lo = _split3(v)
    return _dot(m, hi) + _dot(m, mid) + _dot(m, lo)


def _split2(v):
    hi = v.astype(BF16)
    lo = (v - hi.astype(F32)).astype(BF16)
    return hi, lo


def _dot_2way_rhs(v, m2):
    return _dot(jnp.concatenate(_split2(v), axis=1), m2)


def _dot_2way_lhs(m2, v):
    return _dot(m2, jnp.concatenate(_split2(v), axis=0))


def _softplus(x):
    return jnp.maximum(x, 0.0) + jnp.log1p(jnp.exp(-jnp.abs(x)))


def _sigmoid(x):
    return 0.5 * jnp.tanh(0.5 * x) + 0.5


def _silu(x):
    h = 0.5 * x
    return h + h * jnp.tanh(h)


def _gelu(x):
    c = np.sqrt(2.0 / np.pi).astype(np.float32)
    return 0.5 * x * (1.0 + jnp.tanh(c * (x + 0.044715 * (x * x * x))))


def _layer_norm(v, g, b):
    mu = jnp.mean(v, axis=-1, keepdims=True)
    d = v - mu
    var = jnp.mean(d * d, axis=-1, keepdims=True)
    return d * lax.rsqrt(var + LN_EPS) * g + b


def _rows(i, n):
    if isinstance(i, int):
        return slice(i * n, (i + 1) * n)
    return pl.ds(pl.multiple_of(i * n, n), n)


def _loop(n, body, unroll):
    if unroll:
        for i in range(n):
            body(i, None)
    else:
        lax.fori_loop(0, n, lambda i, c: (body(i, c), c)[1], 0)


def _conv_block(buf_ref, w_ref, b_ref, hist, stride, r0, rows, c0, cw):
    taps = w_ref.shape[0]
    acc = b_ref[:, c0:c0 + cw]
    for k in range(taps):
        off = hist + r0 - (taps - 1 - k) * stride
        if not isinstance(off, int):
            off = pl.multiple_of(off, SUBLANES)
        acc = acc + w_ref[k:k + 1, c0:c0 + cw] * buf_ref[pl.ds(off, rows), c0:c0 + cw]
    return acc


LOG2_SUBLANES = 3
NPOS = SSD_CHUNK // SUBLANES
LOG2_NPOS = 4
assert 1 << LOG2_SUBLANES == SUBLANES and 1 << LOG2_NPOS == NPOS


def _tok_of_row(r):
    return (r & (SUBLANES - 1)) * NPOS + lax.shift_right_logical(r, LOG2_SUBLANES)


def _row_of_tok(t):
    return (t & (NPOS - 1)) * SUBLANES + lax.shift_right_logical(t, LOG2_NPOS)


def _perm_matrix():
    q = SSD_CHUNK
    r = lax.broadcasted_iota(jnp.int32, (q, q), 0)
    c = lax.broadcasted_iota(jnp.int32, (q, q), 1)
    return jnp.where(c == _tok_of_row(r), 1.0, 0.0).astype(BF16)


def _unperm_matrix():
    q = SSD_CHUNK
    t = lax.broadcasted_iota(jnp.int32, (q, q), 0)
    r = lax.broadcasted_iota(jnp.int32, (q, q), 1)
    return jnp.where(r == _row_of_tok(t), 1.0, 0.0).astype(BF16)


def _fill_wrap(buf_ref, wrap_ref, c, wrap, tail_ref, cols=slice(None)):
    q = SSD_CHUNK
    for m in range(wrap // SUBLANES):
        r_cur = (c + 1) * q - wrap + m * SUBLANES
        cur = buf_ref[r_cur:r_cur + SUBLANES, cols]
        if c == 0:
            prv = tail_ref[m * SUBLANES:(m + 1) * SUBLANES, cols]
        else:
            prv = buf_ref[r_cur - q:r_cur - q + SUBLANES, cols]
        sub0 = lax.broadcasted_iota(jnp.int32, cur.shape, 0) == 0
        wrap_ref[c * wrap + m * SUBLANES:c * wrap + (m + 1) * SUBLANES, cols] = jnp.where(
            sub0, pltpu.roll(prv, 1, 0), pltpu.roll(cur, 1, 0))


def _conv_seg(buf_ref, wrap_ref, w_ref, b_ref, c, wrap, c0, cw):
    q = SSD_CHUNK
    taps = w_ref.shape[0]
    cols = slice(c0, c0 + cw)
    acc = b_ref[:, cols] + w_ref[taps - 1:taps, cols] * buf_ref[c * q:(c + 1) * q, cols]
    for k in range(taps - 1):
        back = (taps - 1 - k) * SUBLANES
        shifted = jnp.concatenate(
            [wrap_ref[(c + 1) * wrap - back:(c + 1) * wrap, cols], buf_ref[c * q:(c + 1) * q - back, cols]],
            axis=0)
        acc = acc + w_ref[k:k + 1, cols] * shifted
    return acc


def _seg_tail_rows(wrap):
    n = wrap // SUBLANES
    return [(NPOS - n + m) * SUBLANES + SUBLANES - 1 for m in range(n)]


def _prompt_mixer_kernel(
        x_ref, wz_ref, wxbc_ref, wdt_ref, wlx_ref, wly_ref, wgs_ref, wgl_ref,
        scw_ref, scb_ref, dtb_ref, aneg_ref, dexp_ref, ng_ref, wso_ref,
        lcw_ref, lcb_ref, wax_ref, ba_ref, bx_ref, lam_ref, wlo_ref,
        bg_ref, wo_ref, l1g_ref, l1b_ref, e_ref,
        x1_ref, st_ref, sconv_ref, lst_ref, lconv_ref,
        xb_s, xp_s, xbc_s, swrap_s, stail_s, xc_s, lx_s, lwrap_s, ltail_s, xr_s, ly_s, y_s, z_s, ysb_s, ylb_s,
        ht_s, hl_s, gs_s, gl_s, ys_s, yl_s, o_s, mb_s,
        *, tl, alpha, n_heads, headdim, d_state):
    t = pl.program_id(1)
    nt = pl.num_programs(1)
    d_ssd = n_heads * headdim
    gn = SSD_GROUPS * d_state
    hpg = n_heads // SSD_GROUPS
    gw = hpg * headdim
    d_lru = lx_s.shape[1]
    d_model = x_ref.shape[2]
    q = SSD_CHUNK

    nch = tl // q
    wrap_s = (scw_ref.shape[0] - 1) * SUBLANES
    wrap_l = (lcw_ref.shape[0] - 1) * SUBLANES

    @pl.when(t == 0)
    def _():
        stail_s[...] = jnp.zeros(stail_s.shape, F32)
        ltail_s[...] = jnp.zeros(ltail_s.shape, F32)
        ht_s[...] = jnp.zeros(ht_s.shape, F32)
        hl_s[...] = jnp.zeros(hl_s.shape, F32)

    perm = _perm_matrix()
    perm2 = jnp.concatenate([perm, perm], axis=1)
    for c in range(nch):
        rows = _rows(c, q)
        xp = _dot_2way_lhs(perm2, x_ref[0, rows, :])
        xp_s[rows, :] = xp
        xb_s[rows, :] = xp.astype(BF16)

    cwid = 512

    def proj(dst_ref, w_ref, c0, cw):
        dst_ref[:, c0:c0 + cw] = _dot(xb_s[...], w_ref[:, c0:c0 + cw])

    def lru_conv(c):
        _fill_wrap(lx_s, lwrap_s, c, wrap_l, ltail_s)
        for c0 in range(0, d_lru, cwid):
            xr_s[c * q:(c + 1) * q, c0:c0 + cwid] = _conv_seg(lx_s, lwrap_s, lcw_ref, lcb_ref, c, wrap_l, c0, cwid)
        if c == nch - 1:
            ltail_s[...] = lx_s[tl - wrap_l:tl, :]

    bw = d_lru // LRU_BLOCKS
    sub = lax.broadcasted_iota(jnp.int32, (SUBLANES, bw), 0)
    crow = lax.broadcasted_iota(jnp.int32, (q, bw), 0)

    def lru_block(n):
        cols = slice(n * bw, (n + 1) * bw)
        xr = xr_s[:, cols]
        xrb = xr.astype(BF16)
        rg = _dot(xrb, wax_ref[n])
        r = _sigmoid(rg[:, 0:bw] + ba_ref[:, cols])
        gi = _sigmoid(rg[:, bw:2 * bw] + bx_ref[:, cols])
        log_a = (-LRU_C) * r * _softplus(-lam_ref[:, cols])
        a_all = jnp.exp(log_a)
        mult_all = jnp.sqrt(1.0 - jnp.exp(2.0 * log_a))
        for c in range(nch):
            a = a_all[c * q:(c + 1) * q, :]
            mult = mult_all[c * q:(c + 1) * q, :]
            if c == 0:
                mult = jnp.where(jnp.logical_and(crow == 0, t == 0), 1.0, mult)
            u = mult * gi[c * q:(c + 1) * q, :] * xr[c * q:(c + 1) * q, :]
            a_p = [a[i * SUBLANES:(i + 1) * SUBLANES, :] for i in range(NPOS)]
            u_p = [u[i * SUBLANES:(i + 1) * SUBLANES, :] for i in range(NPOS)]
            h = u_p[0]
            g = a_p[0]
            for i in range(1, NPOS):
                h = a_p[i] * h + u_p[i]
                g = a_p[i] * g
            gs = jnp.where(sub == 0, 0.0, pltpu.roll(g, 1, 0))
            hs = jnp.where(sub == 0, hl_s[0:1, cols], pltpu.roll(h, 1, 0))
            d = 1
            while d < SUBLANES:
                keep = sub >= d
                hs = jnp.where(keep, gs * pltpu.roll(hs, d, 0) + hs, hs)
                gs = jnp.where(keep, gs * pltpu.roll(gs, d, 0), gs)
                d *= 2
            h = hs
            out = []
            for i in range(NPOS):
                h = a_p[i] * h + u_p[i]
                out.append(h)
            hl_s[0:1, cols] = h[SUBLANES - 1:SUBLANES, :]
            hseq = jnp.concatenate(out, axis=0)
            ylb_s[c * q:(c + 1) * q, cols] = (hseq * _gelu(ly_s[c * q:(c + 1) * q, cols])).astype(BF16)

    def ssd_conv(c0):
        cols = slice(c0, c0 + cwid)
        for c in range(nch):
            _fill_wrap(xbc_s, swrap_s, c, wrap_s, stail_s, cols)
        stail_s[:, cols] = xbc_s[tl - wrap_s:tl, cols]
        for c in range(nch):
            xc_s[c * q:(c + 1) * q, cols] = _silu(
                _conv_seg(xbc_s, swrap_s, scw_ref, scb_ref, c, wrap_s, c0, cwid))

    def merge_gate(dst_ref, c0, cw, b0):
        dst_ref[:, c0:c0 + cw] = _sigmoid(dst_ref[:, c0:c0 + cw] + bg_ref[:, b0 + c0:b0 + c0 + cw])

    tok_r = _tok_of_row(lax.broadcasted_iota(jnp.int32, (q, q), 0))
    tok_c = _tok_of_row(lax.broadcasted_iota(jnp.int32, (q, q), 1))
    causal = tok_r >= tok_c
    tri = jnp.where(causal, 1.0, 0.0).astype(BF16)
    lane_i = lax.broadcasted_iota(jnp.int32, (q, LANES), 1)
    left = lane_i < headdim

    def chunk_body(c, carry):
        rows = _rows(c, q)
        dt = _softplus(_dot(xb_s[rows, :], wdt_ref[...]) + dtb_ref[...])
        d_a = dt * aneg_ref[...]
        a_cs = _dot_exact_lhs(tri, d_a)
        a_last = a_cs[q - 1:q, :]
        wgt = dt * jnp.exp(a_last - a_cs)
        ea = jnp.exp(a_cs)
        w_exp = _dot_2way_rhs(wgt, e_ref[...])
        ea_exp = _dot_2way_rhs(ea, e_ref[...])
        a_cs_t = a_cs.T
        dt_t = dt.T
        for g in range(SSD_GROUPS):
            b_g = xc_s[rows, d_ssd + g * d_state:d_ssd + (g + 1) * d_state]
            c_g = xc_s[rows, d_ssd + gn + g * d_state:d_ssd + gn + (g + 1) * d_state]
            b_gb = b_g.astype(BF16)
            c_gb = c_g.astype(BF16)
            cb = _dot_nt(c_gb, b_gb)
            for hp in range(hpg // 2):
                c0 = g * gw + hp * 2 * headdim
                xs_pair = xc_s[rows, c0:c0 + 2 * headdim]
                lmats = []
                for j in range(2):
                    h = g * hpg + hp * 2 + j
                    seg = (jnp.broadcast_to(a_cs[:, h:h + 1], (q, q))
                           - jnp.broadcast_to(a_cs_t[h:h + 1, :], (q, q)))
                    dec = jnp.exp(jnp.where(causal, seg, -jnp.inf))
                    lmats.append((cb * dec * jnp.broadcast_to(dt_t[h:h + 1, :], (q, q))).astype(BF16))
                lpair = jnp.concatenate(lmats, axis=1)
                rhs = jnp.concatenate([jnp.where(left, xs_pair, 0.0),
                                       jnp.where(left, 0.0, xs_pair)], axis=0).astype(BF16)
                y_s[rows, c0:c0 + 2 * headdim] = _dot(lpair, rhs)
            gcols = slice(g * gw, (g + 1) * gw)
            h_g = ht_s[:, gcols]
            y_off = _dot(c_gb, h_g.astype(BF16)) * ea_exp[:, gcols]
            y_s[rows, gcols] = y_s[rows, gcols] + y_off
            xw = (xc_s[rows, gcols] * w_exp[:, gcols]).astype(BF16)
            ht_s[:, gcols] = h_g * ea_exp[q - 1:q, gcols] + _dot_tn(b_gb, xw)
            if carry is not None and g < len(carry):
                carry[g]()
        return carry

    def gate_body(i, carry):
        rows = _rows(i, ROW_BLK)
        y = y_s[rows, :] + dexp_ref[...] * xc_s[rows, 0:d_ssd]
        y = y * _silu(z_s[rows, :])
        ms = jnp.mean(y * y, axis=-1, keepdims=True)
        ysb_s[rows, :] = (y * lax.rsqrt(ms + RMS_EPS) * ng_ref[...]).astype(BF16)
        return carry

    assert nch == 2 and LRU_BLOCKS == 8 and d_lru == 2 * cwid and d_model == 2 * cwid
    d_xbc = xc_s.shape[1]
    nxb = d_xbc // cwid
    proj(lx_s, wlx_ref, 0, cwid)
    proj(lx_s, wlx_ref, cwid, cwid)
    proj(ly_s, wly_ref, 0, cwid); lru_conv(0)
    proj(ly_s, wly_ref, cwid, cwid); lru_conv(1)
    P = functools.partial
    mxu_a = ([P(proj, xbc_s, wxbc_ref, j * cwid, cwid) for j in range(nxb)]
             + [P(proj, gs_s, wgs_ref, j * cwid, cwid) for j in range(2)])
    for n in range(LRU_BLOCKS):
        mxu_a[n]()
        lru_block(n)
        if 1 <= n <= nxb:
            ssd_conv((n - 1) * cwid)
    gates_per_chunk = q // ROW_BLK
    nz = d_ssd // cwid
    chunk_body(0, [P(proj, z_s, wz_ref, j * cwid, cwid) for j in range(nz)])
    merge_gate(gs_s, 0, d_model, 0)
    for i in range(gates_per_chunk):
        gate_body(i, None)

    def wlo_piece(j):
        yl_s[:, j * cwid:(j + 1) * cwid] = _dot(ylb_s[...], wlo_ref[:, j * cwid:(j + 1) * cwid])

    chunk_body(1, [P(proj, gl_s, wgl_ref, 0, cwid), P(proj, gl_s, wgl_ref, cwid, cwid),
                   P(wlo_piece, 0), P(wlo_piece, 1)])
    merge_gate(gl_s, 0, d_model, d_model)
    for i in range(gates_per_chunk, 2 * gates_per_chunk):
        gate_body(i, None)

    ys_s[...] = _dot(ysb_s[...], wso_ref[...])
    for i in range(tl // ROW_BLK):
        rows = _rows(i, ROW_BLK)
        mb_s[rows, :] = (gs_s[rows, :] * ys_s[rows, :] + gl_s[rows, :] * yl_s[rows, :]).astype(BF16)
    o_s[...] = _dot(mb_s[...], wo_ref[...])
    for i in range(tl // ROW_BLK):
        rows = _rows(i, ROW_BLK)
        v = alpha * xp_s[rows, :] + o_s[rows, :]
        x1_ref[0, rows, :] = _layer_norm(v, l1g_ref[...], l1b_ref[...])

    @pl.when(t == nt - 1)
    def _():
        st_ref[0] = ht_s[...].T
        lst_ref[0] = hl_s[0:1, :]
        for m, r in enumerate(_seg_tail_rows(wrap_s)):
            sconv_ref[0, m:m + 1, :] = xbc_s[tl - q + r:tl - q + r + 1, :]
        for m, r in enumerate(_seg_tail_rows(wrap_l)):
            lconv_ref[0, m:m + 1, :] = lx_s[tl - q + r:tl - q + r + 1, :]


def _const_spec(shape):
    nd = len(shape)
    return pl.BlockSpec(shape, lambda *_: (0,) * nd, pipeline_mode=pl.Buffered(1))


def _prompt_mixer(x, wts, tl):
    nb, seq, d_model = x.shape
    n_heads, headdim, d_state = wts['n_heads'], wts['headdim'], wts['d_state']
    d_ssd = n_heads * headdim
    d_xbc = d_ssd + 2 * SSD_GROUPS * d_state
    d_lru = wts['wlx'].shape[1]
    names = ['wz', 'wxbc', 'wdt', 'wlx', 'wly', 'wgs', 'wgl', 'scw', 'scb', 'dtb', 'aneg', 'dexp', 'ng',
             'wso', 'lcw', 'lcb', 'wax', 'ba', 'bx', 'lam', 'wlo', 'bg', 'wo', 'l1g', 'l1b', 'ee']
    consts = [wts[k] for k in names]
    kern = functools.partial(_prompt_mixer_kernel, tl=tl, alpha=wts['alpha'], n_heads=n_heads,
                             headdim=headdim, d_state=d_state)
    out_shape = (
        jax.ShapeDtypeStruct((nb, seq, d_model), F32),
        jax.ShapeDtypeStruct((nb, d_ssd, d_state), F32),
        jax.ShapeDtypeStruct((nb, 3, d_xbc), F32),
        jax.ShapeDtypeStruct((nb, 1, d_lru), F32),
        jax.ShapeDtypeStruct((nb, 3, d_lru), F32),
    )
    out_specs = (
        pl.BlockSpec((1, tl, d_model), lambda b, t: (b, t, 0)),
        pl.BlockSpec((1, d_ssd, d_state), lambda b, t: (b, 0, 0)),
        pl.BlockSpec((1, 3, d_xbc), lambda b, t: (b, 0, 0)),
        pl.BlockSpec((1, 1, d_lru), lambda b, t: (b, 0, 0)),
        pl.BlockSpec((1, 3, d_lru), lambda b, t: (b, 0, 0)),
    )
    nch = tl // SSD_CHUNK
    wrap_s = (wts['scw'].shape[0] - 1) * SUBLANES
    wrap_l = (wts['lcw'].shape[0] - 1) * SUBLANES
    scratch = [
        pltpu.VMEM((tl, d_model), BF16),
        pltpu.VMEM((tl, d_model), F32),
        pltpu.VMEM((tl, d_xbc), F32),
        pltpu.VMEM((nch * wrap_s, d_xbc), F32),
        pltpu.VMEM((wrap_s, d_xbc), F32),
        pltpu.VMEM((tl, d_xbc), F32),
        pltpu.VMEM((tl, d_lru), F32),
        pltpu.VMEM((nch * wrap_l, d_lru), F32),
        pltpu.VMEM((wrap_l, d_lru), F32),
        pltpu.VMEM((tl, d_lru), F32),
        pltpu.VMEM((tl, d_lru), F32),
        pltpu.VMEM((tl, d_ssd), F32),
        pltpu.VMEM((tl, d_ssd), F32),
        pltpu.VMEM((tl, d_ssd), BF16),
        pltpu.VMEM((tl, d_lru), BF16),
        pltpu.VMEM((d_state, d_ssd), F32),
        pltpu.VMEM((SUBLANES, d_lru), F32),
        pltpu.VMEM((tl, d_model), F32),
        pltpu.VMEM((tl, d_model), F32),
        pltpu.VMEM((tl, d_model), F32),
        pltpu.VMEM((tl, d_model), F32),
        pltpu.VMEM((tl, d_model), F32),
        pltpu.VMEM((tl, d_model), BF16),
    ]
    return pl.pallas_call(
        kern,
        grid=(nb, seq // tl),
        in_specs=[pl.BlockSpec((1, tl, d_model), lambda b, t: (b, t, 0))]
        + [_const_spec(c.shape) for c in consts],
        out_specs=out_specs,
        out_shape=out_shape,
        scratch_shapes=scratch,
        compiler_params=pltpu.CompilerParams(
            dimension_semantics=("arbitrary", "arbitrary"),
            vmem_limit_bytes=VMEM_LIMIT_BYTES),
        name="prompt_mixer",
    )(x, *consts)


def _ffn_kernel(x_ref, h0_ref, wg_ref, wu_ref, cw_ref, cb_ref, wd_ref, g_ref, b_ref,
                y_ref, tail_ref, xb_s, gb_s, hb_s, *, tm, stride, hist, alpha, fchunk):
    t = pl.program_id(1)
    nt = pl.num_programs(1)
    taps = cw_ref.shape[0]
    nh = (taps - 1) * stride
    d_ff = gb_s.shape[1]

    @pl.when(t == 0)
    def _():
        gb_s[hist - nh:hist, :] = h0_ref[0]

    xb_s[...] = x_ref[0].astype(BF16)
    acc = None
    for c0 in range(0, d_ff, fchunk):
        cols = slice(c0, c0 + fchunk)
        gb_s[hist:hist + tm, cols] = _dot(xb_s[...], wg_ref[:, cols])
        up = _dot(xb_s[...], wu_ref[:, cols])
        gc = _conv_block(gb_s, cw_ref, cb_ref, hist, stride, 0, tm, c0, fchunk)
        hb_s[...] = (_gelu(gc) * up).astype(BF16)
        part = _dot(hb_s[...], wd_ref[c0 // 2:(c0 + fchunk) // 2, :])
        acc = part if acc is None else acc + part
    v = alpha * x_ref[0] + acc
    y_ref[0] = _layer_norm(v, g_ref[...], b_ref[...])
    gb_s[hist - nh:hist, :] = gb_s[hist + tm - nh:hist + tm, :]

    @pl.when(t == nt - 1)
    def _():
        tail_ref[0] = gb_s[hist - nh:hist, :]


def _ffn_seg_kernel(x_ref, h0_ref, wg_ref, wu_ref, cw_ref, cb_ref, wd_ref, g_ref, b_ref,
                    y_ref, tail_ref, xb_s, gb_s, gwrap_s, gtail_s, hb_s, *, tm, alpha, fchunk):
    t = pl.program_id(1)
    nt = pl.num_programs(1)
    q = SSD_CHUNK
    nch = tm // q
    wrap = (cw_ref.shape[0] - 1) * SUBLANES
    d_ff = gb_s.shape[1]
    tail_rows = _seg_tail_rows(wrap)

    @pl.when(t == 0)
    def _():
        gtail_s[...] = jnp.zeros(gtail_s.shape, F32)
        for m in range(len(tail_rows)):
            r = m * SUBLANES + SUBLANES - 1
            gtail_s[r:r + 1, :] = h0_ref[0, m:m + 1, :]

    xb_s[...] = x_ref[0].astype(BF16)
    gb_s[...] = _dot(xb_s[...], wg_ref[...])
    for c in range(nch):
        _fill_wrap(gb_s, gwrap_s, c, wrap, gtail_s)
    acc = None
    for c0 in range(0, d_ff, fchunk):
        cols = slice(c0, c0 + fchunk)
        up = _dot(xb_s[...], wu_ref[:, cols])
        for c in range(nch):
            gc = _conv_seg(gb_s, gwrap_s, cw_ref, cb_ref, c, wrap, c0, fchunk)
            hb_s[c * q:(c + 1) * q, :] = (_gelu(gc) * up[c * q:(c + 1) * q, :]).astype(BF16)
        part = _dot(hb_s[...], wd_ref[c0 // 2:(c0 + fchunk) // 2, :])
        acc = part if acc is None else acc + part
    gtail_s[...] = gb_s[tm - wrap:tm, :]

    unperm = _unperm_matrix()
    unperm2 = jnp.concatenate([unperm, unperm], axis=1)
    for c in range(nch):
        rows = slice(c * q, (c + 1) * q)
        y = _layer_norm(alpha * x_ref[0, rows, :] + acc[rows, :], g_ref[...], b_ref[...])
        y_ref[0, rows, :] = _dot_2way_lhs(unperm2, y)

    @pl.when(t == nt - 1)
    def _():
        for m, r in enumerate(tail_rows):
            tail_ref[0, m:m + 1, :] = gb_s[tm - q + r:tm - q + r + 1, :]


def _ffn_seg(x, hist0, wts, tm):
    nb, seq, d_model = x.shape
    d_ff = wts['wg'].shape[1]
    taps = wts['fcw'].shape[0]
    wrap = (taps - 1) * SUBLANES
    fchunk = 1024
    consts = [wts[k] for k in ['wg', 'wu', 'fcw', 'fcb', 'wd', 'l2g', 'l2b']]
    kern = functools.partial(_ffn_seg_kernel, tm=tm, alpha=wts['alpha'], fchunk=fchunk)
    return pl.pallas_call(
        kern,
        grid=(nb, seq // tm),
        in_specs=[pl.BlockSpec((1, tm, d_model), lambda b, t: (b, t, 0)),
                  pl.BlockSpec((1, taps - 1, d_ff), lambda b, t: (b, 0, 0))]
        + [_const_spec(c.shape) for c in consts],
        out_specs=(pl.BlockSpec((1, tm, d_model), lambda b, t: (b, t, 0)),
                   pl.BlockSpec((1, taps - 1, d_ff), lambda b, t: (b, 0, 0))),
        out_shape=(jax.ShapeDtypeStruct((nb, seq, d_model), F32),
                   jax.ShapeDtypeStruct((nb, taps - 1, d_ff), F32)),
        scratch_shapes=[pltpu.VMEM((tm, d_model), BF16),
                        pltpu.VMEM((tm, d_ff), F32),
                        pltpu.VMEM((tm // SSD_CHUNK * wrap, d_ff), F32),
                        pltpu.VMEM((wrap, d_ff), F32),
                        pltpu.VMEM((tm, fchunk), BF16)],
        compiler_params=pltpu.CompilerParams(
            dimension_semantics=("arbitrary", "arbitrary"),
            vmem_limit_bytes=VMEM_LIMIT_BYTES),
        name="conv_ffn_seg",
    )(x, hist0, *consts)


def _ffn(x, hist0, wts, tm, stride):
    nb, seq, d_model = x.shape
    d_ff = wts['wg'].shape[1]
    taps = wts['fcw'].shape[0]
    nh = (taps - 1) * stride
    hist = -(-nh // SUBLANES) * SUBLANES
    consts = [wts[k] for k in ['wg', 'wu', 'fcw', 'fcb', 'wd', 'l2g', 'l2b']]
    kern = functools.partial(_ffn_kernel, tm=tm, stride=stride, hist=hist, alpha=wts['alpha'], fchunk=1024)
    return pl.pallas_call(
        kern,
        grid=(nb, seq // tm),
        in_specs=[pl.BlockSpec((1, tm, d_model), lambda b, t: (b, t, 0)),
                  pl.BlockSpec((1, nh, d_ff), lambda b, t: (b, 0, 0))]
        + [_const_spec(c.shape) for c in consts],
        out_specs=(pl.BlockSpec((1, tm, d_model), lambda b, t: (b, t, 0)),
                   pl.BlockSpec((1, nh, d_ff), lambda b, t: (b, 0, 0))),
        out_shape=(jax.ShapeDtypeStruct((nb, seq, d_model), F32),
                   jax.ShapeDtypeStruct((nb, nh, d_ff), F32)),
        scratch_shapes=[pltpu.VMEM((tm, d_model), BF16),
                        pltpu.VMEM((hist + tm, d_ff), F32),
                        pltpu.VMEM((tm, 1024), BF16)],
        compiler_params=pltpu.CompilerParams(
            dimension_semantics=("arbitrary", "arbitrary"),
            vmem_limit_bytes=VMEM_LIMIT_BYTES),
        name="conv_ffn",
    )(x, hist0, *consts)


def _sample_ssd_kernel(
        x_ref, xall_ref, cssd_ref, wxbc_ref, wdt_ref, scw_ref, scb_ref, dtb_ref, aneg_ref, dexp_ref,
        e_ref, e2_ref,
        pre_ref, c_ref, b_ref, ea_ref, ydg_ref, xw_ref,
        xbc_s, xs_s, bs_s, acs_s, dts_s,
        *, nseq, steps, n_heads, headdim, d_state):
    l = pl.program_id(0)
    d_ssd = n_heads * headdim
    gn = SSD_GROUPS * d_state
    hist = (scw_ref.shape[0] - 1) * nseq
    r0 = l * nseq

    def blk(i):
        return pl.ds(pl.multiple_of(i * nseq, nseq), nseq)

    def sblk(i):
        return slice(i * nseq, (i + 1) * nseq)

    @pl.when(l == 0)
    def _():
        xbc_s[0:hist, :] = cssd_ref[...]
        dts_s[...] = _softplus(_dot(xall_ref[...].astype(BF16), wdt_ref[...]) + dtb_ref[...])
        acc = jnp.zeros((nseq, LANES), F32)
        for s in range(steps):
            acc = acc + dts_s[sblk(s), :] * aneg_ref[...]
            acs_s[sblk(s), :] = acc

    xb = x_ref[...].astype(BF16)

    pre = _dot(xb, wxbc_ref[...])
    pre_ref[...] = pre
    xbc_s[pl.ds(pl.multiple_of(hist + r0, nseq), nseq), :] = pre
    cwid = 512
    for c0 in range(0, d_ssd, cwid):
        xs_s[blk(l), c0:c0 + cwid] = _silu(
            _conv_block(xbc_s, scw_ref, scb_ref, hist, nseq, r0, nseq, c0, cwid))
    b_l = _silu(_conv_block(xbc_s, scw_ref, scb_ref, hist, nseq, r0, nseq, d_ssd, gn))
    c_l = _silu(_conv_block(xbc_s, scw_ref, scb_ref, hist, nseq, r0, nseq, d_ssd + gn, gn))
    bs_s[blk(l), :] = b_l
    b_ref[...] = b_l
    c_ref[...] = c_l

    a_cs = acs_s[blk(l), :]
    dt = dts_s[blk(l), :]
    a_end = acs_s[sblk(steps - 1), :]
    ea_ref[...] = _dot_exact_rhs(jnp.exp(a_cs), e_ref[...])
    xw_ref[...] = xs_s[blk(l), :] * _dot_exact_rhs(dt * jnp.exp(a_end - a_cs), e_ref[...])

    ydg_ref[...] = dexp_ref[...] * xs_s[blk(l), :]
    for s in range(steps):
        @pl.when(s <= l)
        def _(s=s):
            coef = _dot_exact_rhs(jnp.exp(a_cs - acs_s[sblk(s), :]) * dts_s[sblk(s), :], e_ref[...])
            cbx = _dot_exact_rhs(bs_s[sblk(s), :] * c_l, e2_ref[...])
            ydg_ref[...] += cbx * coef * xs_s[sblk(s), :]


def _sample_lru_kernel(
        x_ref, clru_ref, slru_ref, wlx_ref, wly_ref, lcw_ref, lcb_ref, wax_ref, ba_ref, bx_ref,
        lam_ref, prelx_ref, ylru_ref, lst_ref, lx_s, hl_s, *, nseq, steps, start_pos):
    l = pl.program_id(0)
    hist = (lcw_ref.shape[0] - 1) * nseq
    d_lru = lx_s.shape[1]
    r0 = l * nseq

    @pl.when(l == 0)
    def _():
        lx_s[0:hist, :] = clru_ref[...]
        hl_s[...] = slru_ref[...]

    xb = x_ref[...].astype(BF16)
    prelx = _dot(xb, wlx_ref[...])
    prelx_ref[...] = prelx
    lx_s[pl.ds(pl.multiple_of(hist + r0, nseq), nseq), :] = prelx
    ly = _dot(xb, wly_ref[...])
    bw = d_lru // LRU_BLOCKS
    first = (l + start_pos) == 0
    for n in range(LRU_BLOCKS):
        cols = slice(n * bw, (n + 1) * bw)
        xr = _conv_block(lx_s, lcw_ref, lcb_ref, hist, nseq, r0, nseq, n * bw, bw)
        xrb = xr.astype(BF16)
        rg = _dot(xrb, wax_ref[n])
        r = _sigmoid(rg[:, 0:bw] + ba_ref[:, cols])
        gi = _sigmoid(rg[:, bw:2 * bw] + bx_ref[:, cols])
        log_a = (-LRU_C) * r * _softplus(-lam_ref[:, cols])
        a = jnp.exp(log_a)
        mult = jnp.where(first, 1.0, jnp.sqrt(1.0 - jnp.exp(2.0 * log_a)))
        h = a * hl_s[:, cols] + mult * gi * xr
        hl_s[:, cols] = h
        ylru_ref[:, cols] = (h * _gelu(ly[:, cols])).astype(BF16)

    @pl.when(l == steps - 1)
    def _():
        lst_ref[...] = hl_s[...]


def _sample_state_kernel(st_ref, c_ref, b_ref, xw_ref, ea_ref, nst_ref, yoff_ref,
                         *, nseq, steps, n_heads, headdim, d_state):
    j = pl.program_id(0)
    half = nseq // 2
    nrow = 2 * steps
    hpg = n_heads // SSD_GROUPS
    gw = hpg * headdim

    def gather(ref, cols):
        return jnp.concatenate([ref[pl.ds(j + k * half, 1), cols] for k in range(nrow)], axis=0)

    c8 = gather(c_ref, slice(None))
    b8 = gather(b_ref, slice(None))
    par = lax.broadcasted_iota(jnp.int32, (nrow, gw), 0) % 2
    assert 2 * headdim == LANES and d_state == LANES
    low = lax.broadcasted_iota(jnp.int32, (nrow, LANES), 1) < headdim
    for g in range(SSD_GROUPS):
        gcols = slice(g * gw, (g + 1) * gw)
        c8g = c8[:, g * d_state:(g + 1) * d_state].astype(BF16)
        b8g = b8[:, g * d_state:(g + 1) * d_state].astype(BF16)
        xw8 = gather(xw_ref, gcols)
        ea8 = gather(ea_ref, gcols)
        cds = []
        for hp in range(hpg // 2):
            pair = ea8[:, hp * LANES:(hp + 1) * LANES]
            swapped = pltpu.roll(pair, headdim, 1)
            cds.append(jnp.where(low, pair, swapped))
            cds.append(jnp.where(low, swapped, pair))
        yo = None
        for e in range(2):
            sg = st_ref[e, gcols, :]
            yo_e = _dot_nt(c8g, sg.astype(BF16))
            yo = yo_e if e == 0 else jnp.where(par == e, yo_e, yo)
            xw_e = jnp.where(par == e, xw8, 0.0).astype(BF16)
            upd = _dot_tn(xw_e, b8g)
            k_last = 2 * (steps - 1) + e
            for hh in range(hpg):
                cd = cds[hh][k_last:k_last + 1, :]
                hr = slice(hh * headdim, (hh + 1) * headdim)
                nst_ref[e, g * gw + hh * headdim:g * gw + (hh + 1) * headdim, :] = sg[hr, :] * cd + upd[hr, :]
        yo = yo * ea8
        for k in range(nrow):
            yoff_ref[pl.ds(j + k * half, 1), gcols] = yo[k:k + 1, :]


def _sample_post_kernel(x_ref, ydg_ref, yoff_ref, ylru_ref, wz_ref, ng_ref, wso_ref, wlo_ref,
                        wgs_ref, wgl_ref, bg_ref, wo_ref, l1g_ref, l1b_ref, x1_ref, *, alpha):
    d_model = x_ref.shape[1]
    xb = x_ref[...].astype(BF16)
    y = (ydg_ref[...] + yoff_ref[...]) * _silu(_dot(xb, wz_ref[...]))
    ms = jnp.mean(y * y, axis=-1, keepdims=True)
    ysb = (y * lax.rsqrt(ms + RMS_EPS) * ng_ref[...]).astype(BF16)
    g_ssd = _sigmoid(_dot(xb, wgs_ref[...]) + bg_ref[:, 0:d_model])
    g_lru = _sigmoid(_dot(xb, wgl_ref[...]) + bg_ref[:, d_model:2 * d_model])
    merged = g_ssd * _dot(ysb, wso_ref[...]) + g_lru * _dot(ylru_ref[...], wlo_ref[...])
    o = _dot(merged.astype(BF16), wo_ref[...])
    x1_ref[...] = _layer_norm(alpha * x_ref[...] + o, l1g_ref[...], l1b_ref[...])


def _sample_mixer(x_lm, cssd_lm, clru_lm, slru_lm, state, wts, nseq, steps, start_pos):
    n_heads, headdim, d_state = wts['n_heads'], wts['headdim'], wts['d_state']
    d_model = x_lm.shape[1]
    d_ssd = n_heads * headdim
    gn = SSD_GROUPS * d_state
    d_xbc = d_ssd + 2 * gn
    d_lru = wts['wlx'].shape[1]
    ntok = steps * nseq
    dims = dict(nseq=nseq, steps=steps, n_heads=n_heads, headdim=headdim, d_state=d_state)
    params = pltpu.CompilerParams(dimension_semantics=("arbitrary",), vmem_limit_bytes=VMEM_LIMIT_BYTES)
    step_blk = lambda w: pl.BlockSpec((nseq, w), lambda l: (l, 0))

    sds = jax.ShapeDtypeStruct
    ssd_names = ['wxbc', 'wdt', 'scw', 'scb', 'dtb', 'aneg', 'dexp', 'e', 'e2']
    ssd_consts = [x_lm, cssd_lm] + [wts[k] for k in ssd_names]
    pre, c_lm, b_lm, ea_lm, ydg_lm, xw_lm = pl.pallas_call(
        functools.partial(_sample_ssd_kernel, **dims),
        grid=(steps,),
        in_specs=[step_blk(d_model)] + [_const_spec(c.shape) for c in ssd_consts],
        out_specs=(step_blk(d_xbc), step_blk(gn), step_blk(gn), step_blk(d_ssd), step_blk(d_ssd),
                   step_blk(d_ssd)),
        out_shape=(sds((ntok, d_xbc), F32), sds((ntok, gn), F32), sds((ntok, gn), F32),
                   sds((ntok, d_ssd), F32), sds((ntok, d_ssd), F32), sds((ntok, d_ssd), F32)),
        scratch_shapes=[
            pltpu.VMEM((cssd_lm.shape[0] + ntok, d_xbc), F32),
            pltpu.VMEM((ntok, d_ssd), F32),
            pltpu.VMEM((ntok, gn), F32),
            pltpu.VMEM((ntok, LANES), F32),
            pltpu.VMEM((ntok, LANES), F32),
        ],
        compiler_params=params,
        name="sample_ssd",
    )(x_lm, *ssd_consts)

    lru_names = ['wlx', 'wly', 'lcw', 'lcb', 'wax', 'ba', 'bx', 'lam']
    lru_consts = [clru_lm, slru_lm] + [wts[k] for k in lru_names]
    prelx, ylru_lm, lst = pl.pallas_call(
        functools.partial(_sample_lru_kernel, nseq=nseq, steps=steps, start_pos=start_pos),
        grid=(steps,),
        in_specs=[step_blk(d_model)] + [_const_spec(c.shape) for c in lru_consts],
        out_specs=(step_blk(d_lru), step_blk(d_lru), pl.BlockSpec((nseq, d_lru), lambda l: (0, 0))),
        out_shape=(sds((ntok, d_lru), F32), sds((ntok, d_lru), BF16), sds((nseq, d_lru), F32)),
        scratch_shapes=[
            pltpu.VMEM((clru_lm.shape[0] + ntok, d_lru), F32),
            pltpu.VMEM((nseq, d_lru), F32),
        ],
        compiler_params=params,
        name="sample_lru",
    )(x_lm, *lru_consts)

    half = nseq // 2
    full = lambda a: _const_spec(a.shape)
    new_state, yoff_lm = pl.pallas_call(
        functools.partial(_sample_state_kernel, **dims),
        grid=(half,),
        in_specs=[pl.BlockSpec((None, 2, d_ssd, d_state), lambda j: (j, 0, 0, 0)),
                  full(c_lm), full(b_lm), full(xw_lm), full(ea_lm)],
        out_specs=(pl.BlockSpec((None, 2, d_ssd, d_state), lambda j: (j, 0, 0, 0)),
                   pl.BlockSpec((ntok, d_ssd), lambda j: (0, 0))),
        out_shape=(sds(state.shape, F32), sds((ntok, d_ssd), F32)),
        compiler_params=params,
        name="sample_state",
    )(state, c_lm, b_lm, xw_lm, ea_lm)

    post_names = ['wz', 'ng', 'wso', 'wlo', 'wgs', 'wgl', 'bg', 'wo', 'l1g', 'l1b']
    post_consts = [wts[k] for k in post_names]
    x1_lm = pl.pallas_call(
        functools.partial(_sample_post_kernel, alpha=wts['alpha']),
        grid=(steps,),
        in_specs=[step_blk(d_model), step_blk(d_ssd), step_blk(d_ssd), step_blk(d_lru)]
        + [_const_spec(c.shape) for c in post_consts],
        out_specs=step_blk(d_model),
        out_shape=sds((ntok, d_model), F32),
        compiler_params=params,
        name="sample_post",
    )(x_lm, ydg_lm, yoff_lm, ylru_lm, *post_consts)
    return x1_lm, new_state, pre, lst, prelx


def _prep_weights(w_in, b_gate, ssd_conv_w, ssd_conv_b, ssd_dt_bias, ssd_a_log, ssd_d, ssd_norm_g,
                  w_ssd_out, lru_conv_w, lru_conv_b, lru_wa, lru_ba, lru_wx, lru_bx, lru_lambda,
                  w_lru_out, w_o, ln1_g, ln1_b, ffn_w_gate, ffn_w_up, ffn_conv_w, ffn_conv_b,
                  ffn_w_down, ln2_g, ln2_b, n_heads, headdim, d_state):
    depth = w_in.shape[0]
    d_model = w_in.shape[1]
    d_ssd = n_heads * headdim
    d_xbc = d_ssd + 2 * SSD_GROUPS * d_state
    d_lru = lru_lambda.shape[1]
    sizes = (d_ssd, d_xbc, n_heads, d_lru, d_lru, d_model, d_model)
    cuts = np.cumsum((0,) + sizes)
    pads = [LANES - n_heads if i == 2 else 0 for i in range(len(sizes))]
    parts = _pack_weight(w_in[0], [(int(cuts[i]), int(cuts[i + 1]), pads[i]) for i in range(len(sizes))])
    row = lambda v: v.reshape(1, -1).astype(F32)
    mat = _pack_weight
    wax = jnp.concatenate([lru_wa[0], lru_wx[0]], axis=-1)
    wax = _pack_weight(wax.reshape(-1, wax.shape[-1])).reshape(wax.shape[0], wax.shape[1] // 2, wax.shape[2])
    pad_heads = lambda v: jnp.pad(v.reshape(1, -1).astype(F32), ((0, 0), (0, LANES - n_heads)))
    head_of_col = np.arange(d_ssd) // headdim
    expand = (np.arange(LANES)[:, None] == head_of_col[None, :]).astype(np.float32)
    return dict(
        n_heads=n_heads, headdim=headdim, d_state=d_state,
        alpha=float((2.0 * depth) ** 0.25),
        wz=parts[0], wxbc=parts[1], wdt=parts[2], wlx=parts[3], wly=parts[4], wgs=parts[5], wgl=parts[6],
        scw=ssd_conv_w[0].astype(F32), scb=row(ssd_conv_b[0]),
        dtb=pad_heads(ssd_dt_bias[0]), aneg=pad_heads(-jnp.exp(ssd_a_log[0].astype(F32))),
        dexp=row(jnp.repeat(ssd_d[0], headdim)), ng=row(ssd_norm_g[0]),
        wso=mat(w_ssd_out[0]),
        lcw=lru_conv_w[0].astype(F32), lcb=row(lru_conv_b[0]),
        ba=row(lru_ba[0]), bx=row(lru_bx[0]), wax=wax,
        lam=row(lru_lambda[0]), wlo=mat(w_lru_out[0]),
        bg=row(b_gate[0]), wo=mat(w_o[0]), l1g=row(ln1_g[0]), l1b=row(ln1_b[0]),
        e=_pack_rows_01(expand),
        ee=_pack_rows_01(np.concatenate([expand, expand], axis=0)),
        e2=_pack_rows_01(np.arange(SSD_GROUPS * d_state)[:, None] // d_state
                         == (head_of_col // (n_heads // SSD_GROUPS))[None, :]),
        wg=mat(ffn_w_gate[0]), wu=mat(ffn_w_up[0]),
        fcw=ffn_conv_w[0].astype(F32), fcb=row(ffn_conv_b[0]), wd=mat(ffn_w_down[0]),
        l2g=row(ln2_g[0]), l2b=row(ln2_b[0]),
    )


def kernel(x_prompt, x_sample, state_ssd, cache_ssd_conv, state_lru, cache_lru_conv, cache_ffn_conv, w_in, b_gate, ssd_conv_w, ssd_conv_b, ssd_dt_bias, ssd_a_log, ssd_d, ssd_norm_g, w_ssd_out, lru_conv_w, lru_conv_b, lru_wa, lru_ba, lru_wx, lru_bx, lru_lambda, w_lru_out, w_o, ln1_g, ln1_b, ffn_w_gate, ffn_w_up, ffn_conv_w, ffn_conv_b, ffn_w_down, ln2_g, ln2_b):
    assert w_in.shape[0] == 1, "single-layer trunk"
    _, _, n_heads, headdim, d_state = state_ssd.shape
    wts = _prep_weights(w_in, b_gate, ssd_conv_w, ssd_conv_b, ssd_dt_bias, ssd_a_log, ssd_d, ssd_norm_g,
                        w_ssd_out, lru_conv_w, lru_conv_b, lru_wa, lru_ba, lru_wx, lru_bx, lru_lambda,
                        w_lru_out, w_o, ln1_g, ln1_b, ffn_w_gate, ffn_w_up, ffn_conv_w, ffn_conv_b,
                        ffn_w_down, ln2_g, ln2_b, n_heads, headdim, d_state)
    bp = x_prompt.shape[0]
    d_ff = ffn_w_gate.shape[2]

    x1_p, p_ssd, p_ssd_buf, p_lru, p_lru_buf = _prompt_mixer(x_prompt, wts, tl=256)
    y_prompt, p_ffn_buf = _ffn_seg(x1_p, jnp.zeros((bp, ffn_conv_w.shape[1] - 1, d_ff), F32), wts, tm=512)
    p_ssd = p_ssd.reshape(1, bp, n_heads, headdim, d_state)

    nb_s, steps, _ = x_sample.shape
    half = nb_s // 2

    def to_lm(a):
        k, c = a.shape[1], a.shape[2]
        return a.reshape(half, 2, k, c).transpose(2, 1, 0, 3).reshape(k * nb_s, c)

    def from_lm(a, k):
        c = a.shape[1]
        return a.reshape(k, 2, half, c).transpose(2, 1, 0, 3).reshape(nb_s, k, c)

    d_ssd = n_heads * headdim
    x1_lm, new_state, pre, lst, prelx = _sample_mixer(
        to_lm(x_sample), to_lm(cache_ssd_conv[0]), to_lm(cache_lru_conv[0]), to_lm(state_lru[0][:, None, :]),
        state_ssd[0].reshape(half, 2, d_ssd, d_state), wts, nb_s, steps, PAST_LEN)
    y_lm, tail = _ffn(x1_lm[None], to_lm(cache_ffn_conv[0])[None], wts, tm=steps * nb_s, stride=nb_s)
    k_ssd = ssd_conv_w.shape[1] - 1
    k_lru = lru_conv_w.shape[1] - 1
    k_ffn = ffn_conv_w.shape[1] - 1
    assert steps >= max(k_ssd, k_lru, k_ffn)
    return (y_prompt, from_lm(y_lm[0], steps), p_ssd, p_ssd_buf[None], p_lru.reshape(1, bp, -1), p_lru_buf[None],
            p_ffn_buf[None],
            new_state.reshape(1, nb_s, n_heads, headdim, d_state),
            from_lm(pre[(steps - k_ssd) * nb_s:], k_ssd)[None],
            from_lm(lst, 1).reshape(1, nb_s, -1),
            from_lm(prelx[(steps - k_lru) * nb_s:], k_lru)[None],
            from_lm(tail[0], k_ffn)[None])
```

```python
import functools

import numpy as np
import jax
import jax.numpy as jnp
from jax import lax
from jax.experimental import pallas as pl
from jax.experimental.pallas import tpu as pltpu

F32 = jnp.float32
BF16 = jnp.bfloat16

SSD_GROUPS = 4
SSD_CHUNK = 128
LRU_BLOCKS = 8
LRU_C = 8.0
LN_EPS = 1e-5
RMS_EPS = 1e-6
PAST_LEN = 16384

LANES = 128
SUBLANES = 8
VMEM_LIMIT_BYTES = 60 * 1024 * 1024

HIST = 8
ROW_BLK = 32
UNROLL = True


def _dot(a, b):
    if b.dtype == jnp.uint32:
        b = pltpu.bitcast(b, BF16)
    return jnp.dot(a, b, preferred_element_type=F32)


def _pack_kernel(w_ref, *out_refs, ranges):
    for o_ref, (a, b, pad) in zip(out_refs, ranges):
        v = w_ref[:, a:b]
        if pad:
            v = jnp.concatenate([v, jnp.zeros((v.shape[0], pad), v.dtype)], axis=1)
        o_ref[...] = pltpu.bitcast(v.astype(BF16), jnp.uint32)


def _pack_weight(w, ranges=None):
    k, n = w.shape
    single = ranges is None
    ranges = ((0, n, 0),) if single else tuple(ranges)
    bk = 128 if n > 4096 else 256
    widths = [b - a + pad for a, b, pad in ranges]
    outs = pl.pallas_call(
        functools.partial(_pack_kernel, ranges=ranges),
        grid=(k // bk,),
        in_specs=[pl.BlockSpec((bk, n), lambda i: (i, 0))],
        out_specs=tuple(pl.BlockSpec((bk // 2, wd), lambda i: (i, 0)) for wd in widths),
        out_shape=tuple(jax.ShapeDtypeStruct((k // 2, wd), jnp.uint32) for wd in widths),
        compiler_params=pltpu.CompilerParams(dimension_semantics=("arbitrary",),
                                             vmem_limit_bytes=VMEM_LIMIT_BYTES),
        name="pack_weight",
    )(w.astype(F32))
    return outs[0] if single else outs


def _pack_rows_01(m):
    bits = np.ascontiguousarray(m, np.float32).view(np.uint32) >> 16
    return jnp.asarray(bits[0::2] | (bits[1::2] << 16), jnp.uint32)


def _dot_nt(a, b):
    return lax.dot_general(a, b, (((1,), (1,)), ((), ())), preferred_element_type=F32)


def _dot_tn(a, b):
    return lax.dot_general(a, b, (((0,), (0,)), ((), ())), preferred_element_type=F32)


def _split3(v):
    hi = v.astype(BF16)
    r1 = v - hi.astype(F32)
    mid = r1.astype(BF16)
    lo = (r1 - mid.astype(F32)).astype(BF16)
    return hi, mid, lo


def _dot_exact_rhs(v, m):
    hi, mid, lo = _split3(v)
    return _dot(hi, m) + _dot(mid, m) + _dot(lo, m)


def _dot_exact_lhs(m, v):
    hi, mid, lo = _split3(v)
    return _dot(m, hi) + _dot(m, mid) + _dot(m, lo)


def _split2(v):
    hi = v.astype(BF16)
    lo = (v - hi.astype(F32)).astype(BF16)
    return hi, lo


def _dot_2way_rhs(v, m2):
    return _dot(jnp.concatenate(_split2(v), axis=1), m2)


def _dot_2way_lhs(m2, v):
    return _dot(m2, jnp.concatenate(_split2(v), axis=0))


def _softplus(x):
    return jnp.maximum(x, 0.0) + jnp.log1p(jnp.exp(-jnp.abs(x)))


def _sigmoid(x):
    return 0.5 * jnp.tanh(0.5 * x) + 0.5


def _silu(x):
    h = 0.5 * x
    return h + h * jnp.tanh(h)


def _gelu(x):
    c = np.sqrt(2.0 / np.pi).astype(np.float32)
    return 0.5 * x * (1.0 + jnp.tanh(c * (x + 0.044715 * (x * x * x))))


def _layer_norm(v, g, b):
    mu = jnp.mean(v, axis=-1, keepdims=True)
    d = v - mu
    var = jnp.mean(d * d, axis=-1, keepdims=True)
    return d * lax.rsqrt(var + LN_EPS) * g + b


def _rows(i, n):
    if isinstance(i, int):
        return slice(i * n, (i + 1) * n)
    return pl.ds(pl.multiple_of(i * n, n), n)


def _loop(n, body, unroll):
    if unroll:
        for i in range(n):
            body(i, None)
    else:
        lax.fori_loop(0, n, lambda i, c: (body(i, c), c)[1], 0)


def _conv_block(buf_ref, w_ref, b_ref, hist, stride, r0, rows, c0, cw):
    taps = w_ref.shape[0]
    acc = b_ref[:, c0:c0 + cw]
    for k in range(taps):
        off = hist + r0 - (taps - 1 - k) * stride
        if not isinstance(off, int):
            off = pl.multiple_of(off, SUBLANES)
        acc = acc + w_ref[k:k + 1, c0:c0 + cw] * buf_ref[pl.ds(off, rows), c0:c0 + cw]
    return acc


LOG2_SUBLANES = 3
NPOS = SSD_CHUNK // SUBLANES
LOG2_NPOS = 4
assert 1 << LOG2_SUBLANES == SUBLANES and 1 << LOG2_NPOS == NPOS


def _tok_of_row(r):
    return (r & (SUBLANES - 1)) * NPOS + lax.shift_right_logical(r, LOG2_SUBLANES)


def _row_of_tok(t):
    return (t & (NPOS - 1)) * SUBLANES + lax.shift_right_logical(t, LOG2_NPOS)


def _perm_matrix():
    q = SSD_CHUNK
    r = lax.broadcasted_iota(jnp.int32, (q, q), 0)
    c = lax.broadcasted_iota(jnp.int32, (q, q), 1)
    return jnp.where(c == _tok_of_row(r), 1.0, 0.0).astype(BF16)


def _unperm_matrix():
    q = SSD_CHUNK
    t = lax.broadcasted_iota(jnp.int32, (q, q), 0)
    r = lax.broadcasted_iota(jnp.int32, (q, q), 1)
    return jnp.where(r == _row_of_tok(t), 1.0, 0.0).astype(BF16)


def _fill_wrap(buf_ref, wrap_ref, c, wrap, tail_ref, cols=slice(None)):
    q = SSD_CHUNK
    for m in range(wrap // SUBLANES):
        r_cur = (c + 1) * q - wrap + m * SUBLANES
        cur = buf_ref[r_cur:r_cur + SUBLANES, cols]
        if c == 0:
            prv = tail_ref[m * SUBLANES:(m + 1) * SUBLANES, cols]
        else:
            prv = buf_ref[r_cur - q:r_cur - q + SUBLANES, cols]
        sub0 = lax.broadcasted_iota(jnp.int32, cur.shape, 0) == 0
        wrap_ref[c * wrap + m * SUBLANES:c * wrap + (m + 1) * SUBLANES, cols] = jnp.where(
            sub0, pltpu.roll(prv, 1, 0), pltpu.roll(cur, 1, 0))


def _conv_seg(buf_ref, wrap_ref, w_ref, b_ref, c, wrap, c0, cw):
    q = SSD_CHUNK
    taps = w_ref.shape[0]
    cols = slice(c0, c0 + cw)
    acc = b_ref[:, cols] + w_ref[taps - 1:taps, cols] * buf_ref[c * q:(c + 1) * q, cols]
    for k in range(taps - 1):
        back = (taps - 1 - k) * SUBLANES
        shifted = jnp.concatenate(
            [wrap_ref[(c + 1) * wrap - back:(c + 1) * wrap, cols], buf_ref[c * q:(c + 1) * q - back, cols]],
            axis=0)
        acc = acc + w_ref[k:k + 1, cols] * shifted
    return acc


def _seg_tail_rows(wrap):
    n = wrap // SUBLANES
    return [(NPOS - n + m) * SUBLANES + SUBLANES - 1 for m in range(n)]


def _prompt_mixer_kernel(
        x_ref, wz_ref, wxbc_ref, wdt_ref, wlx_ref, wly_ref, wgs_ref, wgl_ref,
        scw_ref, scb_ref, dtb_ref, aneg_ref, dexp_ref, ng_ref, wso_ref,
        lcw_ref, lcb_ref, wax_ref, ba_ref, bx_ref, lam_ref, wlo_ref,
        bg_ref, wo_ref, l1g_ref, l1b_ref, e_ref,
        x1_ref, st_ref, sconv_ref, lst_ref, lconv_ref,
        xb_s, xp_s, xbc_s, swrap_s, stail_s, xc_s, lx_s, lwrap_s, ltail_s, xr_s, ly_s, y_s, z_s, ysb_s, ylb_s,
        ht_s, hl_s, gs_s, gl_s, ys_s, yl_s, o_s, mb_s,
        *, tl, alpha, n_heads, headdim, d_state):
    t = pl.program_id(1)
    nt = pl.num_programs(1)
    d_ssd = n_heads * headdim
    gn = SSD_GROUPS * d_state
    hpg = n_heads // SSD_GROUPS
    gw = hpg * headdim
    d_lru = lx_s.shape[1]
    d_model = x_ref.shape[2]
    q = SSD_CHUNK

    nch = tl // q
    wrap_s = (scw_ref.shape[0] - 1) * SUBLANES
    wrap_l = (lcw_ref.shape[0] - 1) * SUBLANES

    @pl.when(t == 0)
    def _():
        stail_s[...] = jnp.zeros(stail_s.shape, F32)
        ltail_s[...] = jnp.zeros(ltail_s.shape, F32)
        ht_s[...] = jnp.zeros(ht_s.shape, F32)
        hl_s[...] = jnp.zeros(hl_s.shape, F32)

    perm = _perm_matrix()
    perm2 = jnp.concatenate([perm, perm], axis=1)
    for c in range(nch):
        rows = _rows(c, q)
        xp = _dot_2way_lhs(perm2, x_ref[0, rows, :])
        xp_s[rows, :] = xp
        xb_s[rows, :] = xp.astype(BF16)

    cwid = 512

    def proj(dst_ref, w_ref, c0, cw):
        dst_ref[:, c0:c0 + cw] = _dot(xb_s[...], w_ref[:, c0:c0 + cw])

    def lru_conv(c):
        _fill_wrap(lx_s, lwrap_s, c, wrap_l, ltail_s)
        for c0 in range(0, d_lru, cwid):
            xr_s[c * q:(c + 1) * q, c0:c0 + cwid] = _conv_seg(lx_s, lwrap_s, lcw_ref, lcb_ref, c, wrap_l, c0, cwid)
        if c == nch - 1:
            ltail_s[...] = lx_s[tl - wrap_l:tl, :]

    bw = d_lru // LRU_BLOCKS
    sub = lax.broadcasted_iota(jnp.int32, (SUBLANES, bw), 0)
    crow = lax.broadcasted_iota(jnp.int32, (q, bw), 0)

    def lru_block(n):
        cols = slice(n * bw, (n + 1) * bw)
        xr = xr_s[:, cols]
        xrb = xr.astype(BF16)
        rg = _dot(xrb, wax_ref[n])
        r = _sigmoid(rg[:, 0:bw] + ba_ref[:, cols])
        gi = _sigmoid(rg[:, bw:2 * bw] + bx_ref[:, cols])
        log_a = (-LRU_C) * r * _softplus(-lam_ref[:, cols])
        a_all = jnp.exp(log_a)
        mult_all = jnp.sqrt(1.0 - jnp.exp(2.0 * log_a))
        for c in range(nch):
            a = a_all[c * q:(c + 1) * q, :]
            mult = mult_all[c * q:(c + 1) * q, :]
            if c == 0:
                mult = jnp.where(jnp.logical_and(crow == 0, t == 0), 1.0, mult)
            u = mult * gi[c * q:(c + 1) * q, :] * xr[c * q:(c + 1) * q, :]
            a_p = [a[i * SUBLANES:(i + 1) * SUBLANES, :] for i in range(NPOS)]
            u_p = [u[i * SUBLANES:(i + 1) * SUBLANES, :] for i in range(NPOS)]
            h = u_p[0]
            g = a_p[0]
            for i in range(1, NPOS):
                h = a_p[i] * h + u_p[i]
                g = a_p[i] * g
            gs = jnp.where(sub == 0, 0.0, pltpu.roll(g, 1, 0))
            hs = jnp.where(sub == 0, hl_s[0:1, cols], pltpu.roll(h, 1, 0))
            d = 1
            while d < SUBLANES:
                keep = sub >= d
                hs = jnp.where(keep, gs * pltpu.roll(hs, d, 0) + hs, hs)
                gs = jnp.where(keep, gs * pltpu.roll(gs, d, 0), gs)
                d *= 2
            h = hs
            out = []
            for i in range(NPOS):
                h = a_p[i] * h + u_p[i]
                out.append(h)
            hl_s[0:1, cols] = h[SUBLANES - 1:SUBLANES, :]
            hseq = jnp.concatenate(out, axis=0)
            ylb_s[c * q:(c + 1) * q, cols] = (hseq * _gelu(ly_s[c * q:(c + 1) * q, cols])).astype(BF16)

    def ssd_conv(c0):
        cols = slice(c0, c0 + cwid)
        for c in range(nch):
            _fill_wrap(xbc_s, swrap_s, c, wrap_s, stail_s, cols)
        stail_s[:, cols] = xbc_s[tl - wrap_s:tl, cols]
        for c in range(nch):
            xc_s[c * q:(c + 1) * q, cols] = _silu(
                _conv_seg(xbc_s, swrap_s, scw_ref, scb_ref, c, wrap_s, c0, cwid))

    def merge_gate(dst_ref, c0, cw, b0):
        dst_ref[:, c0:c0 + cw] = _sigmoid(dst_ref[:, c0:c0 + cw] + bg_ref[:, b0 + c0:b0 + c0 + cw])

    tok_r = _tok_of_row(lax.broadcasted_iota(jnp.int32, (q, q), 0))
    tok_c = _tok_of_row(lax.broadcasted_iota(jnp.int32, (q, q), 1))
    causal = tok_r >= tok_c
    tri = jnp.where(causal, 1.0, 0.0).astype(BF16)
    lane_i = lax.broadcasted_iota(jnp.int32, (q, LANES), 1)
    left = lane_i < headdim

    def chunk_body(c, carry):
        rows = _rows(c, q)
        dt = _softplus(_dot(xb_s[rows, :], wdt_ref[...]) + dtb_ref[...])
        d_a = dt * aneg_ref[...]
        a_cs = _dot_exact_lhs(tri, d_a)
        a_last = a_cs[q - 1:q, :]
        wgt = dt * jnp.exp(a_last - a_cs)
        ea = jnp.exp(a_cs)
        w_exp = _dot_2way_rhs(wgt, e_ref[...])
        ea_exp = _dot_2way_rhs(ea, e_ref[...])
        a_cs_t = a_cs.T
        dt_t = dt.T
        for g in range(SSD_GROUPS):
            b_g = xc_s[rows, d_ssd + g * d_state:d_ssd + (g + 1) * d_state]
            c_g = xc_s[rows, d_ssd + gn + g * d_state:d_ssd + gn + (g + 1) * d_state]
            b_gb = b_g.astype(BF16)
            c_gb = c_g.astype(BF16)
            cb = _dot_nt(c_gb, b_gb)
            for hp in range(hpg // 2):
                c0 = g * gw + hp * 2 * headdim
                xs_pair = xc_s[rows, c0:c0 + 2 * headdim]
                lmats = []
                for j in range(2):
                    h = g * hpg + hp * 2 + j
                    seg = (jnp.broadcast_to(a_cs[:, h:h + 1], (q, q))
                           - jnp.broadcast_to(a_cs_t[h:h + 1, :], (q, q)))
                    dec = jnp.exp(jnp.where(causal, seg, -jnp.inf))
                    lmats.append((cb * dec * jnp.broadcast_to(dt_t[h:h + 1, :], (q, q))).astype(BF16))
                lpair = jnp.concatenate(lmats, axis=1)
                rhs = jnp.concatenate([jnp.where(left, xs_pair, 0.0),
                                       jnp.where(left, 0.0, xs_pair)], axis=0).astype(BF16)
                y_s[rows, c0:c0 + 2 * headdim] = _dot(lpair, rhs)
            gcols = slice(g * gw, (g + 1) * gw)
            h_g = ht_s[:, gcols]
            y_off = _dot(c_gb, h_g.astype(BF16)) * ea_exp[:, gcols]
            y_s[rows, gcols] = y_s[rows, gcols] + y_off
            xw = (xc_s[rows, gcols] * w_exp[:, gcols]).astype(BF16)
            ht_s[:, gcols] = h_g * ea_exp[q - 1:q, gcols] + _dot_tn(b_gb, xw)
            if carry is not None and g < len(carry):
                carry[g]()
        return carry

    def gate_body(i, carry):
        rows = _rows(i, ROW_BLK)
        y = y_s[rows, :] + dexp_ref[...] * xc_s[rows, 0:d_ssd]
        y = y * _silu(z_s[rows, :])
        ms = jnp.mean(y * y, axis=-1, keepdims=True)
        ysb_s[rows, :] = (y * lax.rsqrt(ms + RMS_EPS) * ng_ref[...]).astype(BF16)
        return carry

    assert nch == 2 and LRU_BLOCKS == 8 and d_lru == 2 * cwid and d_model == 2 * cwid
    d_xbc = xc_s.shape[1]
    nxb = d_xbc // cwid
    proj(lx_s, wlx_ref, 0, cwid)
    proj(lx_s, wlx_ref, cwid, cwid)
    proj(ly_s, wly_ref, 0, cwid); lru_conv(0)
    proj(ly_s, wly_ref, cwid, cwid); lru_conv(1)
    P = functools.partial
    mxu_a = ([P(proj, xbc_s, wxbc_ref, j * cwid, cwid) for j in range(nxb)]
             + [P(proj, gs_s, wgs_ref, j * cwid, cwid) for j in range(2)])
    for n in range(LRU_BLOCKS):
        mxu_a[n]()
        lru_block(n)
        if 1 <= n <= nxb:
            ssd_conv((n - 1) * cwid)
    gates_per_chunk = q // ROW_BLK
    nz = d_ssd // cwid
    chunk_body(0, [P(proj, z_s, wz_ref, j * cwid, cwid) for j in range(nz)])
    merge_gate(gs_s, 0, d_model, 0)
    for i in range(gates_per_chunk):
        gate_body(i, None)

    def wlo_piece(j):
        yl_s[:, j * cwid:(j + 1) * cwid] = _dot(ylb_s[...], wlo_ref[:, j * cwid:(j + 1) * cwid])

    chunk_body(1, [P(proj, gl_s, wgl_ref, 0, cwid), P(proj, gl_s, wgl_ref, cwid, cwid),
                   P(wlo_piece, 0), P(wlo_piece, 1)])
    merge_gate(gl_s, 0, d_model, d_model)
    for i in range(gates_per_chunk, 2 * gates_per_chunk):
        gate_body(i, None)

    ys_s[...] = _dot(ysb_s[...], wso_ref[...])
    for i in range(tl // ROW_BLK):
        rows = _rows(i, ROW_BLK)
        mb_s[rows, :] = (gs_s[rows, :] * ys_s[rows, :] + gl_s[rows, :] * yl_s[rows, :]).astype(BF16)
    o_s[...] = _dot(mb_s[...], wo_ref[...])
    for i in range(tl // ROW_BLK):
        rows = _rows(i, ROW_BLK)
        v = alpha * xp_s[rows, :] + o_s[rows, :]
        x1_ref[0, rows, :] = _layer_norm(v, l1g_ref[...], l1b_ref[...])

    @pl.when(t == nt - 1)
    def _():
        st_ref[0] = ht_s[...].T
        lst_ref[0] = hl_s[0:1, :]
        for m, r in enumerate(_seg_tail_rows(wrap_s)):
            sconv_ref[0, m:m + 1, :] = xbc_s[tl - q + r:tl - q + r + 1, :]
        for m, r in enumerate(_seg_tail_rows(wrap_l)):
            lconv_ref[0, m:m + 1, :] = lx_s[tl - q + r:tl - q + r + 1, :]


def _const_spec(shape):
    nd = len(shape)
    return pl.BlockSpec(shape, lambda *_: (0,) * nd, pipeline_mode=pl.Buffered(1))


def _prompt_mixer(x, wts, tl):
    nb, seq, d_model = x.shape
    n_heads, headdim, d_state = wts['n_heads'], wts['headdim'], wts['d_state']
    d_ssd = n_heads * headdim
    d_xbc = d_ssd + 2 * SSD_GROUPS * d_state
    d_lru = wts['wlx'].shape[1]
    names = ['wz', 'wxbc', 'wdt', 'wlx', 'wly', 'wgs', 'wgl', 'scw', 'scb', 'dtb', 'aneg', 'dexp', 'ng',
             'wso', 'lcw', 'lcb', 'wax', 'ba', 'bx', 'lam', 'wlo', 'bg', 'wo', 'l1g', 'l1b', 'ee']
    consts = [wts[k] for k in names]
    kern = functools.partial(_prompt_mixer_kernel, tl=tl, alpha=wts['alpha'], n_heads=n_heads,
                             headdim=headdim, d_state=d_state)
    out_shape = (
        jax.ShapeDtypeStruct((nb, seq, d_model), F32),
        jax.ShapeDtypeStruct((nb, d_ssd, d_state), F32),
        jax.ShapeDtypeStruct((nb, 3, d_xbc), F32),
        jax.ShapeDtypeStruct((nb, 1, d_lru), F32),
        jax.ShapeDtypeStruct((nb, 3, d_lru), F32),
    )
    out_specs = (
        pl.BlockSpec((1, tl, d_model), lambda b, t: (b, t, 0)),
        pl.BlockSpec((1, d_ssd, d_state), lambda b, t: (b, 0, 0)),
        pl.BlockSpec((1, 3, d_xbc), lambda b, t: (b, 0, 0)),
        pl.BlockSpec((1, 1, d_lru), lambda b, t: (b, 0, 0)),
        pl.BlockSpec((1, 3, d_lru), lambda b, t: (b, 0, 0)),
    )
    nch = tl // SSD_CHUNK
    wrap_s = (wts['scw'].shape[0] - 1) * SUBLANES
    wrap_l = (wts['lcw'].shape[0] - 1) * SUBLANES
    scratch = [
        pltpu.VMEM((tl, d_model), BF16),
        pltpu.VMEM((tl, d_model), F32),
        pltpu.VMEM((tl, d_xbc), F32),
        pltpu.VMEM((nch * wrap_s, d_xbc), F32),
        pltpu.VMEM((wrap_s, d_xbc), F32),
        pltpu.VMEM((tl, d_xbc), F32),
        pltpu.VMEM((tl, d_lru), F32),
        pltpu.VMEM((nch * wrap_l, d_lru), F32),
        pltpu.VMEM((wrap_l, d_lru), F32),
        pltpu.VMEM((tl, d_lru), F32),
        pltpu.VMEM((tl, d_lru), F32),
        pltpu.VMEM((tl, d_ssd), F32),
        pltpu.VMEM((tl, d_ssd), F32),
        pltpu.VMEM((tl, d_ssd), BF16),
        pltpu.VMEM((tl, d_lru), BF16),
        pltpu.VMEM((d_state, d_ssd), F32),
        pltpu.VMEM((SUBLANES, d_lru), F32),
        pltpu.VMEM((tl, d_model), F32),
        pltpu.VMEM((tl, d_model), F32),
        pltpu.VMEM((tl, d_model), F32),
        pltpu.VMEM((tl, d_model), F32),
        pltpu.VMEM((tl, d_model), F32),
        pltpu.VMEM((tl, d_model), BF16),
    ]
    return pl.pallas_call(
        kern,
        grid=(nb, seq // tl),
        in_specs=[pl.BlockSpec((1, tl, d_model), lambda b, t: (b, t, 0))]
        + [_const_spec(c.shape) for c in consts],
        out_specs=out_specs,
        out_shape=out_shape,
        scratch_shapes=scratch,
        compiler_params=pltpu.CompilerParams(
            dimension_semantics=("arbitrary", "arbitrary"),
            vmem_limit_bytes=VMEM_LIMIT_BYTES),
        name="prompt_mixer",
    )(x, *consts)


def _ffn_kernel(x_ref, h0_ref, wg_ref, wu_ref, cw_ref, cb_ref, wd_ref, g_ref, b_ref,
                y_ref, tail_ref, xb_s, gb_s, hb_s, *, tm, stride, hist, alpha, fchunk):
    t = pl.program_id(1)
    nt = pl.num_programs(1)
    taps = cw_ref.shape[0]
    nh = (taps - 1) * stride
    d_ff = gb_s.shape[1]

    @pl.when(t == 0)
    def _():
        gb_s[hist - nh:hist, :] = h0_ref[0]

    xb_s[...] = x_ref[0].astype(BF16)
    acc = None
    for c0 in range(0, d_ff, fchunk):
        cols = slice(c0, c0 + fchunk)
        gb_s[hist:hist + tm, cols] = _dot(xb_s[...], wg_ref[:, cols])
        up = _dot(xb_s[...], wu_ref[:, cols])
        gc = _conv_block(gb_s, cw_ref, cb_ref, hist, stride, 0, tm, c0, fchunk)
        hb_s[...] = (_gelu(gc) * up).astype(BF16)
        part = _dot(hb_s[...], wd_ref[c0 // 2:(c0 + fchunk) // 2, :])
        acc = part if acc is None else acc + part
    v = alpha * x_ref[0] + acc
    y_ref[0] = _layer_norm(v, g_ref[...], b_ref[...])
    gb_s[hist - nh:hist, :] = gb_s[hist + tm - nh:hist + tm, :]

    @pl.when(t == nt - 1)
    def _():
        tail_ref[0] = gb_s[hist - nh:hist, :]


def _ffn_seg_kernel(x_ref, h0_ref, st_ref, c8_ref, b8_ref, xw8_ref, ea8_ref,
                    wg_ref, wu_ref, cw_ref, cb_ref, wd_ref, g_ref, b_ref,
                    y_ref, tail_ref, nst_ref, yoff8_ref,
                    xb_s, gb_s, gwrap_s, gtail_s, hb_s, *, tm, alpha, fchunk, state_dims):
    t = pl.program_id(1)
    state_pieces = [
        functools.partial(_state_pair_group, g, st_ref.at[p], c8_ref.at[p], b8_ref.at[p], xw8_ref.at[p],
                          ea8_ref.at[p], nst_ref.at[p], yoff8_ref.at[p], **state_dims)
        for p in range(st_ref.shape[0]) for g in range(SSD_GROUPS)]

    def state_work(n):
        for _ in range(min(n, len(state_pieces))):
            state_pieces.pop(0)()
    nt = pl.num_programs(1)
    q = SSD_CHUNK
    nch = tm // q
    wrap = (cw_ref.shape[0] - 1) * SUBLANES
    d_ff = gb_s.shape[1]
    tail_rows = _seg_tail_rows(wrap)

    @pl.when(t == 0)
    def _():
        gtail_s[...] = jnp.zeros(gtail_s.shape, F32)
        for m in range(len(tail_rows)):
            r = m * SUBLANES + SUBLANES - 1
            gtail_s[r:r + 1, :] = h0_ref[0, m:m + 1, :]

    xb_s[...] = x_ref[0].astype(BF16)
    gb_s[...] = _dot(xb_s[...], wg_ref[...])
    per_slot = -(-len(state_pieces) // (1 + 2 * (d_ff // fchunk)))
    state_work(per_slot)
    for c in range(nch):
        _fill_wrap(gb_s, gwrap_s, c, wrap, gtail_s)
    acc = None
    for c0 in range(0, d_ff, fchunk):
        cols = slice(c0, c0 + fchunk)
        up = _dot(xb_s[...], wu_ref[:, cols])
        state_work(per_slot)
        for c in range(nch):
            gc = _conv_seg(gb_s, gwrap_s, cw_ref, cb_ref, c, wrap, c0, fchunk)
            hb_s[c * q:(c + 1) * q, :] = (_gelu(gc) * up[c * q:(c + 1) * q, :]).astype(BF16)
        part = _dot(hb_s[...], wd_ref[c0 // 2:(c0 + fchunk) // 2, :])
        state_work(per_slot)
        acc = part if acc is None else acc + part
    state_work(len(state_pieces))
    gtail_s[...] = gb_s[tm - wrap:tm, :]

    unperm = _unperm_matrix()
    unperm2 = jnp.concatenate([unperm, unperm], axis=1)
    for c in range(nch):
        rows = slice(c * q, (c + 1) * q)
        y = _layer_norm(alpha * x_ref[0, rows, :] + acc[rows, :], g_ref[...], b_ref[...])
        y_ref[0, rows, :] = _dot_2way_lhs(unperm2, y)

    @pl.when(t == nt - 1)
    def _():
        for m, r in enumerate(tail_rows):
            tail_ref[0, m:m + 1, :] = gb_s[tm - q + r:tm - q + r + 1, :]


def _ffn_seg(x, hist0, state, c8, b8, xw8, ea8, wts, tm, steps):
    nb, seq, d_model = x.shape
    d_ff = wts['wg'].shape[1]
    taps = wts['fcw'].shape[0]
    wrap = (taps - 1) * SUBLANES
    fchunk = 1024
    nt = seq // tm
    npairs = state.shape[0]
    pps = npairs // (nb * nt)
    assert pps * nb * nt == npairs
    consts = [wts[k] for k in ['wg', 'wu', 'fcw', 'fcb', 'wd', 'l2g', 'l2b']]
    state_dims = dict(steps=steps, n_heads=wts['n_heads'], headdim=wts['headdim'], d_state=wts['d_state'])
    kern = functools.partial(_ffn_seg_kernel, tm=tm, alpha=wts['alpha'], fchunk=fchunk, state_dims=state_dims)
    pair_blk = lambda a: pl.BlockSpec((pps,) + a.shape[1:], lambda b, t: (b * nt + t,) + (0,) * (a.ndim - 1))
    return pl.pallas_call(
        kern,
        grid=(nb, nt),
        in_specs=[pl.BlockSpec((1, tm, d_model), lambda b, t: (b, t, 0)),
                  pl.BlockSpec((1, taps - 1, d_ff), lambda b, t: (b, 0, 0)),
                  pair_blk(state), pair_blk(c8), pair_blk(b8), pair_blk(xw8), pair_blk(ea8)]
        + [_const_spec(c.shape) for c in consts],
        out_specs=(pl.BlockSpec((1, tm, d_model), lambda b, t: (b, t, 0)),
                   pl.BlockSpec((1, taps - 1, d_ff), lambda b, t: (b, 0, 0)),
                   pair_blk(state), pair_blk(xw8)),
        out_shape=(jax.ShapeDtypeStruct((nb, seq, d_model), F32),
                   jax.ShapeDtypeStruct((nb, taps - 1, d_ff), F32),
                   jax.ShapeDtypeStruct(state.shape, F32),
                   jax.ShapeDtypeStruct(xw8.shape, F32)),
        scratch_shapes=[pltpu.VMEM((tm, d_model), BF16),
                        pltpu.VMEM((tm, d_ff), F32),
                        pltpu.VMEM((tm // SSD_CHUNK * wrap, d_ff), F32),
                        pltpu.VMEM((wrap, d_ff), F32),
                        pltpu.VMEM((tm, fchunk), BF16)],
        compiler_params=pltpu.CompilerParams(
            dimension_semantics=("arbitrary", "arbitrary"),
            vmem_limit_bytes=VMEM_LIMIT_BYTES),
        name="conv_ffn_seg",
    )(x, hist0, state, c8, b8, xw8, ea8, *consts)


def _ffn(x, hist0, wts, tm, stride):
    nb, seq, d_model = x.shape
    d_ff = wts['wg'].shape[1]
    taps = wts['fcw'].shape[0]
    nh = (taps - 1) * stride
    hist = -(-nh // SUBLANES) * SUBLANES
    consts = [wts[k] for k in ['wg', 'wu', 'fcw', 'fcb', 'wd', 'l2g', 'l2b']]
    kern = functools.partial(_ffn_kernel, tm=tm, stride=stride, hist=hist, alpha=wts['alpha'], fchunk=1024)
    return pl.pallas_call(
        kern,
        grid=(nb, seq // tm),
        in_specs=[pl.BlockSpec((1, tm, d_model), lambda b, t: (b, t, 0)),
                  pl.BlockSpec((1, nh, d_ff), lambda b, t: (b, 0, 0))]
        + [_const_spec(c.shape) for c in consts],
        out_specs=(pl.BlockSpec((1, tm, d_model), lambda b, t: (b, t, 0)),
                   pl.BlockSpec((1, nh, d_ff), lambda b, t: (b, 0, 0))),
        out_shape=(jax.ShapeDtypeStruct((nb, seq, d_model), F32),
                   jax.ShapeDtypeStruct((nb, nh, d_ff), F32)),
        scratch_shapes=[pltpu.VMEM((tm, d_model), BF16),
                        pltpu.VMEM((hist + tm, d_ff), F32),
                        pltpu.VMEM((tm, 1024), BF16)],
        compiler_params=pltpu.CompilerParams(
            dimension_semantics=("arbitrary", "arbitrary"),
            vmem_limit_bytes=VMEM_LIMIT_BYTES),
        name="conv_ffn",
    )(x, hist0, *consts)


def _sample_ssd_kernel(
        x_ref, xall_ref, cssd_ref, wxbc_ref, wdt_ref, scw_ref, scb_ref, dtb_ref, aneg_ref, dexp_ref,
        e_ref, e2_ref,
        pre_ref, c_ref, b_ref, ea_ref, ydg_ref, xw_ref,
        xbc_s, xs_s, bs_s, acs_s, dts_s,
        *, nseq, steps, n_heads, headdim, d_state):
    l = pl.program_id(0)
    d_ssd = n_heads * headdim
    gn = SSD_GROUPS * d_state
    hist = (scw_ref.shape[0] - 1) * nseq
    r0 = l * nseq

    def blk(i):
        return pl.ds(pl.multiple_of(i * nseq, nseq), nseq)

    def sblk(i):
        return slice(i * nseq, (i + 1) * nseq)

    @pl.when(l == 0)
    def _():
        xbc_s[0:hist, :] = cssd_ref[...]
        dts_s[...] = _softplus(_dot(xall_ref[...].astype(BF16), wdt_ref[...]) + dtb_ref[...])
        acc = jnp.zeros((nseq, LANES), F32)
        for s in range(steps):
            acc = acc + dts_s[sblk(s), :] * aneg_ref[...]
            acs_s[sblk(s), :] = acc

    xb = x_ref[...].astype(BF16)

    pre = _dot(xb, wxbc_ref[...])
    pre_ref[...] = pre
    xbc_s[pl.ds(pl.multiple_of(hist + r0, nseq), nseq), :] = pre
    cwid = 512
    for c0 in range(0, d_ssd, cwid):
        xs_s[blk(l), c0:c0 + cwid] = _silu(
            _conv_block(xbc_s, scw_ref, scb_ref, hist, nseq, r0, nseq, c0, cwid))
    b_l = _silu(_conv_block(xbc_s, scw_ref, scb_ref, hist, nseq, r0, nseq, d_ssd, gn))
    c_l = _silu(_conv_block(xbc_s, scw_ref, scb_ref, hist, nseq, r0, nseq, d_ssd + gn, gn))
    bs_s[blk(l), :] = b_l
    b_ref[...] = b_l
    c_ref[...] = c_l

    a_cs = acs_s[blk(l), :]
    dt = dts_s[blk(l), :]
    a_end = acs_s[sblk(steps - 1), :]
    ea_ref[...] = _dot_exact_rhs(jnp.exp(a_cs), e_ref[...])
    xw_ref[...] = xs_s[blk(l), :] * _dot_exact_rhs(dt * jnp.exp(a_end - a_cs), e_ref[...])

    ydg_ref[...] = dexp_ref[...] * xs_s[blk(l), :]
    for s in range(steps):
        @pl.when(s <= l)
        def _(s=s):
            coef = _dot_exact_rhs(jnp.exp(a_cs - acs_s[sblk(s), :]) * dts_s[sblk(s), :], e_ref[...])
            cbx = _dot_exact_rhs(bs_s[sblk(s), :] * c_l, e2_ref[...])
            ydg_ref[...] += cbx * coef * xs_s[sblk(s), :]


def _sample_lru_kernel(
        x_ref, clru_ref, slru_ref, wlx_ref, wly_ref, lcw_ref, lcb_ref, wax_ref, ba_ref, bx_ref,
        lam_ref, prelx_ref, ylru_ref, lst_ref, lx_s, hl_s, *, nseq, steps, start_pos):
    l = pl.program_id(0)
    hist = (lcw_ref.shape[0] - 1) * nseq
    d_lru = lx_s.shape[1]
    r0 = l * nseq

    @pl.when(l == 0)
    def _():
        lx_s[0:hist, :] = clru_ref[...]
        hl_s[...] = slru_ref[...]

    xb = x_ref[...].astype(BF16)
    prelx = _dot(xb, wlx_ref[...])
    prelx_ref[...] = prelx
    lx_s[pl.ds(pl.multiple_of(hist + r0, nseq), nseq), :] = prelx
    ly = _dot(xb, wly_ref[...])
    bw = d_lru // LRU_BLOCKS
    first = (l + start_pos) == 0
    for n in range(LRU_BLOCKS):
        cols = slice(n * bw, (n + 1) * bw)
        xr = _conv_block(lx_s, lcw_ref, lcb_ref, hist, nseq, r0, nseq, n * bw, bw)
        xrb = xr.astype(BF16)
        rg = _dot(xrb, wax_ref[n])
        r = _sigmoid(rg[:, 0:bw] + ba_ref[:, cols])
        gi = _sigmoid(rg[:, bw:2 * bw] + bx_ref[:, cols])
        log_a = (-LRU_C) * r * _softplus(-lam_ref[:, cols])
        a = jnp.exp(log_a)
        mult = jnp.where(first, 1.0, jnp.sqrt(1.0 - jnp.exp(2.0 * log_a)))
        h = a * hl_s[:, cols] + mult * gi * xr
        hl_s[:, cols] = h
        ylru_ref[:, cols] = (h * _gelu(ly[:, cols])).astype(BF16)

    @pl.when(l == steps - 1)
    def _():
        lst_ref[...] = hl_s[...]


def _state_pair_group(g, st_ref, c8_ref, b8_ref, xw8_ref, ea8_ref, nst_ref, yoff8_ref,
                      *, steps, n_heads, headdim, d_state):
    nrow = 2 * steps
    hpg = n_heads // SSD_GROUPS
    gw = hpg * headdim
    assert 2 * headdim == LANES and d_state == LANES
    par = lax.broadcasted_iota(jnp.int32, (nrow, gw), 0) % 2
    low = lax.broadcasted_iota(jnp.int32, (nrow, LANES), 1) < headdim
    gcols = slice(g * gw, (g + 1) * gw)
    c8g = c8_ref[:, g * d_state:(g + 1) * d_state].astype(BF16)
    b8g = b8_ref[:, g * d_state:(g + 1) * d_state].astype(BF16)
    xw8 = xw8_ref[:, gcols]
    ea8 = ea8_ref[:, gcols]
    cds = []
    for hp in range(hpg // 2):
        pair = ea8[:, hp * LANES:(hp + 1) * LANES]
        swapped = pltpu.roll(pair, headdim, 1)
        cds.append(jnp.where(low, pair, swapped))
        cds.append(jnp.where(low, swapped, pair))
    yo = None
    for e in range(2):
        sg = st_ref[e, gcols, :]
        yo_e = _dot_nt(c8g, sg.astype(BF16))
        yo = yo_e if e == 0 else jnp.where(par == e, yo_e, yo)
        xw_e = jnp.where(par == e, xw8, 0.0).astype(BF16)
        upd = _dot_tn(xw_e, b8g)
        k_last = 2 * (steps - 1) + e
        for hh in range(hpg):
            cd = cds[hh][k_last:k_last + 1, :]
            hr = slice(hh * headdim, (hh + 1) * headdim)
            nst_ref[e, g * gw + hh * headdim:g * gw + (hh + 1) * headdim, :] = sg[hr, :] * cd + upd[hr, :]
    yoff8_ref[:, gcols] = yo * ea8


def _sample_post_kernel(x_ref, ydg_ref, yoff_ref, ylru_ref, wz_ref, ng_ref, wso_ref, wlo_ref,
                        wgs_ref, wgl_ref, bg_ref, wo_ref, l1g_ref, l1b_ref, x1_ref, *, alpha):
    d_model = x_ref.shape[1]
    xb = x_ref[...].astype(BF16)
    y = (ydg_ref[...] + yoff_ref[...]) * _silu(_dot(xb, wz_ref[...]))
    ms = jnp.mean(y * y, axis=-1, keepdims=True)
    ysb = (y * lax.rsqrt(ms + RMS_EPS) * ng_ref[...]).astype(BF16)
    g_ssd = _sigmoid(_dot(xb, wgs_ref[...]) + bg_ref[:, 0:d_model])
    g_lru = _sigmoid(_dot(xb, wgl_ref[...]) + bg_ref[:, d_model:2 * d_model])
    merged = g_ssd * _dot(ysb, wso_ref[...]) + g_lru * _dot(ylru_ref[...], wlo_ref[...])
    o = _dot(merged.astype(BF16), wo_ref[...])
    x1_ref[...] = _layer_norm(alpha * x_ref[...] + o, l1g_ref[...], l1b_ref[...])


def _sample_front(x_lm, cssd_lm, clru_lm, slru_lm, wts, nseq, steps, start_pos):
    n_heads, headdim, d_state = wts['n_heads'], wts['headdim'], wts['d_state']
    d_model = x_lm.shape[1]
    d_ssd = n_heads * headdim
    gn = SSD_GROUPS * d_state
    d_xbc = d_ssd + 2 * gn
    d_lru = wts['wlx'].shape[1]
    ntok = steps * nseq
    dims = dict(nseq=nseq, steps=steps, n_heads=n_heads, headdim=headdim, d_state=d_state)
    params = pltpu.CompilerParams(dimension_semantics=("arbitrary",), vmem_limit_bytes=VMEM_LIMIT_BYTES)
    step_blk = lambda w: pl.BlockSpec((nseq, w), lambda l: (l, 0))

    sds = jax.ShapeDtypeStruct
    ssd_names = ['wxbc', 'wdt', 'scw', 'scb', 'dtb', 'aneg', 'dexp', 'e', 'e2']
    ssd_consts = [x_lm, cssd_lm] + [wts[k] for k in ssd_names]
    pre, c_lm, b_lm, ea_lm, ydg_lm, xw_lm = pl.pallas_call(
        functools.partial(_sample_ssd_kernel, **dims),
        grid=(steps,),
        in_specs=[step_blk(d_model)] + [_const_spec(c.shape) for c in ssd_consts],
        out_specs=(step_blk(d_xbc), step_blk(gn), step_blk(gn), step_blk(d_ssd), step_blk(d_ssd),
                   step_blk(d_ssd)),
        out_shape=(sds((ntok, d_xbc), F32), sds((ntok, gn), F32), sds((ntok, gn), F32),
                   sds((ntok, d_ssd), F32), sds((ntok, d_ssd), F32), sds((ntok, d_ssd), F32)),
        scratch_shapes=[
            pltpu.VMEM((cssd_lm.shape[0] + ntok, d_xbc), F32),
            pltpu.VMEM((ntok, d_ssd), F32),
            pltpu.VMEM((ntok, gn), F32),
            pltpu.VMEM((ntok, LANES), F32),
            pltpu.VMEM((ntok, LANES), F32),
        ],
        compiler_params=params,
        name="sample_ssd",
    )(x_lm, *ssd_consts)

    lru_names = ['wlx', 'wly', 'lcw', 'lcb', 'wax', 'ba', 'bx', 'lam']
    lru_consts = [clru_lm, slru_lm] + [wts[k] for k in lru_names]
    prelx, ylru_lm, lst = pl.pallas_call(
        functools.partial(_sample_lru_kernel, nseq=nseq, steps=steps, start_pos=start_pos),
        grid=(steps,),
        in_specs=[step_blk(d_model)] + [_const_spec(c.shape) for c in lru_consts],
        out_specs=(step_blk(d_lru), step_blk(d_lru), pl.BlockSpec((nseq, d_lru), lambda l: (0, 0))),
        out_shape=(sds((ntok, d_lru), F32), sds((ntok, d_lru), BF16), sds((nseq, d_lru), F32)),
        scratch_shapes=[
            pltpu.VMEM((clru_lm.shape[0] + ntok, d_lru), F32),
            pltpu.VMEM((nseq, d_lru), F32),
        ],
        compiler_params=params,
        name="sample_lru",
    )(x_lm, *lru_consts)

    return dict(pre=pre, c=c_lm, b=b_lm, ea=ea_lm, ydg=ydg_lm, xw=xw_lm, prelx=prelx, ylru=ylru_lm, lst=lst)


def _sample_back(x_lm, ydg_lm, yoff_lm, ylru_lm, wts, nseq, steps):
    d_model = x_lm.shape[1]
    d_ssd = wts['n_heads'] * wts['headdim']
    d_lru = wts['wlx'].shape[1]
    ntok = steps * nseq
    params = pltpu.CompilerParams(dimension_semantics=("arbitrary",), vmem_limit_bytes=VMEM_LIMIT_BYTES)
    step_blk = lambda w: pl.BlockSpec((nseq, w), lambda l: (l, 0))
    sds = jax.ShapeDtypeStruct
    post_names = ['wz', 'ng', 'wso', 'wlo', 'wgs', 'wgl', 'bg', 'wo', 'l1g', 'l1b']
    post_consts = [wts[k] for k in post_names]
    x1_lm = pl.pallas_call(
        functools.partial(_sample_post_kernel, alpha=wts['alpha']),
        grid=(steps,),
        in_specs=[step_blk(d_model), step_blk(d_ssd), step_blk(d_ssd), step_blk(d_lru)]
        + [_const_spec(c.shape) for c in post_consts],
        out_specs=step_blk(d_model),
        out_shape=sds((ntok, d_model), F32),
        compiler_params=params,
        name="sample_post",
    )(x_lm, ydg_lm, yoff_lm, ylru_lm, *post_consts)
    return x1_lm


def _prep_weights(w_in, b_gate, ssd_conv_w, ssd_conv_b, ssd_dt_bias, ssd_a_log, ssd_d, ssd_norm_g,
                  w_ssd_out, lru_conv_w, lru_conv_b, lru_wa, lru_ba, lru_wx, lru_bx, lru_lambda,
                  w_lru_out, w_o, ln1_g, ln1_b, ffn_w_gate, ffn_w_up, ffn_conv_w, ffn_conv_b,
                  ffn_w_down, ln2_g, ln2_b, n_heads, headdim, d_state):
    depth = w_in.shape[0]
    d_model = w_in.shape[1]
    d_ssd = n_heads * headdim
    d_xbc = d_ssd + 2 * SSD_GROUPS * d_state
    d_lru = lru_lambda.shape[1]
    sizes = (d_ssd, d_xbc, n_heads, d_lru, d_lru, d_model, d_model)
    cuts = np.cumsum((0,) + sizes)
    pads = [LANES - n_heads if i == 2 else 0 for i in range(len(sizes))]
    parts = _pack_weight(w_in[0], [(int(cuts[i]), int(cuts[i + 1]), pads[i]) for i in range(len(sizes))])
    row = lambda v: v.reshape(1, -1).astype(F32)
    mat = _pack_weight
    wax = jnp.concatenate([lru_wa[0], lru_wx[0]], axis=-1)
    wax = _pack_weight(wax.reshape(-1, wax.shape[-1])).reshape(wax.shape[0], wax.shape[1] // 2, wax.shape[2])
    pad_heads = lambda v: jnp.pad(v.reshape(1, -1).astype(F32), ((0, 0), (0, LANES - n_heads)))
    head_of_col = np.arange(d_ssd) // headdim
    expand = (np.arange(LANES)[:, None] == head_of_col[None, :]).astype(np.float32)
    return dict(
        n_heads=n_heads, headdim=headdim, d_state=d_state,
        alpha=float((2.0 * depth) ** 0.25),
        wz=parts[0], wxbc=parts[1], wdt=parts[2], wlx=parts[3], wly=parts[4], wgs=parts[5], wgl=parts[6],
        scw=ssd_conv_w[0].astype(F32), scb=row(ssd_conv_b[0]),
        dtb=pad_heads(ssd_dt_bias[0]), aneg=pad_heads(-jnp.exp(ssd_a_log[0].astype(F32))),
        dexp=row(jnp.repeat(ssd_d[0], headdim)), ng=row(ssd_norm_g[0]),
        wso=mat(w_ssd_out[0]),
        lcw=lru_conv_w[0].astype(F32), lcb=row(lru_conv_b[0]),
        ba=row(lru_ba[0]), bx=row(lru_bx[0]), wax=wax,
        lam=row(lru_lambda[0]), wlo=mat(w_lru_out[0]),
        bg=row(b_gate[0]), wo=mat(w_o[0]), l1g=row(ln1_g[0]), l1b=row(ln1_b[0]),
        e=_pack_rows_01(expand),
        ee=_pack_rows_01(np.concatenate([expand, expand], axis=0)),
        e2=_pack_rows_01(np.arange(SSD_GROUPS * d_state)[:, None] // d_state
                         == (head_of_col // (n_heads // SSD_GROUPS))[None, :]),
        wg=mat(ffn_w_gate[0]), wu=mat(ffn_w_up[0]),
        fcw=ffn_conv_w[0].astype(F32), fcb=row(ffn_conv_b[0]), wd=mat(ffn_w_down[0]),
        l2g=row(ln2_g[0]), l2b=row(ln2_b[0]),
    )


def kernel(x_prompt, x_sample, state_ssd, cache_ssd_conv, state_lru, cache_lru_conv, cache_ffn_conv, w_in, b_gate, ssd_conv_w, ssd_conv_b, ssd_dt_bias, ssd_a_log, ssd_d, ssd_norm_g, w_ssd_out, lru_conv_w, lru_conv_b, lru_wa, lru_ba, lru_wx, lru_bx, lru_lambda, w_lru_out, w_o, ln1_g, ln1_b, ffn_w_gate, ffn_w_up, ffn_conv_w, ffn_conv_b, ffn_w_down, ln2_g, ln2_b):
    assert w_in.shape[0] == 1, "single-layer trunk"
    _, _, n_heads, headdim, d_state = state_ssd.shape
    wts = _prep_weights(w_in, b_gate, ssd_conv_w, ssd_conv_b, ssd_dt_bias, ssd_a_log, ssd_d, ssd_norm_g,
                        w_ssd_out, lru_conv_w, lru_conv_b, lru_wa, lru_ba, lru_wx, lru_bx, lru_lambda,
                        w_lru_out, w_o, ln1_g, ln1_b, ffn_w_gate, ffn_w_up, ffn_conv_w, ffn_conv_b,
                        ffn_w_down, ln2_g, ln2_b, n_heads, headdim, d_state)
    bp = x_prompt.shape[0]
    d_ff = ffn_w_gate.shape[2]

    nb_s, steps, _ = x_sample.shape
    half = nb_s // 2

    def to_lm(a):
        k, c = a.shape[1], a.shape[2]
        return a.reshape(half, 2, k, c).transpose(2, 1, 0, 3).reshape(k * nb_s, c)

    def from_lm(a, k):
        c = a.shape[1]
        return a.reshape(k, 2, half, c).transpose(2, 1, 0, 3).reshape(nb_s, k, c)

    def to_pairs(a):
        return a.reshape(steps, 2, half, a.shape[1]).transpose(2, 0, 1, 3).reshape(half, 2 * steps, a.shape[1])

    def from_pairs(a):
        return a.reshape(half, steps, 2, a.shape[2]).transpose(1, 2, 0, 3).reshape(steps * nb_s, a.shape[2])

    d_ssd = n_heads * headdim
    x_lm = to_lm(x_sample)
    sf = _sample_front(x_lm, to_lm(cache_ssd_conv[0]), to_lm(cache_lru_conv[0]),
                       to_lm(state_lru[0][:, None, :]), wts, nb_s, steps, PAST_LEN)
    pre, lst, prelx = sf['pre'], sf['lst'], sf['prelx']

    x1_p, p_ssd, p_ssd_buf, p_lru, p_lru_buf = _prompt_mixer(x_prompt, wts, tl=256)
    y_prompt, p_ffn_buf, new_state, yoff8 = _ffn_seg(
        x1_p, jnp.zeros((bp, ffn_conv_w.shape[1] - 1, d_ff), F32),
        state_ssd[0].reshape(half, 2, d_ssd, d_state), to_pairs(sf['c']), to_pairs(sf['b']),
        to_pairs(sf['xw']), to_pairs(sf['ea']), wts, tm=512, steps=steps)
    p_ssd = p_ssd.reshape(1, bp, n_heads, headdim, d_state)

    x1_lm = _sample_back(x_lm, sf['ydg'], from_pairs(yoff8), sf['ylru'], wts, nb_s, steps)
    y_lm, tail = _ffn(x1_lm[None], to_lm(cache_ffn_conv[0])[None], wts, tm=steps * nb_s, stride=nb_s)
    k_ssd = ssd_conv_w.shape[1] - 1
    k_lru = lru_conv_w.shape[1] - 1
    k_ffn = ffn_conv_w.shape[1] - 1
    assert steps >= max(k_ssd, k_lru, k_ffn)
    return (y_prompt, from_lm(y_lm[0], steps), p_ssd, p_ssd_buf[None], p_lru.reshape(1, bp, -1), p_lru_buf[None],
            p_ffn_buf[None],
            new_state.reshape(1, nb_s, n_heads, headdim, d_state),
            from_lm(pre[(steps - k_ssd) * nb_s:], k_ssd)[None],
            from_lm(lst, 1).reshape(1, nb_s, -1),
            from_lm(prelx[(steps - k_lru) * nb_s:], k_lru)[None],
            from_lm(tail[0], k_ffn)[None])
```

```python
import functools

import numpy as np
import jax
import jax.numpy as jnp
from jax import lax
from jax.experimental import pallas as pl
from jax.experimental.pallas import tpu as pltpu

F32 = jnp.float32
BF16 = jnp.bfloat16

SSD_GROUPS = 4
SSD_CHUNK = 128
LRU_BLOCKS = 8
LRU_C = 8.0
LN_EPS = 1e-5
RMS_EPS = 1e-6
PAST_LEN = 16384

LANES = 128
SUBLANES = 8
VMEM_LIMIT_BYTES = 60 * 1024 * 1024

HIST = 8
ROW_BLK = 32
UNROLL = True


def _dot(a, b):
    if b.dtype == jnp.uint32:
        b = pltpu.bitcast(b, BF16)
    return jnp.dot(a, b, preferred_element_type=F32)


def _pack_kernel(w_ref, *out_refs, ranges):
    for o_ref, (a, b, pad, scale) in zip(out_refs, ranges):
        v = w_ref[:, a:b]
        if scale != 1.0:
            v = v * scale
        if pad:
            v = jnp.concatenate([v, jnp.zeros((v.shape[0], pad), v.dtype)], axis=1)
        o_ref[...] = pltpu.bitcast(v.astype(BF16), jnp.uint32)


def _pack_weight(w, ranges=None):
    k, n = w.shape
    single = ranges is None
    ranges = ((0, n, 0, 1.0),) if single else tuple(ranges)
    bk = 128 if n > 4096 else 256
    widths = [b - a + pad for a, b, pad, _ in ranges]
    outs = pl.pallas_call(
        functools.partial(_pack_kernel, ranges=ranges),
        grid=(k // bk,),
        in_specs=[pl.BlockSpec((bk, n), lambda i: (i, 0))],
        out_specs=tuple(pl.BlockSpec((bk // 2, wd), lambda i: (i, 0)) for wd in widths),
        out_shape=tuple(jax.ShapeDtypeStruct((k // 2, wd), jnp.uint32) for wd in widths),
        compiler_params=pltpu.CompilerParams(dimension_semantics=("arbitrary",),
                                             vmem_limit_bytes=VMEM_LIMIT_BYTES),
        name="pack_weight",
    )(w.astype(F32))
    return outs[0] if single else outs


def _pack_t_kernel(wt_ref, wdt_ref, *out_refs, parts, bn, dt_rows):
    j = pl.program_id(0)
    first = 0
    for o_ref, (nblk, scale) in zip(out_refs, parts):
        @pl.when(jnp.logical_and(j >= first, j < first + nblk))
        def _(o_ref=o_ref, scale=scale):
            v = wt_ref[...].T
            if scale != 1.0:
                v = v * scale
            o_ref[...] = pltpu.bitcast(v.astype(BF16), jnp.uint32)
        first += nblk

    @pl.when(j == 0)
    def _():
        rows = lax.broadcasted_iota(jnp.int32, wdt_ref.shape, 0)
        v = jnp.where(rows < dt_rows, wdt_ref[...], 0.0)
        out_refs[-1][...] = pltpu.bitcast(v.T.astype(BF16), jnp.uint32)


def _pack_w_in_t(wt, cuts, dt_index, scales, bn=512):
    n, k = wt.shape
    parts, offs = [], []
    for i in range(len(cuts) - 1):
        if i == dt_index:
            continue
        nblk = (cuts[i + 1] - cuts[i]) // bn
        assert nblk * bn == cuts[i + 1] - cuts[i]
        parts.append((nblk, scales[i]))
        offs += [cuts[i] + b * bn for b in range(nblk)]
    offs = np.asarray(offs, np.int32)
    firsts = np.cumsum([0] + [p[0] for p in parts])

    def row_off(j):
        off = jnp.int32(int(offs[0]))
        for idx in range(1, len(offs)):
            off = jnp.where(j >= idx, jnp.int32(int(offs[idx])), off)
        return off

    def out_map(p):
        return lambda j: (0, jnp.clip(j - int(firsts[p]), 0, parts[p][0] - 1))

    dt_rows = cuts[dt_index + 1] - cuts[dt_index]
    outs = pl.pallas_call(
        functools.partial(_pack_t_kernel, parts=tuple(parts), bn=bn, dt_rows=dt_rows),
        grid=(len(offs),),
        in_specs=[pl.BlockSpec((pl.Element(bn), pl.Element(k)), lambda j: (pl.multiple_of(row_off(j), SUBLANES), 0)),
                  pl.BlockSpec((pl.Element(LANES), pl.Element(k)), lambda j: (cuts[dt_index], 0))],
        out_specs=tuple(pl.BlockSpec((k // 2, bn), out_map(p)) for p in range(len(parts)))
        + (pl.BlockSpec((k // 2, LANES), lambda j: (0, 0)),),
        out_shape=tuple(jax.ShapeDtypeStruct((k // 2, nb_ * bn), jnp.uint32) for nb_, _ in parts)
        + (jax.ShapeDtypeStruct((k // 2, LANES), jnp.uint32),),
        compiler_params=pltpu.CompilerParams(dimension_semantics=("arbitrary",),
                                             vmem_limit_bytes=VMEM_LIMIT_BYTES),
        name="pack_w_in",
    )(wt, wt)
    outs = list(outs)
    dt_part = outs.pop()
    outs.insert(dt_index, dt_part)
    return outs


def _pack_rows_01(m):
    bits = np.ascontiguousarray(m, np.float32).view(np.uint32) >> 16
    return jnp.asarray(bits[0::2] | (bits[1::2] << 16), jnp.uint32)


def _dot_nt(a, b):
    return lax.dot_general(a, b, (((1,), (1,)), ((), ())), preferred_element_type=F32)


def _dot_tn(a, b):
    return lax.dot_general(a, b, (((0,), (0,)), ((), ())), preferred_element_type=F32)


def _split3(v):
    hi = v.astype(BF16)
    r1 = v - hi.astype(F32)
    mid = r1.astype(BF16)
    lo = (r1 - mid.astype(F32)).astype(BF16)
    return hi, mid, lo


def _dot_exact_lhs(m, v):
    hi, mid, lo = _split3(v)
    return _dot(m, hi) + _dot(m, mid) + _dot(m, lo)


def _split2(v):
    hi = v.astype(BF16)
    lo = (v - hi.astype(F32)).astype(BF16)
    return hi, lo


def _dot_2way_rhs(v, m2):
    return _dot(jnp.concatenate(_split2(v), axis=1), m2)


def _dot_2way_lhs(m2, v):
    return _dot(m2, jnp.concatenate(_split2(v), axis=0))


def _softplus(x):
    return jnp.maximum(x, 0.0) + jnp.log1p(jnp.exp(-jnp.abs(x)))


def _sigmoid(x):
    return 0.5 * jnp.tanh(0.5 * x) + 0.5


def _silu_of_half(h):
    return h + h * jnp.tanh(h)


def _gelu(x):
    c = np.sqrt(2.0 / np.pi).astype(np.float32)
    return 0.5 * x * (1.0 + jnp.tanh(c * (x + 0.044715 * (x * x * x))))


def _layer_norm(v, g, b):
    mu = jnp.mean(v, axis=-1, keepdims=True)
    d = v - mu
    var = jnp.mean(d * d, axis=-1, keepdims=True)
    return d * lax.rsqrt(var + LN_EPS) * g + b


def _rows(i, n):
    if isinstance(i, int):
        return slice(i * n, (i + 1) * n)
    return pl.ds(pl.multiple_of(i * n, n), n)


def _loop(n, body, unroll):
    if unroll:
        for i in range(n):
            body(i, None)
    else:
        lax.fori_loop(0, n, lambda i, c: (body(i, c), c)[1], 0)


def _conv_block(buf_ref, w_ref, b_ref, hist, stride, r0, rows, c0, cw):
    taps = w_ref.shape[0]
    acc = b_ref[:, c0:c0 + cw]
    for k in range(taps):
        off = hist + r0 - (taps - 1 - k) * stride
        if not isinstance(off, int):
            off = pl.multiple_of(off, SUBLANES)
        acc = acc + w_ref[k:k + 1, c0:c0 + cw] * buf_ref[pl.ds(off, rows), c0:c0 + cw]
    return acc


LOG2_SUBLANES = 3
NPOS = SSD_CHUNK // SUBLANES
LOG2_NPOS = 4
assert 1 << LOG2_SUBLANES == SUBLANES and 1 << LOG2_NPOS == NPOS


def _tok_of_row(r):
    return (r & (SUBLANES - 1)) * NPOS + lax.shift_right_logical(r, LOG2_SUBLANES)


def _row_of_tok(t):
    return (t & (NPOS - 1)) * SUBLANES + lax.shift_right_logical(t, LOG2_NPOS)


def _perm_matrix():
    q = SSD_CHUNK
    r = lax.broadcasted_iota(jnp.int32, (q, q), 0)
    c = lax.broadcasted_iota(jnp.int32, (q, q), 1)
    return jnp.where(c == _tok_of_row(r), 1.0, 0.0).astype(BF16)


def _unperm_matrix():
    q = SSD_CHUNK
    t = lax.broadcasted_iota(jnp.int32, (q, q), 0)
    r = lax.broadcasted_iota(jnp.int32, (q, q), 1)
    return jnp.where(r == _row_of_tok(t), 1.0, 0.0).astype(BF16)


def _fill_wrap(buf_ref, wrap_ref, c, wrap, tail_ref, cols=slice(None)):
    q = SSD_CHUNK
    for m in range(wrap // SUBLANES):
        r_cur = (c + 1) * q - wrap + m * SUBLANES
        cur = buf_ref[r_cur:r_cur + SUBLANES, cols]
        if c == 0:
            prv = tail_ref[m * SUBLANES:(m + 1) * SUBLANES, cols]
        else:
            prv = buf_ref[r_cur - q:r_cur - q + SUBLANES, cols]
        sub0 = lax.broadcasted_iota(jnp.int32, cur.shape, 0) == 0
        wrap_ref[c * wrap + m * SUBLANES:c * wrap + (m + 1) * SUBLANES, cols] = jnp.where(
            sub0, pltpu.roll(prv, 1, 0), pltpu.roll(cur, 1, 0))


def _conv_seg(buf_ref, wrap_ref, w_ref, b_ref, c, wrap, c0, cw):
    q = SSD_CHUNK
    taps = w_ref.shape[0]
    cols = slice(c0, c0 + cw)
    acc = b_ref[:, cols] + w_ref[taps - 1:taps, cols] * buf_ref[c * q:(c + 1) * q, cols]
    for k in range(taps - 1):
        back = (taps - 1 - k) * SUBLANES
        shifted = jnp.concatenate(
            [wrap_ref[(c + 1) * wrap - back:(c + 1) * wrap, cols], buf_ref[c * q:(c + 1) * q - back, cols]],
            axis=0)
        acc = acc + w_ref[k:k + 1, cols] * shifted
    return acc


def _seg_tail_rows(wrap):
    n = wrap // SUBLANES
    return [(NPOS - n + m) * SUBLANES + SUBLANES - 1 for m in range(n)]


def _prompt_mixer_kernel(
        x_ref, wz_ref, wxbc_ref, wdt_ref, wlx_ref, wly_ref, wgs_ref, wgl_ref,
        scw_ref, scb_ref, dtb_ref, aneg_ref, dexp_ref, ng_ref, wso_ref,
        lcw_ref, lcb_ref, wax_ref, ba_ref, bx_ref, lam_ref, wlo_ref,
        bg_ref, wo_ref, l1g_ref, l1b_ref, e_ref,
        x1_ref, st_ref, sconv_ref, lst_ref, lconv_ref,
        xb_s, xp_s, xbc_s, swrap_s, stail_s, xc_s, lx_s, lwrap_s, ltail_s, xr_s, ly_s, y_s, z_s, ysb_s, ylb_s,
        ht_s, hl_s, gs_s, gl_s, ys_s, yl_s, o_s, mb_s,
        *, tl, alpha, n_heads, headdim, d_state):
    t = pl.program_id(1)
    nt = pl.num_programs(1)
    d_ssd = n_heads * headdim
    gn = SSD_GROUPS * d_state
    hpg = n_heads // SSD_GROUPS
    gw = hpg * headdim
    d_lru = lx_s.shape[1]
    d_model = x_ref.shape[2]
    q = SSD_CHUNK

    nch = tl // q
    wrap_s = (scw_ref.shape[0] - 1) * SUBLANES
    wrap_l = (lcw_ref.shape[0] - 1) * SUBLANES

    @pl.when(t == 0)
    def _():
        stail_s[...] = jnp.zeros(stail_s.shape, F32)
        ltail_s[...] = jnp.zeros(ltail_s.shape, F32)
        ht_s[...] = jnp.zeros(ht_s.shape, F32)
        hl_s[...] = jnp.zeros(hl_s.shape, F32)

    perm = _perm_matrix()
    perm2 = jnp.concatenate([perm, perm], axis=1)
    for c in range(nch):
        rows = _rows(c, q)
        xp = _dot_2way_lhs(perm2, x_ref[0, rows, :])
        xp_s[rows, :] = xp
        xb_s[rows, :] = xp.astype(BF16)

    cwid = 512

    def proj(dst_ref, w_ref, c0, cw):
        dst_ref[:, c0:c0 + cw] = _dot(xb_s[...], w_ref[:, c0:c0 + cw])

    def lru_conv(c):
        _fill_wrap(lx_s, lwrap_s, c, wrap_l, ltail_s)
        for c0 in range(0, d_lru, cwid):
            xr_s[c * q:(c + 1) * q, c0:c0 + cwid] = _conv_seg(lx_s, lwrap_s, lcw_ref, lcb_ref, c, wrap_l, c0, cwid)
        if c == nch - 1:
            ltail_s[...] = lx_s[tl - wrap_l:tl, :]

    bw = d_lru // LRU_BLOCKS
    sub = lax.broadcasted_iota(jnp.int32, (SUBLANES, bw), 0)
    crow = lax.broadcasted_iota(jnp.int32, (q, bw), 0)

    def lru_block(n):
        cols = slice(n * bw, (n + 1) * bw)
        xr = xr_s[:, cols]
        xrb = xr.astype(BF16)
        rg = _dot(xrb, wax_ref[n])
        r = _sigmoid(rg[:, 0:bw] + ba_ref[:, cols])
        gi = _sigmoid(rg[:, bw:2 * bw] + bx_ref[:, cols])
        log_a = (-LRU_C) * r * _softplus(-lam_ref[:, cols])
        a_all = jnp.exp(log_a)
        mult_all = jnp.sqrt(1.0 - jnp.exp(2.0 * log_a))
        for c in range(nch):
            a = a_all[c * q:(c + 1) * q, :]
            mult = mult_all[c * q:(c + 1) * q, :]
            if c == 0:
                mult = jnp.where(jnp.logical_and(crow == 0, t == 0), 1.0, mult)
            u = mult * gi[c * q:(c + 1) * q, :] * xr[c * q:(c + 1) * q, :]
            a_p = [a[i * SUBLANES:(i + 1) * SUBLANES, :] for i in range(NPOS)]
            u_p = [u[i * SUBLANES:(i + 1) * SUBLANES, :] for i in range(NPOS)]
            h = u_p[0]
            g = a_p[0]
            for i in range(1, NPOS):
                h = a_p[i] * h + u_p[i]
                g = a_p[i] * g
            gs = jnp.where(sub == 0, 0.0, pltpu.roll(g, 1, 0))
            hs = jnp.where(sub == 0, hl_s[0:1, cols], pltpu.roll(h, 1, 0))
            d = 1
            while d < SUBLANES:
                keep = sub >= d
                hs = jnp.where(keep, gs * pltpu.roll(hs, d, 0) + hs, hs)
                gs = jnp.where(keep, gs * pltpu.roll(gs, d, 0), gs)
                d *= 2
            h = hs
            out = []
            for i in range(NPOS):
                h = a_p[i] * h + u_p[i]
                out.append(h)
            hl_s[0:1, cols] = h[SUBLANES - 1:SUBLANES, :]
            hseq = jnp.concatenate(out, axis=0)
            ylb_s[c * q:(c + 1) * q, cols] = (hseq * _gelu(ly_s[c * q:(c + 1) * q, cols])).astype(BF16)

    def ssd_conv(c0):
        cols = slice(c0, c0 + cwid)
        for c in range(nch):
            _fill_wrap(xbc_s, swrap_s, c, wrap_s, stail_s, cols)
        stail_s[:, cols] = xbc_s[tl - wrap_s:tl, cols]
        for c in range(nch):
            xc_s[c * q:(c + 1) * q, cols] = _silu_of_half(
                _conv_seg(xbc_s, swrap_s, scw_ref, scb_ref, c, wrap_s, c0, cwid))

    def merge_gate(dst_ref, c0, cw, b0):
        dst_ref[:, c0:c0 + cw] = _sigmoid(dst_ref[:, c0:c0 + cw] + bg_ref[:, b0 + c0:b0 + c0 + cw])

    tok_r = _tok_of_row(lax.broadcasted_iota(jnp.int32, (q, q), 0))
    tok_c = _tok_of_row(lax.broadcasted_iota(jnp.int32, (q, q), 1))
    causal = tok_r >= tok_c
    tri = jnp.where(causal, 1.0, 0.0).astype(BF16)
    lane_i = lax.broadcasted_iota(jnp.int32, (q, LANES), 1)
    left = lane_i < headdim

    def chunk_body(c, carry):
        rows = _rows(c, q)
        dt = _softplus(_dot(xb_s[rows, :], wdt_ref[...]) + dtb_ref[...])
        d_a = dt * aneg_ref[...]
        a_cs = _dot_exact_lhs(tri, d_a)
        a_last = a_cs[q - 1:q, :]
        wgt = dt * jnp.exp(a_last - a_cs)
        ea = jnp.exp(a_cs)
        w_exp = _dot_2way_rhs(wgt, e_ref[...])
        ea_exp = _dot_2way_rhs(ea, e_ref[...])
        a_cs_t = a_cs.T
        dt_t = dt.T
        for g in range(SSD_GROUPS):
            b_g = xc_s[rows, d_ssd + g * d_state:d_ssd + (g + 1) * d_state]
            c_g = xc_s[rows, d_ssd + gn + g * d_state:d_ssd + gn + (g + 1) * d_state]
            b_gb = b_g.astype(BF16)
            c_gb = c_g.astype(BF16)
            cb = _dot_nt(c_gb, b_gb)
            for hp in range(hpg // 2):
                c0 = g * gw + hp * 2 * headdim
                xs_pair = xc_s[rows, c0:c0 + 2 * headdim]
                lmats = []
                for j in range(2):
                    h = g * hpg + hp * 2 + j
                    seg = (jnp.broadcast_to(a_cs[:, h:h + 1], (q, q))
                           - jnp.broadcast_to(a_cs_t[h:h + 1, :], (q, q)))
                    dec = jnp.exp(jnp.where(causal, seg, -jnp.inf))
                    lmats.append((cb * dec * jnp.broadcast_to(dt_t[h:h + 1, :], (q, q))).astype(BF16))
                lpair = jnp.concatenate(lmats, axis=1)
                rhs = jnp.concatenate([jnp.where(left, xs_pair, 0.0),
                                       jnp.where(left, 0.0, xs_pair)], axis=0).astype(BF16)
                y_s[rows, c0:c0 + 2 * headdim] = _dot(lpair, rhs)
            gcols = slice(g * gw, (g + 1) * gw)
            h_g = ht_s[:, gcols]
            y_off = _dot(c_gb, h_g.astype(BF16)) * ea_exp[:, gcols]
            y_s[rows, gcols] = y_s[rows, gcols] + y_off
            xw = (xc_s[rows, gcols] * w_exp[:, gcols]).astype(BF16)
            ht_s[:, gcols] = h_g * ea_exp[q - 1:q, gcols] + _dot_tn(b_gb, xw)
            if carry is not None and g < len(carry):
                carry[g]()
        return carry

    def gate_body(i, carry):
        rows = _rows(i, ROW_BLK)
        y = y_s[rows, :] + dexp_ref[...] * xc_s[rows, 0:d_ssd]
        y = y * _silu_of_half(z_s[rows, :])
        ms = jnp.mean(y * y, axis=-1, keepdims=True)
        ysb_s[rows, :] = (y * lax.rsqrt(ms + RMS_EPS) * ng_ref[...]).astype(BF16)
        return carry

    assert nch == 2 and LRU_BLOCKS == 8 and d_lru == 2 * cwid and d_model == 2 * cwid
    d_xbc = xc_s.shape[1]
    nxb = d_xbc // cwid
    proj(lx_s, wlx_ref, 0, cwid)
    proj(lx_s, wlx_ref, cwid, cwid)
    proj(ly_s, wly_ref, 0, cwid); lru_conv(0)
    proj(ly_s, wly_ref, cwid, cwid); lru_conv(1)
    P = functools.partial
    mxu_a = ([P(proj, xbc_s, wxbc_ref, j * cwid, cwid) for j in range(nxb)]
             + [P(proj, gs_s, wgs_ref, j * cwid, cwid) for j in range(2)])
    for n in range(LRU_BLOCKS):
        mxu_a[n]()
        lru_block(n)
        if 1 <= n <= nxb:
            ssd_conv((n - 1) * cwid)
    gates_per_chunk = q // ROW_BLK
    nz = d_ssd // cwid
    chunk_body(0, [P(proj, z_s, wz_ref, j * cwid, cwid) for j in range(nz)])
    merge_gate(gs_s, 0, d_model, 0)
    for i in range(gates_per_chunk):
        gate_body(i, None)

    def wlo_piece(j):
        yl_s[:, j * cwid:(j + 1) * cwid] = _dot(ylb_s[...], wlo_ref[:, j * cwid:(j + 1) * cwid])

    chunk_body(1, [P(proj, gl_s, wgl_ref, 0, cwid), P(proj, gl_s, wgl_ref, cwid, cwid),
                   P(wlo_piece, 0), P(wlo_piece, 1)])
    merge_gate(gl_s, 0, d_model, d_model)
    for i in range(gates_per_chunk, 2 * gates_per_chunk):
        gate_body(i, None)

    ys_s[...] = _dot(ysb_s[...], wso_ref[...])
    for i in range(tl // ROW_BLK):
        rows = _rows(i, ROW_BLK)
        mb_s[rows, :] = (gs_s[rows, :] * ys_s[rows, :] + gl_s[rows, :] * yl_s[rows, :]).astype(BF16)
    o_s[...] = _dot(mb_s[...], wo_ref[...])
    for i in range(tl // ROW_BLK):
        rows = _rows(i, ROW_BLK)
        v = alpha * xp_s[rows, :] + o_s[rows, :]
        x1_ref[0, rows, :] = _layer_norm(v, l1g_ref[...], l1b_ref[...])

    @pl.when(t == nt - 1)
    def _():
        st_ref[0] = ht_s[...].T
        lst_ref[0] = hl_s[0:1, :]
        for m, r in enumerate(_seg_tail_rows(wrap_s)):
            sconv_ref[0, m:m + 1, :] = xbc_s[tl - q + r:tl - q + r + 1, :]
        for m, r in enumerate(_seg_tail_rows(wrap_l)):
            lconv_ref[0, m:m + 1, :] = lx_s[tl - q + r:tl - q + r + 1, :]


def _const_spec(shape):
    nd = len(shape)
    return pl.BlockSpec(shape, lambda *_: (0,) * nd, pipeline_mode=pl.Buffered(1))


def _prompt_mixer(x, wts, tl):
    nb, seq, d_model = x.shape
    n_heads, headdim, d_state = wts['n_heads'], wts['headdim'], wts['d_state']
    d_ssd = n_heads * headdim
    d_xbc = d_ssd + 2 * SSD_GROUPS * d_state
    d_lru = wts['wlx'].shape[1]
    names = ['wz', 'wxbc', 'wdt', 'wlx', 'wly', 'wgs', 'wgl', 'scw', 'scb', 'dtb', 'aneg', 'dexp', 'ng',
             'wso', 'lcw', 'lcb', 'wax', 'ba', 'bx', 'lam', 'wlo', 'bg', 'wo', 'l1g', 'l1b', 'ee']
    consts = [wts[k] for k in names]
    kern = functools.partial(_prompt_mixer_kernel, tl=tl, alpha=wts['alpha'], n_heads=n_heads,
                             headdim=headdim, d_state=d_state)
    out_shape = (
        jax.ShapeDtypeStruct((nb, seq, d_model), F32),
        jax.ShapeDtypeStruct((nb, d_ssd, d_state), F32),
        jax.ShapeDtypeStruct((nb, 3, d_xbc), F32),
        jax.ShapeDtypeStruct((nb, 1, d_lru), F32),
        jax.ShapeDtypeStruct((nb, 3, d_lru), F32),
    )
    out_specs = (
        pl.BlockSpec((1, tl, d_model), lambda b, t: (b, t, 0)),
        pl.BlockSpec((1, d_ssd, d_state), lambda b, t: (b, 0, 0)),
        pl.BlockSpec((1, 3, d_xbc), lambda b, t: (b, 0, 0)),
        pl.BlockSpec((1, 1, d_lru), lambda b, t: (b, 0, 0)),
        pl.BlockSpec((1, 3, d_lru), lambda b, t: (b, 0, 0)),
    )
    nch = tl // SSD_CHUNK
    wrap_s = (wts['scw'].shape[0] - 1) * SUBLANES
    wrap_l = (wts['lcw'].shape[0] - 1) * SUBLANES
    scratch = [
        pltpu.VMEM((tl, d_model), BF16),
        pltpu.VMEM((tl, d_model), F32),
        pltpu.VMEM((tl, d_xbc), F32),
        pltpu.VMEM((nch * wrap_s, d_xbc), F32),
        pltpu.VMEM((wrap_s, d_xbc), F32),
        pltpu.VMEM((tl, d_xbc), F32),
        pltpu.VMEM((tl, d_lru), F32),
        pltpu.VMEM((nch * wrap_l, d_lru), F32),
        pltpu.VMEM((wrap_l, d_lru), F32),
        pltpu.VMEM((tl, d_lru), F32),
        pltpu.VMEM((tl, d_lru), F32),
        pltpu.VMEM((tl, d_ssd), F32),
        pltpu.VMEM((tl, d_ssd), F32),
        pltpu.VMEM((tl, d_ssd), BF16),
        pltpu.VMEM((tl, d_lru), BF16),
        pltpu.VMEM((d_state, d_ssd), F32),
        pltpu.VMEM((SUBLANES, d_lru), F32),
        pltpu.VMEM((tl, d_model), F32),
        pltpu.VMEM((tl, d_model), F32),
        pltpu.VMEM((tl, d_model), F32),
        pltpu.VMEM((tl, d_model), F32),
        pltpu.VMEM((tl, d_model), F32),
        pltpu.VMEM((tl, d_model), BF16),
    ]
    return pl.pallas_call(
        kern,
        grid=(nb, seq // tl),
        in_specs=[pl.BlockSpec((1, tl, d_model), lambda b, t: (b, t, 0))]
        + [_const_spec(c.shape) for c in consts],
        out_specs=out_specs,
        out_shape=out_shape,
        scratch_shapes=scratch,
        compiler_params=pltpu.CompilerParams(
            dimension_semantics=("arbitrary", "arbitrary"),
            vmem_limit_bytes=VMEM_LIMIT_BYTES),
        name="prompt_mixer",
    )(x, *consts)


def _ffn_kernel(x_ref, h0_ref, wg_ref, wu_ref, cw_ref, cb_ref, wd_ref, g_ref, b_ref,
                y_ref, tail_ref, xb_s, gb_s, hb_s, *, tm, stride, hist, alpha, fchunk):
    t = pl.program_id(1)
    nt = pl.num_programs(1)
    taps = cw_ref.shape[0]
    nh = (taps - 1) * stride
    d_ff = gb_s.shape[1]

    @pl.when(t == 0)
    def _():
        gb_s[hist - nh:hist, :] = h0_ref[0]

    xb_s[...] = x_ref[0].astype(BF16)
    acc = None
    for c0 in range(0, d_ff, fchunk):
        cols = slice(c0, c0 + fchunk)
        gb_s[hist:hist + tm, cols] = _dot(xb_s[...], wg_ref[:, cols])
        up = _dot(xb_s[...], wu_ref[:, cols])
        gc = _conv_block(gb_s, cw_ref, cb_ref, hist, stride, 0, tm, c0, fchunk)
        hb_s[...] = (_gelu(gc) * up).astype(BF16)
        part = _dot(hb_s[...], wd_ref[c0 // 2:(c0 + fchunk) // 2, :])
        acc = part if acc is None else acc + part
    v = alpha * x_ref[0] + acc
    y_ref[0] = _layer_norm(v, g_ref[...], b_ref[...])
    gb_s[hist - nh:hist, :] = gb_s[hist + tm - nh:hist + tm, :]

    @pl.when(t == nt - 1)
    def _():
        tail_ref[0] = gb_s[hist - nh:hist, :]


def _ffn_seg_kernel(x_ref, h0_ref, st_ref, c8_ref, b8_ref, xw8_ref, ea8_ref,
                    wg_ref, wu_ref, cw_ref, cb_ref, wd_ref, g_ref, b_ref,
                    y_ref, tail_ref, nst_ref, yoff8_ref,
                    xb_s, gb_s, gwrap_s, gtail_s, hb_s, *, tm, alpha, fchunk, state_dims):
    t = pl.program_id(1)
    state_pieces = [
        functools.partial(_state_pair_group, g, st_ref.at[p], c8_ref.at[p], b8_ref.at[p], xw8_ref.at[p],
                          ea8_ref.at[p], nst_ref.at[p], yoff8_ref.at[p], **state_dims)
        for p in range(st_ref.shape[0]) for g in range(SSD_GROUPS)]

    def state_work(n):
        for _ in range(min(n, len(state_pieces))):
            state_pieces.pop(0)()
    nt = pl.num_programs(1)
    q = SSD_CHUNK
    nch = tm // q
    wrap = (cw_ref.shape[0] - 1) * SUBLANES
    d_ff = gb_s.shape[1]
    tail_rows = _seg_tail_rows(wrap)

    @pl.when(t == 0)
    def _():
        gtail_s[...] = jnp.zeros(gtail_s.shape, F32)
        for m in range(len(tail_rows)):
            r = m * SUBLANES + SUBLANES - 1
            gtail_s[r:r + 1, :] = h0_ref[0, m:m + 1, :]

    xb_s[...] = x_ref[0].astype(BF16)
    gb_s[...] = _dot(xb_s[...], wg_ref[...])
    per_slot = -(-len(state_pieces) // (1 + 2 * (d_ff // fchunk)))
    state_work(per_slot)
    for c in range(nch):
        _fill_wrap(gb_s, gwrap_s, c, wrap, gtail_s)
    acc = None
    for c0 in range(0, d_ff, fchunk):
        cols = slice(c0, c0 + fchunk)
        up = _dot(xb_s[...], wu_ref[:, cols])
        state_work(per_slot)
        for c in range(nch):
            gc = _conv_seg(gb_s, gwrap_s, cw_ref, cb_ref, c, wrap, c0, fchunk)
            hb_s[c * q:(c + 1) * q, :] = (_gelu(gc) * up[c * q:(c + 1) * q, :]).astype(BF16)
        part = _dot(hb_s[...], wd_ref[c0 // 2:(c0 + fchunk) // 2, :])
        state_work(per_slot)
        acc = part if acc is None else acc + part
    state_work(len(state_pieces))
    gtail_s[...] = gb_s[tm - wrap:tm, :]

    unperm = _unperm_matrix()
    unperm2 = jnp.concatenate([unperm, unperm], axis=1)
    for c in range(nch):
        rows = slice(c * q, (c + 1) * q)
        y = _layer_norm(alpha * x_ref[0, rows, :] + acc[rows, :], g_ref[...], b_ref[...])
        y_ref[0, rows, :] = _dot_2way_lhs(unperm2, y)

    @pl.when(t == nt - 1)
    def _():
        for m, r in enumerate(tail_rows):
            tail_ref[0, m:m + 1, :] = gb_s[tm - q + r:tm - q + r + 1, :]


def _ffn_seg(x, hist0, state, c8, b8, xw8, ea8, wts, tm, steps):
    nb, seq, d_model = x.shape
    d_ff = wts['wg'].shape[1]
    taps = wts['fcw'].shape[0]
    wrap = (taps - 1) * SUBLANES
    fchunk = 1024
    nt = seq // tm
    npairs = state.shape[0]
    pps = npairs // (nb * nt)
    assert pps * nb * nt == npairs
    consts = [wts[k] for k in ['wg', 'wu', 'fcw', 'fcb', 'wd', 'l2g', 'l2b']]
    state_dims = dict(steps=steps, n_heads=wts['n_heads'], headdim=wts['headdim'], d_state=wts['d_state'])
    kern = functools.partial(_ffn_seg_kernel, tm=tm, alpha=wts['alpha'], fchunk=fchunk, state_dims=state_dims)
    pair_blk = lambda a: pl.BlockSpec((pps,) + a.shape[1:], lambda b, t: (b * nt + t,) + (0,) * (a.ndim - 1))
    return pl.pallas_call(
        kern,
        grid=(nb, nt),
        in_specs=[pl.BlockSpec((1, tm, d_model), lambda b, t: (b, t, 0)),
                  pl.BlockSpec((1, taps - 1, d_ff), lambda b, t: (b, 0, 0)),
                  pair_blk(state), pair_blk(c8), pair_blk(b8), pair_blk(xw8), pair_blk(ea8)]
        + [_const_spec(c.shape) for c in consts],
        out_specs=(pl.BlockSpec((1, tm, d_model), lambda b, t: (b, t, 0)),
                   pl.BlockSpec((1, taps - 1, d_ff), lambda b, t: (b, 0, 0)),
                   pair_blk(state), pair_blk(xw8)),
        out_shape=(jax.ShapeDtypeStruct((nb, seq, d_model), F32),
                   jax.ShapeDtypeStruct((nb, taps - 1, d_ff), F32),
                   jax.ShapeDtypeStruct(state.shape, F32),
                   jax.ShapeDtypeStruct(xw8.shape, F32)),
        scratch_shapes=[pltpu.VMEM((tm, d_model), BF16),
                        pltpu.VMEM((tm, d_ff), F32),
                        pltpu.VMEM((tm // SSD_CHUNK * wrap, d_ff), F32),
                        pltpu.VMEM((wrap, d_ff), F32),
                        pltpu.VMEM((tm, fchunk), BF16)],
        compiler_params=pltpu.CompilerParams(
            dimension_semantics=("arbitrary", "arbitrary"),
            vmem_limit_bytes=VMEM_LIMIT_BYTES),
        name="conv_ffn_seg",
    )(x, hist0, state, c8, b8, xw8, ea8, *consts)


def _ffn(x, hist0, wts, tm, stride):
    nb, seq, d_model = x.shape
    d_ff = wts['wg'].shape[1]
    taps = wts['fcw'].shape[0]
    nh = (taps - 1) * stride
    hist = -(-nh // SUBLANES) * SUBLANES
    consts = [wts[k] for k in ['wg', 'wu', 'fcw', 'fcb', 'wd', 'l2g', 'l2b']]
    kern = functools.partial(_ffn_kernel, tm=tm, stride=stride, hist=hist, alpha=wts['alpha'], fchunk=1024)
    return pl.pallas_call(
        kern,
        grid=(nb, seq // tm),
        in_specs=[pl.BlockSpec((1, tm, d_model), lambda b, t: (b, t, 0)),
                  pl.BlockSpec((1, nh, d_ff), lambda b, t: (b, 0, 0))]
        + [_const_spec(c.shape) for c in consts],
        out_specs=(pl.BlockSpec((1, tm, d_model), lambda b, t: (b, t, 0)),
                   pl.BlockSpec((1, nh, d_ff), lambda b, t: (b, 0, 0))),
        out_shape=(jax.ShapeDtypeStruct((nb, seq, d_model), F32),
                   jax.ShapeDtypeStruct((nb, nh, d_ff), F32)),
        scratch_shapes=[pltpu.VMEM((tm, d_model), BF16),
                        pltpu.VMEM((hist + tm, d_ff), F32),
                        pltpu.VMEM((tm, 1024), BF16)],
        compiler_params=pltpu.CompilerParams(
            dimension_semantics=("arbitrary", "arbitrary"),
            vmem_limit_bytes=VMEM_LIMIT_BYTES),
        name="conv_ffn",
    )(x, hist0, *consts)


def _sample_ssd_kernel(
        x_ref, xall_ref, cssd_ref, wxbc_ref, wdt_ref, scw_ref, scb_ref, dtb_ref, aneg_ref, dexp_ref,
        e_ref, e2_ref,
        pre_ref, c_ref, b_ref, ea_ref, ydg_ref, xw_ref,
        xbc_s, xs_s, bs_s, acs_s, dts_s,
        *, nseq, steps, n_heads, headdim, d_state):
    l = pl.program_id(0)
    d_ssd = n_heads * headdim
    gn = SSD_GROUPS * d_state
    hist = (scw_ref.shape[0] - 1) * nseq
    r0 = l * nseq

    def blk(i):
        return pl.ds(pl.multiple_of(i * nseq, nseq), nseq)

    def sblk(i):
        return slice(i * nseq, (i + 1) * nseq)

    @pl.when(l == 0)
    def _():
        xbc_s[0:hist, :] = cssd_ref[...]
        dts_s[...] = _softplus(_dot(xall_ref[...].astype(BF16), wdt_ref[...]) + dtb_ref[...])
        acc = jnp.zeros((nseq, LANES), F32)
        for s in range(steps):
            acc = acc + dts_s[sblk(s), :] * aneg_ref[...]
            acs_s[sblk(s), :] = acc

    xb = x_ref[...].astype(BF16)

    pre = _dot(xb, wxbc_ref[...])
    pre_ref[...] = pre
    xbc_s[pl.ds(pl.multiple_of(hist + r0, nseq), nseq), :] = pre
    cwid = 512
    for c0 in range(0, d_ssd, cwid):
        xs_s[blk(l), c0:c0 + cwid] = _silu_of_half(
            _conv_block(xbc_s, scw_ref, scb_ref, hist, nseq, r0, nseq, c0, cwid))
    b_l = _silu_of_half(_conv_block(xbc_s, scw_ref, scb_ref, hist, nseq, r0, nseq, d_ssd, gn))
    c_l = _silu_of_half(_conv_block(xbc_s, scw_ref, scb_ref, hist, nseq, r0, nseq, d_ssd + gn, gn))
    bs_s[blk(l), :] = b_l
    b_ref[...] = b_l
    c_ref[...] = c_l

    a_cs = acs_s[blk(l), :]
    dt = dts_s[blk(l), :]
    a_end = acs_s[sblk(steps - 1), :]
    ea_ref[...] = _dot_2way_rhs(jnp.exp(a_cs), e_ref[...])
    xw_ref[...] = xs_s[blk(l), :] * _dot_2way_rhs(dt * jnp.exp(a_end - a_cs), e_ref[...])

    ydg_ref[...] = dexp_ref[...] * xs_s[blk(l), :]
    for s in range(steps):
        @pl.when(s <= l)
        def _(s=s):
            coef = _dot_2way_rhs(jnp.exp(a_cs - acs_s[sblk(s), :]) * dts_s[sblk(s), :], e_ref[...])
            cbx = _dot_2way_rhs(bs_s[sblk(s), :] * c_l, e2_ref[...])
            ydg_ref[...] += cbx * coef * xs_s[sblk(s), :]


def _sample_lru_kernel(
        x_ref, clru_ref, slru_ref, wlx_ref, wly_ref, lcw_ref, lcb_ref, wax_ref, ba_ref, bx_ref,
        lam_ref, prelx_ref, ylru_ref, lst_ref, lx_s, hl_s, *, nseq, steps, start_pos):
    l = pl.program_id(0)
    hist = (lcw_ref.shape[0] - 1) * nseq
    d_lru = lx_s.shape[1]
    r0 = l * nseq

    @pl.when(l == 0)
    def _():
        lx_s[0:hist, :] = clru_ref[...]
        hl_s[...] = slru_ref[...]

    xb = x_ref[...].astype(BF16)
    prelx = _dot(xb, wlx_ref[...])
    prelx_ref[...] = prelx
    lx_s[pl.ds(pl.multiple_of(hist + r0, nseq), nseq), :] = prelx
    ly = _dot(xb, wly_ref[...])
    bw = d_lru // LRU_BLOCKS
    first = (l + start_pos) == 0
    for n in range(LRU_BLOCKS):
        cols = slice(n * bw, (n + 1) * bw)
        xr = _conv_block(lx_s, lcw_ref, lcb_ref, hist, nseq, r0, nseq, n * bw, bw)
        xrb = xr.astype(BF16)
        rg = _dot(xrb, wax_ref[n])
        r = _sigmoid(rg[:, 0:bw] + ba_ref[:, cols])
        gi = _sigmoid(rg[:, bw:2 * bw] + bx_ref[:, cols])
        log_a = (-LRU_C) * r * _softplus(-lam_ref[:, cols])
        a = jnp.exp(log_a)
        mult = jnp.where(first, 1.0, jnp.sqrt(1.0 - jnp.exp(2.0 * log_a)))
        h = a * hl_s[:, cols] + mult * gi * xr
        hl_s[:, cols] = h
        ylru_ref[:, cols] = (h * _gelu(ly[:, cols])).astype(BF16)

    @pl.when(l == steps - 1)
    def _():
        lst_ref[...] = hl_s[...]


def _state_pair_group(g, st_ref, c8_ref, b8_ref, xw8_ref, ea8_ref, nst_ref, yoff8_ref,
                      *, steps, n_heads, headdim, d_state):
    nrow = 2 * steps
    hpg = n_heads // SSD_GROUPS
    gw = hpg * headdim
    assert 2 * headdim == LANES and d_state == LANES
    par = lax.broadcasted_iota(jnp.int32, (nrow, gw), 0) % 2
    low = lax.broadcasted_iota(jnp.int32, (nrow, LANES), 1) < headdim
    gcols = slice(g * gw, (g + 1) * gw)
    c8g = c8_ref[:, g * d_state:(g + 1) * d_state].astype(BF16)
    b8g = b8_ref[:, g * d_state:(g + 1) * d_state].astype(BF16)
    xw8 = xw8_ref[:, gcols]
    ea8 = ea8_ref[:, gcols]
    cds = []
    for hp in range(hpg // 2):
        pair = ea8[:, hp * LANES:(hp + 1) * LANES]
        swapped = pltpu.roll(pair, headdim, 1)
        cds.append(jnp.where(low, pair, swapped))
        cds.append(jnp.where(low, swapped, pair))
    yo = None
    for e in range(2):
        sg = st_ref[e, gcols, :]
        yo_e = _dot_nt(c8g, sg.astype(BF16))
        yo = yo_e if e == 0 else jnp.where(par == e, yo_e, yo)
        xw_e = jnp.where(par == e, xw8, 0.0).astype(BF16)
        upd = _dot_tn(xw_e, b8g)
        k_last = 2 * (steps - 1) + e
        for hh in range(hpg):
            cd = cds[hh][k_last:k_last + 1, :]
            hr = slice(hh * headdim, (hh + 1) * headdim)
            nst_ref[e, g * gw + hh * headdim:g * gw + (hh + 1) * headdim, :] = sg[hr, :] * cd + upd[hr, :]
    yoff8_ref[:, gcols] = yo * ea8


def _sample_post_kernel(x_ref, ydg_ref, yoff_ref, ylru_ref, wz_ref, ng_ref, wso_ref, wlo_ref,
                        wgs_ref, wgl_ref, bg_ref, wo_ref, l1g_ref, l1b_ref, x1_ref, *, alpha):
    d_model = x_ref.shape[1]
    xb = x_ref[...].astype(BF16)
    y = (ydg_ref[...] + yoff_ref[...]) * _silu_of_half(_dot(xb, wz_ref[...]))
    ms = jnp.mean(y * y, axis=-1, keepdims=True)
    ysb = (y * lax.rsqrt(ms + RMS_EPS) * ng_ref[...]).astype(BF16)
    g_ssd = _sigmoid(_dot(xb, wgs_ref[...]) + bg_ref[:, 0:d_model])
    g_lru = _sigmoid(_dot(xb, wgl_ref[...]) + bg_ref[:, d_model:2 * d_model])
    merged = g_ssd * _dot(ysb, wso_ref[...]) + g_lru * _dot(ylru_ref[...], wlo_ref[...])
    o = _dot(merged.astype(BF16), wo_ref[...])
    x1_ref[...] = _layer_norm(alpha * x_ref[...] + o, l1g_ref[...], l1b_ref[...])


def _sample_front(x_lm, cssd_lm, clru_lm, slru_lm, wts, nseq, steps, start_pos):
    n_heads, headdim, d_state = wts['n_heads'], wts['headdim'], wts['d_state']
    d_model = x_lm.shape[1]
    d_ssd = n_heads * headdim
    gn = SSD_GROUPS * d_state
    d_xbc = d_ssd + 2 * gn
    d_lru = wts['wlx'].shape[1]
    ntok = steps * nseq
    dims = dict(nseq=nseq, steps=steps, n_heads=n_heads, headdim=headdim, d_state=d_state)
    params = pltpu.CompilerParams(dimension_semantics=("arbitrary",), vmem_limit_bytes=VMEM_LIMIT_BYTES)
    step_blk = lambda w: pl.BlockSpec((nseq, w), lambda l: (l, 0))

    sds = jax.ShapeDtypeStruct
    ssd_names = ['wxbc', 'wdt', 'scw', 'scb', 'dtb', 'aneg', 'dexp', 'ee', 'e2e']
    ssd_consts = [x_lm, cssd_lm] + [wts[k] for k in ssd_names]
    pre, c_lm, b_lm, ea_lm, ydg_lm, xw_lm = pl.pallas_call(
        functools.partial(_sample_ssd_kernel, **dims),
        grid=(steps,),
        in_specs=[step_blk(d_model)] + [_const_spec(c.shape) for c in ssd_consts],
        out_specs=(step_blk(d_xbc), step_blk(gn), step_blk(gn), step_blk(d_ssd), step_blk(d_ssd),
                   step_blk(d_ssd)),
        out_shape=(sds((ntok, d_xbc), F32), sds((ntok, gn), F32), sds((ntok, gn), F32),
                   sds((ntok, d_ssd), F32), sds((ntok, d_ssd), F32), sds((ntok, d_ssd), F32)),
        scratch_shapes=[
            pltpu.VMEM((cssd_lm.shape[0] + ntok, d_xbc), F32),
            pltpu.VMEM((ntok, d_ssd), F32),
            pltpu.VMEM((ntok, gn), F32),
            pltpu.VMEM((ntok, LANES), F32),
            pltpu.VMEM((ntok, LANES), F32),
        ],
        compiler_params=params,
        name="sample_ssd",
    )(x_lm, *ssd_consts)

    lru_names = ['wlx', 'wly', 'lcw', 'lcb', 'wax', 'ba', 'bx', 'lam']
    lru_consts = [clru_lm, slru_lm] + [wts[k] for k in lru_names]
    prelx, ylru_lm, lst = pl.pallas_call(
        functools.partial(_sample_lru_kernel, nseq=nseq, steps=steps, start_pos=start_pos),
        grid=(steps,),
        in_specs=[step_blk(d_model)] + [_const_spec(c.shape) for c in lru_consts],
        out_specs=(step_blk(d_lru), step_blk(d_lru), pl.BlockSpec((nseq, d_lru), lambda l: (0, 0))),
        out_shape=(sds((ntok, d_lru), F32), sds((ntok, d_lru), BF16), sds((nseq, d_lru), F32)),
        scratch_shapes=[
            pltpu.VMEM((clru_lm.shape[0] + ntok, d_lru), F32),
            pltpu.VMEM((nseq, d_lru), F32),
        ],
        compiler_params=params,
        name="sample_lru",
    )(x_lm, *lru_consts)

    return dict(pre=pre, c=c_lm, b=b_lm, ea=ea_lm, ydg=ydg_lm, xw=xw_lm, prelx=prelx, ylru=ylru_lm, lst=lst)


def _sample_back(x_lm, ydg_lm, yoff_lm, ylru_lm, wts, nseq, steps):
    d_model = x_lm.shape[1]
    d_ssd = wts['n_heads'] * wts['headdim']
    d_lru = wts['wlx'].shape[1]
    ntok = steps * nseq
    params = pltpu.CompilerParams(dimension_semantics=("arbitrary",), vmem_limit_bytes=VMEM_LIMIT_BYTES)
    step_blk = lambda w: pl.BlockSpec((nseq, w), lambda l: (l, 0))
    sds = jax.ShapeDtypeStruct
    post_names = ['wz', 'ng', 'wso', 'wlo', 'wgs', 'wgl', 'bg', 'wo', 'l1g', 'l1b']
    post_consts = [wts[k] for k in post_names]
    x1_lm = pl.pallas_call(
        functools.partial(_sample_post_kernel, alpha=wts['alpha']),
        grid=(steps,),
        in_specs=[step_blk(d_model), step_blk(d_ssd), step_blk(d_ssd), step_blk(d_lru)]
        + [_const_spec(c.shape) for c in post_consts],
        out_specs=step_blk(d_model),
        out_shape=sds((ntok, d_model), F32),
        compiler_params=params,
        name="sample_post",
    )(x_lm, ydg_lm, yoff_lm, ylru_lm, *post_consts)
    return x1_lm


def _prep_weights(w_in, b_gate, ssd_conv_w, ssd_conv_b, ssd_dt_bias, ssd_a_log, ssd_d, ssd_norm_g,
                  w_ssd_out, lru_conv_w, lru_conv_b, lru_wa, lru_ba, lru_wx, lru_bx, lru_lambda,
                  w_lru_out, w_o, ln1_g, ln1_b, ffn_w_gate, ffn_w_up, ffn_conv_w, ffn_conv_b,
                  ffn_w_down, ln2_g, ln2_b, n_heads, headdim, d_state):
    depth = w_in.shape[0]
    d_model = w_in.shape[1]
    d_ssd = n_heads * headdim
    d_xbc = d_ssd + 2 * SSD_GROUPS * d_state
    d_lru = lru_lambda.shape[1]
    sizes = (d_ssd, d_xbc, n_heads, d_lru, d_lru, d_model, d_model)
    cuts = np.cumsum((0,) + sizes)
    scales = [0.5 if i == 0 else 1.0 for i in range(len(sizes))]
    parts = _pack_w_in_t(jnp.swapaxes(w_in, 1, 2)[0].astype(F32), [int(c) for c in cuts], 2, scales)
    row = lambda v: v.reshape(1, -1).astype(F32)
    mat = _pack_weight
    wax = jnp.concatenate([lru_wa[0], lru_wx[0]], axis=-1)
    wax = _pack_weight(wax.reshape(-1, wax.shape[-1])).reshape(wax.shape[0], wax.shape[1] // 2, wax.shape[2])
    pad_heads = lambda v: jnp.pad(v.reshape(1, -1).astype(F32), ((0, 0), (0, LANES - n_heads)))
    head_of_col = np.arange(d_ssd) // headdim
    expand = (np.arange(LANES)[:, None] == head_of_col[None, :]).astype(np.float32)
    group_sum = (np.arange(SSD_GROUPS * d_state)[:, None] // d_state
                 == (head_of_col // (n_heads // SSD_GROUPS))[None, :])
    return dict(
        n_heads=n_heads, headdim=headdim, d_state=d_state,
        alpha=float((2.0 * depth) ** 0.25),
        wz=parts[0], wxbc=parts[1], wdt=parts[2], wlx=parts[3], wly=parts[4], wgs=parts[5], wgl=parts[6],
        scw=0.5 * ssd_conv_w[0].astype(F32), scb=0.5 * row(ssd_conv_b[0]),
        dtb=pad_heads(ssd_dt_bias[0]), aneg=pad_heads(-jnp.exp(ssd_a_log[0].astype(F32))),
        dexp=row(jnp.repeat(ssd_d[0], headdim)), ng=row(ssd_norm_g[0]),
        wso=mat(w_ssd_out[0]),
        lcw=lru_conv_w[0].astype(F32), lcb=row(lru_conv_b[0]),
        ba=row(lru_ba[0]), bx=row(lru_bx[0]), wax=wax,
        lam=row(lru_lambda[0]), wlo=mat(w_lru_out[0]),
        bg=row(b_gate[0]), wo=mat(w_o[0]), l1g=row(ln1_g[0]), l1b=row(ln1_b[0]),
        ee=_pack_rows_01(np.concatenate([expand, expand], axis=0)),
        e2e=_pack_rows_01(np.concatenate([group_sum, group_sum], axis=0)),
        wg=mat(ffn_w_gate[0]), wu=mat(ffn_w_up[0]),
        fcw=ffn_conv_w[0].astype(F32), fcb=row(ffn_conv_b[0]), wd=mat(ffn_w_down[0]),
        l2g=row(ln2_g[0]), l2b=row(ln2_b[0]),
    )


def kernel(x_prompt, x_sample, state_ssd, cache_ssd_conv, state_lru, cache_lru_conv, cache_ffn_conv, w_in, b_gate, ssd_conv_w, ssd_conv_b, ssd_dt_bias, ssd_a_log, ssd_d, ssd_norm_g, w_ssd_out, lru_conv_w, lru_conv_b, lru_wa, lru_ba, lru_wx, lru_bx, lru_lambda, w_lru_out, w_o, ln1_g, ln1_b, ffn_w_gate, ffn_w_up, ffn_conv_w, ffn_conv_b, ffn_w_down, ln2_g, ln2_b):
    assert w_in.shape[0] == 1, "single-layer trunk"
    _, _, n_heads, headdim, d_state = state_ssd.shape
    wts = _prep_weights(w_in, b_gate, ssd_conv_w, ssd_conv_b, ssd_dt_bias, ssd_a_log, ssd_d, ssd_norm_g,
                        w_ssd_out, lru_conv_w, lru_conv_b, lru_wa, lru_ba, lru_wx, lru_bx, lru_lambda,
                        w_lru_out, w_o, ln1_g, ln1_b, ffn_w_gate, ffn_w_up, ffn_conv_w, ffn_conv_b,
                        ffn_w_down, ln2_g, ln2_b, n_heads, headdim, d_state)
    bp = x_prompt.shape[0]
    d_ff = ffn_w_gate.shape[2]

    nb_s, steps, _ = x_sample.shape
    half = nb_s // 2

    def to_lm(a):
        k, c = a.shape[1], a.shape[2]
        return a.reshape(half, 2, k, c).transpose(2, 1, 0, 3).reshape(k * nb_s, c)

    def from_lm(a, k):
        c = a.shape[1]
        return a.reshape(k, 2, half, c).transpose(2, 1, 0, 3).reshape(nb_s, k, c)

    def to_pairs(a):
        return a.reshape(steps, 2, half, a.shape[1]).transpose(2, 0, 1, 3).reshape(half, 2 * steps, a.shape[1])

    def from_pairs(a):
        return a.reshape(half, steps, 2, a.shape[2]).transpose(1, 2, 0, 3).reshape(steps * nb_s, a.shape[2])

    d_ssd = n_heads * headdim
    x_lm = to_lm(x_sample)
    sf = _sample_front(x_lm, to_lm(cache_ssd_conv[0]), to_lm(cache_lru_conv[0]),
                       to_lm(state_lru[0][:, None, :]), wts, nb_s, steps, PAST_LEN)
    pre, lst, prelx = sf['pre'], sf['lst'], sf['prelx']

    x1_p, p_ssd, p_ssd_buf, p_lru, p_lru_buf = _prompt_mixer(x_prompt, wts, tl=256)
    y_prompt, p_ffn_buf, new_state, yoff8 = _ffn_seg(
        x1_p, jnp.zeros((bp, ffn_conv_w.shape[1] - 1, d_ff), F32),
        state_ssd[0].reshape(half, 2, d_ssd, d_state), to_pairs(sf['c']), to_pairs(sf['b']),
        to_pairs(sf['xw']), to_pairs(sf['ea']), wts, tm=512, steps=steps)
    p_ssd = p_ssd.reshape(1, bp, n_heads, headdim, d_state)

    x1_lm = _sample_back(x_lm, sf['ydg'], from_pairs(yoff8), sf['ylru'], wts, nb_s, steps)
    y_lm, tail = _ffn(x1_lm[None], to_lm(cache_ffn_conv[0])[None], wts, tm=steps * nb_s, stride=nb_s)
    k_ssd = ssd_conv_w.shape[1] - 1
    k_lru = lru_conv_w.shape[1] - 1
    k_ffn = ffn_conv_w.shape[1] - 1
    assert steps >= max(k_ssd, k_lru, k_ffn)
    return (y_prompt, from_lm(y_lm[0], steps), p_ssd, p_ssd_buf[None], p_lru.reshape(1, bp, -1), p_lru_buf[None],
            p_ffn_buf[None],
            new_state.reshape(1, nb_s, n_heads, headdim, d_state),
            from_lm(pre[(steps - k_ssd) * nb_s:], k_ssd)[None],
            from_lm(lst, 1).reshape(1, nb_s, -1),
            from_lm(prelx[(steps - k_lru) * nb_s:], k_lru)[None],
            from_lm(tail[0], k_ffn)[None])
```

```python
import functools

import numpy as np
import jax
import jax.numpy as jnp
from jax import lax
from jax.experimental import pallas as pl
from jax.experimental.pallas import tpu as pltpu

F32 = jnp.float32
BF16 = jnp.bfloat16

SSD_GROUPS = 4
SSD_CHUNK = 128
LRU_BLOCKS = 8
LRU_C = 8.0
LN_EPS = 1e-5
RMS_EPS = 1e-6
PAST_LEN = 16384

LANES = 128
SUBLANES = 8
VMEM_LIMIT_BYTES = 60 * 1024 * 1024

ROW_BLK = 32
MIXER_TILE = 2 * SSD_CHUNK
FFN_TILE = 4 * SSD_CHUNK


def _dot(a, b):
    if b.dtype == jnp.uint32:
        b = pltpu.bitcast(b, BF16)
    return jnp.dot(a, b, preferred_element_type=F32)


def _pack_kernel(w_ref, *out_refs, ranges):
    for o_ref, (a, b, pad, scale) in zip(out_refs, ranges):
        v = w_ref[:, a:b]
        if scale != 1.0:
            v = v * scale
        if pad:
            v = jnp.concatenate([v, jnp.zeros((v.shape[0], pad), v.dtype)], axis=1)
        o_ref[...] = pltpu.bitcast(v.astype(BF16), jnp.uint32)


def _pack_weight(w, ranges=None):
    k, n = w.shape
    single = ranges is None
    ranges = ((0, n, 0, 1.0),) if single else tuple(ranges)
    bk = 128 if n > 4096 else 256
    widths = [b - a + pad for a, b, pad, _ in ranges]
    outs = pl.pallas_call(
        functools.partial(_pack_kernel, ranges=ranges),
        grid=(k // bk,),
        in_specs=[pl.BlockSpec((bk, n), lambda i: (i, 0))],
        out_specs=tuple(pl.BlockSpec((bk // 2, wd), lambda i: (i, 0)) for wd in widths),
        out_shape=tuple(jax.ShapeDtypeStruct((k // 2, wd), jnp.uint32) for wd in widths),
        compiler_params=pltpu.CompilerParams(dimension_semantics=("arbitrary",),
                                             vmem_limit_bytes=VMEM_LIMIT_BYTES),
        name="pack_weight",
    )(w.astype(F32))
    return outs[0] if single else outs


def _pack_t_kernel(wt_ref, wdt_ref, *out_refs, parts, bn, dt_rows):
    j = pl.program_id(0)
    first = 0
    for o_ref, (nblk, scale) in zip(out_refs, parts):
        @pl.when(jnp.logical_and(j >= first, j < first + nblk))
        def _(o_ref=o_ref, scale=scale):
            v = wt_ref[...].T
            if scale != 1.0:
                v = v * scale
            o_ref[...] = pltpu.bitcast(v.astype(BF16), jnp.uint32)
        first += nblk

    @pl.when(j == 0)
    def _():
        rows = lax.broadcasted_iota(jnp.int32, wdt_ref.shape, 0)
        v = jnp.where(rows < dt_rows, wdt_ref[...], 0.0)
        out_refs[-1][...] = pltpu.bitcast(v.T.astype(BF16), jnp.uint32)


def _pack_w_in_t(wt, cuts, dt_index, scales, bn=512):
    n, k = wt.shape
    parts, offs = [], []
    for i in range(len(cuts) - 1):
        if i == dt_index:
            continue
        nblk = (cuts[i + 1] - cuts[i]) // bn
        assert nblk * bn == cuts[i + 1] - cuts[i]
        parts.append((nblk, scales[i]))
        offs += [cuts[i] + b * bn for b in range(nblk)]
    offs = np.asarray(offs, np.int32)
    firsts = np.cumsum([0] + [p[0] for p in parts])

    def row_off(j):
        off = jnp.int32(int(offs[0]))
        for idx in range(1, len(offs)):
            off = jnp.where(j >= idx, jnp.int32(int(offs[idx])), off)
        return off

    def out_map(p):
        return lambda j: (0, jnp.clip(j - int(firsts[p]), 0, parts[p][0] - 1))

    dt_rows = cuts[dt_index + 1] - cuts[dt_index]
    outs = pl.pallas_call(
        functools.partial(_pack_t_kernel, parts=tuple(parts), bn=bn, dt_rows=dt_rows),
        grid=(len(offs),),
        in_specs=[pl.BlockSpec((pl.Element(bn), pl.Element(k)), lambda j: (pl.multiple_of(row_off(j), SUBLANES), 0)),
                  pl.BlockSpec((pl.Element(LANES), pl.Element(k)), lambda j: (cuts[dt_index], 0))],
        out_specs=tuple(pl.BlockSpec((k // 2, bn), out_map(p)) for p in range(len(parts)))
        + (pl.BlockSpec((k // 2, LANES), lambda j: (0, 0)),),
        out_shape=tuple(jax.ShapeDtypeStruct((k // 2, nb_ * bn), jnp.uint32) for nb_, _ in parts)
        + (jax.ShapeDtypeStruct((k // 2, LANES), jnp.uint32),),
        compiler_params=pltpu.CompilerParams(dimension_semantics=("arbitrary",),
                                             vmem_limit_bytes=VMEM_LIMIT_BYTES),
        name="pack_w_in",
    )(wt, wt)
    outs = list(outs)
    dt_part = outs.pop()
    outs.insert(dt_index, dt_part)
    return outs


def _pack_rows_01(m):
    bits = np.ascontiguousarray(m, np.float32).view(np.uint32) >> 16
    return jnp.asarray(bits[0::2] | (bits[1::2] << 16), jnp.uint32)


def _dot_nt(a, b):
    return lax.dot_general(a, b, (((1,), (1,)), ((), ())), preferred_element_type=F32)


def _dot_tn(a, b):
    return lax.dot_general(a, b, (((0,), (0,)), ((), ())), preferred_element_type=F32)


def _split3(v):
    hi = v.astype(BF16)
    r1 = v - hi.astype(F32)
    mid = r1.astype(BF16)
    lo = (r1 - mid.astype(F32)).astype(BF16)
    return hi, mid, lo


def _dot_exact_lhs(m, v):
    hi, mid, lo = _split3(v)
    return _dot(m, hi) + _dot(m, mid) + _dot(m, lo)


def _split2(v):
    hi = v.astype(BF16)
    lo = (v - hi.astype(F32)).astype(BF16)
    return hi, lo


def _dot_2way_rhs(v, m2):
    return _dot(jnp.concatenate(_split2(v), axis=1), m2)


def _dot_2way_lhs(m2, v):
    return _dot(m2, jnp.concatenate(_split2(v), axis=0))


def _softplus(x):
    return jnp.maximum(x, 0.0) + jnp.log1p(jnp.exp(-jnp.abs(x)))


def _sigmoid(x):
    return 0.5 * jnp.tanh(0.5 * x) + 0.5


def _silu_of_half(h):
    return h + h * jnp.tanh(h)


def _gelu(x):
    c = np.sqrt(2.0 / np.pi).astype(np.float32)
    return 0.5 * x * (1.0 + jnp.tanh(c * (x + 0.044715 * (x * x * x))))


def _layer_norm(v, g, b):
    mu = jnp.mean(v, axis=-1, keepdims=True)
    d = v - mu
    var = jnp.mean(d * d, axis=-1, keepdims=True)
    return d * lax.rsqrt(var + LN_EPS) * g + b


def _rows(i, n):
    if isinstance(i, int):
        return slice(i * n, (i + 1) * n)
    return pl.ds(pl.multiple_of(i * n, n), n)


def _conv_block(buf_ref, w_ref, b_ref, hist, stride, r0, rows, c0, cw):
    taps = w_ref.shape[0]
    acc = b_ref[:, c0:c0 + cw]
    for k in range(taps):
        off = hist + r0 - (taps - 1 - k) * stride
        if not isinstance(off, int):
            off = pl.multiple_of(off, SUBLANES)
        acc = acc + w_ref[k:k + 1, c0:c0 + cw] * buf_ref[pl.ds(off, rows), c0:c0 + cw]
    return acc


LOG2_SUBLANES = 3
NPOS = SSD_CHUNK // SUBLANES
LOG2_NPOS = 4
assert 1 << LOG2_SUBLANES == SUBLANES and 1 << LOG2_NPOS == NPOS


def _tok_of_row(r):
    return (r & (SUBLANES - 1)) * NPOS + lax.shift_right_logical(r, LOG2_SUBLANES)


def _row_of_tok(t):
    return (t & (NPOS - 1)) * SUBLANES + lax.shift_right_logical(t, LOG2_NPOS)


def _perm_matrix():
    q = SSD_CHUNK
    r = lax.broadcasted_iota(jnp.int32, (q, q), 0)
    c = lax.broadcasted_iota(jnp.int32, (q, q), 1)
    return jnp.where(c == _tok_of_row(r), 1.0, 0.0).astype(BF16)


def _unperm_matrix():
    q = SSD_CHUNK
    t = lax.broadcasted_iota(jnp.int32, (q, q), 0)
    r = lax.broadcasted_iota(jnp.int32, (q, q), 1)
    return jnp.where(r == _row_of_tok(t), 1.0, 0.0).astype(BF16)


def _fill_wrap(buf_ref, wrap_ref, c, wrap, tail_ref, cols=slice(None)):
    q = SSD_CHUNK
    for m in range(wrap // SUBLANES):
        r_cur = (c + 1) * q - wrap + m * SUBLANES
        cur = buf_ref[r_cur:r_cur + SUBLANES, cols]
        if c == 0:
            prv = tail_ref[m * SUBLANES:(m + 1) * SUBLANES, cols]
        else:
            prv = buf_ref[r_cur - q:r_cur - q + SUBLANES, cols]
        sub0 = lax.broadcasted_iota(jnp.int32, cur.shape, 0) == 0
        wrap_ref[c * wrap + m * SUBLANES:c * wrap + (m + 1) * SUBLANES, cols] = jnp.where(
            sub0, pltpu.roll(prv, 1, 0), pltpu.roll(cur, 1, 0))


def _conv_seg(buf_ref, wrap_ref, w_ref, b_ref, c, wrap, c0, cw):
    q = SSD_CHUNK
    taps = w_ref.shape[0]
    cols = slice(c0, c0 + cw)
    acc = b_ref[:, cols] + w_ref[taps - 1:taps, cols] * buf_ref[c * q:(c + 1) * q, cols]
    for k in range(taps - 1):
        back = (taps - 1 - k) * SUBLANES
        shifted = jnp.concatenate(
            [wrap_ref[(c + 1) * wrap - back:(c + 1) * wrap, cols], buf_ref[c * q:(c + 1) * q - back, cols]],
            axis=0)
        acc = acc + w_ref[k:k + 1, cols] * shifted
    return acc


def _seg_tail_rows(wrap):
    n = wrap // SUBLANES
    return [(NPOS - n + m) * SUBLANES + SUBLANES - 1 for m in range(n)]


def _prompt_mixer_kernel(
        x_ref, wz_ref, wxbc_ref, wdt_ref, wlx_ref, wly_ref, wgs_ref, wgl_ref,
        scw_ref, scb_ref, dtb_ref, aneg_ref, dexp_ref, ng_ref, wso_ref,
        lcw_ref, lcb_ref, wax_ref, ba_ref, bx_ref, lam_ref, wlo_ref,
        bg_ref, wo_ref, l1g_ref, l1b_ref, e_ref,
        x1_ref, st_ref, sconv_ref, lst_ref, lconv_ref,
        xb_s, xp_s, xbc_s, swrap_s, stail_s, xc_s, lx_s, lwrap_s, ltail_s, xr_s, ly_s, y_s, z_s, ysb_s, ylb_s,
        ht_s, hl_s, gs_s, gl_s, ys_s, yl_s, o_s, mb_s,
        *, tl, alpha, n_heads, headdim, d_state):
    t = pl.program_id(1)
    nt = pl.num_programs(1)
    d_ssd = n_heads * headdim
    gn = SSD_GROUPS * d_state
    hpg = n_heads // SSD_GROUPS
    gw = hpg * headdim
    d_lru = lx_s.shape[1]
    d_model = x_ref.shape[2]
    q = SSD_CHUNK

    nch = tl // q
    wrap_s = (scw_ref.shape[0] - 1) * SUBLANES
    wrap_l = (lcw_ref.shape[0] - 1) * SUBLANES

    @pl.when(t == 0)
    def _():
        stail_s[...] = jnp.zeros(stail_s.shape, F32)
        ltail_s[...] = jnp.zeros(ltail_s.shape, F32)
        ht_s[...] = jnp.zeros(ht_s.shape, F32)
        hl_s[...] = jnp.zeros(hl_s.shape, F32)

    perm = _perm_matrix()
    perm2 = jnp.concatenate([perm, perm], axis=1)
    for c in range(nch):
        rows = _rows(c, q)
        xp = _dot_2way_lhs(perm2, x_ref[0, rows, :])
        xp_s[rows, :] = xp
        xb_s[rows, :] = xp.astype(BF16)

    cwid = 512

    def proj(dst_ref, w_ref, c0, cw):
        dst_ref[:, c0:c0 + cw] = _dot(xb_s[...], w_ref[:, c0:c0 + cw])

    def lru_conv(c):
        _fill_wrap(lx_s, lwrap_s, c, wrap_l, ltail_s)
        for c0 in range(0, d_lru, cwid):
            xr_s[c * q:(c + 1) * q, c0:c0 + cwid] = _conv_seg(lx_s, lwrap_s, lcw_ref, lcb_ref, c, wrap_l, c0, cwid)
        if c == nch - 1:
            ltail_s[...] = lx_s[tl - wrap_l:tl, :]

    bw = d_lru // LRU_BLOCKS
    sub = lax.broadcasted_iota(jnp.int32, (SUBLANES, bw), 0)
    crow = lax.broadcasted_iota(jnp.int32, (q, bw), 0)

    def lru_block(n):
        cols = slice(n * bw, (n + 1) * bw)
        xr = xr_s[:, cols]
        xrb = xr.astype(BF16)
        rg = _dot(xrb, wax_ref[n])
        r = _sigmoid(rg[:, 0:bw] + ba_ref[:, cols])
        gi = _sigmoid(rg[:, bw:2 * bw] + bx_ref[:, cols])
        log_a = (-LRU_C) * r * _softplus(-lam_ref[:, cols])
        a_all = jnp.exp(log_a)
        mult_all = jnp.sqrt(1.0 - a_all * a_all)
        for c in range(nch):
            a = a_all[c * q:(c + 1) * q, :]
            mult = mult_all[c * q:(c + 1) * q, :]
            if c == 0:
                mult = jnp.where(jnp.logical_and(crow == 0, t == 0), 1.0, mult)
            u = mult * gi[c * q:(c + 1) * q, :] * xr[c * q:(c + 1) * q, :]
            a_p = [a[i * SUBLANES:(i + 1) * SUBLANES, :] for i in range(NPOS)]
            u_p = [u[i * SUBLANES:(i + 1) * SUBLANES, :] for i in range(NPOS)]
            h = u_p[0]
            g = a_p[0]
            for i in range(1, NPOS):
                h = a_p[i] * h + u_p[i]
                g = a_p[i] * g
            gs = jnp.where(sub == 0, 0.0, pltpu.roll(g, 1, 0))
            hs = jnp.where(sub == 0, hl_s[0:1, cols], pltpu.roll(h, 1, 0))
            d = 1
            while d < SUBLANES:
                keep = sub >= d
                hs = jnp.where(keep, gs * pltpu.roll(hs, d, 0) + hs, hs)
                gs = jnp.where(keep, gs * pltpu.roll(gs, d, 0), gs)
                d *= 2
            h = hs
            out = []
            for i in range(NPOS):
                h = a_p[i] * h + u_p[i]
                out.append(h)
            hl_s[0:1, cols] = h[SUBLANES - 1:SUBLANES, :]
            hseq = jnp.concatenate(out, axis=0)
            ylb_s[c * q:(c + 1) * q, cols] = (hseq * _gelu(ly_s[c * q:(c + 1) * q, cols])).astype(BF16)

    def ssd_conv(c0):
        cols = slice(c0, c0 + cwid)
        for c in range(nch):
            _fill_wrap(xbc_s, swrap_s, c, wrap_s, stail_s, cols)
        stail_s[:, cols] = xbc_s[tl - wrap_s:tl, cols]
        for c in range(nch):
            xc_s[c * q:(c + 1) * q, cols] = _silu_of_half(
                _conv_seg(xbc_s, swrap_s, scw_ref, scb_ref, c, wrap_s, c0, cwid))

    def merge_gate(dst_ref, c0, cw, b0):
        dst_ref[:, c0:c0 + cw] = _sigmoid(dst_ref[:, c0:c0 + cw] + bg_ref[:, b0 + c0:b0 + c0 + cw])

    tok_r = _tok_of_row(lax.broadcasted_iota(jnp.int32, (q, q), 0))
    tok_c = _tok_of_row(lax.broadcasted_iota(jnp.int32, (q, q), 1))
    causal = tok_r >= tok_c
    tri = jnp.where(causal, 1.0, 0.0).astype(BF16)
    lane_i = lax.broadcasted_iota(jnp.int32, (q, LANES), 1)
    left = lane_i < headdim

    def chunk_body(c, carry):
        rows = _rows(c, q)
        dt = _softplus(_dot(xb_s[rows, :], wdt_ref[...]) + dtb_ref[...])
        d_a = dt * aneg_ref[...]
        a_cs = _dot_exact_lhs(tri, d_a)
        a_last = a_cs[q - 1:q, :]
        wgt = dt * jnp.exp(a_last - a_cs)
        ea = jnp.exp(a_cs)
        w_exp = _dot_2way_rhs(wgt, e_ref[...])
        ea_exp = _dot_2way_rhs(ea, e_ref[...])
        a_cs_t = a_cs.T
        dt_t = dt.T
        for g in range(SSD_GROUPS):
            b_g = xc_s[rows, d_ssd + g * d_state:d_ssd + (g + 1) * d_state]
            c_g = xc_s[rows, d_ssd + gn + g * d_state:d_ssd + gn + (g + 1) * d_state]
            b_gb = b_g.astype(BF16)
            c_gb = c_g.astype(BF16)
            cb = _dot_nt(c_gb, b_gb)
            for hp in range(hpg // 2):
                c0 = g * gw + hp * 2 * headdim
                xs_pair = xc_s[rows, c0:c0 + 2 * headdim]
                lmats = []
                for j in range(2):
                    h = g * hpg + hp * 2 + j
                    seg = (jnp.broadcast_to(a_cs[:, h:h + 1], (q, q))
                           - jnp.broadcast_to(a_cs_t[h:h + 1, :], (q, q)))
                    dec = jnp.exp(jnp.where(causal, seg, -jnp.inf))
                    lmats.append((cb * dec * jnp.broadcast_to(dt_t[h:h + 1, :], (q, q))).astype(BF16))
                lpair = jnp.concatenate(lmats, axis=1)
                rhs = jnp.concatenate([jnp.where(left, xs_pair, 0.0),
                                       jnp.where(left, 0.0, xs_pair)], axis=0).astype(BF16)
                y_s[rows, c0:c0 + 2 * headdim] = _dot(lpair, rhs)
            gcols = slice(g * gw, (g + 1) * gw)
            h_g = ht_s[:, gcols]
            y_off = _dot(c_gb, h_g.astype(BF16)) * ea_exp[:, gcols]
            y_s[rows, gcols] = y_s[rows, gcols] + y_off
            xw = (xc_s[rows, gcols] * w_exp[:, gcols]).astype(BF16)
            ht_s[:, gcols] = h_g * ea_exp[q - 1:q, gcols] + _dot_tn(b_gb, xw)
            if carry is not None and g < len(carry):
                carry[g]()
        return carry

    def gate_body(i, carry):
        rows = _rows(i, ROW_BLK)
        y = y_s[rows, :] + dexp_ref[...] * xc_s[rows, 0:d_ssd]
        y = y * _silu_of_half(z_s[rows, :])
        ms = jnp.mean(y * y, axis=-1, keepdims=True)
        ysb_s[rows, :] = (y * lax.rsqrt(ms + RMS_EPS) * ng_ref[...]).astype(BF16)
        return carry

    assert nch == 2 and LRU_BLOCKS == 8 and d_lru == 2 * cwid and d_model == 2 * cwid
    d_xbc = xc_s.shape[1]
    nxb = d_xbc // cwid
    proj(lx_s, wlx_ref, 0, cwid)
    proj(lx_s, wlx_ref, cwid, cwid)
    proj(ly_s, wly_ref, 0, cwid); lru_conv(0)
    proj(ly_s, wly_ref, cwid, cwid); lru_conv(1)
    P = functools.partial
    mxu_a = ([P(proj, xbc_s, wxbc_ref, j * cwid, cwid) for j in range(nxb)]
             + [P(proj, gs_s, wgs_ref, j * cwid, cwid) for j in range(2)])
    for n in range(LRU_BLOCKS):
        mxu_a[n]()
        lru_block(n)
        if 1 <= n <= nxb:
            ssd_conv((n - 1) * cwid)
    gates_per_chunk = q // ROW_BLK
    nz = d_ssd // cwid
    chunk_body(0, [P(proj, z_s, wz_ref, j * cwid, cwid) for j in range(nz)])
    merge_gate(gs_s, 0, d_model, 0)
    for i in range(gates_per_chunk):
        gate_body(i, None)

    def wlo_piece(j):
        yl_s[:, j * cwid:(j + 1) * cwid] = _dot(ylb_s[...], wlo_ref[:, j * cwid:(j + 1) * cwid])

    chunk_body(1, [P(proj, gl_s, wgl_ref, 0, cwid), P(proj, gl_s, wgl_ref, cwid, cwid),
                   P(wlo_piece, 0), P(wlo_piece, 1)])
    merge_gate(gl_s, 0, d_model, d_model)
    for i in range(gates_per_chunk, 2 * gates_per_chunk):
        gate_body(i, None)

    ys_s[...] = _dot(ysb_s[...], wso_ref[...])
    for i in range(tl // ROW_BLK):
        rows = _rows(i, ROW_BLK)
        mb_s[rows, :] = (gs_s[rows, :] * ys_s[rows, :] + gl_s[rows, :] * yl_s[rows, :]).astype(BF16)
    o_s[...] = _dot(mb_s[...], wo_ref[...])
    for i in range(tl // ROW_BLK):
        rows = _rows(i, ROW_BLK)
        v = alpha * xp_s[rows, :] + o_s[rows, :]
        x1_ref[0, rows, :] = _layer_norm(v, l1g_ref[...], l1b_ref[...])

    @pl.when(t == nt - 1)
    def _():
        st_ref[0] = ht_s[...].T
        lst_ref[0] = hl_s[0:1, :]
        for m, r in enumerate(_seg_tail_rows(wrap_s)):
            sconv_ref[0, m:m + 1, :] = xbc_s[tl - q + r:tl - q + r + 1, :]
        for m, r in enumerate(_seg_tail_rows(wrap_l)):
            lconv_ref[0, m:m + 1, :] = lx_s[tl - q + r:tl - q + r + 1, :]


def _const_spec(shape):
    nd = len(shape)
    return pl.BlockSpec(shape, lambda *_: (0,) * nd, pipeline_mode=pl.Buffered(1))


def _prompt_mixer(x, wts, tl):
    nb, seq, d_model = x.shape
    n_heads, headdim, d_state = wts['n_heads'], wts['headdim'], wts['d_state']
    d_ssd = n_heads * headdim
    d_xbc = d_ssd + 2 * SSD_GROUPS * d_state
    d_lru = wts['wlx'].shape[1]
    names = ['wz', 'wxbc', 'wdt', 'wlx', 'wly', 'wgs', 'wgl', 'scw', 'scb', 'dtb', 'aneg', 'dexp', 'ng',
             'wso', 'lcw', 'lcb', 'wax', 'ba', 'bx', 'lam', 'wlo', 'bg', 'wo', 'l1g', 'l1b', 'ee']
    consts = [wts[k] for k in names]
    kern = functools.partial(_prompt_mixer_kernel, tl=tl, alpha=wts['alpha'], n_heads=n_heads,
                             headdim=headdim, d_state=d_state)
    out_shape = (
        jax.ShapeDtypeStruct((nb, seq, d_model), F32),
        jax.ShapeDtypeStruct((nb, d_ssd, d_state), F32),
        jax.ShapeDtypeStruct((nb, 3, d_xbc), F32),
        jax.ShapeDtypeStruct((nb, 1, d_lru), F32),
        jax.ShapeDtypeStruct((nb, 3, d_lru), F32),
    )
    out_specs = (
        pl.BlockSpec((1, tl, d_model), lambda b, t: (b, t, 0)),
        pl.BlockSpec((1, d_ssd, d_state), lambda b, t: (b, 0, 0)),
        pl.BlockSpec((1, 3, d_xbc), lambda b, t: (b, 0, 0)),
        pl.BlockSpec((1, 1, d_lru), lambda b, t: (b, 0, 0)),
        pl.BlockSpec((1, 3, d_lru), lambda b, t: (b, 0, 0)),
    )
    nch = tl // SSD_CHUNK
    wrap_s = (wts['scw'].shape[0] - 1) * SUBLANES
    wrap_l = (wts['lcw'].shape[0] - 1) * SUBLANES
    scratch = [
        pltpu.VMEM((tl, d_model), BF16),
        pltpu.VMEM((tl, d_model), F32),
        pltpu.VMEM((tl, d_xbc), F32),
        pltpu.VMEM((nch * wrap_s, d_xbc), F32),
        pltpu.VMEM((wrap_s, d_xbc), F32),
        pltpu.VMEM((tl, d_xbc), F32),
        pltpu.VMEM((tl, d_lru), F32),
        pltpu.VMEM((nch * wrap_l, d_lru), F32),
        pltpu.VMEM((wrap_l, d_lru), F32),
        pltpu.VMEM((tl, d_lru), F32),
        pltpu.VMEM((tl, d_lru), F32),
        pltpu.VMEM((tl, d_ssd), F32),
        pltpu.VMEM((tl, d_ssd), F32),
        pltpu.VMEM((tl, d_ssd), BF16),
        pltpu.VMEM((tl, d_lru), BF16),
        pltpu.VMEM((d_state, d_ssd), F32),
        pltpu.VMEM((SUBLANES, d_lru), F32),
        pltpu.VMEM((tl, d_model), F32),
        pltpu.VMEM((tl, d_model), F32),
        pltpu.VMEM((tl, d_model), F32),
        pltpu.VMEM((tl, d_model), F32),
        pltpu.VMEM((tl, d_model), F32),
        pltpu.VMEM((tl, d_model), BF16),
    ]
    return pl.pallas_call(
        kern,
        grid=(nb, seq // tl),
        in_specs=[pl.BlockSpec((1, tl, d_model), lambda b, t: (b, t, 0))]
        + [_const_spec(c.shape) for c in consts],
        out_specs=out_specs,
        out_shape=out_shape,
        scratch_shapes=scratch,
        compiler_params=pltpu.CompilerParams(
            dimension_semantics=("arbitrary", "arbitrary"),
            vmem_limit_bytes=VMEM_LIMIT_BYTES),
        name="prompt_mixer",
    )(x, *consts)


def _ffn_kernel(x_ref, h0_ref, wg_ref, wu_ref, cw_ref, cb_ref, wd_ref, g_ref, b_ref,
                y_ref, tail_ref, xb_s, gb_s, hb_s, *, tm, stride, hist, alpha, fchunk):
    t = pl.program_id(1)
    nt = pl.num_programs(1)
    taps = cw_ref.shape[0]
    nh = (taps - 1) * stride
    d_ff = gb_s.shape[1]

    @pl.when(t == 0)
    def _():
        gb_s[hist - nh:hist, :] = h0_ref[0]

    xb_s[...] = x_ref[0].astype(BF16)
    acc = None
    for c0 in range(0, d_ff, fchunk):
        cols = slice(c0, c0 + fchunk)
        gb_s[hist:hist + tm, cols] = _dot(xb_s[...], wg_ref[:, cols])
        up = _dot(xb_s[...], wu_ref[:, cols])
        gc = _conv_block(gb_s, cw_ref, cb_ref, hist, stride, 0, tm, c0, fchunk)
        hb_s[...] = (_gelu(gc) * up).astype(BF16)
        part = _dot(hb_s[...], wd_ref[c0 // 2:(c0 + fchunk) // 2, :])
        acc = part if acc is None else acc + part
    v = alpha * x_ref[0] + acc
    y_ref[0] = _layer_norm(v, g_ref[...], b_ref[...])
    gb_s[hist - nh:hist, :] = gb_s[hist + tm - nh:hist + tm, :]

    @pl.when(t == nt - 1)
    def _():
        tail_ref[0] = gb_s[hist - nh:hist, :]


def _ffn_seg_kernel(x_ref, h0_ref, st_ref, c8_ref, b8_ref, xw8_ref, ea8_ref,
                    wg_ref, wu_ref, cw_ref, cb_ref, wd_ref, g_ref, b_ref,
                    y_ref, tail_ref, nst_ref, yoff8_ref,
                    xb_s, gb_s, gwrap_s, gtail_s, hb_s, *, tm, alpha, fchunk, state_dims):
    t = pl.program_id(1)
    state_pieces = [
        functools.partial(_state_pair_group, g, st_ref.at[p], c8_ref.at[p], b8_ref.at[p], xw8_ref.at[p],
                          ea8_ref.at[p], nst_ref.at[p], yoff8_ref.at[p], **state_dims)
        for p in range(st_ref.shape[0]) for g in range(SSD_GROUPS)]

    def state_work(n):
        for _ in range(min(n, len(state_pieces))):
            state_pieces.pop(0)()
    nt = pl.num_programs(1)
    q = SSD_CHUNK
    nch = tm // q
    wrap = (cw_ref.shape[0] - 1) * SUBLANES
    d_ff = gb_s.shape[1]
    tail_rows = _seg_tail_rows(wrap)

    @pl.when(t == 0)
    def _():
        gtail_s[...] = jnp.zeros(gtail_s.shape, F32)
        for m in range(len(tail_rows)):
            r = m * SUBLANES + SUBLANES - 1
            gtail_s[r:r + 1, :] = h0_ref[0, m:m + 1, :]

    xb_s[...] = x_ref[0].astype(BF16)
    def gate_proj(c0):
        cols = slice(c0, c0 + fchunk)
        gb_s[:, cols] = _dot(xb_s[...], wg_ref[:, cols])
        for c in range(nch):
            _fill_wrap(gb_s, gwrap_s, c, wrap, gtail_s, cols)
        gtail_s[:, cols] = gb_s[tm - wrap:tm, cols]

    per_slot = -(-len(state_pieces) // (1 + 2 * (d_ff // fchunk)))
    gate_proj(0)
    state_work(per_slot)
    acc = None
    for c0 in range(0, d_ff, fchunk):
        cols = slice(c0, c0 + fchunk)
        up = _dot(xb_s[...], wu_ref[:, cols])
        if c0 + fchunk < d_ff:
            gate_proj(c0 + fchunk)
        state_work(per_slot)
        for c in range(nch):
            gc = _conv_seg(gb_s, gwrap_s, cw_ref, cb_ref, c, wrap, c0, fchunk)
            hb_s[c * q:(c + 1) * q, :] = (_gelu(gc) * up[c * q:(c + 1) * q, :]).astype(BF16)
        part = _dot(hb_s[...], wd_ref[c0 // 2:(c0 + fchunk) // 2, :])
        state_work(per_slot)
        acc = part if acc is None else acc + part
    state_work(len(state_pieces))

    unperm = _unperm_matrix()
    unperm2 = jnp.concatenate([unperm, unperm], axis=1)
    for c in range(nch):
        rows = slice(c * q, (c + 1) * q)
        y = _layer_norm(alpha * x_ref[0, rows, :] + acc[rows, :], g_ref[...], b_ref[...])
        y_ref[0, rows, :] = _dot_2way_lhs(unperm2, y)

    @pl.when(t == nt - 1)
    def _():
        for m, r in enumerate(tail_rows):
            tail_ref[0, m:m + 1, :] = gb_s[tm - q + r:tm - q + r + 1, :]


def _ffn_seg(x, hist0, state, c8, b8, xw8, ea8, wts, tm, steps):
    nb, seq, d_model = x.shape
    d_ff = wts['wg'].shape[1]
    taps = wts['fcw'].shape[0]
    wrap = (taps - 1) * SUBLANES
    fchunk = 1024
    nt = seq // tm
    npairs = state.shape[0]
    pps = npairs // (nb * nt)
    assert pps * nb * nt == npairs
    consts = [wts[k] for k in ['wg', 'wu', 'fcw', 'fcb', 'wd', 'l2g', 'l2b']]
    state_dims = dict(steps=steps, n_heads=wts['n_heads'], headdim=wts['headdim'], d_state=wts['d_state'])
    kern = functools.partial(_ffn_seg_kernel, tm=tm, alpha=wts['alpha'], fchunk=fchunk, state_dims=state_dims)
    pair_blk = lambda a: pl.BlockSpec((pps,) + a.shape[1:], lambda b, t: (b * nt + t,) + (0,) * (a.ndim - 1))
    return pl.pallas_call(
        kern,
        grid=(nb, nt),
        in_specs=[pl.BlockSpec((1, tm, d_model), lambda b, t: (b, t, 0)),
                  pl.BlockSpec((1, taps - 1, d_ff), lambda b, t: (b, 0, 0)),
                  pair_blk(state), pair_blk(c8), pair_blk(b8), pair_blk(xw8), pair_blk(ea8)]
        + [_const_spec(c.shape) for c in consts],
        out_specs=(pl.BlockSpec((1, tm, d_model), lambda b, t: (b, t, 0)),
                   pl.BlockSpec((1, taps - 1, d_ff), lambda b, t: (b, 0, 0)),
                   pair_blk(state), pair_blk(xw8)),
        out_shape=(jax.ShapeDtypeStruct((nb, seq, d_model), F32),
                   jax.ShapeDtypeStruct((nb, taps - 1, d_ff), F32),
                   jax.ShapeDtypeStruct(state.shape, F32),
                   jax.ShapeDtypeStruct(xw8.shape, F32)),
        scratch_shapes=[pltpu.VMEM((tm, d_model), BF16),
                        pltpu.VMEM((tm, d_ff), F32),
                        pltpu.VMEM((tm // SSD_CHUNK * wrap, d_ff), F32),
                        pltpu.VMEM((wrap, d_ff), F32),
                        pltpu.VMEM((tm, fchunk), BF16)],
        compiler_params=pltpu.CompilerParams(
            dimension_semantics=("arbitrary", "arbitrary"),
            vmem_limit_bytes=VMEM_LIMIT_BYTES),
        name="conv_ffn_seg",
    )(x, hist0, state, c8, b8, xw8, ea8, *consts)


def _ffn(x, hist0, wts, tm, stride):
    nb, seq, d_model = x.shape
    d_ff = wts['wg'].shape[1]
    taps = wts['fcw'].shape[0]
    nh = (taps - 1) * stride
    hist = -(-nh // SUBLANES) * SUBLANES
    consts = [wts[k] for k in ['wg', 'wu', 'fcw', 'fcb', 'wd', 'l2g', 'l2b']]
    kern = functools.partial(_ffn_kernel, tm=tm, stride=stride, hist=hist, alpha=wts['alpha'], fchunk=1024)
    return pl.pallas_call(
        kern,
        grid=(nb, seq // tm),
        in_specs=[pl.BlockSpec((1, tm, d_model), lambda b, t: (b, t, 0)),
                  pl.BlockSpec((1, nh, d_ff), lambda b, t: (b, 0, 0))]
        + [_const_spec(c.shape) for c in consts],
        out_specs=(pl.BlockSpec((1, tm, d_model), lambda b, t: (b, t, 0)),
                   pl.BlockSpec((1, nh, d_ff), lambda b, t: (b, 0, 0))),
        out_shape=(jax.ShapeDtypeStruct((nb, seq, d_model), F32),
                   jax.ShapeDtypeStruct((nb, nh, d_ff), F32)),
        scratch_shapes=[pltpu.VMEM((tm, d_model), BF16),
                        pltpu.VMEM((hist + tm, d_ff), F32),
                        pltpu.VMEM((tm, 1024), BF16)],
        compiler_params=pltpu.CompilerParams(
            dimension_semantics=("arbitrary", "arbitrary"),
            vmem_limit_bytes=VMEM_LIMIT_BYTES),
        name="conv_ffn",
    )(x, hist0, *consts)


def _sample_ssd_kernel(
        x_ref, xall_ref, cssd_ref, wxbc_ref, wdt_ref, scw_ref, scb_ref, dtb_ref, aneg_ref, dexp_ref,
        e_ref, e2_ref,
        pre_ref, c_ref, b_ref, ea_ref, ydg_ref, xw_ref,
        xbc_s, xs_s, bs_s, acs_s, dts_s,
        *, nseq, steps, n_heads, headdim, d_state):
    l = pl.program_id(0)
    d_ssd = n_heads * headdim
    gn = SSD_GROUPS * d_state
    hist = (scw_ref.shape[0] - 1) * nseq
    r0 = l * nseq

    def blk(i):
        return pl.ds(pl.multiple_of(i * nseq, nseq), nseq)

    def sblk(i):
        return slice(i * nseq, (i + 1) * nseq)

    @pl.when(l == 0)
    def _():
        xbc_s[0:hist, :] = cssd_ref[...]
        dts_s[...] = _softplus(_dot(xall_ref[...].astype(BF16), wdt_ref[...]) + dtb_ref[...])
        acc = jnp.zeros((nseq, LANES), F32)
        for s in range(steps):
            acc = acc + dts_s[sblk(s), :] * aneg_ref[...]
            acs_s[sblk(s), :] = acc

    xb = x_ref[...].astype(BF16)

    pre = _dot(xb, wxbc_ref[...])
    pre_ref[...] = pre
    xbc_s[pl.ds(pl.multiple_of(hist + r0, nseq), nseq), :] = pre
    cwid = 512
    for c0 in range(0, d_ssd, cwid):
        xs_s[blk(l), c0:c0 + cwid] = _silu_of_half(
            _conv_block(xbc_s, scw_ref, scb_ref, hist, nseq, r0, nseq, c0, cwid))
    b_l = _silu_of_half(_conv_block(xbc_s, scw_ref, scb_ref, hist, nseq, r0, nseq, d_ssd, gn))
    c_l = _silu_of_half(_conv_block(xbc_s, scw_ref, scb_ref, hist, nseq, r0, nseq, d_ssd + gn, gn))
    bs_s[blk(l), :] = b_l
    b_ref[...] = b_l
    c_ref[...] = c_l

    a_cs = acs_s[blk(l), :]
    dt = dts_s[blk(l), :]
    a_end = acs_s[sblk(steps - 1), :]
    ea_ref[...] = _dot_2way_rhs(jnp.exp(a_cs), e_ref[...])
    xw_ref[...] = xs_s[blk(l), :] * _dot_2way_rhs(dt * jnp.exp(a_end - a_cs), e_ref[...])

    ydg_ref[...] = dexp_ref[...] * xs_s[blk(l), :]
    for s in range(steps):
        @pl.when(s <= l)
        def _(s=s):
            coef = _dot_2way_rhs(jnp.exp(a_cs - acs_s[sblk(s), :]) * dts_s[sblk(s), :], e_ref[...])
            cbx = _dot_2way_rhs(bs_s[sblk(s), :] * c_l, e2_ref[...])
            ydg_ref[...] += cbx * coef * xs_s[sblk(s), :]


def _sample_lru_kernel(
        x_ref, clru_ref, slru_ref, wlx_ref, wly_ref, lcw_ref, lcb_ref, wax_ref, ba_ref, bx_ref,
        lam_ref, prelx_ref, ylru_ref, lst_ref, lx_s, hl_s, *, nseq, steps, start_pos):
    l = pl.program_id(0)
    hist = (lcw_ref.shape[0] - 1) * nseq
    d_lru = lx_s.shape[1]
    r0 = l * nseq

    @pl.when(l == 0)
    def _():
        lx_s[0:hist, :] = clru_ref[...]
        hl_s[...] = slru_ref[...]

    xb = x_ref[...].astype(BF16)
    prelx = _dot(xb, wlx_ref[...])
    prelx_ref[...] = prelx
    lx_s[pl.ds(pl.multiple_of(hist + r0, nseq), nseq), :] = prelx
    ly = _dot(xb, wly_ref[...])
    bw = d_lru // LRU_BLOCKS
    first = (l + start_pos) == 0
    for n in range(LRU_BLOCKS):
        cols = slice(n * bw, (n + 1) * bw)
        xr = _conv_block(lx_s, lcw_ref, lcb_ref, hist, nseq, r0, nseq, n * bw, bw)
        xrb = xr.astype(BF16)
        rg = _dot(xrb, wax_ref[n])
        r = _sigmoid(rg[:, 0:bw] + ba_ref[:, cols])
        gi = _sigmoid(rg[:, bw:2 * bw] + bx_ref[:, cols])
        log_a = (-LRU_C) * r * _softplus(-lam_ref[:, cols])
        a = jnp.exp(log_a)
        mult = jnp.where(first, 1.0, jnp.sqrt(1.0 - a * a))
        h = a * hl_s[:, cols] + mult * gi * xr
        hl_s[:, cols] = h
        ylru_ref[:, cols] = (h * _gelu(ly[:, cols])).astype(BF16)

    @pl.when(l == steps - 1)
    def _():
        lst_ref[...] = hl_s[...]


def _state_pair_group(g, st_ref, c8_ref, b8_ref, xw8_ref, ea8_ref, nst_ref, yoff8_ref,
                      *, steps, n_heads, headdim, d_state):
    nrow = 2 * steps
    hpg = n_heads // SSD_GROUPS
    gw = hpg * headdim
    assert 2 * headdim == LANES and d_state == LANES
    par = lax.broadcasted_iota(jnp.int32, (nrow, gw), 0) % 2
    low = lax.broadcasted_iota(jnp.int32, (nrow, LANES), 1) < headdim
    gcols = slice(g * gw, (g + 1) * gw)
    c8g = c8_ref[:, g * d_state:(g + 1) * d_state].astype(BF16)
    b8g = b8_ref[:, g * d_state:(g + 1) * d_state].astype(BF16)
    xw8 = xw8_ref[:, gcols]
    ea8 = ea8_ref[:, gcols]
    cds = []
    for hp in range(hpg // 2):
        pair = ea8[:, hp * LANES:(hp + 1) * LANES]
        swapped = pltpu.roll(pair, headdim, 1)
        cds.append(jnp.where(low, pair, swapped))
        cds.append(jnp.where(low, swapped, pair))
    yo = None
    for e in range(2):
        sg = st_ref[e, gcols, :]
        yo_e = _dot_nt(c8g, sg.astype(BF16))
        yo = yo_e if e == 0 else jnp.where(par == e, yo_e, yo)
        xw_e = jnp.where(par == e, xw8, 0.0).astype(BF16)
        upd = _dot_tn(xw_e, b8g)
        k_last = 2 * (steps - 1) + e
        for hh in range(hpg):
            cd = cds[hh][k_last:k_last + 1, :]
            hr = slice(hh * headdim, (hh + 1) * headdim)
            nst_ref[e, g * gw + hh * headdim:g * gw + (hh + 1) * headdim, :] = sg[hr, :] * cd + upd[hr, :]
    yoff8_ref[:, gcols] = yo * ea8


def _sample_post_kernel(x_ref, ydg_ref, yoff_ref, ylru_ref, wz_ref, ng_ref, wso_ref, wlo_ref,
                        wgs_ref, wgl_ref, bg_ref, wo_ref, l1g_ref, l1b_ref, x1_ref, *, alpha):
    d_model = x_ref.shape[1]
    xb = x_ref[...].astype(BF16)
    y = (ydg_ref[...] + yoff_ref[...]) * _silu_of_half(_dot(xb, wz_ref[...]))
    ms = jnp.mean(y * y, axis=-1, keepdims=True)
    ysb = (y * lax.rsqrt(ms + RMS_EPS) * ng_ref[...]).astype(BF16)
    g_ssd = _sigmoid(_dot(xb, wgs_ref[...]) + bg_ref[:, 0:d_model])
    g_lru = _sigmoid(_dot(xb, wgl_ref[...]) + bg_ref[:, d_model:2 * d_model])
    merged = g_ssd * _dot(ysb, wso_ref[...]) + g_lru * _dot(ylru_ref[...], wlo_ref[...])
    o = _dot(merged.astype(BF16), wo_ref[...])
    x1_ref[...] = _layer_norm(alpha * x_ref[...] + o, l1g_ref[...], l1b_ref[...])


def _sample_front(x_lm, cssd_lm, clru_lm, slru_lm, wts, nseq, steps, start_pos):
    n_heads, headdim, d_state = wts['n_heads'], wts['headdim'], wts['d_state']
    d_model = x_lm.shape[1]
    d_ssd = n_heads * headdim
    gn = SSD_GROUPS * d_state
    d_xbc = d_ssd + 2 * gn
    d_lru = wts['wlx'].shape[1]
    ntok = steps * nseq
    dims = dict(nseq=nseq, steps=steps, n_heads=n_heads, headdim=headdim, d_state=d_state)
    params = pltpu.CompilerParams(dimension_semantics=("arbitrary",), vmem_limit_bytes=VMEM_LIMIT_BYTES)
    step_blk = lambda w: pl.BlockSpec((nseq, w), lambda l: (l, 0))

    sds = jax.ShapeDtypeStruct
    ssd_names = ['wxbc', 'wdt', 'scw', 'scb', 'dtb', 'aneg', 'dexp', 'ee', 'e2e']
    ssd_consts = [x_lm, cssd_lm] + [wts[k] for k in ssd_names]
    pre, c_lm, b_lm, ea_lm, ydg_lm, xw_lm = pl.pallas_call(
        functools.partial(_sample_ssd_kernel, **dims),
        grid=(steps,),
        in_specs=[step_blk(d_model)] + [_const_spec(c.shape) for c in ssd_consts],
        out_specs=(step_blk(d_xbc), step_blk(gn), step_blk(gn), step_blk(d_ssd), step_blk(d_ssd),
                   step_blk(d_ssd)),
        out_shape=(sds((ntok, d_xbc), F32), sds((ntok, gn), F32), sds((ntok, gn), F32),
                   sds((ntok, d_ssd), F32), sds((ntok, d_ssd), F32), sds((ntok, d_ssd), F32)),
        scratch_shapes=[
            pltpu.VMEM((cssd_lm.shape[0] + ntok, d_xbc), F32),
            pltpu.VMEM((ntok, d_ssd), F32),
            pltpu.VMEM((ntok, gn), F32),
            pltpu.VMEM((ntok, LANES), F32),
            pltpu.VMEM((ntok, LANES), F32),
        ],
        compiler_params=params,
        name="sample_ssd",
    )(x_lm, *ssd_consts)

    lru_names = ['wlx', 'wly', 'lcw', 'lcb', 'wax', 'ba', 'bx', 'lam']
    lru_consts = [clru_lm, slru_lm] + [wts[k] for k in lru_names]
    prelx, ylru_lm, lst = pl.pallas_call(
        functools.partial(_sample_lru_kernel, nseq=nseq, steps=steps, start_pos=start_pos),
        grid=(steps,),
        in_specs=[step_blk(d_model)] + [_const_spec(c.shape) for c in lru_consts],
        out_specs=(step_blk(d_lru), step_blk(d_lru), pl.BlockSpec((nseq, d_lru), lambda l: (0, 0))),
        out_shape=(sds((ntok, d_lru), F32), sds((ntok, d_lru), BF16), sds((nseq, d_lru), F32)),
        scratch_shapes=[
            pltpu.VMEM((clru_lm.shape[0] + ntok, d_lru), F32),
            pltpu.VMEM((nseq, d_lru), F32),
        ],
        compiler_params=params,
        name="sample_lru",
    )(x_lm, *lru_consts)

    return dict(pre=pre, c=c_lm, b=b_lm, ea=ea_lm, ydg=ydg_lm, xw=xw_lm, prelx=prelx, ylru=ylru_lm, lst=lst)


def _sample_back(x_lm, ydg_lm, yoff_lm, ylru_lm, wts, nseq, steps):
    d_model = x_lm.shape[1]
    d_ssd = wts['n_heads'] * wts['headdim']
    d_lru = wts['wlx'].shape[1]
    ntok = steps * nseq
    params = pltpu.CompilerParams(dimension_semantics=("arbitrary",), vmem_limit_bytes=VMEM_LIMIT_BYTES)
    step_blk = lambda w: pl.BlockSpec((nseq, w), lambda l: (l, 0))
    sds = jax.ShapeDtypeStruct
    post_names = ['wz', 'ng', 'wso', 'wlo', 'wgs', 'wgl', 'bg', 'wo', 'l1g', 'l1b']
    post_consts = [wts[k] for k in post_names]
    x1_lm = pl.pallas_call(
        functools.partial(_sample_post_kernel, alpha=wts['alpha']),
        grid=(steps,),
        in_specs=[step_blk(d_model), step_blk(d_ssd), step_blk(d_ssd), step_blk(d_lru)]
        + [_const_spec(c.shape) for c in post_consts],
        out_specs=step_blk(d_model),
        out_shape=sds((ntok, d_model), F32),
        compiler_params=params,
        name="sample_post",
    )(x_lm, ydg_lm, yoff_lm, ylru_lm, *post_consts)
    return x1_lm


def _prep_weights(w_in, b_gate, ssd_conv_w, ssd_conv_b, ssd_dt_bias, ssd_a_log, ssd_d, ssd_norm_g,
                  w_ssd_out, lru_conv_w, lru_conv_b, lru_wa, lru_ba, lru_wx, lru_bx, lru_lambda,
                  w_lru_out, w_o, ln1_g, ln1_b, ffn_w_gate, ffn_w_up, ffn_conv_w, ffn_conv_b,
                  ffn_w_down, ln2_g, ln2_b, n_heads, headdim, d_state):
    depth = w_in.shape[0]
    d_model = w_in.shape[1]
    d_ssd = n_heads * headdim
    d_xbc = d_ssd + 2 * SSD_GROUPS * d_state
    d_lru = lru_lambda.shape[1]
    sizes = (d_ssd, d_xbc, n_heads, d_lru, d_lru, d_model, d_model)
    cuts = np.cumsum((0,) + sizes)
    scales = [0.5 if i == 0 else 1.0 for i in range(len(sizes))]
    parts = _pack_w_in_t(jnp.swapaxes(w_in, 1, 2)[0].astype(F32), [int(c) for c in cuts], 2, scales)
    row = lambda v: v.reshape(1, -1).astype(F32)
    mat = _pack_weight
    wax = jnp.concatenate([lru_wa[0], lru_wx[0]], axis=-1)
    wax = _pack_weight(wax.reshape(-1, wax.shape[-1])).reshape(wax.shape[0], wax.shape[1] // 2, wax.shape[2])
    pad_heads = lambda v: jnp.pad(v.reshape(1, -1).astype(F32), ((0, 0), (0, LANES - n_heads)))
    head_of_col = np.arange(d_ssd) // headdim
    expand = (np.arange(LANES)[:, None] == head_of_col[None, :]).astype(np.float32)
    group_sum = (np.arange(SSD_GROUPS * d_state)[:, None] // d_state
                 == (head_of_col // (n_heads // SSD_GROUPS))[None, :])
    return dict(
        n_heads=n_heads, headdim=headdim, d_state=d_state,
        alpha=float((2.0 * depth) ** 0.25),
        wz=parts[0], wxbc=parts[1], wdt=parts[2], wlx=parts[3], wly=parts[4], wgs=parts[5], wgl=parts[6],
        scw=0.5 * ssd_conv_w[0].astype(F32), scb=0.5 * row(ssd_conv_b[0]),
        dtb=pad_heads(ssd_dt_bias[0]), aneg=pad_heads(-jnp.exp(ssd_a_log[0].astype(F32))),
        dexp=row(jnp.repeat(ssd_d[0], headdim)), ng=row(ssd_norm_g[0]),
        wso=mat(w_ssd_out[0]),
        lcw=lru_conv_w[0].astype(F32), lcb=row(lru_conv_b[0]),
        ba=row(lru_ba[0]), bx=row(lru_bx[0]), wax=wax,
        lam=row(lru_lambda[0]), wlo=mat(w_lru_out[0]),
        bg=row(b_gate[0]), wo=mat(w_o[0]), l1g=row(ln1_g[0]), l1b=row(ln1_b[0]),
        ee=_pack_rows_01(np.concatenate([expand, expand], axis=0)),
        e2e=_pack_rows_01(np.concatenate([group_sum, group_sum], axis=0)),
        wg=mat(ffn_w_gate[0]), wu=mat(ffn_w_up[0]),
        fcw=ffn_conv_w[0].astype(F32), fcb=row(ffn_conv_b[0]), wd=mat(ffn_w_down[0]),
        l2g=row(ln2_g[0]), l2b=row(ln2_b[0]),
    )


def kernel(x_prompt, x_sample, state_ssd, cache_ssd_conv, state_lru, cache_lru_conv, cache_ffn_conv, w_in, b_gate, ssd_conv_w, ssd_conv_b, ssd_dt_bias, ssd_a_log, ssd_d, ssd_norm_g, w_ssd_out, lru_conv_w, lru_conv_b, lru_wa, lru_ba, lru_wx, lru_bx, lru_lambda, w_lru_out, w_o, ln1_g, ln1_b, ffn_w_gate, ffn_w_up, ffn_conv_w, ffn_conv_b, ffn_w_down, ln2_g, ln2_b):
    assert w_in.shape[0] == 1, "single-layer trunk"
    _, _, n_heads, headdim, d_state = state_ssd.shape
    wts = _prep_weights(w_in, b_gate, ssd_conv_w, ssd_conv_b, ssd_dt_bias, ssd_a_log, ssd_d, ssd_norm_g,
                        w_ssd_out, lru_conv_w, lru_conv_b, lru_wa, lru_ba, lru_wx, lru_bx, lru_lambda,
                        w_lru_out, w_o, ln1_g, ln1_b, ffn_w_gate, ffn_w_up, ffn_conv_w, ffn_conv_b,
                        ffn_w_down, ln2_g, ln2_b, n_heads, headdim, d_state)
    bp = x_prompt.shape[0]
    d_ff = ffn_w_gate.shape[2]

    nb_s, steps, _ = x_sample.shape
    half = nb_s // 2

    def to_lm(a):
        k, c = a.shape[1], a.shape[2]
        return a.reshape(half, 2, k, c).transpose(2, 1, 0, 3).reshape(k * nb_s, c)

    def from_lm(a, k):
        c = a.shape[1]
        return a.reshape(k, 2, half, c).transpose(2, 1, 0, 3).reshape(nb_s, k, c)

    def to_pairs(a):
        return a.reshape(steps, 2, half, a.shape[1]).transpose(2, 0, 1, 3).reshape(half, 2 * steps, a.shape[1])

    def from_pairs(a):
        return a.reshape(half, steps, 2, a.shape[2]).transpose(1, 2, 0, 3).reshape(steps * nb_s, a.shape[2])

    d_ssd = n_heads * headdim
    x_lm = to_lm(x_sample)
    sf = _sample_front(x_lm, to_lm(cache_ssd_conv[0]), to_lm(cache_lru_conv[0]),
                       to_lm(state_lru[0][:, None, :]), wts, nb_s, steps, PAST_LEN)
    pre, lst, prelx = sf['pre'], sf['lst'], sf['prelx']

    x1_p, p_ssd, p_ssd_buf, p_lru, p_lru_buf = _prompt_mixer(x_prompt, wts, tl=MIXER_TILE)
    y_prompt, p_ffn_buf, new_state, yoff8 = _ffn_seg(
        x1_p, jnp.zeros((bp, ffn_conv_w.shape[1] - 1, d_ff), F32),
        state_ssd[0].reshape(half, 2, d_ssd, d_state), to_pairs(sf['c']), to_pairs(sf['b']),
        to_pairs(sf['xw']), to_pairs(sf['ea']), wts, tm=FFN_TILE, steps=steps)
    p_ssd = p_ssd.reshape(1, bp, n_heads, headdim, d_state)

    x1_lm = _sample_back(x_lm, sf['ydg'], from_pairs(yoff8), sf['ylru'], wts, nb_s, steps)
    y_lm, tail = _ffn(x1_lm[None], to_lm(cache_ffn_conv[0])[None], wts, tm=steps * nb_s, stride=nb_s)
    k_ssd = ssd_conv_w.shape[1] - 1
    k_lru = lru_conv_w.shape[1] - 1
    k_ffn = ffn_conv_w.shape[1] - 1
    assert steps >= max(k_ssd, k_lru, k_ffn)
    return (y_prompt, from_lm(y_lm[0], steps), p_ssd, p_ssd_buf[None], p_lru.reshape(1, bp, -1), p_lru_buf[None],
            p_ffn_buf[None],
            new_state.reshape(1, nb_s, n_heads, headdim, d_state),
            from_lm(pre[(steps - k_ssd) * nb_s:], k_ssd)[None],
            from_lm(lst, 1).reshape(1, nb_s, -1),
            from_lm(prelx[(steps - k_lru) * nb_s:], k_lru)[None],
            from_lm(tail[0], k_ffn)[None])
```

```python
import functools

import numpy as np
import jax
import jax.numpy as jnp
from jax import lax
from jax.experimental import pallas as pl
from jax.experimental.pallas import tpu as pltpu

F32 = jnp.float32
BF16 = jnp.bfloat16

SSD_GROUPS = 4
SSD_CHUNK = 128
LRU_BLOCKS = 8
LRU_C = 8.0
LN_EPS = 1e-5
RMS_EPS = 1e-6
PAST_LEN = 16384

LANES = 128
SUBLANES = 8
VMEM_LIMIT_BYTES = 60 * 1024 * 1024

ROW_BLK = 32
MIXER_TILE = 2 * SSD_CHUNK
FFN_TILE = 4 * SSD_CHUNK


def _dot(a, b):
    if b.dtype == jnp.uint32:
        b = pltpu.bitcast(b, BF16)
    return jnp.dot(a, b, preferred_element_type=F32)


def _pack_kernel(w_ref, *out_refs, ranges):
    for o_ref, (a, b, pad, scale) in zip(out_refs, ranges):
        v = w_ref[:, a:b]
        if scale != 1.0:
            v = v * scale
        if pad:
            v = jnp.concatenate([v, jnp.zeros((v.shape[0], pad), v.dtype)], axis=1)
        o_ref[...] = pltpu.bitcast(v.astype(BF16), jnp.uint32)


def _pack_weight(w, ranges=None):
    k, n = w.shape
    single = ranges is None
    ranges = ((0, n, 0, 1.0),) if single else tuple(ranges)
    bk = 128 if n > 4096 else 256
    widths = [b - a + pad for a, b, pad, _ in ranges]
    outs = pl.pallas_call(
        functools.partial(_pack_kernel, ranges=ranges),
        grid=(k // bk,),
        in_specs=[pl.BlockSpec((bk, n), lambda i: (i, 0))],
        out_specs=tuple(pl.BlockSpec((bk // 2, wd), lambda i: (i, 0)) for wd in widths),
        out_shape=tuple(jax.ShapeDtypeStruct((k // 2, wd), jnp.uint32) for wd in widths),
        compiler_params=pltpu.CompilerParams(dimension_semantics=("arbitrary",),
                                             vmem_limit_bytes=VMEM_LIMIT_BYTES),
        name="pack_weight",
    )(w.astype(F32))
    return outs[0] if single else outs


def _pack_t_kernel(wt_ref, wdt_ref, *out_refs, parts, bn, dt_rows):
    j = pl.program_id(0)
    first = 0
    for o_ref, (nblk, scale) in zip(out_refs, parts):
        @pl.when(jnp.logical_and(j >= first, j < first + nblk))
        def _(o_ref=o_ref, scale=scale):
            v = wt_ref[...].T
            if scale != 1.0:
                v = v * scale
            o_ref[...] = pltpu.bitcast(v.astype(BF16), jnp.uint32)
        first += nblk

    @pl.when(j == 0)
    def _():
        rows = lax.broadcasted_iota(jnp.int32, wdt_ref.shape, 0)
        v = jnp.where(rows < dt_rows, wdt_ref[...], 0.0)
        out_refs[-1][...] = pltpu.bitcast(v.T.astype(BF16), jnp.uint32)


def _pack_w_in_t(wt, cuts, dt_index, scales, bn=512):
    n, k = wt.shape
    parts, offs = [], []
    for i in range(len(cuts) - 1):
        if i == dt_index:
            continue
        nblk = (cuts[i + 1] - cuts[i]) // bn
        assert nblk * bn == cuts[i + 1] - cuts[i]
        parts.append((nblk, scales[i]))
        offs += [cuts[i] + b * bn for b in range(nblk)]
    offs = np.asarray(offs, np.int32)
    firsts = np.cumsum([0] + [p[0] for p in parts])

    def row_off(j):
        off = jnp.int32(int(offs[0]))
        for idx in range(1, len(offs)):
            off = jnp.where(j >= idx, jnp.int32(int(offs[idx])), off)
        return off

    def out_map(p):
        return lambda j: (0, jnp.clip(j - int(firsts[p]), 0, parts[p][0] - 1))

    dt_rows = cuts[dt_index + 1] - cuts[dt_index]
    outs = pl.pallas_call(
        functools.partial(_pack_t_kernel, parts=tuple(parts), bn=bn, dt_rows=dt_rows),
        grid=(len(offs),),
        in_specs=[pl.BlockSpec((pl.Element(bn), pl.Element(k)), lambda j: (pl.multiple_of(row_off(j), SUBLANES), 0)),
                  pl.BlockSpec((pl.Element(LANES), pl.Element(k)), lambda j: (cuts[dt_index], 0))],
        out_specs=tuple(pl.BlockSpec((k // 2, bn), out_map(p)) for p in range(len(parts)))
        + (pl.BlockSpec((k // 2, LANES), lambda j: (0, 0)),),
        out_shape=tuple(jax.ShapeDtypeStruct((k // 2, nb_ * bn), jnp.uint32) for nb_, _ in parts)
        + (jax.ShapeDtypeStruct((k // 2, LANES), jnp.uint32),),
        compiler_params=pltpu.CompilerParams(dimension_semantics=("arbitrary",),
                                             vmem_limit_bytes=VMEM_LIMIT_BYTES),
        name="pack_w_in",
    )(wt, wt)
    outs = list(outs)
    dt_part = outs.pop()
    outs.insert(dt_index, dt_part)
    return outs


def _pack_rows_01(m):
    bits = np.ascontiguousarray(m, np.float32).view(np.uint32) >> 16
    return jnp.asarray(bits[0::2] | (bits[1::2] << 16), jnp.uint32)


def _dot_nt(a, b):
    return lax.dot_general(a, b, (((1,), (1,)), ((), ())), preferred_element_type=F32)


def _dot_tn(a, b):
    return lax.dot_general(a, b, (((0,), (0,)), ((), ())), preferred_element_type=F32)


def _split3(v):
    hi = v.astype(BF16)
    r1 = v - hi.astype(F32)
    mid = r1.astype(BF16)
    lo = (r1 - mid.astype(F32)).astype(BF16)
    return hi, mid, lo


def _dot_exact_lhs(m, v):
    hi, mid, lo = _split3(v)
    return _dot(m, hi) + _dot(m, mid) + _dot(m, lo)


def _split2(v):
    hi = v.astype(BF16)
    lo = (v - hi.astype(F32)).astype(BF16)
    return hi, lo


def _dot_2way_rhs(v, m2):
    return _dot(jnp.concatenate(_split2(v), axis=1), m2)


def _dot_2way_lhs(m2, v):
    return _dot(m2, jnp.concatenate(_split2(v), axis=0))


def _softplus(x):
    return jnp.maximum(x, 0.0) + jnp.log1p(jnp.exp(-jnp.abs(x)))


def _sigmoid(x):
    return 0.5 * jnp.tanh(0.5 * x) + 0.5


def _silu_of_half(h):
    return h + h * jnp.tanh(h)


def _gelu(x):
    c = np.sqrt(2.0 / np.pi).astype(np.float32)
    return 0.5 * x * (1.0 + jnp.tanh(c * (x + 0.044715 * (x * x * x))))


def _layer_norm(v, g, b):
    mu = jnp.mean(v, axis=-1, keepdims=True)
    d = v - mu
    var = jnp.mean(d * d, axis=-1, keepdims=True)
    return d * lax.rsqrt(var + LN_EPS) * g + b


def _rows(i, n):
    if isinstance(i, int):
        return slice(i * n, (i + 1) * n)
    return pl.ds(pl.multiple_of(i * n, n), n)


def _conv_block(buf_ref, w_ref, b_ref, hist, stride, r0, rows, c0, cw):
    taps = w_ref.shape[0]
    acc = b_ref[:, c0:c0 + cw]
    for k in range(taps):
        off = hist + r0 - (taps - 1 - k) * stride
        if not isinstance(off, int):
            off = pl.multiple_of(off, SUBLANES)
        acc = acc + w_ref[k:k + 1, c0:c0 + cw] * buf_ref[pl.ds(off, rows), c0:c0 + cw]
    return acc


LOG2_SUBLANES = 3
NPOS = SSD_CHUNK // SUBLANES
LOG2_NPOS = 4
assert 1 << LOG2_SUBLANES == SUBLANES and 1 << LOG2_NPOS == NPOS


def _tok_of_row(r):
    return (r & (SUBLANES - 1)) * NPOS + lax.shift_right_logical(r, LOG2_SUBLANES)


def _row_of_tok(t):
    return (t & (NPOS - 1)) * SUBLANES + lax.shift_right_logical(t, LOG2_NPOS)


def _perm_matrix():
    q = SSD_CHUNK
    r = lax.broadcasted_iota(jnp.int32, (q, q), 0)
    c = lax.broadcasted_iota(jnp.int32, (q, q), 1)
    return jnp.where(c == _tok_of_row(r), 1.0, 0.0).astype(BF16)


def _unperm_matrix():
    q = SSD_CHUNK
    t = lax.broadcasted_iota(jnp.int32, (q, q), 0)
    r = lax.broadcasted_iota(jnp.int32, (q, q), 1)
    return jnp.where(r == _row_of_tok(t), 1.0, 0.0).astype(BF16)


def _fill_wrap(buf_ref, wrap_ref, c, wrap, tail_ref, cols=slice(None)):
    q = SSD_CHUNK
    for m in range(wrap // SUBLANES):
        r_cur = (c + 1) * q - wrap + m * SUBLANES
        cur = buf_ref[r_cur:r_cur + SUBLANES, cols]
        if c == 0:
            prv = tail_ref[m * SUBLANES:(m + 1) * SUBLANES, cols]
        else:
            prv = buf_ref[r_cur - q:r_cur - q + SUBLANES, cols]
        sub0 = lax.broadcasted_iota(jnp.int32, cur.shape, 0) == 0
        wrap_ref[c * wrap + m * SUBLANES:c * wrap + (m + 1) * SUBLANES, cols] = jnp.where(
            sub0, pltpu.roll(prv, 1, 0), pltpu.roll(cur, 1, 0))


def _conv_seg(buf_ref, wrap_ref, w_ref, b_ref, c, wrap, c0, cw):
    q = SSD_CHUNK
    taps = w_ref.shape[0]
    cols = slice(c0, c0 + cw)
    acc = b_ref[:, cols] + w_ref[taps - 1:taps, cols] * buf_ref[c * q:(c + 1) * q, cols]
    for k in range(taps - 1):
        back = (taps - 1 - k) * SUBLANES
        shifted = jnp.concatenate(
            [wrap_ref[(c + 1) * wrap - back:(c + 1) * wrap, cols], buf_ref[c * q:(c + 1) * q - back, cols]],
            axis=0)
        acc = acc + w_ref[k:k + 1, cols] * shifted
    return acc


def _seg_tail_rows(wrap):
    n = wrap // SUBLANES
    return [(NPOS - n + m) * SUBLANES + SUBLANES - 1 for m in range(n)]


def _prompt_mixer_kernel(
        x_ref, wz_ref, wxbc_ref, wdt_ref, wlx_ref, wly_ref, wgs_ref, wgl_ref,
        scw_ref, scb_ref, dtb_ref, aneg_ref, dexp_ref, ng_ref, wso_ref,
        lcw_ref, lcb_ref, wax_ref, ba_ref, bx_ref, lam_ref, wlo_ref,
        bg_ref, wo_ref, l1g_ref, l1b_ref, e_ref,
        x1_ref, st_ref, sconv_ref, lst_ref, lconv_ref,
        xb_s, xp_s, xbc_s, swrap_s, stail_s, xc_s, lx_s, lwrap_s, ltail_s, xr_s, ly_s, y_s, z_s, ysb_s, ylb_s,
        ht_s, hl_s, gs_s, gl_s, ys_s, yl_s, o_s, mb_s,
        *, tl, alpha, n_heads, headdim, d_state):
    t = pl.program_id(1)
    nt = pl.num_programs(1)
    d_ssd = n_heads * headdim
    gn = SSD_GROUPS * d_state
    hpg = n_heads // SSD_GROUPS
    gw = hpg * headdim
    d_lru = lx_s.shape[1]
    d_model = x_ref.shape[2]
    q = SSD_CHUNK

    nch = tl // q
    wrap_s = (scw_ref.shape[0] - 1) * SUBLANES
    wrap_l = (lcw_ref.shape[0] - 1) * SUBLANES

    @pl.when(t == 0)
    def _():
        stail_s[...] = jnp.zeros(stail_s.shape, F32)
        ltail_s[...] = jnp.zeros(ltail_s.shape, F32)
        ht_s[...] = jnp.zeros(ht_s.shape, F32)
        hl_s[...] = jnp.zeros(hl_s.shape, F32)

    perm = _perm_matrix()
    perm2 = jnp.concatenate([perm, perm], axis=1)
    for c in range(nch):
        rows = _rows(c, q)
        xp = _dot_2way_lhs(perm2, x_ref[0, rows, :])
        xp_s[rows, :] = xp
        xb_s[rows, :] = xp.astype(BF16)

    cwid = 512

    def proj(dst_ref, w_ref, c0, cw):
        dst_ref[:, c0:c0 + cw] = _dot(xb_s[...], w_ref[:, c0:c0 + cw])

    def lru_conv(c):
        _fill_wrap(lx_s, lwrap_s, c, wrap_l, ltail_s)
        for c0 in range(0, d_lru, cwid):
            xr_s[c * q:(c + 1) * q, c0:c0 + cwid] = _conv_seg(lx_s, lwrap_s, lcw_ref, lcb_ref, c, wrap_l, c0, cwid)
        if c == nch - 1:
            ltail_s[...] = lx_s[tl - wrap_l:tl, :]

    bw = d_lru // LRU_BLOCKS
    sub = lax.broadcasted_iota(jnp.int32, (SUBLANES, bw), 0)
    crow = lax.broadcasted_iota(jnp.int32, (q, bw), 0)

    def lru_block(n):
        cols = slice(n * bw, (n + 1) * bw)
        xr = xr_s[:, cols]
        xrb = xr.astype(BF16)
        rg = _dot(xrb, wax_ref[n])
        r = _sigmoid(rg[:, 0:bw] + ba_ref[:, cols])
        gi = _sigmoid(rg[:, bw:2 * bw] + bx_ref[:, cols])
        log_a = (-LRU_C) * r * _softplus(-lam_ref[:, cols])
        a_all = jnp.exp(log_a)
        mult_all = jnp.sqrt(1.0 - a_all * a_all)
        for c in range(nch):
            a = a_all[c * q:(c + 1) * q, :]
            mult = mult_all[c * q:(c + 1) * q, :]
            if c == 0:
                mult = jnp.where(jnp.logical_and(crow == 0, t == 0), 1.0, mult)
            u = mult * gi[c * q:(c + 1) * q, :] * xr[c * q:(c + 1) * q, :]
            a_p = [a[i * SUBLANES:(i + 1) * SUBLANES, :] for i in range(NPOS)]
            u_p = [u[i * SUBLANES:(i + 1) * SUBLANES, :] for i in range(NPOS)]
            h = u_p[0]
            g = a_p[0]
            for i in range(1, NPOS):
                h = a_p[i] * h + u_p[i]
                g = a_p[i] * g
            gs = jnp.where(sub == 0, 0.0, pltpu.roll(g, 1, 0))
            hs = jnp.where(sub == 0, hl_s[0:1, cols], pltpu.roll(h, 1, 0))
            d = 1
            while d < SUBLANES:
                keep = sub >= d
                hs = jnp.where(keep, gs * pltpu.roll(hs, d, 0) + hs, hs)
                gs = jnp.where(keep, gs * pltpu.roll(gs, d, 0), gs)
                d *= 2
            h = hs
            out = []
            for i in range(NPOS):
                h = a_p[i] * h + u_p[i]
                out.append(h)
            hl_s[0:1, cols] = h[SUBLANES - 1:SUBLANES, :]
            hseq = jnp.concatenate(out, axis=0)
            ylb_s[c * q:(c + 1) * q, cols] = (hseq * _gelu(ly_s[c * q:(c + 1) * q, cols])).astype(BF16)

    def ssd_conv(c0):
        cols = slice(c0, c0 + cwid)
        for c in range(nch):
            _fill_wrap(xbc_s, swrap_s, c, wrap_s, stail_s, cols)
        stail_s[:, cols] = xbc_s[tl - wrap_s:tl, cols]
        for c in range(nch):
            xc_s[c * q:(c + 1) * q, cols] = _silu_of_half(
                _conv_seg(xbc_s, swrap_s, scw_ref, scb_ref, c, wrap_s, c0, cwid))

    def merge_gate(dst_ref, c0, cw, b0):
        dst_ref[:, c0:c0 + cw] = _sigmoid(dst_ref[:, c0:c0 + cw] + bg_ref[:, b0 + c0:b0 + c0 + cw])

    tok_r = _tok_of_row(lax.broadcasted_iota(jnp.int32, (q, q), 0))
    tok_c = _tok_of_row(lax.broadcasted_iota(jnp.int32, (q, q), 1))
    causal = tok_r >= tok_c
    tri = jnp.where(causal, 1.0, 0.0).astype(BF16)
    lane_i = lax.broadcasted_iota(jnp.int32, (q, LANES), 1)
    left = lane_i < headdim

    def chunk_body(c, carry):
        rows = _rows(c, q)
        dt = _softplus(_dot(xb_s[rows, :], wdt_ref[...]) + dtb_ref[...])
        d_a = dt * aneg_ref[...]
        a_cs = _dot_exact_lhs(tri, d_a)
        a_last = a_cs[q - 1:q, :]
        wgt = dt * jnp.exp(a_last - a_cs)
        ea = jnp.exp(a_cs)
        w_exp = _dot_2way_rhs(wgt, e_ref[...])
        ea_exp = _dot_2way_rhs(ea, e_ref[...])
        a_cs_t = a_cs.T
        dt_t = dt.T
        for g in range(SSD_GROUPS):
            b_g = xc_s[rows, d_ssd + g * d_state:d_ssd + (g + 1) * d_state]
            c_g = xc_s[rows, d_ssd + gn + g * d_state:d_ssd + gn + (g + 1) * d_state]
            b_gb = b_g.astype(BF16)
            c_gb = c_g.astype(BF16)
            cb = _dot_nt(c_gb, b_gb)
            for hp in range(hpg // 2):
                c0 = g * gw + hp * 2 * headdim
                xs_pair = xc_s[rows, c0:c0 + 2 * headdim]
                lmats = []
                for j in range(2):
                    h = g * hpg + hp * 2 + j
                    seg = (jnp.broadcast_to(a_cs[:, h:h + 1], (q, q))
                           - jnp.broadcast_to(a_cs_t[h:h + 1, :], (q, q)))
                    dec = jnp.exp(jnp.where(causal, seg, -jnp.inf))
                    lmats.append((cb * dec * jnp.broadcast_to(dt_t[h:h + 1, :], (q, q))).astype(BF16))
                lpair = jnp.concatenate(lmats, axis=1)
                rhs = jnp.concatenate([jnp.where(left, xs_pair, 0.0),
                                       jnp.where(left, 0.0, xs_pair)], axis=0).astype(BF16)
                y_s[rows, c0:c0 + 2 * headdim] = _dot(lpair, rhs)
            gcols = slice(g * gw, (g + 1) * gw)
            h_g = ht_s[:, gcols]
            y_off = _dot(c_gb, h_g.astype(BF16)) * ea_exp[:, gcols]
            y_s[rows, gcols] = y_s[rows, gcols] + y_off
            xw = (xc_s[rows, gcols] * w_exp[:, gcols]).astype(BF16)
            ht_s[:, gcols] = h_g * ea_exp[q - 1:q, gcols] + _dot_tn(b_gb, xw)
            if carry is not None and g < len(carry):
                carry[g]()
        return carry

    def gate_body(i, carry):
        rows = _rows(i, ROW_BLK)
        y = y_s[rows, :] + dexp_ref[...] * xc_s[rows, 0:d_ssd]
        y = y * _silu_of_half(z_s[rows, :])
        ms = jnp.mean(y * y, axis=-1, keepdims=True)
        ysb_s[rows, :] = (y * lax.rsqrt(ms + RMS_EPS) * ng_ref[...]).astype(BF16)
        return carry

    assert nch == 2 and LRU_BLOCKS == 8 and d_lru == 2 * cwid and d_model == 2 * cwid
    d_xbc = xc_s.shape[1]
    nxb = d_xbc // cwid
    proj(lx_s, wlx_ref, 0, cwid)
    proj(lx_s, wlx_ref, cwid, cwid)
    proj(ly_s, wly_ref, 0, cwid); lru_conv(0)
    proj(ly_s, wly_ref, cwid, cwid); lru_conv(1)
    P = functools.partial
    mxu_a = ([P(proj, xbc_s, wxbc_ref, j * cwid, cwid) for j in range(nxb)]
             + [P(proj, gs_s, wgs_ref, j * cwid, cwid) for j in range(2)])
    for n in range(LRU_BLOCKS):
        mxu_a[n]()
        lru_block(n)
        if 1 <= n <= nxb:
            ssd_conv((n - 1) * cwid)
    gates_per_chunk = q // ROW_BLK
    nz = d_ssd // cwid
    chunk_body(0, [P(proj, z_s, wz_ref, j * cwid, cwid) for j in range(nz)])
    merge_gate(gs_s, 0, d_model, 0)
    for i in range(gates_per_chunk):
        gate_body(i, None)

    def wlo_piece(j):
        yl_s[:, j * cwid:(j + 1) * cwid] = _dot(ylb_s[...], wlo_ref[:, j * cwid:(j + 1) * cwid])

    chunk_body(1, [P(proj, gl_s, wgl_ref, 0, cwid), P(proj, gl_s, wgl_ref, cwid, cwid),
                   P(wlo_piece, 0), P(wlo_piece, 1)])
    merge_gate(gl_s, 0, d_model, d_model)
    for i in range(gates_per_chunk, 2 * gates_per_chunk):
        gate_body(i, None)

    ys_s[...] = _dot(ysb_s[...], wso_ref[...])
    for i in range(tl // ROW_BLK):
        rows = _rows(i, ROW_BLK)
        mb_s[rows, :] = (gs_s[rows, :] * ys_s[rows, :] + gl_s[rows, :] * yl_s[rows, :]).astype(BF16)
    o_s[...] = _dot(mb_s[...], wo_ref[...])
    for i in range(tl // ROW_BLK):
        rows = _rows(i, ROW_BLK)
        v = alpha * xp_s[rows, :] + o_s[rows, :]
        x1_ref[0, rows, :] = _layer_norm(v, l1g_ref[...], l1b_ref[...])

    @pl.when(t == nt - 1)
    def _():
        st_ref[0] = ht_s[...].T
        lst_ref[0] = hl_s[0:1, :]
        for m, r in enumerate(_seg_tail_rows(wrap_s)):
            sconv_ref[0, m:m + 1, :] = xbc_s[tl - q + r:tl - q + r + 1, :]
        for m, r in enumerate(_seg_tail_rows(wrap_l)):
            lconv_ref[0, m:m + 1, :] = lx_s[tl - q + r:tl - q + r + 1, :]


def _const_spec(shape):
    nd = len(shape)
    return pl.BlockSpec(shape, lambda *_: (0,) * nd, pipeline_mode=pl.Buffered(1))


def _prompt_mixer(x, wts, tl):
    nb, seq, d_model = x.shape
    n_heads, headdim, d_state = wts['n_heads'], wts['headdim'], wts['d_state']
    d_ssd = n_heads * headdim
    d_xbc = d_ssd + 2 * SSD_GROUPS * d_state
    d_lru = wts['wlx'].shape[1]
    names = ['wz', 'wxbc', 'wdt', 'wlx', 'wly', 'wgs', 'wgl', 'scw', 'scb', 'dtb', 'aneg', 'dexp', 'ng',
             'wso', 'lcw', 'lcb', 'wax', 'ba', 'bx', 'lam', 'wlo', 'bg', 'wo', 'l1g', 'l1b', 'ee']
    consts = [wts[k] for k in names]
    kern = functools.partial(_prompt_mixer_kernel, tl=tl, alpha=wts['alpha'], n_heads=n_heads,
                             headdim=headdim, d_state=d_state)
    out_shape = (
        jax.ShapeDtypeStruct((nb, seq, d_model), F32),
        jax.ShapeDtypeStruct((nb, d_ssd, d_state), F32),
        jax.ShapeDtypeStruct((nb, 3, d_xbc), F32),
        jax.ShapeDtypeStruct((nb, 1, d_lru), F32),
        jax.ShapeDtypeStruct((nb, 3, d_lru), F32),
    )
    out_specs = (
        pl.BlockSpec((1, tl, d_model), lambda b, t: (b, t, 0)),
        pl.BlockSpec((1, d_ssd, d_state), lambda b, t: (b, 0, 0)),
        pl.BlockSpec((1, 3, d_xbc), lambda b, t: (b, 0, 0)),
        pl.BlockSpec((1, 1, d_lru), lambda b, t: (b, 0, 0)),
        pl.BlockSpec((1, 3, d_lru), lambda b, t: (b, 0, 0)),
    )
    nch = tl // SSD_CHUNK
    wrap_s = (wts['scw'].shape[0] - 1) * SUBLANES
    wrap_l = (wts['lcw'].shape[0] - 1) * SUBLANES
    scratch = [
        pltpu.VMEM((tl, d_model), BF16),
        pltpu.VMEM((tl, d_model), F32),
        pltpu.VMEM((tl, d_xbc), F32),
        pltpu.VMEM((nch * wrap_s, d_xbc), F32),
        pltpu.VMEM((wrap_s, d_xbc), F32),
        pltpu.VMEM((tl, d_xbc), F32),
        pltpu.VMEM((tl, d_lru), F32),
        pltpu.VMEM((nch * wrap_l, d_lru), F32),
        pltpu.VMEM((wrap_l, d_lru), F32),
        pltpu.VMEM((tl, d_lru), F32),
        pltpu.VMEM((tl, d_lru), F32),
        pltpu.VMEM((tl, d_ssd), F32),
        pltpu.VMEM((tl, d_ssd), F32),
        pltpu.VMEM((tl, d_ssd), BF16),
        pltpu.VMEM((tl, d_lru), BF16),
        pltpu.VMEM((d_state, d_ssd), F32),
        pltpu.VMEM((SUBLANES, d_lru), F32),
        pltpu.VMEM((tl, d_model), F32),
        pltpu.VMEM((tl, d_model), F32),
        pltpu.VMEM((tl, d_model), F32),
        pltpu.VMEM((tl, d_model), F32),
        pltpu.VMEM((tl, d_model), F32),
        pltpu.VMEM((tl, d_model), BF16),
    ]
    return pl.pallas_call(
        kern,
        grid=(nb, seq // tl),
        in_specs=[pl.BlockSpec((1, tl, d_model), lambda b, t: (b, t, 0))]
        + [_const_spec(c.shape) for c in consts],
        out_specs=out_specs,
        out_shape=out_shape,
        scratch_shapes=scratch,
        compiler_params=pltpu.CompilerParams(
            dimension_semantics=("arbitrary", "arbitrary"),
            vmem_limit_bytes=VMEM_LIMIT_BYTES),
        name="prompt_mixer",
    )(x, *consts)


def _ffn_kernel(x_ref, h0_ref, wg_ref, wu_ref, cw_ref, cb_ref, wd_ref, g_ref, b_ref,
                y_ref, tail_ref, xb_s, gb_s, hb_s, *, tm, stride, hist, alpha, fchunk):
    t = pl.program_id(1)
    nt = pl.num_programs(1)
    taps = cw_ref.shape[0]
    nh = (taps - 1) * stride
    d_ff = gb_s.shape[1]

    @pl.when(t == 0)
    def _():
        gb_s[hist - nh:hist, :] = h0_ref[0]

    xb_s[...] = x_ref[0].astype(BF16)
    acc = None
    for c0 in range(0, d_ff, fchunk):
        cols = slice(c0, c0 + fchunk)
        gb_s[hist:hist + tm, cols] = _dot(xb_s[...], wg_ref[:, cols])
        up = _dot(xb_s[...], wu_ref[:, cols])
        gc = _conv_block(gb_s, cw_ref, cb_ref, hist, stride, 0, tm, c0, fchunk)
        hb_s[...] = (_gelu(gc) * up).astype(BF16)
        part = _dot(hb_s[...], wd_ref[c0 // 2:(c0 + fchunk) // 2, :])
        acc = part if acc is None else acc + part
    v = alpha * x_ref[0] + acc
    y_ref[0] = _layer_norm(v, g_ref[...], b_ref[...])
    gb_s[hist - nh:hist, :] = gb_s[hist + tm - nh:hist + tm, :]

    @pl.when(t == nt - 1)
    def _():
        tail_ref[0] = gb_s[hist - nh:hist, :]


def _ffn_seg_kernel(x_ref, h0_ref, st_ref, cb8_ref, xe8_ref,
                    wg_ref, wu_ref, cw_ref, cb_ref, wd_ref, g_ref, b_ref,
                    y_ref, tail_ref, nst_ref, yoff8_ref,
                    xb_s, gb_s, gwrap_s, gtail_s, hb_s, *, tm, alpha, fchunk, state_dims):
    t = pl.program_id(1)
    state_pieces = [
        functools.partial(_state_pair_group, g, st_ref.at[p], cb8_ref.at[p], xe8_ref.at[p],
                          nst_ref.at[p], yoff8_ref.at[p], **state_dims)
        for p in range(st_ref.shape[0]) for g in range(SSD_GROUPS)]

    def state_work(n):
        for _ in range(min(n, len(state_pieces))):
            state_pieces.pop(0)()
    nt = pl.num_programs(1)
    q = SSD_CHUNK
    nch = tm // q
    wrap = (cw_ref.shape[0] - 1) * SUBLANES
    d_ff = gb_s.shape[1]
    tail_rows = _seg_tail_rows(wrap)

    @pl.when(t == 0)
    def _():
        gtail_s[...] = jnp.zeros(gtail_s.shape, F32)
        for m in range(len(tail_rows)):
            r = m * SUBLANES + SUBLANES - 1
            gtail_s[r:r + 1, :] = h0_ref[0, m:m + 1, :]

    xb_s[...] = x_ref[0].astype(BF16)
    def gate_proj(c0):
        cols = slice(c0, c0 + fchunk)
        gb_s[:, cols] = _dot(xb_s[...], wg_ref[:, cols])
        for c in range(nch):
            _fill_wrap(gb_s, gwrap_s, c, wrap, gtail_s, cols)
        gtail_s[:, cols] = gb_s[tm - wrap:tm, cols]

    per_slot = -(-len(state_pieces) // (1 + 2 * (d_ff // fchunk)))
    gate_proj(0)
    state_work(per_slot)
    acc = None
    for c0 in range(0, d_ff, fchunk):
        cols = slice(c0, c0 + fchunk)
        up = _dot(xb_s[...], wu_ref[:, cols])
        if c0 + fchunk < d_ff:
            gate_proj(c0 + fchunk)
        state_work(per_slot)
        for c in range(nch):
            gc = _conv_seg(gb_s, gwrap_s, cw_ref, cb_ref, c, wrap, c0, fchunk)
            hb_s[c * q:(c + 1) * q, :] = (_gelu(gc) * up[c * q:(c + 1) * q, :]).astype(BF16)
        part = _dot(hb_s[...], wd_ref[c0 // 2:(c0 + fchunk) // 2, :])
        state_work(per_slot)
        acc = part if acc is None else acc + part
    state_work(len(state_pieces))

    unperm = _unperm_matrix()
    unperm2 = jnp.concatenate([unperm, unperm], axis=1)
    for c in range(nch):
        rows = slice(c * q, (c + 1) * q)
        y = _layer_norm(alpha * x_ref[0, rows, :] + acc[rows, :], g_ref[...], b_ref[...])
        y_ref[0, rows, :] = _dot_2way_lhs(unperm2, y)

    @pl.when(t == nt - 1)
    def _():
        for m, r in enumerate(tail_rows):
            tail_ref[0, m:m + 1, :] = gb_s[tm - q + r:tm - q + r + 1, :]


def _ffn_seg(x, hist0, state, cb8, xe8, wts, tm, steps):
    nb, seq, d_model = x.shape
    d_ff = wts['wg'].shape[1]
    taps = wts['fcw'].shape[0]
    wrap = (taps - 1) * SUBLANES
    fchunk = 1024
    nt = seq // tm
    npairs = state.shape[0]
    pps = npairs // (nb * nt)
    assert pps * nb * nt == npairs
    consts = [wts[k] for k in ['wg', 'wu', 'fcw', 'fcb', 'wd', 'l2g', 'l2b']]
    state_dims = dict(steps=steps, n_heads=wts['n_heads'], headdim=wts['headdim'], d_state=wts['d_state'])
    kern = functools.partial(_ffn_seg_kernel, tm=tm, alpha=wts['alpha'], fchunk=fchunk, state_dims=state_dims)
    pair_blk = lambda a: pl.BlockSpec((pps,) + tuple(a.shape[1:]),
                                      lambda b, t: (b * nt + t,) + (0,) * (len(a.shape) - 1))
    yoff8 = jax.ShapeDtypeStruct((npairs, 2 * steps, xe8.shape[2] // 2), F32)
    return pl.pallas_call(
        kern,
        grid=(nb, nt),
        in_specs=[pl.BlockSpec((1, tm, d_model), lambda b, t: (b, t, 0)),
                  pl.BlockSpec((1, taps - 1, d_ff), lambda b, t: (b, 0, 0)),
                  pair_blk(state), pair_blk(cb8), pair_blk(xe8)]
        + [_const_spec(c.shape) for c in consts],
        out_specs=(pl.BlockSpec((1, tm, d_model), lambda b, t: (b, t, 0)),
                   pl.BlockSpec((1, taps - 1, d_ff), lambda b, t: (b, 0, 0)),
                   pair_blk(state), pair_blk(yoff8)),
        out_shape=(jax.ShapeDtypeStruct((nb, seq, d_model), F32),
                   jax.ShapeDtypeStruct((nb, taps - 1, d_ff), F32),
                   jax.ShapeDtypeStruct(state.shape, F32), yoff8),
        scratch_shapes=[pltpu.VMEM((tm, d_model), BF16),
                        pltpu.VMEM((tm, d_ff), F32),
                        pltpu.VMEM((tm // SSD_CHUNK * wrap, d_ff), F32),
                        pltpu.VMEM((wrap, d_ff), F32),
                        pltpu.VMEM((tm, fchunk), BF16)],
        compiler_params=pltpu.CompilerParams(
            dimension_semantics=("arbitrary", "arbitrary"),
            vmem_limit_bytes=VMEM_LIMIT_BYTES),
        name="conv_ffn_seg",
    )(x, hist0, state, cb8, xe8, *consts)


def _ffn(x, hist0, wts, tm, stride):
    nb, seq, d_model = x.shape
    d_ff = wts['wg'].shape[1]
    taps = wts['fcw'].shape[0]
    nh = (taps - 1) * stride
    hist = -(-nh // SUBLANES) * SUBLANES
    consts = [wts[k] for k in ['wg', 'wu', 'fcw', 'fcb', 'wd', 'l2g', 'l2b']]
    kern = functools.partial(_ffn_kernel, tm=tm, stride=stride, hist=hist, alpha=wts['alpha'], fchunk=1024)
    return pl.pallas_call(
        kern,
        grid=(nb, seq // tm),
        in_specs=[pl.BlockSpec((1, tm, d_model), lambda b, t: (b, t, 0)),
                  pl.BlockSpec((1, nh, d_ff), lambda b, t: (b, 0, 0))]
        + [_const_spec(c.shape) for c in consts],
        out_specs=(pl.BlockSpec((1, tm, d_model), lambda b, t: (b, t, 0)),
                   pl.BlockSpec((1, nh, d_ff), lambda b, t: (b, 0, 0))),
        out_shape=(jax.ShapeDtypeStruct((nb, seq, d_model), F32),
                   jax.ShapeDtypeStruct((nb, nh, d_ff), F32)),
        scratch_shapes=[pltpu.VMEM((tm, d_model), BF16),
                        pltpu.VMEM((hist + tm, d_ff), F32),
                        pltpu.VMEM((tm, 1024), BF16)],
        compiler_params=pltpu.CompilerParams(
            dimension_semantics=("arbitrary", "arbitrary"),
            vmem_limit_bytes=VMEM_LIMIT_BYTES),
        name="conv_ffn",
    )(x, hist0, *consts)


def _sample_ssd_kernel(
        x_ref, xall_ref, cssd_ref, wxbc_ref, wdt_ref, scw_ref, scb_ref, dtb_ref, aneg_ref, dexp_ref,
        e_ref, e2_ref,
        pre_ref, cb_ref, xe_ref, ydg_ref,
        xbc_s, xs_s, bs_s, acs_s, dts_s,
        *, nseq, steps, n_heads, headdim, d_state):
    l = pl.program_id(0)
    d_ssd = n_heads * headdim
    gn = SSD_GROUPS * d_state
    hist = (scw_ref.shape[0] - 1) * nseq
    r0 = l * nseq

    def blk(i):
        return pl.ds(pl.multiple_of(i * nseq, nseq), nseq)

    def sblk(i):
        return slice(i * nseq, (i + 1) * nseq)

    @pl.when(l == 0)
    def _():
        xbc_s[0:hist, :] = cssd_ref[...]
        dts_s[...] = _softplus(_dot(xall_ref[...].astype(BF16), wdt_ref[...]) + dtb_ref[...])
        acc = jnp.zeros((nseq, LANES), F32)
        for s in range(steps):
            acc = acc + dts_s[sblk(s), :] * aneg_ref[...]
            acs_s[sblk(s), :] = acc

    xb = x_ref[...].astype(BF16)

    pre = _dot(xb, wxbc_ref[...])
    pre_ref[...] = pre
    xbc_s[pl.ds(pl.multiple_of(hist + r0, nseq), nseq), :] = pre
    cwid = 512
    for c0 in range(0, d_ssd, cwid):
        xs_s[blk(l), c0:c0 + cwid] = _silu_of_half(
            _conv_block(xbc_s, scw_ref, scb_ref, hist, nseq, r0, nseq, c0, cwid))
    b_l = _silu_of_half(_conv_block(xbc_s, scw_ref, scb_ref, hist, nseq, r0, nseq, d_ssd, gn))
    c_l = _silu_of_half(_conv_block(xbc_s, scw_ref, scb_ref, hist, nseq, r0, nseq, d_ssd + gn, gn))
    bs_s[blk(l), :] = b_l
    cb_ref[:, 0:gn] = c_l
    cb_ref[:, gn:2 * gn] = b_l

    a_cs = acs_s[blk(l), :]
    dt = dts_s[blk(l), :]
    a_end = acs_s[sblk(steps - 1), :]
    xe_ref[:, d_ssd:2 * d_ssd] = _dot_2way_rhs(jnp.exp(a_cs), e_ref[...])
    xe_ref[:, 0:d_ssd] = xs_s[blk(l), :] * _dot_2way_rhs(dt * jnp.exp(a_end - a_cs), e_ref[...])

    ydg_ref[...] = dexp_ref[...] * xs_s[blk(l), :]
    for s in range(steps):
        @pl.when(s <= l)
        def _(s=s):
            coef = _dot_2way_rhs(jnp.exp(a_cs - acs_s[sblk(s), :]) * dts_s[sblk(s), :], e_ref[...])
            cbx = _dot_2way_rhs(bs_s[sblk(s), :] * c_l, e2_ref[...])
            ydg_ref[...] += cbx * coef * xs_s[sblk(s), :]


def _sample_lru_kernel(
        x_ref, clru_ref, slru_ref, wlx_ref, wly_ref, lcw_ref, lcb_ref, wax_ref, ba_ref, bx_ref,
        lam_ref, prelx_ref, ylru_ref, lst_ref, lx_s, hl_s, *, nseq, steps, start_pos):
    l = pl.program_id(0)
    hist = (lcw_ref.shape[0] - 1) * nseq
    d_lru = lx_s.shape[1]
    r0 = l * nseq

    @pl.when(l == 0)
    def _():
        lx_s[0:hist, :] = clru_ref[...]
        hl_s[...] = slru_ref[...]

    xb = x_ref[...].astype(BF16)
    prelx = _dot(xb, wlx_ref[...])
    prelx_ref[...] = prelx
    lx_s[pl.ds(pl.multiple_of(hist + r0, nseq), nseq), :] = prelx
    ly = _dot(xb, wly_ref[...])
    bw = d_lru // LRU_BLOCKS
    first = (l + start_pos) == 0
    for n in range(LRU_BLOCKS):
        cols = slice(n * bw, (n + 1) * bw)
        xr = _conv_block(lx_s, lcw_ref, lcb_ref, hist, nseq, r0, nseq, n * bw, bw)
        xrb = xr.astype(BF16)
        rg = _dot(xrb, wax_ref[n])
        r = _sigmoid(rg[:, 0:bw] + ba_ref[:, cols])
        gi = _sigmoid(rg[:, bw:2 * bw] + bx_ref[:, cols])
        log_a = (-LRU_C) * r * _softplus(-lam_ref[:, cols])
        a = jnp.exp(log_a)
        mult = jnp.where(first, 1.0, jnp.sqrt(1.0 - a * a))
        h = a * hl_s[:, cols] + mult * gi * xr
        hl_s[:, cols] = h
        ylru_ref[:, cols] = (h * _gelu(ly[:, cols])).astype(BF16)

    @pl.when(l == steps - 1)
    def _():
        lst_ref[...] = hl_s[...]


def _state_pair_group(g, st_ref, cb8_ref, xe8_ref, nst_ref, yoff8_ref,
                      *, steps, n_heads, headdim, d_state):
    nrow = 2 * steps
    hpg = n_heads // SSD_GROUPS
    gw = hpg * headdim
    gn = SSD_GROUPS * d_state
    d_ssd = n_heads * headdim
    assert 2 * headdim == LANES and d_state == LANES
    par = lax.broadcasted_iota(jnp.int32, (nrow, gw), 0) % 2
    low = lax.broadcasted_iota(jnp.int32, (nrow, LANES), 1) < headdim
    gcols = slice(g * gw, (g + 1) * gw)
    c8g = cb8_ref[:, g * d_state:(g + 1) * d_state].astype(BF16)
    b8g = cb8_ref[:, gn + g * d_state:gn + (g + 1) * d_state].astype(BF16)
    xw8 = xe8_ref[:, g * gw:(g + 1) * gw]
    ea8 = xe8_ref[:, d_ssd + g * gw:d_ssd + (g + 1) * gw]
    cds = []
    for hp in range(hpg // 2):
        pair = ea8[:, hp * LANES:(hp + 1) * LANES]
        swapped = pltpu.roll(pair, headdim, 1)
        cds.append(jnp.where(low, pair, swapped))
        cds.append(jnp.where(low, swapped, pair))
    yo = None
    for e in range(2):
        sg = st_ref[e, gcols, :]
        yo_e = _dot_nt(c8g, sg.astype(BF16))
        yo = yo_e if e == 0 else jnp.where(par == e, yo_e, yo)
        xw_e = jnp.where(par == e, xw8, 0.0).astype(BF16)
        upd = _dot_tn(xw_e, b8g)
        k_last = 2 * (steps - 1) + e
        for hh in range(hpg):
            cd = cds[hh][k_last:k_last + 1, :]
            hr = slice(hh * headdim, (hh + 1) * headdim)
            nst_ref[e, g * gw + hh * headdim:g * gw + (hh + 1) * headdim, :] = sg[hr, :] * cd + upd[hr, :]
    yoff8_ref[:, gcols] = yo * ea8


def _sample_post_kernel(x_ref, ydg_ref, yoff_ref, ylru_ref, wz_ref, ng_ref, wso_ref, wlo_ref,
                        wgs_ref, wgl_ref, bg_ref, wo_ref, l1g_ref, l1b_ref, x1_ref, *, alpha):
    d_model = x_ref.shape[1]
    xb = x_ref[...].astype(BF16)
    y = (ydg_ref[...] + yoff_ref[...]) * _silu_of_half(_dot(xb, wz_ref[...]))
    ms = jnp.mean(y * y, axis=-1, keepdims=True)
    ysb = (y * lax.rsqrt(ms + RMS_EPS) * ng_ref[...]).astype(BF16)
    g_ssd = _sigmoid(_dot(xb, wgs_ref[...]) + bg_ref[:, 0:d_model])
    g_lru = _sigmoid(_dot(xb, wgl_ref[...]) + bg_ref[:, d_model:2 * d_model])
    merged = g_ssd * _dot(ysb, wso_ref[...]) + g_lru * _dot(ylru_ref[...], wlo_ref[...])
    o = _dot(merged.astype(BF16), wo_ref[...])
    x1_ref[...] = _layer_norm(alpha * x_ref[...] + o, l1g_ref[...], l1b_ref[...])


def _sample_front(x_lm, cssd_lm, clru_lm, slru_lm, wts, nseq, steps, start_pos):
    n_heads, headdim, d_state = wts['n_heads'], wts['headdim'], wts['d_state']
    d_model = x_lm.shape[1]
    d_ssd = n_heads * headdim
    gn = SSD_GROUPS * d_state
    d_xbc = d_ssd + 2 * gn
    d_lru = wts['wlx'].shape[1]
    ntok = steps * nseq
    dims = dict(nseq=nseq, steps=steps, n_heads=n_heads, headdim=headdim, d_state=d_state)
    params = pltpu.CompilerParams(dimension_semantics=("arbitrary",), vmem_limit_bytes=VMEM_LIMIT_BYTES)
    step_blk = lambda w: pl.BlockSpec((nseq, w), lambda l: (l, 0))

    sds = jax.ShapeDtypeStruct
    ssd_names = ['wxbc', 'wdt', 'scw', 'scb', 'dtb', 'aneg', 'dexp', 'ee', 'e2e']
    ssd_consts = [x_lm, cssd_lm] + [wts[k] for k in ssd_names]
    pre, cb_lm, xe_lm, ydg_lm = pl.pallas_call(
        functools.partial(_sample_ssd_kernel, **dims),
        grid=(steps,),
        in_specs=[step_blk(d_model)] + [_const_spec(c.shape) for c in ssd_consts],
        out_specs=(step_blk(d_xbc), step_blk(2 * gn), step_blk(2 * d_ssd), step_blk(d_ssd)),
        out_shape=(sds((ntok, d_xbc), F32), sds((ntok, 2 * gn), F32), sds((ntok, 2 * d_ssd), F32),
                   sds((ntok, d_ssd), F32)),
        scratch_shapes=[
            pltpu.VMEM((cssd_lm.shape[0] + ntok, d_xbc), F32),
            pltpu.VMEM((ntok, d_ssd), F32),
            pltpu.VMEM((ntok, gn), F32),
            pltpu.VMEM((ntok, LANES), F32),
            pltpu.VMEM((ntok, LANES), F32),
        ],
        compiler_params=params,
        name="sample_ssd",
    )(x_lm, *ssd_consts)

    lru_names = ['wlx', 'wly', 'lcw', 'lcb', 'wax', 'ba', 'bx', 'lam']
    lru_consts = [clru_lm, slru_lm] + [wts[k] for k in lru_names]
    prelx, ylru_lm, lst = pl.pallas_call(
        functools.partial(_sample_lru_kernel, nseq=nseq, steps=steps, start_pos=start_pos),
        grid=(steps,),
        in_specs=[step_blk(d_model)] + [_const_spec(c.shape) for c in lru_consts],
        out_specs=(step_blk(d_lru), step_blk(d_lru), pl.BlockSpec((nseq, d_lru), lambda l: (0, 0))),
        out_shape=(sds((ntok, d_lru), F32), sds((ntok, d_lru), BF16), sds((nseq, d_lru), F32)),
        scratch_shapes=[
            pltpu.VMEM((clru_lm.shape[0] + ntok, d_lru), F32),
            pltpu.VMEM((nseq, d_lru), F32),
        ],
        compiler_params=params,
        name="sample_lru",
    )(x_lm, *lru_consts)

    return dict(pre=pre, cb=cb_lm, xe=xe_lm, ydg=ydg_lm, prelx=prelx, ylru=ylru_lm, lst=lst)


def _sample_back(x_lm, ydg_lm, yoff_lm, ylru_lm, wts, nseq, steps):
    d_model = x_lm.shape[1]
    d_ssd = wts['n_heads'] * wts['headdim']
    d_lru = wts['wlx'].shape[1]
    ntok = steps * nseq
    params = pltpu.CompilerParams(dimension_semantics=("arbitrary",), vmem_limit_bytes=VMEM_LIMIT_BYTES)
    step_blk = lambda w: pl.BlockSpec((nseq, w), lambda l: (l, 0))
    sds = jax.ShapeDtypeStruct
    post_names = ['wz', 'ng', 'wso', 'wlo', 'wgs', 'wgl', 'bg', 'wo', 'l1g', 'l1b']
    post_consts = [wts[k] for k in post_names]
    x1_lm = pl.pallas_call(
        functools.partial(_sample_post_kernel, alpha=wts['alpha']),
        grid=(steps,),
        in_specs=[step_blk(d_model), step_blk(d_ssd), step_blk(d_ssd), step_blk(d_lru)]
        + [_const_spec(c.shape) for c in post_consts],
        out_specs=step_blk(d_model),
        out_shape=sds((ntok, d_model), F32),
        compiler_params=params,
        name="sample_post",
    )(x_lm, ydg_lm, yoff_lm, ylru_lm, *post_consts)
    return x1_lm


def _prep_weights(w_in, b_gate, ssd_conv_w, ssd_conv_b, ssd_dt_bias, ssd_a_log, ssd_d, ssd_norm_g,
                  w_ssd_out, lru_conv_w, lru_conv_b, lru_wa, lru_ba, lru_wx, lru_bx, lru_lambda,
                  w_lru_out, w_o, ln1_g, ln1_b, ffn_w_gate, ffn_w_up, ffn_conv_w, ffn_conv_b,
                  ffn_w_down, ln2_g, ln2_b, n_heads, headdim, d_state):
    depth = w_in.shape[0]
    d_model = w_in.shape[1]
    d_ssd = n_heads * headdim
    d_xbc = d_ssd + 2 * SSD_GROUPS * d_state
    d_lru = lru_lambda.shape[1]
    sizes = (d_ssd, d_xbc, n_heads, d_lru, d_lru, d_model, d_model)
    cuts = np.cumsum((0,) + sizes)
    scales = [0.5 if i == 0 else 1.0 for i in range(len(sizes))]
    parts = _pack_w_in_t(jnp.swapaxes(w_in, 1, 2)[0].astype(F32), [int(c) for c in cuts], 2, scales)
    row = lambda v: v.reshape(1, -1).astype(F32)
    mat = _pack_weight
    wax = jnp.concatenate([lru_wa[0], lru_wx[0]], axis=-1)
    wax = _pack_weight(wax.reshape(-1, wax.shape[-1])).reshape(wax.shape[0], wax.shape[1] // 2, wax.shape[2])
    pad_heads = lambda v: jnp.pad(v.reshape(1, -1).astype(F32), ((0, 0), (0, LANES - n_heads)))
    head_of_col = np.arange(d_ssd) // headdim
    expand = (np.arange(LANES)[:, None] == head_of_col[None, :]).astype(np.float32)
    group_sum = (np.arange(SSD_GROUPS * d_state)[:, None] // d_state
                 == (head_of_col // (n_heads // SSD_GROUPS))[None, :])
    return dict(
        n_heads=n_heads, headdim=headdim, d_state=d_state,
        alpha=float((2.0 * depth) ** 0.25),
        wz=parts[0], wxbc=parts[1], wdt=parts[2], wlx=parts[3], wly=parts[4], wgs=parts[5], wgl=parts[6],
        scw=0.5 * ssd_conv_w[0].astype(F32), scb=0.5 * row(ssd_conv_b[0]),
        dtb=pad_heads(ssd_dt_bias[0]), aneg=pad_heads(-jnp.exp(ssd_a_log[0].astype(F32))),
        dexp=row(jnp.repeat(ssd_d[0], headdim)), ng=row(ssd_norm_g[0]),
        wso=mat(w_ssd_out[0]),
        lcw=lru_conv_w[0].astype(F32), lcb=row(lru_conv_b[0]),
        ba=row(lru_ba[0]), bx=row(lru_bx[0]), wax=wax,
        lam=row(lru_lambda[0]), wlo=mat(w_lru_out[0]),
        bg=row(b_gate[0]), wo=mat(w_o[0]), l1g=row(ln1_g[0]), l1b=row(ln1_b[0]),
        ee=_pack_rows_01(np.concatenate([expand, expand], axis=0)),
        e2e=_pack_rows_01(np.concatenate([group_sum, group_sum], axis=0)),
        wg=mat(ffn_w_gate[0]), wu=mat(ffn_w_up[0]),
        fcw=ffn_conv_w[0].astype(F32), fcb=row(ffn_conv_b[0]), wd=mat(ffn_w_down[0]),
        l2g=row(ln2_g[0]), l2b=row(ln2_b[0]),
    )


def kernel(x_prompt, x_sample, state_ssd, cache_ssd_conv, state_lru, cache_lru_conv, cache_ffn_conv, w_in, b_gate, ssd_conv_w, ssd_conv_b, ssd_dt_bias, ssd_a_log, ssd_d, ssd_norm_g, w_ssd_out, lru_conv_w, lru_conv_b, lru_wa, lru_ba, lru_wx, lru_bx, lru_lambda, w_lru_out, w_o, ln1_g, ln1_b, ffn_w_gate, ffn_w_up, ffn_conv_w, ffn_conv_b, ffn_w_down, ln2_g, ln2_b):
    assert w_in.shape[0] == 1, "single-layer trunk"
    _, _, n_heads, headdim, d_state = state_ssd.shape
    wts = _prep_weights(w_in, b_gate, ssd_conv_w, ssd_conv_b, ssd_dt_bias, ssd_a_log, ssd_d, ssd_norm_g,
                        w_ssd_out, lru_conv_w, lru_conv_b, lru_wa, lru_ba, lru_wx, lru_bx, lru_lambda,
                        w_lru_out, w_o, ln1_g, ln1_b, ffn_w_gate, ffn_w_up, ffn_conv_w, ffn_conv_b,
                        ffn_w_down, ln2_g, ln2_b, n_heads, headdim, d_state)
    bp = x_prompt.shape[0]
    d_ff = ffn_w_gate.shape[2]

    nb_s, steps, _ = x_sample.shape
    half = nb_s // 2

    def to_lm(a):
        return jnp.swapaxes(a, 0, 1).reshape(a.shape[1] * nb_s, a.shape[2])

    def from_lm(a, k):
        return jnp.swapaxes(a.reshape(k, nb_s, a.shape[1]), 0, 1)

    def to_pairs(a):
        return jnp.swapaxes(a.reshape(steps, half, 2, a.shape[1]), 0, 1).reshape(half, 2 * steps, a.shape[1])

    def from_pairs(a):
        return jnp.swapaxes(a.reshape(half, steps, 2, a.shape[2]), 0, 1).reshape(steps * nb_s, a.shape[2])

    d_ssd = n_heads * headdim
    x_lm = to_lm(x_sample)
    sf = _sample_front(x_lm, to_lm(cache_ssd_conv[0]), to_lm(cache_lru_conv[0]),
                       to_lm(state_lru[0][:, None, :]), wts, nb_s, steps, PAST_LEN)
    pre, lst, prelx = sf['pre'], sf['lst'], sf['prelx']

    x1_p, p_ssd, p_ssd_buf, p_lru, p_lru_buf = _prompt_mixer(x_prompt, wts, tl=MIXER_TILE)
    y_prompt, p_ffn_buf, new_state, yoff8 = _ffn_seg(
        x1_p, jnp.zeros((bp, ffn_conv_w.shape[1] - 1, d_ff), F32),
        state_ssd[0].reshape(half, 2, d_ssd, d_state), to_pairs(sf['cb']), to_pairs(sf['xe']),
        wts, tm=FFN_TILE, steps=steps)
    p_ssd = p_ssd.reshape(1, bp, n_heads, headdim, d_state)

    x1_lm = _sample_back(x_lm, sf['ydg'], from_pairs(yoff8), sf['ylru'], wts, nb_s, steps)
    y_lm, tail = _ffn(x1_lm[None], to_lm(cache_ffn_conv[0])[None], wts, tm=steps * nb_s, stride=nb_s)
    k_ssd = ssd_conv_w.shape[1] - 1
    k_lru = lru_conv_w.shape[1] - 1
    k_ffn = ffn_conv_w.shape[1] - 1
    assert steps >= max(k_ssd, k_lru, k_ffn)
    return (y_prompt, from_lm(y_lm[0], steps), p_ssd, p_ssd_buf[None], p_lru.reshape(1, bp, -1), p_lru_buf[None],
            p_ffn_buf[None],
            new_state.reshape(1, nb_s, n_heads, headdim, d_state),
            from_lm(pre[(steps - k_ssd) * nb_s:], k_ssd)[None],
            from_lm(lst, 1).reshape(1, nb_s, -1),
            from_lm(prelx[(steps - k_lru) * nb_s:], k_lru)[None],
            from_lm(tail[0], k_ffn)[None])
```

```python
import functools

import numpy as np
import jax
import jax.numpy as jnp
from jax import lax
from jax.experimental import pallas as pl
from jax.experimental.pallas import tpu as pltpu

F32 = jnp.float32
BF16 = jnp.bfloat16

SSD_GROUPS = 4
SSD_CHUNK = 128
LRU_BLOCKS = 8
LRU_C = 8.0
LN_EPS = 1e-5
RMS_EPS = 1e-6
PAST_LEN = 16384

LANES = 128
SUBLANES = 8
VMEM_LIMIT_BYTES = 60 * 1024 * 1024

ROW_BLK = 32
MIXER_TILE = 2 * SSD_CHUNK
FFN_TILE = 4 * SSD_CHUNK


def _dot(a, b):
    if b.dtype == jnp.uint32:
        b = pltpu.bitcast(b, BF16)
    return jnp.dot(a, b, preferred_element_type=F32)


def _pack_kernel(w_ref, *out_refs, ranges):
    for o_ref, (a, b, pad, scale) in zip(out_refs, ranges):
        v = w_ref[:, a:b]
        if scale != 1.0:
            v = v * scale
        if pad:
            v = jnp.concatenate([v, jnp.zeros((v.shape[0], pad), v.dtype)], axis=1)
        o_ref[...] = pltpu.bitcast(v.astype(BF16), jnp.uint32)


def _pack_weight(w, ranges=None):
    k, n = w.shape
    single = ranges is None
    ranges = ((0, n, 0, 1.0),) if single else tuple(ranges)
    bk = 128 if n > 4096 else 256
    widths = [b - a + pad for a, b, pad, _ in ranges]
    outs = pl.pallas_call(
        functools.partial(_pack_kernel, ranges=ranges),
        grid=(k // bk,),
        in_specs=[pl.BlockSpec((bk, n), lambda i: (i, 0))],
        out_specs=tuple(pl.BlockSpec((bk // 2, wd), lambda i: (i, 0)) for wd in widths),
        out_shape=tuple(jax.ShapeDtypeStruct((k // 2, wd), jnp.uint32) for wd in widths),
        compiler_params=pltpu.CompilerParams(dimension_semantics=("arbitrary",),
                                             vmem_limit_bytes=VMEM_LIMIT_BYTES),
        name="pack_weight",
    )(w.astype(F32))
    return outs[0] if single else outs


def _pack_t_kernel(wt_ref, wdt_ref, *out_refs, parts, bn, dt_rows):
    j = pl.program_id(0)
    first = 0
    for o_ref, (nblk, scale) in zip(out_refs, parts):
        @pl.when(jnp.logical_and(j >= first, j < first + nblk))
        def _(o_ref=o_ref, scale=scale):
            v = wt_ref[...].T
            if scale != 1.0:
                v = v * scale
            o_ref[...] = pltpu.bitcast(v.astype(BF16), jnp.uint32)
        first += nblk

    @pl.when(j == 0)
    def _():
        rows = lax.broadcasted_iota(jnp.int32, wdt_ref.shape, 0)
        v = jnp.where(rows < dt_rows, wdt_ref[...], 0.0)
        out_refs[-1][...] = pltpu.bitcast(v.T.astype(BF16), jnp.uint32)


def _pack_w_in_t(wt, cuts, dt_index, scales, bn=512):
    n, k = wt.shape
    parts, offs = [], []
    for i in range(len(cuts) - 1):
        if i == dt_index:
            continue
        nblk = (cuts[i + 1] - cuts[i]) // bn
        assert nblk * bn == cuts[i + 1] - cuts[i]
        parts.append((nblk, scales[i]))
        offs += [cuts[i] + b * bn for b in range(nblk)]
    offs = np.asarray(offs, np.int32)
    firsts = np.cumsum([0] + [p[0] for p in parts])

    def row_off(j):
        off = jnp.int32(int(offs[0]))
        for idx in range(1, len(offs)):
            off = jnp.where(j >= idx, jnp.int32(int(offs[idx])), off)
        return off

    def out_map(p):
        return lambda j: (0, jnp.clip(j - int(firsts[p]), 0, parts[p][0] - 1))

    dt_rows = cuts[dt_index + 1] - cuts[dt_index]
    outs = pl.pallas_call(
        functools.partial(_pack_t_kernel, parts=tuple(parts), bn=bn, dt_rows=dt_rows),
        grid=(len(offs),),
        in_specs=[pl.BlockSpec((pl.Element(bn), pl.Element(k)), lambda j: (pl.multiple_of(row_off(j), SUBLANES), 0)),
                  pl.BlockSpec((pl.Element(LANES), pl.Element(k)), lambda j: (cuts[dt_index], 0))],
        out_specs=tuple(pl.BlockSpec((k // 2, bn), out_map(p)) for p in range(len(parts)))
        + (pl.BlockSpec((k // 2, LANES), lambda j: (0, 0)),),
        out_shape=tuple(jax.ShapeDtypeStruct((k // 2, nb_ * bn), jnp.uint32) for nb_, _ in parts)
        + (jax.ShapeDtypeStruct((k // 2, LANES), jnp.uint32),),
        compiler_params=pltpu.CompilerParams(dimension_semantics=("arbitrary",),
                                             vmem_limit_bytes=VMEM_LIMIT_BYTES),
        name="pack_w_in",
    )(wt, wt)
    outs = list(outs)
    dt_part = outs.pop()
    outs.insert(dt_index, dt_part)
    return outs


def _pack_rows_01(m):
    bits = np.ascontiguousarray(m, np.float32).view(np.uint32) >> 16
    return jnp.asarray(bits[0::2] | (bits[1::2] << 16), jnp.uint32)


def _dot_nt(a, b):
    return lax.dot_general(a, b, (((1,), (1,)), ((), ())), preferred_element_type=F32)


def _dot_tn(a, b):
    return lax.dot_general(a, b, (((0,), (0,)), ((), ())), preferred_element_type=F32)


def _split3(v):
    hi = v.astype(BF16)
    r1 = v - hi.astype(F32)
    mid = r1.astype(BF16)
    lo = (r1 - mid.astype(F32)).astype(BF16)
    return hi, mid, lo


def _dot_exact_lhs(m, v):
    hi, mid, lo = _split3(v)
    return _dot(m, hi) + _dot(m, mid) + _dot(m, lo)


def _split2(v):
    hi = v.astype(BF16)
    lo = (v - hi.astype(F32)).astype(BF16)
    return hi, lo


def _dot_2way_rhs(v, m2):
    return _dot(jnp.concatenate(_split2(v), axis=1), m2)


def _dot_2way_lhs(m2, v):
    return _dot(m2, jnp.concatenate(_split2(v), axis=0))


def _softplus(x):
    return jnp.maximum(x, 0.0) + jnp.log1p(jnp.exp(-jnp.abs(x)))


def _sigmoid(x):
    return 0.5 * jnp.tanh(0.5 * x) + 0.5


def _silu_of_half(h):
    return h + h * jnp.tanh(h)


def _gelu(x):
    c = np.sqrt(2.0 / np.pi).astype(np.float32)
    return 0.5 * x * (1.0 + jnp.tanh(c * (x + 0.044715 * (x * x * x))))


def _layer_norm(v, g, b):
    mu = jnp.mean(v, axis=-1, keepdims=True)
    d = v - mu
    var = jnp.mean(d * d, axis=-1, keepdims=True)
    return d * lax.rsqrt(var + LN_EPS) * g + b


def _rows(i, n):
    if isinstance(i, int):
        return slice(i * n, (i + 1) * n)
    return pl.ds(pl.multiple_of(i * n, n), n)


def _conv_block(buf_ref, w_ref, b_ref, hist, stride, r0, rows, c0, cw):
    taps = w_ref.shape[0]
    acc = b_ref[:, c0:c0 + cw]
    for k in range(taps):
        off = hist + r0 - (taps - 1 - k) * stride
        if not isinstance(off, int):
            off = pl.multiple_of(off, SUBLANES)
        acc = acc + w_ref[k:k + 1, c0:c0 + cw] * buf_ref[pl.ds(off, rows), c0:c0 + cw]
    return acc


LOG2_SUBLANES = 3
NPOS = SSD_CHUNK // SUBLANES
LOG2_NPOS = 4
assert 1 << LOG2_SUBLANES == SUBLANES and 1 << LOG2_NPOS == NPOS


def _tok_of_row(r):
    return (r & (SUBLANES - 1)) * NPOS + lax.shift_right_logical(r, LOG2_SUBLANES)


def _row_of_tok(t):
    return (t & (NPOS - 1)) * SUBLANES + lax.shift_right_logical(t, LOG2_NPOS)


def _perm_matrix():
    q = SSD_CHUNK
    r = lax.broadcasted_iota(jnp.int32, (q, q), 0)
    c = lax.broadcasted_iota(jnp.int32, (q, q), 1)
    return jnp.where(c == _tok_of_row(r), 1.0, 0.0).astype(BF16)


def _unperm_matrix():
    q = SSD_CHUNK
    t = lax.broadcasted_iota(jnp.int32, (q, q), 0)
    r = lax.broadcasted_iota(jnp.int32, (q, q), 1)
    return jnp.where(r == _row_of_tok(t), 1.0, 0.0).astype(BF16)


def _fill_wrap(buf_ref, wrap_ref, c, wrap, tail_ref, cols=slice(None)):
    q = SSD_CHUNK
    for m in range(wrap // SUBLANES):
        r_cur = (c + 1) * q - wrap + m * SUBLANES
        cur = buf_ref[r_cur:r_cur + SUBLANES, cols]
        if c == 0:
            prv = tail_ref[m * SUBLANES:(m + 1) * SUBLANES, cols]
        else:
            prv = buf_ref[r_cur - q:r_cur - q + SUBLANES, cols]
        sub0 = lax.broadcasted_iota(jnp.int32, cur.shape, 0) == 0
        wrap_ref[c * wrap + m * SUBLANES:c * wrap + (m + 1) * SUBLANES, cols] = jnp.where(
            sub0, pltpu.roll(prv, 1, 0), pltpu.roll(cur, 1, 0))


def _conv_seg(buf_ref, wrap_ref, w_ref, b_ref, c, wrap, c0, cw):
    q = SSD_CHUNK
    taps = w_ref.shape[0]
    cols = slice(c0, c0 + cw)
    acc = b_ref[:, cols] + w_ref[taps - 1:taps, cols] * buf_ref[c * q:(c + 1) * q, cols]
    for k in range(taps - 1):
        back = (taps - 1 - k) * SUBLANES
        shifted = jnp.concatenate(
            [wrap_ref[(c + 1) * wrap - back:(c + 1) * wrap, cols], buf_ref[c * q:(c + 1) * q - back, cols]],
            axis=0)
        acc = acc + w_ref[k:k + 1, cols] * shifted
    return acc


def _seg_tail_rows(wrap):
    n = wrap // SUBLANES
    return [(NPOS - n + m) * SUBLANES + SUBLANES - 1 for m in range(n)]


def _prompt_mixer_kernel(
        x_ref, wz_ref, wxbc_ref, wdt_ref, wlx_ref, wly_ref, wgs_ref, wgl_ref,
        scw_ref, scb_ref, dtb_ref, aneg_ref, dexp_ref, ng_ref, wso_ref,
        lcw_ref, lcb_ref, wax_ref, ba_ref, bx_ref, lam_ref, wlo_ref,
        bg_ref, wo_ref, l1g_ref, l1b_ref, e_ref,
        x1_ref, st_ref, sconv_ref, lst_ref, lconv_ref,
        xb_s, xp_s, xbc_s, swrap_s, stail_s, xc_s, lx_s, lwrap_s, ltail_s, xr_s, ly_s, y_s, z_s, ysb_s, ylb_s,
        ht_s, hl_s, gs_s, gl_s, ys_s, yl_s, o_s, mb_s,
        *, tl, alpha, n_heads, headdim, d_state):
    t = pl.program_id(1)
    nt = pl.num_programs(1)
    d_ssd = n_heads * headdim
    gn = SSD_GROUPS * d_state
    hpg = n_heads // SSD_GROUPS
    gw = hpg * headdim
    d_lru = lx_s.shape[1]
    d_model = x_ref.shape[2]
    q = SSD_CHUNK

    nch = tl // q
    wrap_s = (scw_ref.shape[0] - 1) * SUBLANES
    wrap_l = (lcw_ref.shape[0] - 1) * SUBLANES

    @pl.when(t == 0)
    def _():
        stail_s[...] = jnp.zeros(stail_s.shape, F32)
        ltail_s[...] = jnp.zeros(ltail_s.shape, F32)
        ht_s[...] = jnp.zeros(ht_s.shape, F32)
        hl_s[...] = jnp.zeros(hl_s.shape, F32)

    perm = _perm_matrix()
    perm2 = jnp.concatenate([perm, perm], axis=1)
    for c in range(nch):
        rows = _rows(c, q)
        xp = _dot_2way_lhs(perm2, x_ref[0, rows, :])
        xp_s[rows, :] = xp
        xb_s[rows, :] = xp.astype(BF16)

    cwid = 512

    def proj(dst_ref, w_ref, c0, cw):
        dst_ref[:, c0:c0 + cw] = _dot(xb_s[...], w_ref[:, c0:c0 + cw])

    def lru_conv(c):
        _fill_wrap(lx_s, lwrap_s, c, wrap_l, ltail_s)
        for c0 in range(0, d_lru, cwid):
            xr_s[c * q:(c + 1) * q, c0:c0 + cwid] = _conv_seg(lx_s, lwrap_s, lcw_ref, lcb_ref, c, wrap_l, c0, cwid)
        if c == nch - 1:
            ltail_s[...] = lx_s[tl - wrap_l:tl, :]

    bw = d_lru // LRU_BLOCKS
    sub = lax.broadcasted_iota(jnp.int32, (SUBLANES, bw), 0)
    crow = lax.broadcasted_iota(jnp.int32, (q, bw), 0)

    def lru_block(n):
        cols = slice(n * bw, (n + 1) * bw)
        xr = xr_s[:, cols]
        xrb = xr.astype(BF16)
        rg = _dot(xrb, wax_ref[n])
        r = _sigmoid(rg[:, 0:bw] + ba_ref[:, cols])
        gi = _sigmoid(rg[:, bw:2 * bw] + bx_ref[:, cols])
        log_a = (-LRU_C) * r * _softplus(-lam_ref[:, cols])
        a_all = jnp.exp(log_a)
        mult_all = jnp.sqrt(1.0 - a_all * a_all)
        for c in range(nch):
            a = a_all[c * q:(c + 1) * q, :]
            mult = mult_all[c * q:(c + 1) * q, :]
            if c == 0:
                mult = jnp.where(jnp.logical_and(crow == 0, t == 0), 1.0, mult)
            u = mult * gi[c * q:(c + 1) * q, :] * xr[c * q:(c + 1) * q, :]
            a_p = [a[i * SUBLANES:(i + 1) * SUBLANES, :] for i in range(NPOS)]
            u_p = [u[i * SUBLANES:(i + 1) * SUBLANES, :] for i in range(NPOS)]
            h = u_p[0]
            g = a_p[0]
            for i in range(1, NPOS):
                h = a_p[i] * h + u_p[i]
                g = a_p[i] * g
            gs = jnp.where(sub == 0, 0.0, pltpu.roll(g, 1, 0))
            hs = jnp.where(sub == 0, hl_s[0:1, cols], pltpu.roll(h, 1, 0))
            d = 1
            while d < SUBLANES:
                keep = sub >= d
                hs = jnp.where(keep, gs * pltpu.roll(hs, d, 0) + hs, hs)
                gs = jnp.where(keep, gs * pltpu.roll(gs, d, 0), gs)
                d *= 2
            h = hs
            out = []
            for i in range(NPOS):
                h = a_p[i] * h + u_p[i]
                out.append(h)
            hl_s[0:1, cols] = h[SUBLANES - 1:SUBLANES, :]
            hseq = jnp.concatenate(out, axis=0)
            ylb_s[c * q:(c + 1) * q, cols] = (hseq * _gelu(ly_s[c * q:(c + 1) * q, cols])).astype(BF16)

    def ssd_conv(c0):
        cols = slice(c0, c0 + cwid)
        for c in range(nch):
            _fill_wrap(xbc_s, swrap_s, c, wrap_s, stail_s, cols)
        stail_s[:, cols] = xbc_s[tl - wrap_s:tl, cols]
        for c in range(nch):
            xc_s[c * q:(c + 1) * q, cols] = _silu_of_half(
                _conv_seg(xbc_s, swrap_s, scw_ref, scb_ref, c, wrap_s, c0, cwid))

    def merge_gate(dst_ref, c0, cw, b0):
        dst_ref[:, c0:c0 + cw] = _sigmoid(dst_ref[:, c0:c0 + cw] + bg_ref[:, b0 + c0:b0 + c0 + cw])

    tok_r = _tok_of_row(lax.broadcasted_iota(jnp.int32, (q, q), 0))
    tok_c = _tok_of_row(lax.broadcasted_iota(jnp.int32, (q, q), 1))
    causal = tok_r >= tok_c
    tri = jnp.where(causal, 1.0, 0.0).astype(BF16)
    lane_i = lax.broadcasted_iota(jnp.int32, (q, LANES), 1)
    left = lane_i < headdim

    def chunk_body(c, carry):
        rows = _rows(c, q)
        dt = _softplus(_dot(xb_s[rows, :], wdt_ref[...]) + dtb_ref[...])
        d_a = dt * aneg_ref[...]
        a_cs = _dot_exact_lhs(tri, d_a)
        a_last = a_cs[q - 1:q, :]
        wgt = dt * jnp.exp(a_last - a_cs)
        ea = jnp.exp(a_cs)
        w_exp = _dot_2way_rhs(wgt, e_ref[...])
        ea_exp = _dot_2way_rhs(ea, e_ref[...])
        a_cs_t = a_cs.T
        dt_t = dt.T
        for g in range(SSD_GROUPS):
            b_g = xc_s[rows, d_ssd + g * d_state:d_ssd + (g + 1) * d_state]
            c_g = xc_s[rows, d_ssd + gn + g * d_state:d_ssd + gn + (g + 1) * d_state]
            b_gb = b_g.astype(BF16)
            c_gb = c_g.astype(BF16)
            cb = _dot_nt(c_gb, b_gb)
            for hp in range(hpg // 2):
                c0 = g * gw + hp * 2 * headdim
                xs_pair = xc_s[rows, c0:c0 + 2 * headdim]
                lmats = []
                for j in range(2):
                    h = g * hpg + hp * 2 + j
                    seg = (jnp.broadcast_to(a_cs[:, h:h + 1], (q, q))
                           - jnp.broadcast_to(a_cs_t[h:h + 1, :], (q, q)))
                    dec = jnp.exp(jnp.where(causal, seg, -jnp.inf))
                    lmats.append((cb * dec * jnp.broadcast_to(dt_t[h:h + 1, :], (q, q))).astype(BF16))
                lpair = jnp.concatenate(lmats, axis=1)
                rhs = jnp.concatenate([jnp.where(left, xs_pair, 0.0),
                                       jnp.where(left, 0.0, xs_pair)], axis=0).astype(BF16)
                y_s[rows, c0:c0 + 2 * headdim] = _dot(lpair, rhs)
            gcols = slice(g * gw, (g + 1) * gw)
            h_g = ht_s[:, gcols]
            y_off = _dot(c_gb, h_g.astype(BF16)) * ea_exp[:, gcols]
            y_s[rows, gcols] = y_s[rows, gcols] + y_off
            xw = (xc_s[rows, gcols] * w_exp[:, gcols]).astype(BF16)
            ht_s[:, gcols] = h_g * ea_exp[q - 1:q, gcols] + _dot_tn(b_gb, xw)
            if carry is not None and g < len(carry):
                carry[g]()
        return carry

    def gate_body(i, carry):
        rows = _rows(i, ROW_BLK)
        y = y_s[rows, :] + dexp_ref[...] * xc_s[rows, 0:d_ssd]
        y = y * _silu_of_half(z_s[rows, :])
        ms = jnp.mean(y * y, axis=-1, keepdims=True)
        ysb_s[rows, :] = (y * lax.rsqrt(ms + RMS_EPS) * ng_ref[...]).astype(BF16)
        return carry

    assert nch == 2 and LRU_BLOCKS == 8 and d_lru == 2 * cwid and d_model == 2 * cwid
    d_xbc = xc_s.shape[1]
    nxb = d_xbc // cwid
    proj(lx_s, wlx_ref, 0, cwid)
    proj(lx_s, wlx_ref, cwid, cwid)
    proj(ly_s, wly_ref, 0, cwid); lru_conv(0)
    proj(ly_s, wly_ref, cwid, cwid); lru_conv(1)
    P = functools.partial
    mxu_a = ([P(proj, xbc_s, wxbc_ref, j * cwid, cwid) for j in range(nxb)]
             + [P(proj, gs_s, wgs_ref, j * cwid, cwid) for j in range(2)])
    for n in range(LRU_BLOCKS):
        mxu_a[n]()
        lru_block(n)
        if 1 <= n <= nxb:
            ssd_conv((n - 1) * cwid)
    gates_per_chunk = q // ROW_BLK
    nz = d_ssd // cwid
    chunk_body(0, [P(proj, z_s, wz_ref, j * cwid, cwid) for j in range(nz)])
    merge_gate(gs_s, 0, d_model, 0)
    for i in range(gates_per_chunk):
        gate_body(i, None)

    def wlo_piece(j):
        yl_s[:, j * cwid:(j + 1) * cwid] = _dot(ylb_s[...], wlo_ref[:, j * cwid:(j + 1) * cwid])

    chunk_body(1, [P(proj, gl_s, wgl_ref, 0, cwid), P(proj, gl_s, wgl_ref, cwid, cwid),
                   P(wlo_piece, 0), P(wlo_piece, 1)])
    merge_gate(gl_s, 0, d_model, d_model)
    for i in range(gates_per_chunk, 2 * gates_per_chunk):
        gate_body(i, None)

    ys_s[...] = _dot(ysb_s[...], wso_ref[...])
    for i in range(tl // ROW_BLK):
        rows = _rows(i, ROW_BLK)
        mb_s[rows, :] = (gs_s[rows, :] * ys_s[rows, :] + gl_s[rows, :] * yl_s[rows, :]).astype(BF16)
    o_s[...] = _dot(mb_s[...], wo_ref[...])
    for i in range(tl // ROW_BLK):
        rows = _rows(i, ROW_BLK)
        v = alpha * xp_s[rows, :] + o_s[rows, :]
        x1_ref[0, rows, :] = _layer_norm(v, l1g_ref[...], l1b_ref[...])

    @pl.when(t == nt - 1)
    def _():
        st_ref[0] = ht_s[...].T
        lst_ref[0] = hl_s[0:1, :]
        for m, r in enumerate(_seg_tail_rows(wrap_s)):
            sconv_ref[0, m:m + 1, :] = xbc_s[tl - q + r:tl - q + r + 1, :]
        for m, r in enumerate(_seg_tail_rows(wrap_l)):
            lconv_ref[0, m:m + 1, :] = lx_s[tl - q + r:tl - q + r + 1, :]


def _const_spec(shape):
    nd = len(shape)
    return pl.BlockSpec(shape, lambda *_: (0,) * nd, pipeline_mode=pl.Buffered(1))


def _prompt_mixer(x, wts, tl):
    nb, seq, d_model = x.shape
    n_heads, headdim, d_state = wts['n_heads'], wts['headdim'], wts['d_state']
    d_ssd = n_heads * headdim
    d_xbc = d_ssd + 2 * SSD_GROUPS * d_state
    d_lru = wts['wlx'].shape[1]
    names = ['wz', 'wxbc', 'wdt', 'wlx', 'wly', 'wgs', 'wgl', 'scw', 'scb', 'dtb', 'aneg', 'dexp', 'ng',
             'wso', 'lcw', 'lcb', 'wax', 'ba', 'bx', 'lam', 'wlo', 'bg', 'wo', 'l1g', 'l1b', 'ee']
    consts = [wts[k] for k in names]
    kern = functools.partial(_prompt_mixer_kernel, tl=tl, alpha=wts['alpha'], n_heads=n_heads,
                             headdim=headdim, d_state=d_state)
    out_shape = (
        jax.ShapeDtypeStruct((nb, seq, d_model), F32),
        jax.ShapeDtypeStruct((nb, d_ssd, d_state), F32),
        jax.ShapeDtypeStruct((nb, 3, d_xbc), F32),
        jax.ShapeDtypeStruct((nb, 1, d_lru), F32),
        jax.ShapeDtypeStruct((nb, 3, d_lru), F32),
    )
    out_specs = (
        pl.BlockSpec((1, tl, d_model), lambda b, t: (b, t, 0)),
        pl.BlockSpec((1, d_ssd, d_state), lambda b, t: (b, 0, 0)),
        pl.BlockSpec((1, 3, d_xbc), lambda b, t: (b, 0, 0)),
        pl.BlockSpec((1, 1, d_lru), lambda b, t: (b, 0, 0)),
        pl.BlockSpec((1, 3, d_lru), lambda b, t: (b, 0, 0)),
    )
    nch = tl // SSD_CHUNK
    wrap_s = (wts['scw'].shape[0] - 1) * SUBLANES
    wrap_l = (wts['lcw'].shape[0] - 1) * SUBLANES
    scratch = [
        pltpu.VMEM((tl, d_model), BF16),
        pltpu.VMEM((tl, d_model), F32),
        pltpu.VMEM((tl, d_xbc), F32),
        pltpu.VMEM((nch * wrap_s, d_xbc), F32),
        pltpu.VMEM((wrap_s, d_xbc), F32),
        pltpu.VMEM((tl, d_xbc), F32),
        pltpu.VMEM((tl, d_lru), F32),
        pltpu.VMEM((nch * wrap_l, d_lru), F32),
        pltpu.VMEM((wrap_l, d_lru), F32),
        pltpu.VMEM((tl, d_lru), F32),
        pltpu.VMEM((tl, d_lru), F32),
        pltpu.VMEM((tl, d_ssd), F32),
        pltpu.VMEM((tl, d_ssd), F32),
        pltpu.VMEM((tl, d_ssd), BF16),
        pltpu.VMEM((tl, d_lru), BF16),
        pltpu.VMEM((d_state, d_ssd), F32),
        pltpu.VMEM((SUBLANES, d_lru), F32),
        pltpu.VMEM((tl, d_model), F32),
        pltpu.VMEM((tl, d_model), F32),
        pltpu.VMEM((tl, d_model), F32),
        pltpu.VMEM((tl, d_model), F32),
        pltpu.VMEM((tl, d_model), F32),
        pltpu.VMEM((tl, d_model), BF16),
    ]
    return pl.pallas_call(
        kern,
        grid=(nb, seq // tl),
        in_specs=[pl.BlockSpec((1, tl, d_model), lambda b, t: (b, t, 0))]
        + [_const_spec(c.shape) for c in consts],
        out_specs=out_specs,
        out_shape=out_shape,
        scratch_shapes=scratch,
        compiler_params=pltpu.CompilerParams(
            dimension_semantics=("arbitrary", "arbitrary"),
            vmem_limit_bytes=VMEM_LIMIT_BYTES),
        name="prompt_mixer",
    )(x, *consts)


def _ffn_kernel(x_ref, h0_ref, wg_ref, wu_ref, cw_ref, cb_ref, wd_ref, g_ref, b_ref,
                y_ref, tail_ref, xb_s, gb_s, hb_s, *, tm, stride, hist, alpha, fchunk):
    t = pl.program_id(1)
    nt = pl.num_programs(1)
    taps = cw_ref.shape[0]
    nh = (taps - 1) * stride
    d_ff = gb_s.shape[1]

    @pl.when(t == 0)
    def _():
        gb_s[hist - nh:hist, :] = h0_ref[0]

    xb_s[...] = x_ref[0].astype(BF16)
    acc = None
    for c0 in range(0, d_ff, fchunk):
        cols = slice(c0, c0 + fchunk)
        gb_s[hist:hist + tm, cols] = _dot(xb_s[...], wg_ref[:, cols])
        up = _dot(xb_s[...], wu_ref[:, cols])
        gc = _conv_block(gb_s, cw_ref, cb_ref, hist, stride, 0, tm, c0, fchunk)
        hb_s[...] = (_gelu(gc) * up).astype(BF16)
        part = _dot(hb_s[...], wd_ref[c0 // 2:(c0 + fchunk) // 2, :])
        acc = part if acc is None else acc + part
    v = alpha * x_ref[0] + acc
    y_ref[0] = _layer_norm(v, g_ref[...], b_ref[...])
    gb_s[hist - nh:hist, :] = gb_s[hist + tm - nh:hist + tm, :]

    @pl.when(t == nt - 1)
    def _():
        tail_ref[0] = gb_s[hist - nh:hist, :]


def _ffn_seg_kernel(*refs, tm, alpha, fchunk, state_dims):
    steps = state_dims['steps']
    x_ref, h0_ref, st_ref = refs[0:3]
    cb_refs, xe_refs = refs[3:3 + steps], refs[3 + steps:3 + 2 * steps]
    wg_ref, wu_ref, cw_ref, cb_ref, wd_ref, g_ref, b_ref = refs[3 + 2 * steps:10 + 2 * steps]
    y_ref, tail_ref, nst_ref = refs[10 + 2 * steps:13 + 2 * steps]
    yoff_refs = refs[13 + 2 * steps:13 + 3 * steps]
    xb_s, gb_s, gwrap_s, gtail_s, hb_s, cb8_s, xe8_s, yo8_s = refs[13 + 3 * steps:]

    t = pl.program_id(1)
    pps = st_ref.shape[0]
    step_idx = pl.program_id(0) * pl.num_programs(1) + t
    row0 = lax.rem(step_idx * (2 * pps), SUBLANES)
    for p in range(pps):
        for l in range(steps):
            for e in range(2):
                src = pl.ds(row0 + 2 * p + e, 1)
                cb8_s[p, 2 * l + e:2 * l + e + 1, :] = cb_refs[l][src, :]
                xe8_s[p, 2 * l + e:2 * l + e + 1, :] = xe_refs[l][src, :]
    state_pieces = [
        functools.partial(_state_pair_group, g, st_ref.at[p], cb8_s.at[p], xe8_s.at[p],
                          nst_ref.at[p], yo8_s.at[p], **state_dims)
        for p in range(pps) for g in range(SSD_GROUPS)]

    def state_work(n):
        for _ in range(min(n, len(state_pieces))):
            state_pieces.pop(0)()
    nt = pl.num_programs(1)
    q = SSD_CHUNK
    nch = tm // q
    wrap = (cw_ref.shape[0] - 1) * SUBLANES
    d_ff = gb_s.shape[1]
    tail_rows = _seg_tail_rows(wrap)

    @pl.when(t == 0)
    def _():
        gtail_s[...] = jnp.zeros(gtail_s.shape, F32)
        for m in range(len(tail_rows)):
            r = m * SUBLANES + SUBLANES - 1
            gtail_s[r:r + 1, :] = h0_ref[0, m:m + 1, :]

    xb_s[...] = x_ref[0].astype(BF16)
    def gate_proj(c0):
        cols = slice(c0, c0 + fchunk)
        gb_s[:, cols] = _dot(xb_s[...], wg_ref[:, cols])
        for c in range(nch):
            _fill_wrap(gb_s, gwrap_s, c, wrap, gtail_s, cols)
        gtail_s[:, cols] = gb_s[tm - wrap:tm, cols]

    per_slot = -(-len(state_pieces) // (1 + 2 * (d_ff // fchunk)))
    gate_proj(0)
    state_work(per_slot)
    acc = None
    for c0 in range(0, d_ff, fchunk):
        cols = slice(c0, c0 + fchunk)
        up = _dot(xb_s[...], wu_ref[:, cols])
        if c0 + fchunk < d_ff:
            gate_proj(c0 + fchunk)
        state_work(per_slot)
        for c in range(nch):
            gc = _conv_seg(gb_s, gwrap_s, cw_ref, cb_ref, c, wrap, c0, fchunk)
            hb_s[c * q:(c + 1) * q, :] = (_gelu(gc) * up[c * q:(c + 1) * q, :]).astype(BF16)
        part = _dot(hb_s[...], wd_ref[c0 // 2:(c0 + fchunk) // 2, :])
        state_work(per_slot)
        acc = part if acc is None else acc + part
    state_work(len(state_pieces))
    for p in range(pps):
        for l in range(steps):
            for e in range(2):
                yoff_refs[l][pl.ds(row0 + 2 * p + e, 1), :] = yo8_s[p, 2 * l + e:2 * l + e + 1, :]

    unperm = _unperm_matrix()
    unperm2 = jnp.concatenate([unperm, unperm], axis=1)
    for c in range(nch):
        rows = slice(c * q, (c + 1) * q)
        y = _layer_norm(alpha * x_ref[0, rows, :] + acc[rows, :], g_ref[...], b_ref[...])
        y_ref[0, rows, :] = _dot_2way_lhs(unperm2, y)

    @pl.when(t == nt - 1)
    def _():
        for m, r in enumerate(tail_rows):
            tail_ref[0, m:m + 1, :] = gb_s[tm - q + r:tm - q + r + 1, :]


def _ffn_seg(x, hist0, state, cb_lm, xe_lm, wts, tm, steps):
    nb, seq, d_model = x.shape
    nseq = 2 * state.shape[0]
    d_ssd = xe_lm.shape[1] // 2
    assert cb_lm.shape[0] == steps * nseq and nseq % SUBLANES == 0
    d_ff = wts['wg'].shape[1]
    taps = wts['fcw'].shape[0]
    wrap = (taps - 1) * SUBLANES
    fchunk = 1024
    nt = seq // tm
    npairs = state.shape[0]
    pps = npairs // (nb * nt)
    assert pps * nb * nt == npairs
    consts = [wts[k] for k in ['wg', 'wu', 'fcw', 'fcb', 'wd', 'l2g', 'l2b']]
    state_dims = dict(steps=steps, n_heads=wts['n_heads'], headdim=wts['headdim'], d_state=wts['d_state'])
    kern = functools.partial(_ffn_seg_kernel, tm=tm, alpha=wts['alpha'], fchunk=fchunk, state_dims=state_dims)
    pair_blk = lambda a: pl.BlockSpec((pps,) + tuple(a.shape[1:]),
                                      lambda b, t: (b * nt + t,) + (0,) * (len(a.shape) - 1))

    def rows_blk(l, width):
        return pl.BlockSpec((SUBLANES, width),
                            lambda b, t: (l * (nseq // SUBLANES) + ((b * nt + t) * 2 * pps) // SUBLANES, 0))

    def yoff_blk():
        return pl.BlockSpec((SUBLANES, d_ssd), lambda b, t: (((b * nt + t) * 2 * pps) // SUBLANES, 0))

    outs = pl.pallas_call(
        kern,
        grid=(nb, nt),
        in_specs=[pl.BlockSpec((1, tm, d_model), lambda b, t: (b, t, 0)),
                  pl.BlockSpec((1, taps - 1, d_ff), lambda b, t: (b, 0, 0)),
                  pair_blk(state)]
        + [rows_blk(l, cb_lm.shape[1]) for l in range(steps)]
        + [rows_blk(l, xe_lm.shape[1]) for l in range(steps)]
        + [_const_spec(c.shape) for c in consts],
        out_specs=(pl.BlockSpec((1, tm, d_model), lambda b, t: (b, t, 0)),
                   pl.BlockSpec((1, taps - 1, d_ff), lambda b, t: (b, 0, 0)),
                   pair_blk(state)) + tuple(yoff_blk() for _ in range(steps)),
        out_shape=(jax.ShapeDtypeStruct((nb, seq, d_model), F32),
                   jax.ShapeDtypeStruct((nb, taps - 1, d_ff), F32),
                   jax.ShapeDtypeStruct(state.shape, F32))
        + tuple(jax.ShapeDtypeStruct((nseq, d_ssd), F32) for _ in range(steps)),
        scratch_shapes=[pltpu.VMEM((tm, d_model), BF16),
                        pltpu.VMEM((tm, d_ff), F32),
                        pltpu.VMEM((tm // SSD_CHUNK * wrap, d_ff), F32),
                        pltpu.VMEM((wrap, d_ff), F32),
                        pltpu.VMEM((tm, fchunk), BF16),
                        pltpu.VMEM((pps, 2 * steps, cb_lm.shape[1]), F32),
                        pltpu.VMEM((pps, 2 * steps, xe_lm.shape[1]), F32),
                        pltpu.VMEM((pps, 2 * steps, d_ssd), F32)],
        compiler_params=pltpu.CompilerParams(
            dimension_semantics=("arbitrary", "arbitrary"),
            vmem_limit_bytes=VMEM_LIMIT_BYTES),
        name="conv_ffn_seg",
    )(x, hist0, state, *([cb_lm] * steps), *([xe_lm] * steps), *consts)
    return outs[0], outs[1], outs[2], outs[3:]


def _ffn(x, hist0, wts, tm, stride):
    nb, seq, d_model = x.shape
    d_ff = wts['wg'].shape[1]
    taps = wts['fcw'].shape[0]
    nh = (taps - 1) * stride
    hist = -(-nh // SUBLANES) * SUBLANES
    consts = [wts[k] for k in ['wg', 'wu', 'fcw', 'fcb', 'wd', 'l2g', 'l2b']]
    kern = functools.partial(_ffn_kernel, tm=tm, stride=stride, hist=hist, alpha=wts['alpha'], fchunk=1024)
    return pl.pallas_call(
        kern,
        grid=(nb, seq // tm),
        in_specs=[pl.BlockSpec((1, tm, d_model), lambda b, t: (b, t, 0)),
                  pl.BlockSpec((1, nh, d_ff), lambda b, t: (b, 0, 0))]
        + [_const_spec(c.shape) for c in consts],
        out_specs=(pl.BlockSpec((1, tm, d_model), lambda b, t: (b, t, 0)),
                   pl.BlockSpec((1, nh, d_ff), lambda b, t: (b, 0, 0))),
        out_shape=(jax.ShapeDtypeStruct((nb, seq, d_model), F32),
                   jax.ShapeDtypeStruct((nb, nh, d_ff), F32)),
        scratch_shapes=[pltpu.VMEM((tm, d_model), BF16),
                        pltpu.VMEM((hist + tm, d_ff), F32),
                        pltpu.VMEM((tm, 1024), BF16)],
        compiler_params=pltpu.CompilerParams(
            dimension_semantics=("arbitrary", "arbitrary"),
            vmem_limit_bytes=VMEM_LIMIT_BYTES),
        name="conv_ffn",
    )(x, hist0, *consts)


def _sample_ssd_kernel(
        x_ref, xall_ref, cssd_ref, wxbc_ref, wdt_ref, scw_ref, scb_ref, dtb_ref, aneg_ref, dexp_ref,
        e_ref, e2_ref,
        pre_ref, cb_ref, xe_ref, ydg_ref,
        xbc_s, xs_s, bs_s, acs_s, dts_s,
        *, nseq, steps, n_heads, headdim, d_state):
    l = pl.program_id(0)
    d_ssd = n_heads * headdim
    gn = SSD_GROUPS * d_state
    hist = (scw_ref.shape[0] - 1) * nseq
    r0 = l * nseq

    def blk(i):
        return pl.ds(pl.multiple_of(i * nseq, nseq), nseq)

    def sblk(i):
        return slice(i * nseq, (i + 1) * nseq)

    @pl.when(l == 0)
    def _():
        xbc_s[0:hist, :] = cssd_ref[...]
        dts_s[...] = _softplus(_dot(xall_ref[...].astype(BF16), wdt_ref[...]) + dtb_ref[...])
        acc = jnp.zeros((nseq, LANES), F32)
        for s in range(steps):
            acc = acc + dts_s[sblk(s), :] * aneg_ref[...]
            acs_s[sblk(s), :] = acc

    xb = x_ref[...].astype(BF16)

    pre = _dot(xb, wxbc_ref[...])
    pre_ref[...] = pre
    xbc_s[pl.ds(pl.multiple_of(hist + r0, nseq), nseq), :] = pre
    cwid = 512
    for c0 in range(0, d_ssd, cwid):
        xs_s[blk(l), c0:c0 + cwid] = _silu_of_half(
            _conv_block(xbc_s, scw_ref, scb_ref, hist, nseq, r0, nseq, c0, cwid))
    b_l = _silu_of_half(_conv_block(xbc_s, scw_ref, scb_ref, hist, nseq, r0, nseq, d_ssd, gn))
    c_l = _silu_of_half(_conv_block(xbc_s, scw_ref, scb_ref, hist, nseq, r0, nseq, d_ssd + gn, gn))
    bs_s[blk(l), :] = b_l
    cb_ref[:, 0:gn] = c_l
    cb_ref[:, gn:2 * gn] = b_l

    a_cs = acs_s[blk(l), :]
    dt = dts_s[blk(l), :]
    a_end = acs_s[sblk(steps - 1), :]
    xe_ref[:, d_ssd:2 * d_ssd] = _dot_2way_rhs(jnp.exp(a_cs), e_ref[...])
    xe_ref[:, 0:d_ssd] = xs_s[blk(l), :] * _dot_2way_rhs(dt * jnp.exp(a_end - a_cs), e_ref[...])

    ydg_ref[...] = dexp_ref[...] * xs_s[blk(l), :]
    for s in range(steps):
        @pl.when(s <= l)
        def _(s=s):
            coef = _dot_2way_rhs(jnp.exp(a_cs - acs_s[sblk(s), :]) * dts_s[sblk(s), :], e_ref[...])
            cbx = _dot_2way_rhs(bs_s[sblk(s), :] * c_l, e2_ref[...])
            ydg_ref[...] += cbx * coef * xs_s[sblk(s), :]


def _sample_lru_kernel(
        x_ref, clru_ref, slru_ref, wlx_ref, wly_ref, lcw_ref, lcb_ref, wax_ref, ba_ref, bx_ref,
        lam_ref, prelx_ref, ylru_ref, lst_ref, lx_s, hl_s, *, nseq, steps, start_pos):
    l = pl.program_id(0)
    hist = (lcw_ref.shape[0] - 1) * nseq
    d_lru = lx_s.shape[1]
    r0 = l * nseq

    @pl.when(l == 0)
    def _():
        lx_s[0:hist, :] = clru_ref[...]
        hl_s[...] = slru_ref[...]

    xb = x_ref[...].astype(BF16)
    prelx = _dot(xb, wlx_ref[...])
    prelx_ref[...] = prelx
    lx_s[pl.ds(pl.multiple_of(hist + r0, nseq), nseq), :] = prelx
    ly = _dot(xb, wly_ref[...])
    bw = d_lru // LRU_BLOCKS
    first = (l + start_pos) == 0
    for n in range(LRU_BLOCKS):
        cols = slice(n * bw, (n + 1) * bw)
        xr = _conv_block(lx_s, lcw_ref, lcb_ref, hist, nseq, r0, nseq, n * bw, bw)
        xrb = xr.astype(BF16)
        rg = _dot(xrb, wax_ref[n])
        r = _sigmoid(rg[:, 0:bw] + ba_ref[:, cols])
        gi = _sigmoid(rg[:, bw:2 * bw] + bx_ref[:, cols])
        log_a = (-LRU_C) * r * _softplus(-lam_ref[:, cols])
        a = jnp.exp(log_a)
        mult = jnp.where(first, 1.0, jnp.sqrt(1.0 - a * a))
        h = a * hl_s[:, cols] + mult * gi * xr
        hl_s[:, cols] = h
        ylru_ref[:, cols] = (h * _gelu(ly[:, cols])).astype(BF16)

    @pl.when(l == steps - 1)
    def _():
        lst_ref[...] = hl_s[...]


def _state_pair_group(g, st_ref, cb8_ref, xe8_ref, nst_ref, yoff8_ref,
                      *, steps, n_heads, headdim, d_state):
    nrow = 2 * steps
    hpg = n_heads // SSD_GROUPS
    gw = hpg * headdim
    gn = SSD_GROUPS * d_state
    d_ssd = n_heads * headdim
    assert 2 * headdim == LANES and d_state == LANES
    par = lax.broadcasted_iota(jnp.int32, (nrow, gw), 0) % 2
    low = lax.broadcasted_iota(jnp.int32, (nrow, LANES), 1) < headdim
    gcols = slice(g * gw, (g + 1) * gw)
    c8g = cb8_ref[:, g * d_state:(g + 1) * d_state].astype(BF16)
    b8g = cb8_ref[:, gn + g * d_state:gn + (g + 1) * d_state].astype(BF16)
    xw8 = xe8_ref[:, g * gw:(g + 1) * gw]
    ea8 = xe8_ref[:, d_ssd + g * gw:d_ssd + (g + 1) * gw]
    cds = []
    for hp in range(hpg // 2):
        pair = ea8[:, hp * LANES:(hp + 1) * LANES]
        swapped = pltpu.roll(pair, headdim, 1)
        cds.append(jnp.where(low, pair, swapped))
        cds.append(jnp.where(low, swapped, pair))
    yo = None
    for e in range(2):
        sg = st_ref[e, gcols, :]
        yo_e = _dot_nt(c8g, sg.astype(BF16))
        yo = yo_e if e == 0 else jnp.where(par == e, yo_e, yo)
        xw_e = jnp.where(par == e, xw8, 0.0).astype(BF16)
        upd = _dot_tn(xw_e, b8g)
        k_last = 2 * (steps - 1) + e
        for hh in range(hpg):
            cd = cds[hh][k_last:k_last + 1, :]
            hr = slice(hh * headdim, (hh + 1) * headdim)
            nst_ref[e, g * gw + hh * headdim:g * gw + (hh + 1) * headdim, :] = sg[hr, :] * cd + upd[hr, :]
    yoff8_ref[:, gcols] = yo * ea8


def _sample_post_kernel(x_ref, ydg_ref, yoff_ref, ylru_ref, wz_ref, ng_ref, wso_ref, wlo_ref,
                        wgs_ref, wgl_ref, bg_ref, wo_ref, l1g_ref, l1b_ref, x1_ref, *, alpha):
    d_model = x_ref.shape[1]
    xb = x_ref[...].astype(BF16)
    y = (ydg_ref[...] + yoff_ref[...]) * _silu_of_half(_dot(xb, wz_ref[...]))
    ms = jnp.mean(y * y, axis=-1, keepdims=True)
    ysb = (y * lax.rsqrt(ms + RMS_EPS) * ng_ref[...]).astype(BF16)
    g_ssd = _sigmoid(_dot(xb, wgs_ref[...]) + bg_ref[:, 0:d_model])
    g_lru = _sigmoid(_dot(xb, wgl_ref[...]) + bg_ref[:, d_model:2 * d_model])
    merged = g_ssd * _dot(ysb, wso_ref[...]) + g_lru * _dot(ylru_ref[...], wlo_ref[...])
    o = _dot(merged.astype(BF16), wo_ref[...])
    x1_ref[...] = _layer_norm(alpha * x_ref[...] + o, l1g_ref[...], l1b_ref[...])


def _sample_front(x_lm, cssd_lm, clru_lm, slru_lm, wts, nseq, steps, start_pos):
    n_heads, headdim, d_state = wts['n_heads'], wts['headdim'], wts['d_state']
    d_model = x_lm.shape[1]
    d_ssd = n_heads * headdim
    gn = SSD_GROUPS * d_state
    d_xbc = d_ssd + 2 * gn
    d_lru = wts['wlx'].shape[1]
    ntok = steps * nseq
    dims = dict(nseq=nseq, steps=steps, n_heads=n_heads, headdim=headdim, d_state=d_state)
    params = pltpu.CompilerParams(dimension_semantics=("arbitrary",), vmem_limit_bytes=VMEM_LIMIT_BYTES)
    step_blk = lambda w: pl.BlockSpec((nseq, w), lambda l: (l, 0))

    sds = jax.ShapeDtypeStruct
    ssd_names = ['wxbc', 'wdt', 'scw', 'scb', 'dtb', 'aneg', 'dexp', 'ee', 'e2e']
    ssd_consts = [x_lm, cssd_lm] + [wts[k] for k in ssd_names]
    pre, cb_lm, xe_lm, ydg_lm = pl.pallas_call(
        functools.partial(_sample_ssd_kernel, **dims),
        grid=(steps,),
        in_specs=[step_blk(d_model)] + [_const_spec(c.shape) for c in ssd_consts],
        out_specs=(step_blk(d_xbc), step_blk(2 * gn), step_blk(2 * d_ssd), step_blk(d_ssd)),
        out_shape=(sds((ntok, d_xbc), F32), sds((ntok, 2 * gn), F32), sds((ntok, 2 * d_ssd), F32),
                   sds((ntok, d_ssd), F32)),
        scratch_shapes=[
            pltpu.VMEM((cssd_lm.shape[0] + ntok, d_xbc), F32),
            pltpu.VMEM((ntok, d_ssd), F32),
            pltpu.VMEM((ntok, gn), F32),
            pltpu.VMEM((ntok, LANES), F32),
            pltpu.VMEM((ntok, LANES), F32),
        ],
        compiler_params=params,
        name="sample_ssd",
    )(x_lm, *ssd_consts)

    lru_names = ['wlx', 'wly', 'lcw', 'lcb', 'wax', 'ba', 'bx', 'lam']
    lru_consts = [clru_lm, slru_lm] + [wts[k] for k in lru_names]
    prelx, ylru_lm, lst = pl.pallas_call(
        functools.partial(_sample_lru_kernel, nseq=nseq, steps=steps, start_pos=start_pos),
        grid=(steps,),
        in_specs=[step_blk(d_model)] + [_const_spec(c.shape) for c in lru_consts],
        out_specs=(step_blk(d_lru), step_blk(d_lru), pl.BlockSpec((nseq, d_lru), lambda l: (0, 0))),
        out_shape=(sds((ntok, d_lru), F32), sds((ntok, d_lru), BF16), sds((nseq, d_lru), F32)),
        scratch_shapes=[
            pltpu.VMEM((clru_lm.shape[0] + ntok, d_lru), F32),
            pltpu.VMEM((nseq, d_lru), F32),
        ],
        compiler_params=params,
        name="sample_lru",
    )(x_lm, *lru_consts)

    return dict(pre=pre, cb=cb_lm, xe=xe_lm, ydg=ydg_lm, prelx=prelx, ylru=ylru_lm, lst=lst)


def _sample_back(x_lm, ydg_lm, yoff_lm, ylru_lm, wts, nseq, steps):
    d_model = x_lm.shape[1]
    d_ssd = wts['n_heads'] * wts['headdim']
    d_lru = wts['wlx'].shape[1]
    ntok = steps * nseq
    params = pltpu.CompilerParams(dimension_semantics=("arbitrary",), vmem_limit_bytes=VMEM_LIMIT_BYTES)
    step_blk = lambda w: pl.BlockSpec((nseq, w), lambda l: (l, 0))
    sds = jax.ShapeDtypeStruct
    post_names = ['wz', 'ng', 'wso', 'wlo', 'wgs', 'wgl', 'bg', 'wo', 'l1g', 'l1b']
    post_consts = [wts[k] for k in post_names]
    x1_lm = pl.pallas_call(
        functools.partial(_sample_post_kernel, alpha=wts['alpha']),
        grid=(steps,),
        in_specs=[step_blk(d_model), step_blk(d_ssd), step_blk(d_ssd), step_blk(d_lru)]
        + [_const_spec(c.shape) for c in post_consts],
        out_specs=step_blk(d_model),
        out_shape=sds((ntok, d_model), F32),
        compiler_params=params,
        name="sample_post",
    )(x_lm, ydg_lm, yoff_lm, ylru_lm, *post_consts)
    return x1_lm


def _prep_weights(w_in, b_gate, ssd_conv_w, ssd_conv_b, ssd_dt_bias, ssd_a_log, ssd_d, ssd_norm_g,
                  w_ssd_out, lru_conv_w, lru_conv_b, lru_wa, lru_ba, lru_wx, lru_bx, lru_lambda,
                  w_lru_out, w_o, ln1_g, ln1_b, ffn_w_gate, ffn_w_up, ffn_conv_w, ffn_conv_b,
                  ffn_w_down, ln2_g, ln2_b, n_heads, headdim, d_state):
    depth = w_in.shape[0]
    d_model = w_in.shape[1]
    d_ssd = n_heads * headdim
    d_xbc = d_ssd + 2 * SSD_GROUPS * d_state
    d_lru = lru_lambda.shape[1]
    sizes = (d_ssd, d_xbc, n_heads, d_lru, d_lru, d_model, d_model)
    cuts = np.cumsum((0,) + sizes)
    scales = [0.5 if i == 0 else 1.0 for i in range(len(sizes))]
    parts = _pack_w_in_t(jnp.swapaxes(w_in, 1, 2)[0].astype(F32), [int(c) for c in cuts], 2, scales)
    row = lambda v: v.reshape(1, -1).astype(F32)
    mat = _pack_weight
    wax = jnp.concatenate([lru_wa[0], lru_wx[0]], axis=-1)
    wax = _pack_weight(wax.reshape(-1, wax.shape[-1])).reshape(wax.shape[0], wax.shape[1] // 2, wax.shape[2])
    pad_heads = lambda v: jnp.pad(v.reshape(1, -1).astype(F32), ((0, 0), (0, LANES - n_heads)))
    head_of_col = np.arange(d_ssd) // headdim
    expand = (np.arange(LANES)[:, None] == head_of_col[None, :]).astype(np.float32)
    group_sum = (np.arange(SSD_GROUPS * d_state)[:, None] // d_state
                 == (head_of_col // (n_heads // SSD_GROUPS))[None, :])
    return dict(
        n_heads=n_heads, headdim=headdim, d_state=d_state,
        alpha=float((2.0 * depth) ** 0.25),
        wz=parts[0], wxbc=parts[1], wdt=parts[2], wlx=parts[3], wly=parts[4], wgs=parts[5], wgl=parts[6],
        scw=0.5 * ssd_conv_w[0].astype(F32), scb=0.5 * row(ssd_conv_b[0]),
        dtb=pad_heads(ssd_dt_bias[0]), aneg=pad_heads(-jnp.exp(ssd_a_log[0].astype(F32))),
        dexp=row(jnp.repeat(ssd_d[0], headdim)), ng=row(ssd_norm_g[0]),
        wso=mat(w_ssd_out[0]),
        lcw=lru_conv_w[0].astype(F32), lcb=row(lru_conv_b[0]),
        ba=row(lru_ba[0]), bx=row(lru_bx[0]), wax=wax,
        lam=row(lru_lambda[0]), wlo=mat(w_lru_out[0]),
        bg=row(b_gate[0]), wo=mat(w_o[0]), l1g=row(ln1_g[0]), l1b=row(ln1_b[0]),
        ee=_pack_rows_01(np.concatenate([expand, expand], axis=0)),
        e2e=_pack_rows_01(np.concatenate([group_sum, group_sum], axis=0)),
        wg=mat(ffn_w_gate[0]), wu=mat(ffn_w_up[0]),
        fcw=ffn_conv_w[0].astype(F32), fcb=row(ffn_conv_b[0]), wd=mat(ffn_w_down[0]),
        l2g=row(ln2_g[0]), l2b=row(ln2_b[0]),
    )


def kernel(x_prompt, x_sample, state_ssd, cache_ssd_conv, state_lru, cache_lru_conv, cache_ffn_conv, w_in, b_gate, ssd_conv_w, ssd_conv_b, ssd_dt_bias, ssd_a_log, ssd_d, ssd_norm_g, w_ssd_out, lru_conv_w, lru_conv_b, lru_wa, lru_ba, lru_wx, lru_bx, lru_lambda, w_lru_out, w_o, ln1_g, ln1_b, ffn_w_gate, ffn_w_up, ffn_conv_w, ffn_conv_b, ffn_w_down, ln2_g, ln2_b):
    assert w_in.shape[0] == 1, "single-layer trunk"
    _, _, n_heads, headdim, d_state = state_ssd.shape
    wts = _prep_weights(w_in, b_gate, ssd_conv_w, ssd_conv_b, ssd_dt_bias, ssd_a_log, ssd_d, ssd_norm_g,
                        w_ssd_out, lru_conv_w, lru_conv_b, lru_wa, lru_ba, lru_wx, lru_bx, lru_lambda,
                        w_lru_out, w_o, ln1_g, ln1_b, ffn_w_gate, ffn_w_up, ffn_conv_w, ffn_conv_b,
                        ffn_w_down, ln2_g, ln2_b, n_heads, headdim, d_state)
    bp = x_prompt.shape[0]
    d_ff = ffn_w_gate.shape[2]

    nb_s, steps, _ = x_sample.shape
    half = nb_s // 2

    def to_lm(a):
        return jnp.swapaxes(a, 0, 1).reshape(a.shape[1] * nb_s, a.shape[2])

    def from_lm(a, k):
        return jnp.swapaxes(a.reshape(k, nb_s, a.shape[1]), 0, 1)

    d_ssd = n_heads * headdim
    x_lm = to_lm(x_sample)
    sf = _sample_front(x_lm, to_lm(cache_ssd_conv[0]), to_lm(cache_lru_conv[0]),
                       to_lm(state_lru[0][:, None, :]), wts, nb_s, steps, PAST_LEN)
    pre, lst, prelx = sf['pre'], sf['lst'], sf['prelx']

    x1_p, p_ssd, p_ssd_buf, p_lru, p_lru_buf = _prompt_mixer(x_prompt, wts, tl=MIXER_TILE)
    y_prompt, p_ffn_buf, new_state, yoff_steps = _ffn_seg(
        x1_p, jnp.zeros((bp, ffn_conv_w.shape[1] - 1, d_ff), F32),
        state_ssd[0].reshape(half, 2, d_ssd, d_state), sf['cb'], sf['xe'], wts, tm=FFN_TILE, steps=steps)
    p_ssd = p_ssd.reshape(1, bp, n_heads, headdim, d_state)

    x1_lm = _sample_back(x_lm, sf['ydg'], jnp.concatenate(yoff_steps, axis=0), sf['ylru'], wts, nb_s, steps)
    y_lm, tail = _ffn(x1_lm[None], to_lm(cache_ffn_conv[0])[None], wts, tm=steps * nb_s, stride=nb_s)
    k_ssd = ssd_conv_w.shape[1] - 1
    k_lru = lru_conv_w.shape[1] - 1
    k_ffn = ffn_conv_w.shape[1] - 1
    assert steps >= max(k_ssd, k_lru, k_ffn)
    return (y_prompt, from_lm(y_lm[0], steps), p_ssd, p_ssd_buf[None], p_lru.reshape(1, bp, -1), p_lru_buf[None],
            p_ffn_buf[None],
            new_state.reshape(1, nb_s, n_heads, headdim, d_state),
            from_lm(pre[(steps - k_ssd) * nb_s:], k_ssd)[None],
            from_lm(lst, 1).reshape(1, nb_s, -1),
            from_lm(prelx[(steps - k_lru) * nb_s:], k_lru)[None],
            from_lm(tail[0], k_ffn)[None])
```

```python
import functools

import numpy as np
import jax
import jax.numpy as jnp
from jax import lax
from jax.experimental import pallas as pl
from jax.experimental.pallas import tpu as pltpu

F32 = jnp.float32
BF16 = jnp.bfloat16

SSD_GROUPS = 4
SSD_CHUNK = 128
LRU_BLOCKS = 8
LRU_C = 8.0
LN_EPS = 1e-5
RMS_EPS = 1e-6
PAST_LEN = 16384

LANES = 128
SUBLANES = 8
VMEM_LIMIT_BYTES = 60 * 1024 * 1024

ROW_BLK = 32
MIXER_TILE = 2 * SSD_CHUNK
FFN_TILE = 4 * SSD_CHUNK


def _dot(a, b):
    if b.dtype == jnp.uint32:
        b = pltpu.bitcast(b, BF16)
    return jnp.dot(a, b, preferred_element_type=F32)


def _pack_kernel(*refs):
    n = len(refs) // 2
    for w_ref, o_ref in zip(refs[:n], refs[n:]):
        o_ref[...] = pltpu.bitcast(w_ref[...].astype(BF16), jnp.uint32)


def _pack_weights(ws):
    k = ws[0].shape[0]
    assert all(w.shape[0] == k for w in ws)
    bk = 256
    return pl.pallas_call(
        _pack_kernel,
        grid=(k // bk,),
        in_specs=[pl.BlockSpec((bk, w.shape[1]), lambda i: (i, 0)) for w in ws],
        out_specs=tuple(pl.BlockSpec((bk // 2, w.shape[1]), lambda i: (i, 0)) for w in ws),
        out_shape=tuple(jax.ShapeDtypeStruct((k // 2, w.shape[1]), jnp.uint32) for w in ws),
        compiler_params=pltpu.CompilerParams(dimension_semantics=("arbitrary",),
                                             vmem_limit_bytes=VMEM_LIMIT_BYTES),
        name="pack_weight",
    )(*[w.astype(F32) for w in ws])


def _pack_t_kernel(wt_ref, wdt_ref, *out_refs, parts, bn, dt_rows):
    j = pl.program_id(0)
    first = 0
    for o_ref, (nblk, scale) in zip(out_refs, parts):
        @pl.when(jnp.logical_and(j >= first, j < first + nblk))
        def _(o_ref=o_ref, scale=scale):
            v = wt_ref[...].T
            if scale != 1.0:
                v = v * scale
            o_ref[...] = pltpu.bitcast(v.astype(BF16), jnp.uint32)
        first += nblk

    @pl.when(j == 0)
    def _():
        rows = lax.broadcasted_iota(jnp.int32, wdt_ref.shape, 0)
        v = jnp.where(rows < dt_rows, wdt_ref[...], 0.0)
        out_refs[-1][...] = pltpu.bitcast(v.T.astype(BF16), jnp.uint32)


def _pack_w_in_t(wt, cuts, dt_index, scales, bn=512):
    n, k = wt.shape
    parts, offs = [], []
    for i in range(len(cuts) - 1):
        if i == dt_index:
            continue
        nblk = (cuts[i + 1] - cuts[i]) // bn
        assert nblk * bn == cuts[i + 1] - cuts[i]
        parts.append((nblk, scales[i]))
        offs += [cuts[i] + b * bn for b in range(nblk)]
    offs = np.asarray(offs, np.int32)
    firsts = np.cumsum([0] + [p[0] for p in parts])

    def row_off(j):
        off = jnp.int32(int(offs[0]))
        for idx in range(1, len(offs)):
            off = jnp.where(j >= idx, jnp.int32(int(offs[idx])), off)
        return off

    def out_map(p):
        return lambda j: (0, jnp.clip(j - int(firsts[p]), 0, parts[p][0] - 1))

    dt_rows = cuts[dt_index + 1] - cuts[dt_index]
    outs = pl.pallas_call(
        functools.partial(_pack_t_kernel, parts=tuple(parts), bn=bn, dt_rows=dt_rows),
        grid=(len(offs),),
        in_specs=[pl.BlockSpec((pl.Element(bn), pl.Element(k)), lambda j: (pl.multiple_of(row_off(j), SUBLANES), 0)),
                  pl.BlockSpec((pl.Element(LANES), pl.Element(k)), lambda j: (cuts[dt_index], 0))],
        out_specs=tuple(pl.BlockSpec((k // 2, bn), out_map(p)) for p in range(len(parts)))
        + (pl.BlockSpec((k // 2, LANES), lambda j: (0, 0)),),
        out_shape=tuple(jax.ShapeDtypeStruct((k // 2, nb_ * bn), jnp.uint32) for nb_, _ in parts)
        + (jax.ShapeDtypeStruct((k // 2, LANES), jnp.uint32),),
        compiler_params=pltpu.CompilerParams(dimension_semantics=("arbitrary",),
                                             vmem_limit_bytes=VMEM_LIMIT_BYTES),
        name="pack_w_in",
    )(wt, wt)
    outs = list(outs)
    dt_part = outs.pop()
    outs.insert(dt_index, dt_part)
    return outs


def _pack_rows_01(m):
    bits = np.ascontiguousarray(m, np.float32).view(np.uint32) >> 16
    return jnp.asarray(bits[0::2] | (bits[1::2] << 16), jnp.uint32)


def _dot_nt(a, b):
    return lax.dot_general(a, b, (((1,), (1,)), ((), ())), preferred_element_type=F32)


def _dot_tn(a, b):
    return lax.dot_general(a, b, (((0,), (0,)), ((), ())), preferred_element_type=F32)


def _split3(v):
    hi = v.astype(BF16)
    r1 = v - hi.astype(F32)
    mid = r1.astype(BF16)
    lo = (r1 - mid.astype(F32)).astype(BF16)
    return hi, mid, lo


def _dot_exact_lhs(m, v):
    hi, mid, lo = _split3(v)
    return _dot(m, hi) + _dot(m, mid) + _dot(m, lo)


def _split2(v):
    hi = v.astype(BF16)
    lo = (v - hi.astype(F32)).astype(BF16)
    return hi, lo


def _dot_2way_rhs(v, m2):
    return _dot(jnp.concatenate(_split2(v), axis=1), m2)


def _dot_2way_lhs(m2, v):
    return _dot(m2, jnp.concatenate(_split2(v), axis=0))


def _softplus(x):
    return jnp.maximum(x, 0.0) + jnp.log1p(jnp.exp(-jnp.abs(x)))


def _sigmoid(x):
    return 0.5 * jnp.tanh(0.5 * x) + 0.5


def _silu_of_half(h):
    return h + h * jnp.tanh(h)


def _gelu(x):
    c = np.sqrt(2.0 / np.pi).astype(np.float32)
    return 0.5 * x * (1.0 + jnp.tanh(c * (x + 0.044715 * (x * x * x))))


def _layer_norm(v, g, b):
    mu = jnp.mean(v, axis=-1, keepdims=True)
    d = v - mu
    var = jnp.mean(d * d, axis=-1, keepdims=True)
    return d * lax.rsqrt(var + LN_EPS) * g + b


def _rows(i, n):
    if isinstance(i, int):
        return slice(i * n, (i + 1) * n)
    return pl.ds(pl.multiple_of(i * n, n), n)


def _conv_block(buf_ref, w_ref, b_ref, hist, stride, r0, rows, c0, cw):
    taps = w_ref.shape[0]
    acc = b_ref[:, c0:c0 + cw]
    for k in range(taps):
        off = hist + r0 - (taps - 1 - k) * stride
        if not isinstance(off, int):
            off = pl.multiple_of(off, SUBLANES)
        acc = acc + w_ref[k:k + 1, c0:c0 + cw] * buf_ref[pl.ds(off, rows), c0:c0 + cw]
    return acc


LOG2_SUBLANES = 3
NPOS = SSD_CHUNK // SUBLANES
LOG2_NPOS = 4
assert 1 << LOG2_SUBLANES == SUBLANES and 1 << LOG2_NPOS == NPOS


def _tok_of_row(r):
    return (r & (SUBLANES - 1)) * NPOS + lax.shift_right_logical(r, LOG2_SUBLANES)


def _row_of_tok(t):
    return (t & (NPOS - 1)) * SUBLANES + lax.shift_right_logical(t, LOG2_NPOS)


def _perm_matrix():
    q = SSD_CHUNK
    r = lax.broadcasted_iota(jnp.int32, (q, q), 0)
    c = lax.broadcasted_iota(jnp.int32, (q, q), 1)
    return jnp.where(c == _tok_of_row(r), 1.0, 0.0).astype(BF16)


def _unperm_matrix():
    q = SSD_CHUNK
    t = lax.broadcasted_iota(jnp.int32, (q, q), 0)
    r = lax.broadcasted_iota(jnp.int32, (q, q), 1)
    return jnp.where(r == _row_of_tok(t), 1.0, 0.0).astype(BF16)


def _fill_wrap(buf_ref, wrap_ref, c, wrap, tail_ref, cols=slice(None)):
    q = SSD_CHUNK
    for m in range(wrap // SUBLANES):
        r_cur = (c + 1) * q - wrap + m * SUBLANES
        cur = buf_ref[r_cur:r_cur + SUBLANES, cols]
        if c == 0:
            prv = tail_ref[m * SUBLANES:(m + 1) * SUBLANES, cols]
        else:
            prv = buf_ref[r_cur - q:r_cur - q + SUBLANES, cols]
        sub0 = lax.broadcasted_iota(jnp.int32, cur.shape, 0) == 0
        wrap_ref[c * wrap + m * SUBLANES:c * wrap + (m + 1) * SUBLANES, cols] = jnp.where(
            sub0, pltpu.roll(prv, 1, 0), pltpu.roll(cur, 1, 0))


def _conv_seg(buf_ref, wrap_ref, w_ref, b_ref, c, wrap, c0, cw):
    q = SSD_CHUNK
    taps = w_ref.shape[0]
    cols = slice(c0, c0 + cw)
    acc = b_ref[:, cols] + w_ref[taps - 1:taps, cols] * buf_ref[c * q:(c + 1) * q, cols]
    for k in range(taps - 1):
        back = (taps - 1 - k) * SUBLANES
        shifted = jnp.concatenate(
            [wrap_ref[(c + 1) * wrap - back:(c + 1) * wrap, cols], buf_ref[c * q:(c + 1) * q - back, cols]],
            axis=0)
        acc = acc + w_ref[k:k + 1, cols] * shifted
    return acc


def _seg_tail_rows(wrap):
    n = wrap // SUBLANES
    return [(NPOS - n + m) * SUBLANES + SUBLANES - 1 for m in range(n)]


def _prompt_mixer_kernel(
        x_ref, wz_ref, wxbc_ref, wdt_ref, wlx_ref, wly_ref, wgs_ref, wgl_ref,
        scw_ref, scb_ref, dtb_ref, aneg_ref, dexp_ref, ng_ref, wso_ref,
        lcw_ref, lcb_ref, wax_ref, ba_ref, bx_ref, lam_ref, wlo_ref,
        bg_ref, wo_ref, l1g_ref, l1b_ref, e_ref,
        x1_ref, st_ref, sconv_ref, lst_ref, lconv_ref,
        xb_s, xp_s, xbc_s, swrap_s, stail_s, xc_s, lx_s, lwrap_s, ltail_s, xr_s, ly_s, y_s, z_s, ysb_s, ylb_s,
        ht_s, hl_s, gs_s, gl_s, ys_s, yl_s, o_s, mb_s,
        *, tl, alpha, n_heads, headdim, d_state):
    t = pl.program_id(1)
    nt = pl.num_programs(1)
    d_ssd = n_heads * headdim
    gn = SSD_GROUPS * d_state
    hpg = n_heads // SSD_GROUPS
    gw = hpg * headdim
    d_lru = lx_s.shape[1]
    d_model = x_ref.shape[2]
    q = SSD_CHUNK

    nch = tl // q
    wrap_s = (scw_ref.shape[0] - 1) * SUBLANES
    wrap_l = (lcw_ref.shape[0] - 1) * SUBLANES

    @pl.when(t == 0)
    def _():
        stail_s[...] = jnp.zeros(stail_s.shape, F32)
        ltail_s[...] = jnp.zeros(ltail_s.shape, F32)
        ht_s[...] = jnp.zeros(ht_s.shape, F32)
        hl_s[...] = jnp.zeros(hl_s.shape, F32)

    perm = _perm_matrix()
    perm2 = jnp.concatenate([perm, perm], axis=1)
    for c in range(nch):
        rows = _rows(c, q)
        xp = _dot_2way_lhs(perm2, x_ref[0, rows, :])
        xp_s[rows, :] = xp
        xb_s[rows, :] = xp.astype(BF16)

    cwid = 512

    def proj(dst_ref, w_ref, c0, cw):
        dst_ref[:, c0:c0 + cw] = _dot(xb_s[...], w_ref[:, c0:c0 + cw])

    def lru_conv(c):
        _fill_wrap(lx_s, lwrap_s, c, wrap_l, ltail_s)
        for c0 in range(0, d_lru, cwid):
            xr_s[c * q:(c + 1) * q, c0:c0 + cwid] = _conv_seg(lx_s, lwrap_s, lcw_ref, lcb_ref, c, wrap_l, c0, cwid)
        if c == nch - 1:
            ltail_s[...] = lx_s[tl - wrap_l:tl, :]

    bw = d_lru // LRU_BLOCKS
    sub = lax.broadcasted_iota(jnp.int32, (SUBLANES, bw), 0)
    crow = lax.broadcasted_iota(jnp.int32, (q, bw), 0)

    def lru_block(n):
        cols = slice(n * bw, (n + 1) * bw)
        xr = xr_s[:, cols]
        xrb = xr.astype(BF16)
        rg = _dot(xrb, wax_ref[n])
        r = _sigmoid(rg[:, 0:bw] + ba_ref[:, cols])
        gi = _sigmoid(rg[:, bw:2 * bw] + bx_ref[:, cols])
        log_a = (-LRU_C) * r * _softplus(-lam_ref[:, cols])
        a_all = jnp.exp(log_a)
        mult_all = jnp.sqrt(1.0 - a_all * a_all)
        for c in range(nch):
            a = a_all[c * q:(c + 1) * q, :]
            mult = mult_all[c * q:(c + 1) * q, :]
            if c == 0:
                mult = jnp.where(jnp.logical_and(crow == 0, t == 0), 1.0, mult)
            u = mult * gi[c * q:(c + 1) * q, :] * xr[c * q:(c + 1) * q, :]
            a_p = [a[i * SUBLANES:(i + 1) * SUBLANES, :] for i in range(NPOS)]
            u_p = [u[i * SUBLANES:(i + 1) * SUBLANES, :] for i in range(NPOS)]
            h = u_p[0]
            g = a_p[0]
            for i in range(1, NPOS):
                h = a_p[i] * h + u_p[i]
                g = a_p[i] * g
            gs = jnp.where(sub == 0, 0.0, pltpu.roll(g, 1, 0))
            hs = jnp.where(sub == 0, hl_s[0:1, cols], pltpu.roll(h, 1, 0))
            d = 1
            while d < SUBLANES:
                keep = sub >= d
                hs = jnp.where(keep, gs * pltpu.roll(hs, d, 0) + hs, hs)
                gs = jnp.where(keep, gs * pltpu.roll(gs, d, 0), gs)
                d *= 2
            h = hs
            out = []
            for i in range(NPOS):
                h = a_p[i] * h + u_p[i]
                out.append(h)
            hl_s[0:1, cols] = h[SUBLANES - 1:SUBLANES, :]
            hseq = jnp.concatenate(out, axis=0)
            ylb_s[c * q:(c + 1) * q, cols] = (hseq * _gelu(ly_s[c * q:(c + 1) * q, cols])).astype(BF16)

    def ssd_conv(c0):
        cols = slice(c0, c0 + cwid)
        for c in range(nch):
            _fill_wrap(xbc_s, swrap_s, c, wrap_s, stail_s, cols)
        stail_s[:, cols] = xbc_s[tl - wrap_s:tl, cols]
        for c in range(nch):
            xc_s[c * q:(c + 1) * q, cols] = _silu_of_half(
                _conv_seg(xbc_s, swrap_s, scw_ref, scb_ref, c, wrap_s, c0, cwid))

    def merge_gate(dst_ref, c0, cw, b0):
        dst_ref[:, c0:c0 + cw] = _sigmoid(dst_ref[:, c0:c0 + cw] + bg_ref[:, b0 + c0:b0 + c0 + cw])

    tok_r = _tok_of_row(lax.broadcasted_iota(jnp.int32, (q, q), 0))
    tok_c = _tok_of_row(lax.broadcasted_iota(jnp.int32, (q, q), 1))
    causal = tok_r >= tok_c
    tri = jnp.where(causal, 1.0, 0.0).astype(BF16)
    lane_i = lax.broadcasted_iota(jnp.int32, (q, LANES), 1)
    left = lane_i < headdim

    def chunk_body(c, carry):
        rows = _rows(c, q)
        dt = _softplus(_dot(xb_s[rows, :], wdt_ref[...]) + dtb_ref[...])
        d_a = dt * aneg_ref[...]
        a_cs = _dot_exact_lhs(tri, d_a)
        a_last = a_cs[q - 1:q, :]
        wgt = dt * jnp.exp(a_last - a_cs)
        ea = jnp.exp(a_cs)
        w_exp = _dot_2way_rhs(wgt, e_ref[...])
        ea_exp = _dot_2way_rhs(ea, e_ref[...])
        a_cs_t = a_cs.T
        dt_t = dt.T
        for g in range(SSD_GROUPS):
            b_g = xc_s[rows, d_ssd + g * d_state:d_ssd + (g + 1) * d_state]
            c_g = xc_s[rows, d_ssd + gn + g * d_state:d_ssd + gn + (g + 1) * d_state]
            b_gb = b_g.astype(BF16)
            c_gb = c_g.astype(BF16)
            cb = _dot_nt(c_gb, b_gb)
            for hp in range(hpg // 2):
                c0 = g * gw + hp * 2 * headdim
                xs_pair = xc_s[rows, c0:c0 + 2 * headdim]
                lmats = []
                for j in range(2):
                    h = g * hpg + hp * 2 + j
                    seg = (jnp.broadcast_to(a_cs[:, h:h + 1], (q, q))
                           - jnp.broadcast_to(a_cs_t[h:h + 1, :], (q, q)))
                    dec = jnp.exp(jnp.where(causal, seg, -jnp.inf))
                    lmats.append((cb * dec * jnp.broadcast_to(dt_t[h:h + 1, :], (q, q))).astype(BF16))
                lpair = jnp.concatenate(lmats, axis=1)
                rhs = jnp.concatenate([jnp.where(left, xs_pair, 0.0),
                                       jnp.where(left, 0.0, xs_pair)], axis=0).astype(BF16)
                y_s[rows, c0:c0 + 2 * headdim] = _dot(lpair, rhs)
            gcols = slice(g * gw, (g + 1) * gw)
            h_g = ht_s[:, gcols]
            y_off = _dot(c_gb, h_g.astype(BF16)) * ea_exp[:, gcols]
            y_s[rows, gcols] = y_s[rows, gcols] + y_off
            xw = (xc_s[rows, gcols] * w_exp[:, gcols]).astype(BF16)
            ht_s[:, gcols] = h_g * ea_exp[q - 1:q, gcols] + _dot_tn(b_gb, xw)
            if carry is not None and g < len(carry):
                carry[g]()
        return carry

    def gate_body(i, carry):
        rows = _rows(i, ROW_BLK)
        y = y_s[rows, :] + dexp_ref[...] * xc_s[rows, 0:d_ssd]
        y = y * _silu_of_half(z_s[rows, :])
        ms = jnp.mean(y * y, axis=-1, keepdims=True)
        ysb_s[rows, :] = (y * lax.rsqrt(ms + RMS_EPS) * ng_ref[...]).astype(BF16)
        return carry

    assert nch == 2 and LRU_BLOCKS == 8 and d_lru == 2 * cwid and d_model == 2 * cwid
    d_xbc = xc_s.shape[1]
    nxb = d_xbc // cwid
    proj(lx_s, wlx_ref, 0, cwid)
    proj(lx_s, wlx_ref, cwid, cwid)
    proj(ly_s, wly_ref, 0, cwid); lru_conv(0)
    proj(ly_s, wly_ref, cwid, cwid); lru_conv(1)
    P = functools.partial
    mxu_a = ([P(proj, xbc_s, wxbc_ref, j * cwid, cwid) for j in range(nxb)]
             + [P(proj, gs_s, wgs_ref, j * cwid, cwid) for j in range(2)])
    for n in range(LRU_BLOCKS):
        mxu_a[n]()
        lru_block(n)
        if 1 <= n <= nxb:
            ssd_conv((n - 1) * cwid)
    gates_per_chunk = q // ROW_BLK
    nz = d_ssd // cwid
    chunk_body(0, [P(proj, z_s, wz_ref, j * cwid, cwid) for j in range(nz)])
    merge_gate(gs_s, 0, d_model, 0)
    for i in range(gates_per_chunk):
        gate_body(i, None)

    def wlo_piece(j):
        yl_s[:, j * cwid:(j + 1) * cwid] = _dot(ylb_s[...], wlo_ref[:, j * cwid:(j + 1) * cwid])

    chunk_body(1, [P(proj, gl_s, wgl_ref, 0, cwid), P(proj, gl_s, wgl_ref, cwid, cwid),
                   P(wlo_piece, 0), P(wlo_piece, 1)])
    merge_gate(gl_s, 0, d_model, d_model)
    for i in range(gates_per_chunk, 2 * gates_per_chunk):
        gate_body(i, None)

    ys_s[...] = _dot(ysb_s[...], wso_ref[...])
    for i in range(tl // ROW_BLK):
        rows = _rows(i, ROW_BLK)
        mb_s[rows, :] = (gs_s[rows, :] * ys_s[rows, :] + gl_s[rows, :] * yl_s[rows, :]).astype(BF16)
    o_s[...] = _dot(mb_s[...], wo_ref[...])
    for i in range(tl // ROW_BLK):
        rows = _rows(i, ROW_BLK)
        v = alpha * xp_s[rows, :] + o_s[rows, :]
        x1_ref[0, rows, :] = _layer_norm(v, l1g_ref[...], l1b_ref[...])

    @pl.when(t == nt - 1)
    def _():
        st_ref[0] = ht_s[...].T
        lst_ref[0] = hl_s[0:1, :]
        for m, r in enumerate(_seg_tail_rows(wrap_s)):
            sconv_ref[0, m:m + 1, :] = xbc_s[tl - q + r:tl - q + r + 1, :]
        for m, r in enumerate(_seg_tail_rows(wrap_l)):
            lconv_ref[0, m:m + 1, :] = lx_s[tl - q + r:tl - q + r + 1, :]


def _const_spec(shape):
    nd = len(shape)
    return pl.BlockSpec(shape, lambda *_: (0,) * nd, pipeline_mode=pl.Buffered(1))


def _prompt_mixer(x, wts, tl):
    nb, seq, d_model = x.shape
    n_heads, headdim, d_state = wts['n_heads'], wts['headdim'], wts['d_state']
    d_ssd = n_heads * headdim
    d_xbc = d_ssd + 2 * SSD_GROUPS * d_state
    d_lru = wts['wlx'].shape[1]
    names = ['wz', 'wxbc', 'wdt', 'wlx', 'wly', 'wgs', 'wgl', 'scw', 'scb', 'dtb', 'aneg', 'dexp', 'ng',
             'wso', 'lcw', 'lcb', 'wax', 'ba', 'bx', 'lam', 'wlo', 'bg', 'wo', 'l1g', 'l1b', 'ee']
    consts = [wts[k] for k in names]
    kern = functools.partial(_prompt_mixer_kernel, tl=tl, alpha=wts['alpha'], n_heads=n_heads,
                             headdim=headdim, d_state=d_state)
    out_shape = (
        jax.ShapeDtypeStruct((nb, seq, d_model), F32),
        jax.ShapeDtypeStruct((nb, d_ssd, d_state), F32),
        jax.ShapeDtypeStruct((nb, 3, d_xbc), F32),
        jax.ShapeDtypeStruct((nb, 1, d_lru), F32),
        jax.ShapeDtypeStruct((nb, 3, d_lru), F32),
    )
    out_specs = (
        pl.BlockSpec((1, tl, d_model), lambda b, t: (b, t, 0)),
        pl.BlockSpec((1, d_ssd, d_state), lambda b, t: (b, 0, 0)),
        pl.BlockSpec((1, 3, d_xbc), lambda b, t: (b, 0, 0)),
        pl.BlockSpec((1, 1, d_lru), lambda b, t: (b, 0, 0)),
        pl.BlockSpec((1, 3, d_lru), lambda b, t: (b, 0, 0)),
    )
    nch = tl // SSD_CHUNK
    wrap_s = (wts['scw'].shape[0] - 1) * SUBLANES
    wrap_l = (wts['lcw'].shape[0] - 1) * SUBLANES
    scratch = [
        pltpu.VMEM((tl, d_model), BF16),
        pltpu.VMEM((tl, d_model), F32),
        pltpu.VMEM((tl, d_xbc), F32),
        pltpu.VMEM((nch * wrap_s, d_xbc), F32),
        pltpu.VMEM((wrap_s, d_xbc), F32),
        pltpu.VMEM((tl, d_xbc), F32),
        pltpu.VMEM((tl, d_lru), F32),
        pltpu.VMEM((nch * wrap_l, d_lru), F32),
        pltpu.VMEM((wrap_l, d_lru), F32),
        pltpu.VMEM((tl, d_lru), F32),
        pltpu.VMEM((tl, d_lru), F32),
        pltpu.VMEM((tl, d_ssd), F32),
        pltpu.VMEM((tl, d_ssd), F32),
        pltpu.VMEM((tl, d_ssd), BF16),
        pltpu.VMEM((tl, d_lru), BF16),
        pltpu.VMEM((d_state, d_ssd), F32),
        pltpu.VMEM((SUBLANES, d_lru), F32),
        pltpu.VMEM((tl, d_model), F32),
        pltpu.VMEM((tl, d_model), F32),
        pltpu.VMEM((tl, d_model), F32),
        pltpu.VMEM((tl, d_model), F32),
        pltpu.VMEM((tl, d_model), F32),
        pltpu.VMEM((tl, d_model), BF16),
    ]
    return pl.pallas_call(
        kern,
        grid=(nb, seq // tl),
        in_specs=[pl.BlockSpec((1, tl, d_model), lambda b, t: (b, t, 0))]
        + [_const_spec(c.shape) for c in consts],
        out_specs=out_specs,
        out_shape=out_shape,
        scratch_shapes=scratch,
        compiler_params=pltpu.CompilerParams(
            dimension_semantics=("arbitrary", "arbitrary"),
            vmem_limit_bytes=VMEM_LIMIT_BYTES),
        name="prompt_mixer",
    )(x, *consts)


def _ffn_kernel(x_ref, h0_ref, wg_ref, wu_ref, cw_ref, cb_ref, wd_ref, g_ref, b_ref,
                y_ref, tail_ref, xb_s, gb_s, hb_s, *, tm, stride, hist, alpha, fchunk):
    t = pl.program_id(1)
    nt = pl.num_programs(1)
    taps = cw_ref.shape[0]
    nh = (taps - 1) * stride
    d_ff = gb_s.shape[1]

    @pl.when(t == 0)
    def _():
        gb_s[hist - nh:hist, :] = h0_ref[0]

    xb_s[...] = x_ref[0].astype(BF16)
    acc = None
    for c0 in range(0, d_ff, fchunk):
        cols = slice(c0, c0 + fchunk)
        gb_s[hist:hist + tm, cols] = _dot(xb_s[...], wg_ref[:, cols])
        up = _dot(xb_s[...], wu_ref[:, cols])
        gc = _conv_block(gb_s, cw_ref, cb_ref, hist, stride, 0, tm, c0, fchunk)
        hb_s[...] = (_gelu(gc) * up).astype(BF16)
        part = _dot(hb_s[...], wd_ref[c0 // 2:(c0 + fchunk) // 2, :])
        acc = part if acc is None else acc + part
    v = alpha * x_ref[0] + acc
    y_ref[0] = _layer_norm(v, g_ref[...], b_ref[...])
    gb_s[hist - nh:hist, :] = gb_s[hist + tm - nh:hist + tm, :]

    @pl.when(t == nt - 1)
    def _():
        tail_ref[0] = gb_s[hist - nh:hist, :]


def _ffn_seg_kernel(*refs, tm, alpha, fchunk, state_dims):
    steps = state_dims['steps']
    x_ref, h0_ref, st_ref = refs[0:3]
    cb_refs, xe_refs = refs[3:3 + steps], refs[3 + steps:3 + 2 * steps]
    wg_ref, wu_ref, cw_ref, cb_ref, wd_ref, g_ref, b_ref = refs[3 + 2 * steps:10 + 2 * steps]
    y_ref, tail_ref, nst_ref = refs[10 + 2 * steps:13 + 2 * steps]
    yoff_refs = refs[13 + 2 * steps:13 + 3 * steps]
    xb_s, gb_s, gwrap_s, gtail_s, hb_s, cb8_s, xe8_s, yo8_s = refs[13 + 3 * steps:]

    t = pl.program_id(1)
    pps = st_ref.shape[0]
    step_idx = pl.program_id(0) * pl.num_programs(1) + t
    row0 = lax.rem(step_idx * (2 * pps), SUBLANES)
    for p in range(pps):
        for l in range(steps):
            for e in range(2):
                src = pl.ds(row0 + 2 * p + e, 1)
                cb8_s[p, 2 * l + e:2 * l + e + 1, :] = cb_refs[l][src, :]
                xe8_s[p, 2 * l + e:2 * l + e + 1, :] = xe_refs[l][src, :]
    state_pieces = [
        functools.partial(_state_pair_group, g, st_ref.at[p], cb8_s.at[p], xe8_s.at[p],
                          nst_ref.at[p], yo8_s.at[p], **state_dims)
        for p in range(pps) for g in range(SSD_GROUPS)]

    def state_work(n):
        for _ in range(min(n, len(state_pieces))):
            state_pieces.pop(0)()
    nt = pl.num_programs(1)
    q = SSD_CHUNK
    nch = tm // q
    wrap = (cw_ref.shape[0] - 1) * SUBLANES
    d_ff = gb_s.shape[1]
    tail_rows = _seg_tail_rows(wrap)

    @pl.when(t == 0)
    def _():
        gtail_s[...] = jnp.zeros(gtail_s.shape, F32)
        for m in range(len(tail_rows)):
            r = m * SUBLANES + SUBLANES - 1
            gtail_s[r:r + 1, :] = h0_ref[0, m:m + 1, :]

    xb_s[...] = x_ref[0].astype(BF16)
    def gate_proj(c0):
        cols = slice(c0, c0 + fchunk)
        gb_s[:, cols] = _dot(xb_s[...], wg_ref[:, cols])
        for c in range(nch):
            _fill_wrap(gb_s, gwrap_s, c, wrap, gtail_s, cols)
        gtail_s[:, cols] = gb_s[tm - wrap:tm, cols]

    per_slot = -(-len(state_pieces) // (1 + 2 * (d_ff // fchunk)))
    gate_proj(0)
    state_work(per_slot)
    acc = None
    for c0 in range(0, d_ff, fchunk):
        cols = slice(c0, c0 + fchunk)
        up = _dot(xb_s[...], wu_ref[:, cols])
        if c0 + fchunk < d_ff:
            gate_proj(c0 + fchunk)
        state_work(per_slot)
        for c in range(nch):
            gc = _conv_seg(gb_s, gwrap_s, cw_ref, cb_ref, c, wrap, c0, fchunk)
            hb_s[c * q:(c + 1) * q, :] = (_gelu(gc) * up[c * q:(c + 1) * q, :]).astype(BF16)
        part = _dot(hb_s[...], wd_ref[c0 // 2:(c0 + fchunk) // 2, :])
        state_work(per_slot)
        acc = part if acc is None else acc + part
    state_work(len(state_pieces))
    for p in range(pps):
        for l in range(steps):
            for e in range(2):
                yoff_refs[l][pl.ds(row0 + 2 * p + e, 1), :] = yo8_s[p, 2 * l + e:2 * l + e + 1, :]

    unperm = _unperm_matrix()
    unperm2 = jnp.concatenate([unperm, unperm], axis=1)
    for c in range(nch):
        rows = slice(c * q, (c + 1) * q)
        y = _layer_norm(alpha * x_ref[0, rows, :] + acc[rows, :], g_ref[...], b_ref[...])
        y_ref[0, rows, :] = _dot_2way_lhs(unperm2, y)

    @pl.when(t == nt - 1)
    def _():
        for m, r in enumerate(tail_rows):
            tail_ref[0, m:m + 1, :] = gb_s[tm - q + r:tm - q + r + 1, :]


def _ffn_seg(x, hist0, state, cb_lm, xe_lm, wts, tm, steps):
    nb, seq, d_model = x.shape
    nseq = 2 * state.shape[0]
    d_ssd = xe_lm.shape[1] // 2
    assert cb_lm.shape[0] == steps * nseq and nseq % SUBLANES == 0
    d_ff = wts['wg'].shape[1]
    taps = wts['fcw'].shape[0]
    wrap = (taps - 1) * SUBLANES
    fchunk = 1024
    nt = seq // tm
    npairs = state.shape[0]
    pps = npairs // (nb * nt)
    assert pps * nb * nt == npairs
    consts = [wts[k] for k in ['wg', 'wu', 'fcw', 'fcb', 'wd', 'l2g', 'l2b']]
    state_dims = dict(steps=steps, n_heads=wts['n_heads'], headdim=wts['headdim'], d_state=wts['d_state'])
    kern = functools.partial(_ffn_seg_kernel, tm=tm, alpha=wts['alpha'], fchunk=fchunk, state_dims=state_dims)
    pair_blk = lambda a: pl.BlockSpec((pps,) + tuple(a.shape[1:]),
                                      lambda b, t: (b * nt + t,) + (0,) * (len(a.shape) - 1))

    def rows_blk(l, width):
        return pl.BlockSpec((SUBLANES, width),
                            lambda b, t: (l * (nseq // SUBLANES) + ((b * nt + t) * 2 * pps) // SUBLANES, 0))

    def yoff_blk():
        return pl.BlockSpec((SUBLANES, d_ssd), lambda b, t: (((b * nt + t) * 2 * pps) // SUBLANES, 0))

    outs = pl.pallas_call(
        kern,
        grid=(nb, nt),
        in_specs=[pl.BlockSpec((1, tm, d_model), lambda b, t: (b, t, 0)),
                  pl.BlockSpec((1, taps - 1, d_ff), lambda b, t: (b, 0, 0)),
                  pair_blk(state)]
        + [rows_blk(l, cb_lm.shape[1]) for l in range(steps)]
        + [rows_blk(l, xe_lm.shape[1]) for l in range(steps)]
        + [_const_spec(c.shape) for c in consts],
        out_specs=(pl.BlockSpec((1, tm, d_model), lambda b, t: (b, t, 0)),
                   pl.BlockSpec((1, taps - 1, d_ff), lambda b, t: (b, 0, 0)),
                   pair_blk(state)) + tuple(yoff_blk() for _ in range(steps)),
        out_shape=(jax.ShapeDtypeStruct((nb, seq, d_model), F32),
                   jax.ShapeDtypeStruct((nb, taps - 1, d_ff), F32),
                   jax.ShapeDtypeStruct(state.shape, F32))
        + tuple(jax.ShapeDtypeStruct((nseq, d_ssd), F32) for _ in range(steps)),
        scratch_shapes=[pltpu.VMEM((tm, d_model), BF16),
                        pltpu.VMEM((tm, d_ff), F32),
                        pltpu.VMEM((tm // SSD_CHUNK * wrap, d_ff), F32),
                        pltpu.VMEM((wrap, d_ff), F32),
                        pltpu.VMEM((tm, fchunk), BF16),
                        pltpu.VMEM((pps, 2 * steps, cb_lm.shape[1]), F32),
                        pltpu.VMEM((pps, 2 * steps, xe_lm.shape[1]), F32),
                        pltpu.VMEM((pps, 2 * steps, d_ssd), F32)],
        compiler_params=pltpu.CompilerParams(
            dimension_semantics=("arbitrary", "arbitrary"),
            vmem_limit_bytes=VMEM_LIMIT_BYTES),
        name="conv_ffn_seg",
    )(x, hist0, state, *([cb_lm] * steps), *([xe_lm] * steps), *consts)
    return outs[0], outs[1], outs[2], outs[3:]


def _ffn(x, hist0, wts, tm, stride):
    nb, seq, d_model = x.shape
    d_ff = wts['wg'].shape[1]
    taps = wts['fcw'].shape[0]
    nh = (taps - 1) * stride
    hist = -(-nh // SUBLANES) * SUBLANES
    consts = [wts[k] for k in ['wg', 'wu', 'fcw', 'fcb', 'wd', 'l2g', 'l2b']]
    kern = functools.partial(_ffn_kernel, tm=tm, stride=stride, hist=hist, alpha=wts['alpha'], fchunk=1024)
    return pl.pallas_call(
        kern,
        grid=(nb, seq // tm),
        in_specs=[pl.BlockSpec((1, tm, d_model), lambda b, t: (b, t, 0)),
                  pl.BlockSpec((1, nh, d_ff), lambda b, t: (b, 0, 0))]
        + [_const_spec(c.shape) for c in consts],
        out_specs=(pl.BlockSpec((1, tm, d_model), lambda b, t: (b, t, 0)),
                   pl.BlockSpec((1, nh, d_ff), lambda b, t: (b, 0, 0))),
        out_shape=(jax.ShapeDtypeStruct((nb, seq, d_model), F32),
                   jax.ShapeDtypeStruct((nb, nh, d_ff), F32)),
        scratch_shapes=[pltpu.VMEM((tm, d_model), BF16),
                        pltpu.VMEM((hist + tm, d_ff), F32),
                        pltpu.VMEM((tm, 1024), BF16)],
        compiler_params=pltpu.CompilerParams(
            dimension_semantics=("arbitrary", "arbitrary"),
            vmem_limit_bytes=VMEM_LIMIT_BYTES),
        name="conv_ffn",
    )(x, hist0, *consts)


def _sample_ssd_kernel(
        x_ref, xall_ref, cssd_ref, wxbc_ref, wdt_ref, scw_ref, scb_ref, dtb_ref, aneg_ref, dexp_ref,
        e_ref, gsum_ref, e2_ref,
        pre_ref, cb_ref, xe_ref, ydg_ref,
        xbc_s, xs_s, bs_s, acs_s, dts_s,
        *, nseq, steps, n_heads, headdim, d_state):
    l = pl.program_id(0)
    d_ssd = n_heads * headdim
    gn = SSD_GROUPS * d_state
    hist = (scw_ref.shape[0] - 1) * nseq
    r0 = l * nseq

    def blk(i):
        return pl.ds(pl.multiple_of(i * nseq, nseq), nseq)

    def sblk(i):
        return slice(i * nseq, (i + 1) * nseq)

    @pl.when(l == 0)
    def _():
        xbc_s[0:hist, :] = cssd_ref[...]
        dts_s[...] = _softplus(_dot(xall_ref[...].astype(BF16), wdt_ref[...]) + dtb_ref[...])
        acc = jnp.zeros((nseq, LANES), F32)
        for s in range(steps):
            acc = acc + dts_s[sblk(s), :] * aneg_ref[...]
            acs_s[sblk(s), :] = acc

    xb = x_ref[...].astype(BF16)

    pre = _dot(xb, wxbc_ref[...])
    pre_ref[...] = pre
    xbc_s[pl.ds(pl.multiple_of(hist + r0, nseq), nseq), :] = pre
    cwid = 512
    for c0 in range(0, d_ssd, cwid):
        xs_s[blk(l), c0:c0 + cwid] = _silu_of_half(
            _conv_block(xbc_s, scw_ref, scb_ref, hist, nseq, r0, nseq, c0, cwid))
    b_l = _silu_of_half(_conv_block(xbc_s, scw_ref, scb_ref, hist, nseq, r0, nseq, d_ssd, gn))
    c_l = _silu_of_half(_conv_block(xbc_s, scw_ref, scb_ref, hist, nseq, r0, nseq, d_ssd + gn, gn))
    bs_s[blk(l), :] = b_l
    cb_ref[:, 0:gn] = c_l
    cb_ref[:, gn:2 * gn] = b_l

    a_cs = acs_s[blk(l), :]
    dt = dts_s[blk(l), :]
    a_end = acs_s[sblk(steps - 1), :]
    xe_ref[:, d_ssd:2 * d_ssd] = _dot_2way_rhs(jnp.exp(a_cs), e_ref[...])
    xe_ref[:, 0:d_ssd] = xs_s[blk(l), :] * _dot_2way_rhs(dt * jnp.exp(a_end - a_cs), e_ref[...])

    ydg_ref[...] = dexp_ref[...] * xs_s[blk(l), :]
    for s in range(steps):
        @pl.when(s <= l)
        def _(s=s):
            coef = _dot_2way_rhs(jnp.exp(a_cs - acs_s[sblk(s), :]) * dts_s[sblk(s), :], e_ref[...])
            cbx = _dot_2way_rhs(_dot_2way_rhs(bs_s[sblk(s), :] * c_l, gsum_ref[...]), e2_ref[...])
            ydg_ref[...] += cbx * coef * xs_s[sblk(s), :]


def _sample_lru_kernel(
        x_ref, clru_ref, slru_ref, wlx_ref, wly_ref, lcw_ref, lcb_ref, wax_ref, ba_ref, bx_ref,
        lam_ref, prelx_ref, ylru_ref, lst_ref, lx_s, hl_s, *, nseq, steps, start_pos):
    l = pl.program_id(0)
    hist = (lcw_ref.shape[0] - 1) * nseq
    d_lru = lx_s.shape[1]
    r0 = l * nseq

    @pl.when(l == 0)
    def _():
        lx_s[0:hist, :] = clru_ref[...]
        hl_s[...] = slru_ref[...]

    xb = x_ref[...].astype(BF16)
    prelx = _dot(xb, wlx_ref[...])
    prelx_ref[...] = prelx
    lx_s[pl.ds(pl.multiple_of(hist + r0, nseq), nseq), :] = prelx
    ly = _dot(xb, wly_ref[...])
    bw = d_lru // LRU_BLOCKS
    first = (l + start_pos) == 0
    for n in range(LRU_BLOCKS):
        cols = slice(n * bw, (n + 1) * bw)
        xr = _conv_block(lx_s, lcw_ref, lcb_ref, hist, nseq, r0, nseq, n * bw, bw)
        xrb = xr.astype(BF16)
        rg = _dot(xrb, wax_ref[n])
        r = _sigmoid(rg[:, 0:bw] + ba_ref[:, cols])
        gi = _sigmoid(rg[:, bw:2 * bw] + bx_ref[:, cols])
        log_a = (-LRU_C) * r * _softplus(-lam_ref[:, cols])
        a = jnp.exp(log_a)
        mult = jnp.where(first, 1.0, jnp.sqrt(1.0 - a * a))
        h = a * hl_s[:, cols] + mult * gi * xr
        hl_s[:, cols] = h
        ylru_ref[:, cols] = (h * _gelu(ly[:, cols])).astype(BF16)

    @pl.when(l == steps - 1)
    def _():
        lst_ref[...] = hl_s[...]


def _state_pair_group(g, st_ref, cb8_ref, xe8_ref, nst_ref, yoff8_ref,
                      *, steps, n_heads, headdim, d_state):
    nrow = 2 * steps
    hpg = n_heads // SSD_GROUPS
    gw = hpg * headdim
    gn = SSD_GROUPS * d_state
    d_ssd = n_heads * headdim
    assert 2 * headdim == LANES and d_state == LANES
    par = lax.broadcasted_iota(jnp.int32, (nrow, gw), 0) % 2
    low = lax.broadcasted_iota(jnp.int32, (nrow, LANES), 1) < headdim
    gcols = slice(g * gw, (g + 1) * gw)
    c8g = cb8_ref[:, g * d_state:(g + 1) * d_state].astype(BF16)
    b8g = cb8_ref[:, gn + g * d_state:gn + (g + 1) * d_state].astype(BF16)
    xw8 = xe8_ref[:, g * gw:(g + 1) * gw]
    ea8 = xe8_ref[:, d_ssd + g * gw:d_ssd + (g + 1) * gw]
    cds = []
    for hp in range(hpg // 2):
        pair = ea8[:, hp * LANES:(hp + 1) * LANES]
        swapped = pltpu.roll(pair, headdim, 1)
        cds.append(jnp.where(low, pair, swapped))
        cds.append(jnp.where(low, swapped, pair))
    yo = None
    for e in range(2):
        sg = st_ref[e, gcols, :]
        yo_e = _dot_nt(c8g, sg.astype(BF16))
        yo = yo_e if e == 0 else jnp.where(par == e, yo_e, yo)
        xw_e = jnp.where(par == e, xw8, 0.0).astype(BF16)
        upd = _dot_tn(xw_e, b8g)
        k_last = 2 * (steps - 1) + e
        for hh in range(hpg):
            cd = cds[hh][k_last:k_last + 1, :]
            hr = slice(hh * headdim, (hh + 1) * headdim)
            nst_ref[e, g * gw + hh * headdim:g * gw + (hh + 1) * headdim, :] = sg[hr, :] * cd + upd[hr, :]
    yoff8_ref[:, gcols] = yo * ea8


def _sample_post_kernel(x_ref, ydg_ref, yoff_ref, ylru_ref, wz_ref, ng_ref, wso_ref, wlo_ref,
                        wgs_ref, wgl_ref, bg_ref, wo_ref, l1g_ref, l1b_ref, x1_ref, *, alpha):
    d_model = x_ref.shape[1]
    xb = x_ref[...].astype(BF16)
    y = (ydg_ref[...] + yoff_ref[...]) * _silu_of_half(_dot(xb, wz_ref[...]))
    ms = jnp.mean(y * y, axis=-1, keepdims=True)
    ysb = (y * lax.rsqrt(ms + RMS_EPS) * ng_ref[...]).astype(BF16)
    g_ssd = _sigmoid(_dot(xb, wgs_ref[...]) + bg_ref[:, 0:d_model])
    g_lru = _sigmoid(_dot(xb, wgl_ref[...]) + bg_ref[:, d_model:2 * d_model])
    merged = g_ssd * _dot(ysb, wso_ref[...]) + g_lru * _dot(ylru_ref[...], wlo_ref[...])
    o = _dot(merged.astype(BF16), wo_ref[...])
    x1_ref[...] = _layer_norm(alpha * x_ref[...] + o, l1g_ref[...], l1b_ref[...])


def _sample_front(x_lm, cssd_lm, clru_lm, slru_lm, wts, nseq, steps, start_pos):
    n_heads, headdim, d_state = wts['n_heads'], wts['headdim'], wts['d_state']
    d_model = x_lm.shape[1]
    d_ssd = n_heads * headdim
    gn = SSD_GROUPS * d_state
    d_xbc = d_ssd + 2 * gn
    d_lru = wts['wlx'].shape[1]
    ntok = steps * nseq
    dims = dict(nseq=nseq, steps=steps, n_heads=n_heads, headdim=headdim, d_state=d_state)
    params = pltpu.CompilerParams(dimension_semantics=("arbitrary",), vmem_limit_bytes=VMEM_LIMIT_BYTES)
    step_blk = lambda w: pl.BlockSpec((nseq, w), lambda l: (l, 0))

    sds = jax.ShapeDtypeStruct
    ssd_names = ['wxbc', 'wdt', 'scw', 'scb', 'dtb', 'aneg', 'dexp', 'ee', 'gsum2', 'gexp2']
    ssd_consts = [x_lm, cssd_lm] + [wts[k] for k in ssd_names]
    pre, cb_lm, xe_lm, ydg_lm = pl.pallas_call(
        functools.partial(_sample_ssd_kernel, **dims),
        grid=(steps,),
        in_specs=[step_blk(d_model)] + [_const_spec(c.shape) for c in ssd_consts],
        out_specs=(step_blk(d_xbc), step_blk(2 * gn), step_blk(2 * d_ssd), step_blk(d_ssd)),
        out_shape=(sds((ntok, d_xbc), F32), sds((ntok, 2 * gn), F32), sds((ntok, 2 * d_ssd), F32),
                   sds((ntok, d_ssd), F32)),
        scratch_shapes=[
            pltpu.VMEM((cssd_lm.shape[0] + ntok, d_xbc), F32),
            pltpu.VMEM((ntok, d_ssd), F32),
            pltpu.VMEM((ntok, gn), F32),
            pltpu.VMEM((ntok, LANES), F32),
            pltpu.VMEM((ntok, LANES), F32),
        ],
        compiler_params=params,
        name="sample_ssd",
    )(x_lm, *ssd_consts)

    lru_names = ['wlx', 'wly', 'lcw', 'lcb', 'wax', 'ba', 'bx', 'lam']
    lru_consts = [clru_lm, slru_lm] + [wts[k] for k in lru_names]
    prelx, ylru_lm, lst = pl.pallas_call(
        functools.partial(_sample_lru_kernel, nseq=nseq, steps=steps, start_pos=start_pos),
        grid=(steps,),
        in_specs=[step_blk(d_model)] + [_const_spec(c.shape) for c in lru_consts],
        out_specs=(step_blk(d_lru), step_blk(d_lru), pl.BlockSpec((nseq, d_lru), lambda l: (0, 0))),
        out_shape=(sds((ntok, d_lru), F32), sds((ntok, d_lru), BF16), sds((nseq, d_lru), F32)),
        scratch_shapes=[
            pltpu.VMEM((clru_lm.shape[0] + ntok, d_lru), F32),
            pltpu.VMEM((nseq, d_lru), F32),
        ],
        compiler_params=params,
        name="sample_lru",
    )(x_lm, *lru_consts)

    return dict(pre=pre, cb=cb_lm, xe=xe_lm, ydg=ydg_lm, prelx=prelx, ylru=ylru_lm, lst=lst)


def _sample_back(x_lm, ydg_lm, yoff_lm, ylru_lm, wts, nseq, steps):
    d_model = x_lm.shape[1]
    d_ssd = wts['n_heads'] * wts['headdim']
    d_lru = wts['wlx'].shape[1]
    ntok = steps * nseq
    params = pltpu.CompilerParams(dimension_semantics=("arbitrary",), vmem_limit_bytes=VMEM_LIMIT_BYTES)
    step_blk = lambda w: pl.BlockSpec((nseq, w), lambda l: (l, 0))
    sds = jax.ShapeDtypeStruct
    post_names = ['wz', 'ng', 'wso', 'wlo', 'wgs', 'wgl', 'bg', 'wo', 'l1g', 'l1b']
    post_consts = [wts[k] for k in post_names]
    x1_lm = pl.pallas_call(
        functools.partial(_sample_post_kernel, alpha=wts['alpha']),
        grid=(steps,),
        in_specs=[step_blk(d_model), step_blk(d_ssd), step_blk(d_ssd), step_blk(d_lru)]
        + [_const_spec(c.shape) for c in post_consts],
        out_specs=step_blk(d_model),
        out_shape=sds((ntok, d_model), F32),
        compiler_params=params,
        name="sample_post",
    )(x_lm, ydg_lm, yoff_lm, ylru_lm, *post_consts)
    return x1_lm


def _prep_weights(w_in, b_gate, ssd_conv_w, ssd_conv_b, ssd_dt_bias, ssd_a_log, ssd_d, ssd_norm_g,
                  w_ssd_out, lru_conv_w, lru_conv_b, lru_wa, lru_ba, lru_wx, lru_bx, lru_lambda,
                  w_lru_out, w_o, ln1_g, ln1_b, ffn_w_gate, ffn_w_up, ffn_conv_w, ffn_conv_b,
                  ffn_w_down, ln2_g, ln2_b, n_heads, headdim, d_state):
    depth = w_in.shape[0]
    d_model = w_in.shape[1]
    d_ssd = n_heads * headdim
    d_xbc = d_ssd + 2 * SSD_GROUPS * d_state
    d_lru = lru_lambda.shape[1]
    sizes = (d_ssd, d_xbc, n_heads, d_lru, d_lru, d_model, d_model)
    cuts = np.cumsum((0,) + sizes)
    scales = [0.5 if i == 0 else 1.0 for i in range(len(sizes))]
    parts = _pack_w_in_t(jnp.swapaxes(w_in, 1, 2)[0].astype(F32), [int(c) for c in cuts], 2, scales)
    row = lambda v: v.reshape(1, -1).astype(F32)
    wax = jnp.concatenate([lru_wa[0], lru_wx[0]], axis=-1)
    wg, wu, wlo, wo, wax_p = _pack_weights([ffn_w_gate[0], ffn_w_up[0], w_lru_out[0], w_o[0],
                                            wax.reshape(-1, wax.shape[-1])])
    wax = wax_p.reshape(wax.shape[0], wax.shape[1] // 2, wax.shape[2])
    wso, = _pack_weights([w_ssd_out[0]])
    wd, = _pack_weights([ffn_w_down[0]])
    pad_heads = lambda v: jnp.pad(v.reshape(1, -1).astype(F32), ((0, 0), (0, LANES - n_heads)))
    head_of_col = np.arange(d_ssd) // headdim
    expand = (np.arange(LANES)[:, None] == head_of_col[None, :]).astype(np.float32)
    group_of_col = head_of_col // (n_heads // SSD_GROUPS)
    group_sum = (np.arange(SSD_GROUPS * d_state)[:, None] // d_state
                 == np.arange(LANES)[None, :])
    group_exp = np.arange(LANES)[:, None] == group_of_col[None, :]
    return dict(
        n_heads=n_heads, headdim=headdim, d_state=d_state,
        alpha=float((2.0 * depth) ** 0.25),
        wz=parts[0], wxbc=parts[1], wdt=parts[2], wlx=parts[3], wly=parts[4], wgs=parts[5], wgl=parts[6],
        scw=0.5 * ssd_conv_w[0].astype(F32), scb=0.5 * row(ssd_conv_b[0]),
        dtb=pad_heads(ssd_dt_bias[0]), aneg=pad_heads(-jnp.exp(ssd_a_log[0].astype(F32))),
        dexp=row(jnp.repeat(ssd_d[0], headdim)), ng=row(ssd_norm_g[0]),
        wso=wso,
        lcw=lru_conv_w[0].astype(F32), lcb=row(lru_conv_b[0]),
        ba=row(lru_ba[0]), bx=row(lru_bx[0]), wax=wax,
        lam=row(lru_lambda[0]), wlo=wlo,
        bg=row(b_gate[0]), wo=wo, l1g=row(ln1_g[0]), l1b=row(ln1_b[0]),
        ee=_pack_rows_01(np.concatenate([expand, expand], axis=0)),
        gsum2=_pack_rows_01(np.concatenate([group_sum, group_sum], axis=0)),
        gexp2=_pack_rows_01(np.concatenate([group_exp, group_exp], axis=0)),
        wg=wg, wu=wu,
        fcw=ffn_conv_w[0].astype(F32), fcb=row(ffn_conv_b[0]), wd=wd,
        l2g=row(ln2_g[0]), l2b=row(ln2_b[0]),
    )


def kernel(x_prompt, x_sample, state_ssd, cache_ssd_conv, state_lru, cache_lru_conv, cache_ffn_conv, w_in, b_gate, ssd_conv_w, ssd_conv_b, ssd_dt_bias, ssd_a_log, ssd_d, ssd_norm_g, w_ssd_out, lru_conv_w, lru_conv_b, lru_wa, lru_ba, lru_wx, lru_bx, lru_lambda, w_lru_out, w_o, ln1_g, ln1_b, ffn_w_gate, ffn_w_up, ffn_conv_w, ffn_conv_b, ffn_w_down, ln2_g, ln2_b):
    assert w_in.shape[0] == 1, "single-layer trunk"
    _, _, n_heads, headdim, d_state = state_ssd.shape
    wts = _prep_weights(w_in, b_gate, ssd_conv_w, ssd_conv_b, ssd_dt_bias, ssd_a_log, ssd_d, ssd_norm_g,
                        w_ssd_out, lru_conv_w, lru_conv_b, lru_wa, lru_ba, lru_wx, lru_bx, lru_lambda,
                        w_lru_out, w_o, ln1_g, ln1_b, ffn_w_gate, ffn_w_up, ffn_conv_w, ffn_conv_b,
                        ffn_w_down, ln2_g, ln2_b, n_heads, headdim, d_state)
    bp = x_prompt.shape[0]
    d_ff = ffn_w_gate.shape[2]

    nb_s, steps, _ = x_sample.shape
    half = nb_s // 2

    def to_lm(a):
        return jnp.swapaxes(a, 0, 1).reshape(a.shape[1] * nb_s, a.shape[2])

    def from_lm(a, k):
        return jnp.swapaxes(a.reshape(k, nb_s, a.shape[1]), 0, 1)

    d_ssd = n_heads * headdim
    x_lm = to_lm(x_sample)
    sf = _sample_front(x_lm, to_lm(cache_ssd_conv[0]), to_lm(cache_lru_conv[0]),
                       to_lm(state_lru[0][:, None, :]), wts, nb_s, steps, PAST_LEN)
    pre, lst, prelx = sf['pre'], sf['lst'], sf['prelx']

    x1_p, p_ssd, p_ssd_buf, p_lru, p_lru_buf = _prompt_mixer(x_prompt, wts, tl=MIXER_TILE)
    y_prompt, p_ffn_buf, new_state, yoff_steps = _ffn_seg(
        x1_p, jnp.zeros((bp, ffn_conv_w.shape[1] - 1, d_ff), F32),
        state_ssd[0].reshape(half, 2, d_ssd, d_state), sf['cb'], sf['xe'], wts, tm=FFN_TILE, steps=steps)
    p_ssd = p_ssd.reshape(1, bp, n_heads, headdim, d_state)

    x1_lm = _sample_back(x_lm, sf['ydg'], jnp.concatenate(yoff_steps, axis=0), sf['ylru'], wts, nb_s, steps)
    y_lm, tail = _ffn(x1_lm[None], to_lm(cache_ffn_conv[0])[None], wts, tm=steps * nb_s, stride=nb_s)
    k_ssd = ssd_conv_w.shape[1] - 1
    k_lru = lru_conv_w.shape[1] - 1
    k_ffn = ffn_conv_w.shape[1] - 1
    assert steps >= max(k_ssd, k_lru, k_ffn)
    return (y_prompt, from_lm(y_lm[0], steps), p_ssd, p_ssd_buf[None], p_lru.reshape(1, bp, -1), p_lru_buf[None],
            p_ffn_buf[None],
            new_state.reshape(1, nb_s, n_heads, headdim, d_state),
            from_lm(pre[(steps - k_ssd) * nb_s:], k_ssd)[None],
            from_lm(lst, 1).reshape(1, nb_s, -1),
            from_lm(prelx[(steps - k_lru) * nb_s:], k_lru)[None],
            from_lm(tail[0], k_ffn)[None])
```

```python
import functools

import numpy as np
import jax
import jax.numpy as jnp
from jax import lax
from jax.experimental import pallas as pl
from jax.experimental.pallas import tpu as pltpu

F32 = jnp.float32
BF16 = jnp.bfloat16

SSD_GROUPS = 4
SSD_CHUNK = 128
LRU_BLOCKS = 8
LRU_C = 8.0
LN_EPS = 1e-5
RMS_EPS = 1e-6
PAST_LEN = 16384

LANES = 128
SUBLANES = 8
VMEM_LIMIT_BYTES = 60 * 1024 * 1024

ROW_BLK = 32
MIXER_TILE = 2 * SSD_CHUNK
FFN_TILE = 4 * SSD_CHUNK


def _dot(a, b):
    if b.dtype == jnp.uint32:
        b = pltpu.bitcast(b, BF16)
    return jnp.dot(a, b, preferred_element_type=F32)


def _pack_kernel(*refs):
    n = len(refs) // 2
    for w_ref, o_ref in zip(refs[:n], refs[n:]):
        o_ref[...] = pltpu.bitcast(w_ref[...].astype(BF16), jnp.uint32)


def _pack_weights(ws):
    k = ws[0].shape[0]
    assert all(w.shape[0] == k for w in ws)
    bk = 512 if k >= 2048 else 256
    return pl.pallas_call(
        _pack_kernel,
        grid=(k // bk,),
        in_specs=[pl.BlockSpec((bk, w.shape[1]), lambda i: (i, 0)) for w in ws],
        out_specs=tuple(pl.BlockSpec((bk // 2, w.shape[1]), lambda i: (i, 0)) for w in ws),
        out_shape=tuple(jax.ShapeDtypeStruct((k // 2, w.shape[1]), jnp.uint32) for w in ws),
        compiler_params=pltpu.CompilerParams(dimension_semantics=("arbitrary",),
                                             vmem_limit_bytes=VMEM_LIMIT_BYTES),
        name="pack_weight",
    )(*[w.astype(F32) for w in ws])


def _pack_t_kernel(wt_ref, wdt_ref, *out_refs, parts, bn, dt_rows):
    j = pl.program_id(0)
    first = 0
    for o_ref, (nblk, scale) in zip(out_refs, parts):
        @pl.when(jnp.logical_and(j >= first, j < first + nblk))
        def _(o_ref=o_ref, scale=scale):
            v = wt_ref[...].T
            if scale != 1.0:
                v = v * scale
            o_ref[...] = pltpu.bitcast(v.astype(BF16), jnp.uint32)
        first += nblk

    @pl.when(j == 0)
    def _():
        rows = lax.broadcasted_iota(jnp.int32, wdt_ref.shape, 0)
        v = jnp.where(rows < dt_rows, wdt_ref[...], 0.0)
        out_refs[-1][...] = pltpu.bitcast(v.T.astype(BF16), jnp.uint32)


def _pack_w_in_t(wt, cuts, dt_index, scales, bn=1024):
    n, k = wt.shape
    parts, offs = [], []
    for i in range(len(cuts) - 1):
        if i == dt_index:
            continue
        nblk = (cuts[i + 1] - cuts[i]) // bn
        assert nblk * bn == cuts[i + 1] - cuts[i]
        parts.append((nblk, scales[i]))
        offs += [cuts[i] + b * bn for b in range(nblk)]
    offs = np.asarray(offs, np.int32)
    firsts = np.cumsum([0] + [p[0] for p in parts])

    def row_off(j):
        off = jnp.int32(int(offs[0]))
        for idx in range(1, len(offs)):
            off = jnp.where(j >= idx, jnp.int32(int(offs[idx])), off)
        return off

    def out_map(p):
        return lambda j: (0, jnp.clip(j - int(firsts[p]), 0, parts[p][0] - 1))

    dt_rows = cuts[dt_index + 1] - cuts[dt_index]
    outs = pl.pallas_call(
        functools.partial(_pack_t_kernel, parts=tuple(parts), bn=bn, dt_rows=dt_rows),
        grid=(len(offs),),
        in_specs=[pl.BlockSpec((pl.Element(bn), pl.Element(k)), lambda j: (pl.multiple_of(row_off(j), SUBLANES), 0)),
                  pl.BlockSpec((pl.Element(LANES), pl.Element(k)), lambda j: (cuts[dt_index], 0))],
        out_specs=tuple(pl.BlockSpec((k // 2, bn), out_map(p)) for p in range(len(parts)))
        + (pl.BlockSpec((k // 2, LANES), lambda j: (0, 0)),),
        out_shape=tuple(jax.ShapeDtypeStruct((k // 2, nb_ * bn), jnp.uint32) for nb_, _ in parts)
        + (jax.ShapeDtypeStruct((k // 2, LANES), jnp.uint32),),
        compiler_params=pltpu.CompilerParams(dimension_semantics=("arbitrary",),
                                             vmem_limit_bytes=VMEM_LIMIT_BYTES),
        name="pack_w_in",
    )(wt, wt)
    outs = list(outs)
    dt_part = outs.pop()
    outs.insert(dt_index, dt_part)
    return outs


def _pack_rows_01(m):
    bits = np.ascontiguousarray(m, np.float32).view(np.uint32) >> 16
    return jnp.asarray(bits[0::2] | (bits[1::2] << 16), jnp.uint32)


def _dot_nt(a, b):
    return lax.dot_general(a, b, (((1,), (1,)), ((), ())), preferred_element_type=F32)


def _dot_tn(a, b):
    return lax.dot_general(a, b, (((0,), (0,)), ((), ())), preferred_element_type=F32)


def _split3(v):
    hi = v.astype(BF16)
    r1 = v - hi.astype(F32)
    mid = r1.astype(BF16)
    lo = (r1 - mid.astype(F32)).astype(BF16)
    return hi, mid, lo


def _dot_exact_lhs(m, v):
    hi, mid, lo = _split3(v)
    return _dot(m, hi) + _dot(m, mid) + _dot(m, lo)


def _split2(v):
    hi = v.astype(BF16)
    lo = (v - hi.astype(F32)).astype(BF16)
    return hi, lo


def _dot_2way_rhs(v, m2):
    return _dot(jnp.concatenate(_split2(v), axis=1), m2)


def _dot_2way_lhs(m2, v):
    return _dot(m2, jnp.concatenate(_split2(v), axis=0))


def _softplus(x):
    return jnp.maximum(x, 0.0) + jnp.log1p(jnp.exp(-jnp.abs(x)))


def _sigmoid(x):
    return 0.5 * jnp.tanh(0.5 * x) + 0.5


def _silu_of_half(h):
    return h + h * jnp.tanh(h)


def _gelu(x):
    c = np.sqrt(2.0 / np.pi).astype(np.float32)
    return 0.5 * x * (1.0 + jnp.tanh(c * (x + 0.044715 * (x * x * x))))


def _layer_norm(v, g, b):
    mu = jnp.mean(v, axis=-1, keepdims=True)
    d = v - mu
    var = jnp.mean(d * d, axis=-1, keepdims=True)
    return d * lax.rsqrt(var + LN_EPS) * g + b


def _rows(i, n):
    if isinstance(i, int):
        return slice(i * n, (i + 1) * n)
    return pl.ds(pl.multiple_of(i * n, n), n)


def _conv_block(buf_ref, w_ref, b_ref, hist, stride, r0, rows, c0, cw):
    taps = w_ref.shape[0]
    acc = b_ref[:, c0:c0 + cw]
    for k in range(taps):
        off = hist + r0 - (taps - 1 - k) * stride
        if not isinstance(off, int):
            off = pl.multiple_of(off, SUBLANES)
        acc = acc + w_ref[k:k + 1, c0:c0 + cw] * buf_ref[pl.ds(off, rows), c0:c0 + cw]
    return acc


LOG2_SUBLANES = 3
NPOS = SSD_CHUNK // SUBLANES
LOG2_NPOS = 4
assert 1 << LOG2_SUBLANES == SUBLANES and 1 << LOG2_NPOS == NPOS


def _tok_of_row(r):
    return (r & (SUBLANES - 1)) * NPOS + lax.shift_right_logical(r, LOG2_SUBLANES)


def _row_of_tok(t):
    return (t & (NPOS - 1)) * SUBLANES + lax.shift_right_logical(t, LOG2_NPOS)


def _perm_matrix():
    q = SSD_CHUNK
    r = lax.broadcasted_iota(jnp.int32, (q, q), 0)
    c = lax.broadcasted_iota(jnp.int32, (q, q), 1)
    return jnp.where(c == _tok_of_row(r), 1.0, 0.0).astype(BF16)


def _unperm_matrix():
    q = SSD_CHUNK
    t = lax.broadcasted_iota(jnp.int32, (q, q), 0)
    r = lax.broadcasted_iota(jnp.int32, (q, q), 1)
    return jnp.where(r == _row_of_tok(t), 1.0, 0.0).astype(BF16)


def _fill_wrap(buf_ref, wrap_ref, c, wrap, tail_ref, cols=slice(None)):
    q = SSD_CHUNK
    for m in range(wrap // SUBLANES):
        r_cur = (c + 1) * q - wrap + m * SUBLANES
        cur = buf_ref[r_cur:r_cur + SUBLANES, cols]
        if c == 0:
            prv = tail_ref[m * SUBLANES:(m + 1) * SUBLANES, cols]
        else:
            prv = buf_ref[r_cur - q:r_cur - q + SUBLANES, cols]
        sub0 = lax.broadcasted_iota(jnp.int32, cur.shape, 0) == 0
        wrap_ref[c * wrap + m * SUBLANES:c * wrap + (m + 1) * SUBLANES, cols] = jnp.where(
            sub0, pltpu.roll(prv, 1, 0), pltpu.roll(cur, 1, 0))


def _conv_seg(buf_ref, wrap_ref, w_ref, b_ref, c, wrap, c0, cw):
    q = SSD_CHUNK
    taps = w_ref.shape[0]
    cols = slice(c0, c0 + cw)
    acc = b_ref[:, cols] + w_ref[taps - 1:taps, cols] * buf_ref[c * q:(c + 1) * q, cols]
    for k in range(taps - 1):
        back = (taps - 1 - k) * SUBLANES
        shifted = jnp.concatenate(
            [wrap_ref[(c + 1) * wrap - back:(c + 1) * wrap, cols], buf_ref[c * q:(c + 1) * q - back, cols]],
            axis=0)
        acc = acc + w_ref[k:k + 1, cols] * shifted
    return acc


def _seg_tail_rows(wrap):
    n = wrap // SUBLANES
    return [(NPOS - n + m) * SUBLANES + SUBLANES - 1 for m in range(n)]


def _prompt_mixer_kernel(
        x_ref, wz_ref, wxbc_ref, wdt_ref, wlx_ref, wly_ref, wgs_ref, wgl_ref,
        scw_ref, scb_ref, dtb_ref, aneg_ref, dexp_ref, ng_ref, wso_ref,
        lcw_ref, lcb_ref, wax_ref, ba_ref, bx_ref, lam_ref, wlo_ref,
        bg_ref, wo_ref, l1g_ref, l1b_ref, e_ref,
        x1_ref, st_ref, sconv_ref, lst_ref, lconv_ref,
        xb_s, xp_s, xbc_s, swrap_s, stail_s, xc_s, lx_s, lwrap_s, ltail_s, xr_s, ly_s, y_s, z_s, ysb_s, ylb_s,
        ht_s, hl_s, gs_s, gl_s, ys_s, yl_s, o_s, mb_s,
        *, tl, alpha, n_heads, headdim, d_state):
    t = pl.program_id(1)
    nt = pl.num_programs(1)
    d_ssd = n_heads * headdim
    gn = SSD_GROUPS * d_state
    hpg = n_heads // SSD_GROUPS
    gw = hpg * headdim
    d_lru = lx_s.shape[1]
    d_model = x_ref.shape[2]
    q = SSD_CHUNK

    nch = tl // q
    wrap_s = (scw_ref.shape[0] - 1) * SUBLANES
    wrap_l = (lcw_ref.shape[0] - 1) * SUBLANES

    @pl.when(t == 0)
    def _():
        stail_s[...] = jnp.zeros(stail_s.shape, F32)
        ltail_s[...] = jnp.zeros(ltail_s.shape, F32)
        ht_s[...] = jnp.zeros(ht_s.shape, F32)
        hl_s[...] = jnp.zeros(hl_s.shape, F32)

    perm = _perm_matrix()
    perm2 = jnp.concatenate([perm, perm], axis=1)
    for c in range(nch):
        rows = _rows(c, q)
        xp = _dot_2way_lhs(perm2, x_ref[0, rows, :])
        xp_s[rows, :] = xp
        xb_s[rows, :] = xp.astype(BF16)

    cwid = 512

    def proj(dst_ref, w_ref, c0, cw):
        dst_ref[:, c0:c0 + cw] = _dot(xb_s[...], w_ref[:, c0:c0 + cw])

    def lru_conv(c):
        _fill_wrap(lx_s, lwrap_s, c, wrap_l, ltail_s)
        for c0 in range(0, d_lru, cwid):
            xr_s[c * q:(c + 1) * q, c0:c0 + cwid] = _conv_seg(lx_s, lwrap_s, lcw_ref, lcb_ref, c, wrap_l, c0, cwid)
        if c == nch - 1:
            ltail_s[...] = lx_s[tl - wrap_l:tl, :]

    bw = d_lru // LRU_BLOCKS
    sub = lax.broadcasted_iota(jnp.int32, (SUBLANES, bw), 0)
    crow = lax.broadcasted_iota(jnp.int32, (q, bw), 0)

    def lru_block(n):
        cols = slice(n * bw, (n + 1) * bw)
        xr = xr_s[:, cols]
        xrb = xr.astype(BF16)
        rg = _dot(xrb, wax_ref[n])
        r = _sigmoid(rg[:, 0:bw] + ba_ref[:, cols])
        gi = _sigmoid(rg[:, bw:2 * bw] + bx_ref[:, cols])
        log_a = (-LRU_C) * r * _softplus(-lam_ref[:, cols])
        a_all = jnp.exp(log_a)
        mult_all = jnp.sqrt(1.0 - a_all * a_all)
        for c in range(nch):
            a = a_all[c * q:(c + 1) * q, :]
            mult = mult_all[c * q:(c + 1) * q, :]
            if c == 0:
                mult = jnp.where(jnp.logical_and(crow == 0, t == 0), 1.0, mult)
            u = mult * gi[c * q:(c + 1) * q, :] * xr[c * q:(c + 1) * q, :]
            a_p = [a[i * SUBLANES:(i + 1) * SUBLANES, :] for i in range(NPOS)]
            u_p = [u[i * SUBLANES:(i + 1) * SUBLANES, :] for i in range(NPOS)]
            h = u_p[0]
            g = a_p[0]
            for i in range(1, NPOS):
                h = a_p[i] * h + u_p[i]
                g = a_p[i] * g
            gs = jnp.where(sub == 0, 0.0, pltpu.roll(g, 1, 0))
            hs = jnp.where(sub == 0, hl_s[0:1, cols], pltpu.roll(h, 1, 0))
            d = 1
            while d < SUBLANES:
                keep = sub >= d
                hs = jnp.where(keep, gs * pltpu.roll(hs, d, 0) + hs, hs)
                gs = jnp.where(keep, gs * pltpu.roll(gs, d, 0), gs)
                d *= 2
            h = hs
            out = []
            for i in range(NPOS):
                h = a_p[i] * h + u_p[i]
                out.append(h)
            hl_s[0:1, cols] = h[SUBLANES - 1:SUBLANES, :]
            hseq = jnp.concatenate(out, axis=0)
            ylb_s[c * q:(c + 1) * q, cols] = (hseq * _gelu(ly_s[c * q:(c + 1) * q, cols])).astype(BF16)

    def ssd_conv(c0):
        cols = slice(c0, c0 + cwid)
        for c in range(nch):
            _fill_wrap(xbc_s, swrap_s, c, wrap_s, stail_s, cols)
        stail_s[:, cols] = xbc_s[tl - wrap_s:tl, cols]
        for c in range(nch):
            xc_s[c * q:(c + 1) * q, cols] = _silu_of_half(
                _conv_seg(xbc_s, swrap_s, scw_ref, scb_ref, c, wrap_s, c0, cwid))

    def merge_gate(dst_ref, c0, cw, b0):
        dst_ref[:, c0:c0 + cw] = _sigmoid(dst_ref[:, c0:c0 + cw] + bg_ref[:, b0 + c0:b0 + c0 + cw])

    tok_r = _tok_of_row(lax.broadcasted_iota(jnp.int32, (q, q), 0))
    tok_c = _tok_of_row(lax.broadcasted_iota(jnp.int32, (q, q), 1))
    causal = tok_r >= tok_c
    tri = jnp.where(causal, 1.0, 0.0).astype(BF16)
    lane_i = lax.broadcasted_iota(jnp.int32, (q, LANES), 1)
    left = lane_i < headdim

    def chunk_body(c, carry):
        rows = _rows(c, q)
        dt = _softplus(_dot(xb_s[rows, :], wdt_ref[...]) + dtb_ref[...])
        d_a = dt * aneg_ref[...]
        a_cs = _dot_exact_lhs(tri, d_a)
        a_last = a_cs[q - 1:q, :]
        wgt = dt * jnp.exp(a_last - a_cs)
        ea = jnp.exp(a_cs)
        w_exp = _dot_2way_rhs(wgt, e_ref[...])
        ea_exp = _dot_2way_rhs(ea, e_ref[...])
        a_cs_t = a_cs.T
        dt_t = dt.T
        for g in range(SSD_GROUPS):
            b_g = xc_s[rows, d_ssd + g * d_state:d_ssd + (g + 1) * d_state]
            c_g = xc_s[rows, d_ssd + gn + g * d_state:d_ssd + gn + (g + 1) * d_state]
            b_gb = b_g.astype(BF16)
            c_gb = c_g.astype(BF16)
            cb = _dot_nt(c_gb, b_gb)
            for hp in range(hpg // 2):
                c0 = g * gw + hp * 2 * headdim
                xs_pair = xc_s[rows, c0:c0 + 2 * headdim]
                lmats = []
                for j in range(2):
                    h = g * hpg + hp * 2 + j
                    seg = (jnp.broadcast_to(a_cs[:, h:h + 1], (q, q))
                           - jnp.broadcast_to(a_cs_t[h:h + 1, :], (q, q)))
                    dec = jnp.exp(jnp.where(causal, seg, -jnp.inf))
                    lmats.append((cb * dec * jnp.broadcast_to(dt_t[h:h + 1, :], (q, q))).astype(BF16))
                lpair = jnp.concatenate(lmats, axis=1)
                rhs = jnp.concatenate([jnp.where(left, xs_pair, 0.0),
                                       jnp.where(left, 0.0, xs_pair)], axis=0).astype(BF16)
                y_s[rows, c0:c0 + 2 * headdim] = _dot(lpair, rhs)
            gcols = slice(g * gw, (g + 1) * gw)
            h_g = ht_s[:, gcols]
            y_off = _dot(c_gb, h_g.astype(BF16)) * ea_exp[:, gcols]
            y_s[rows, gcols] = y_s[rows, gcols] + y_off
            xw = (xc_s[rows, gcols] * w_exp[:, gcols]).astype(BF16)
            ht_s[:, gcols] = h_g * ea_exp[q - 1:q, gcols] + _dot_tn(b_gb, xw)
            if carry is not None and g < len(carry):
                carry[g]()
        return carry

    def gate_body(i, carry):
        rows = _rows(i, ROW_BLK)
        y = y_s[rows, :] + dexp_ref[...] * xc_s[rows, 0:d_ssd]
        y = y * _silu_of_half(z_s[rows, :])
        ms = jnp.mean(y * y, axis=-1, keepdims=True)
        ysb_s[rows, :] = (y * lax.rsqrt(ms + RMS_EPS) * ng_ref[...]).astype(BF16)
        return carry

    assert nch == 2 and LRU_BLOCKS == 8 and d_lru == 2 * cwid and d_model == 2 * cwid
    d_xbc = xc_s.shape[1]
    nxb = d_xbc // cwid
    proj(lx_s, wlx_ref, 0, cwid)
    proj(lx_s, wlx_ref, cwid, cwid)
    proj(ly_s, wly_ref, 0, cwid); lru_conv(0)
    proj(ly_s, wly_ref, cwid, cwid); lru_conv(1)
    P = functools.partial
    mxu_a = ([P(proj, xbc_s, wxbc_ref, j * cwid, cwid) for j in range(nxb)]
             + [P(proj, gs_s, wgs_ref, j * cwid, cwid) for j in range(2)])
    for n in range(LRU_BLOCKS):
        mxu_a[n]()
        lru_block(n)
        if 1 <= n <= nxb:
            ssd_conv((n - 1) * cwid)
    gates_per_chunk = q // ROW_BLK
    nz = d_ssd // cwid
    chunk_body(0, [P(proj, z_s, wz_ref, j * cwid, cwid) for j in range(nz)])
    merge_gate(gs_s, 0, d_model, 0)
    for i in range(gates_per_chunk):
        gate_body(i, None)

    def wlo_piece(j):
        yl_s[:, j * cwid:(j + 1) * cwid] = _dot(ylb_s[...], wlo_ref[:, j * cwid:(j + 1) * cwid])

    chunk_body(1, [P(proj, gl_s, wgl_ref, 0, cwid), P(proj, gl_s, wgl_ref, cwid, cwid),
                   P(wlo_piece, 0), P(wlo_piece, 1)])
    merge_gate(gl_s, 0, d_model, d_model)
    for i in range(gates_per_chunk, 2 * gates_per_chunk):
        gate_body(i, None)

    ys_s[...] = _dot(ysb_s[...], wso_ref[...])
    for i in range(tl // ROW_BLK):
        rows = _rows(i, ROW_BLK)
        mb_s[rows, :] = (gs_s[rows, :] * ys_s[rows, :] + gl_s[rows, :] * yl_s[rows, :]).astype(BF16)
    o_s[...] = _dot(mb_s[...], wo_ref[...])
    for i in range(tl // ROW_BLK):
        rows = _rows(i, ROW_BLK)
        v = alpha * xp_s[rows, :] + o_s[rows, :]
        x1_ref[0, rows, :] = _layer_norm(v, l1g_ref[...], l1b_ref[...])

    @pl.when(t == nt - 1)
    def _():
        st_ref[0] = ht_s[...].T
        lst_ref[0] = hl_s[0:1, :]
        for m, r in enumerate(_seg_tail_rows(wrap_s)):
            sconv_ref[0, m:m + 1, :] = xbc_s[tl - q + r:tl - q + r + 1, :]
        for m, r in enumerate(_seg_tail_rows(wrap_l)):
            lconv_ref[0, m:m + 1, :] = lx_s[tl - q + r:tl - q + r + 1, :]


def _const_spec(shape):
    nd = len(shape)
    return pl.BlockSpec(shape, lambda *_: (0,) * nd, pipeline_mode=pl.Buffered(1))


def _prompt_mixer(x, wts, tl):
    nb, seq, d_model = x.shape
    n_heads, headdim, d_state = wts['n_heads'], wts['headdim'], wts['d_state']
    d_ssd = n_heads * headdim
    d_xbc = d_ssd + 2 * SSD_GROUPS * d_state
    d_lru = wts['wlx'].shape[1]
    names = ['wz', 'wxbc', 'wdt', 'wlx', 'wly', 'wgs', 'wgl', 'scw', 'scb', 'dtb', 'aneg', 'dexp', 'ng',
             'wso', 'lcw', 'lcb', 'wax', 'ba', 'bx', 'lam', 'wlo', 'bg', 'wo', 'l1g', 'l1b', 'ee']
    consts = [wts[k] for k in names]
    kern = functools.partial(_prompt_mixer_kernel, tl=tl, alpha=wts['alpha'], n_heads=n_heads,
                             headdim=headdim, d_state=d_state)
    out_shape = (
        jax.ShapeDtypeStruct((nb, seq, d_model), F32),
        jax.ShapeDtypeStruct((nb, d_ssd, d_state), F32),
        jax.ShapeDtypeStruct((nb, 3, d_xbc), F32),
        jax.ShapeDtypeStruct((nb, 1, d_lru), F32),
        jax.ShapeDtypeStruct((nb, 3, d_lru), F32),
    )
    out_specs = (
        pl.BlockSpec((1, tl, d_model), lambda b, t: (b, t, 0)),
        pl.BlockSpec((1, d_ssd, d_state), lambda b, t: (b, 0, 0)),
        pl.BlockSpec((1, 3, d_xbc), lambda b, t: (b, 0, 0)),
        pl.BlockSpec((1, 1, d_lru), lambda b, t: (b, 0, 0)),
        pl.BlockSpec((1, 3, d_lru), lambda b, t: (b, 0, 0)),
    )
    nch = tl // SSD_CHUNK
    wrap_s = (wts['scw'].shape[0] - 1) * SUBLANES
    wrap_l = (wts['lcw'].shape[0] - 1) * SUBLANES
    scratch = [
        pltpu.VMEM((tl, d_model), BF16),
        pltpu.VMEM((tl, d_model), F32),
        pltpu.VMEM((tl, d_xbc), F32),
        pltpu.VMEM((nch * wrap_s, d_xbc), F32),
        pltpu.VMEM((wrap_s, d_xbc), F32),
        pltpu.VMEM((tl, d_xbc), F32),
        pltpu.VMEM((tl, d_lru), F32),
        pltpu.VMEM((nch * wrap_l, d_lru), F32),
        pltpu.VMEM((wrap_l, d_lru), F32),
        pltpu.VMEM((tl, d_lru), F32),
        pltpu.VMEM((tl, d_lru), F32),
        pltpu.VMEM((tl, d_ssd), F32),
        pltpu.VMEM((tl, d_ssd), F32),
        pltpu.VMEM((tl, d_ssd), BF16),
        pltpu.VMEM((tl, d_lru), BF16),
        pltpu.VMEM((d_state, d_ssd), F32),
        pltpu.VMEM((SUBLANES, d_lru), F32),
        pltpu.VMEM((tl, d_model), F32),
        pltpu.VMEM((tl, d_model), F32),
        pltpu.VMEM((tl, d_model), F32),
        pltpu.VMEM((tl, d_model), F32),
        pltpu.VMEM((tl, d_model), F32),
        pltpu.VMEM((tl, d_model), BF16),
    ]
    return pl.pallas_call(
        kern,
        grid=(nb, seq // tl),
        in_specs=[pl.BlockSpec((1, tl, d_model), lambda b, t: (b, t, 0))]
        + [_const_spec(c.shape) for c in consts],
        out_specs=out_specs,
        out_shape=out_shape,
        scratch_shapes=scratch,
        compiler_params=pltpu.CompilerParams(
            dimension_semantics=("arbitrary", "arbitrary"),
            vmem_limit_bytes=VMEM_LIMIT_BYTES),
        name="prompt_mixer",
    )(x, *consts)


def _ffn_kernel(x_ref, h0_ref, wg_ref, wu_ref, cw_ref, cb_ref, wd_ref, g_ref, b_ref,
                y_ref, tail_ref, xb_s, gb_s, hb_s, *, tm, stride, hist, alpha, fchunk):
    t = pl.program_id(1)
    nt = pl.num_programs(1)
    taps = cw_ref.shape[0]
    nh = (taps - 1) * stride
    d_ff = gb_s.shape[1]

    @pl.when(t == 0)
    def _():
        gb_s[hist - nh:hist, :] = h0_ref[0]

    xb_s[...] = x_ref[0].astype(BF16)
    acc = None
    for c0 in range(0, d_ff, fchunk):
        cols = slice(c0, c0 + fchunk)
        gb_s[hist:hist + tm, cols] = _dot(xb_s[...], wg_ref[:, cols])
        up = _dot(xb_s[...], wu_ref[:, cols])
        gc = _conv_block(gb_s, cw_ref, cb_ref, hist, stride, 0, tm, c0, fchunk)
        hb_s[...] = (_gelu(gc) * up).astype(BF16)
        part = _dot(hb_s[...], wd_ref[c0 // 2:(c0 + fchunk) // 2, :])
        acc = part if acc is None else acc + part
    v = alpha * x_ref[0] + acc
    y_ref[0] = _layer_norm(v, g_ref[...], b_ref[...])
    gb_s[hist - nh:hist, :] = gb_s[hist + tm - nh:hist + tm, :]

    @pl.when(t == nt - 1)
    def _():
        tail_ref[0] = gb_s[hist - nh:hist, :]


def _ffn_seg_kernel(*refs, tm, alpha, fchunk, state_dims):
    steps = state_dims['steps']
    x_ref, h0_ref, st_ref = refs[0:3]
    cb_refs, xe_refs = refs[3:3 + steps], refs[3 + steps:3 + 2 * steps]
    wg_ref, wu_ref, cw_ref, cb_ref, wd_ref, g_ref, b_ref = refs[3 + 2 * steps:10 + 2 * steps]
    y_ref, tail_ref, nst_ref = refs[10 + 2 * steps:13 + 2 * steps]
    yoff_refs = refs[13 + 2 * steps:13 + 3 * steps]
    xb_s, gb_s, gwrap_s, gtail_s, hb_s, cb8_s, xe8_s, yo8_s = refs[13 + 3 * steps:]

    t = pl.program_id(1)
    pps = st_ref.shape[0]
    step_idx = pl.program_id(0) * pl.num_programs(1) + t
    row0 = lax.rem(step_idx * (2 * pps), SUBLANES)
    for p in range(pps):
        for l in range(steps):
            for e in range(2):
                src = pl.ds(row0 + 2 * p + e, 1)
                cb8_s[p, 2 * l + e:2 * l + e + 1, :] = cb_refs[l][src, :]
                xe8_s[p, 2 * l + e:2 * l + e + 1, :] = xe_refs[l][src, :]
    state_pieces = [
        functools.partial(_state_pair_group, g, st_ref.at[p], cb8_s.at[p], xe8_s.at[p],
                          nst_ref.at[p], yo8_s.at[p], **state_dims)
        for p in range(pps) for g in range(SSD_GROUPS)]

    def state_work(n):
        for _ in range(min(n, len(state_pieces))):
            state_pieces.pop(0)()
    nt = pl.num_programs(1)
    q = SSD_CHUNK
    nch = tm // q
    wrap = (cw_ref.shape[0] - 1) * SUBLANES
    d_ff = gb_s.shape[1]
    tail_rows = _seg_tail_rows(wrap)

    @pl.when(t == 0)
    def _():
        gtail_s[...] = jnp.zeros(gtail_s.shape, F32)
        for m in range(len(tail_rows)):
            r = m * SUBLANES + SUBLANES - 1
            gtail_s[r:r + 1, :] = h0_ref[0, m:m + 1, :]

    xb_s[...] = x_ref[0].astype(BF16)
    def gate_proj(c0):
        cols = slice(c0, c0 + fchunk)
        gb_s[:, cols] = _dot(xb_s[...], wg_ref[:, cols])
        for c in range(nch):
            _fill_wrap(gb_s, gwrap_s, c, wrap, gtail_s, cols)
        gtail_s[:, cols] = gb_s[tm - wrap:tm, cols]

    per_slot = -(-len(state_pieces) // (1 + 2 * (d_ff // fchunk)))
    gate_proj(0)
    state_work(per_slot)
    acc = None
    for c0 in range(0, d_ff, fchunk):
        cols = slice(c0, c0 + fchunk)
        up = _dot(xb_s[...], wu_ref[:, cols])
        if c0 + fchunk < d_ff:
            gate_proj(c0 + fchunk)
        state_work(per_slot)
        for c in range(nch):
            gc = _conv_seg(gb_s, gwrap_s, cw_ref, cb_ref, c, wrap, c0, fchunk)
            hb_s[c * q:(c + 1) * q, :] = (_gelu(gc) * up[c * q:(c + 1) * q, :]).astype(BF16)
        part = _dot(hb_s[...], wd_ref[c0 // 2:(c0 + fchunk) // 2, :])
        state_work(per_slot)
        acc = part if acc is None else acc + part
    state_work(len(state_pieces))
    for p in range(pps):
        for l in range(steps):
            for e in range(2):
                yoff_refs[l][pl.ds(row0 + 2 * p + e, 1), :] = yo8_s[p, 2 * l + e:2 * l + e + 1, :]

    unperm = _unperm_matrix()
    unperm2 = jnp.concatenate([unperm, unperm], axis=1)
    for c in range(nch):
        rows = slice(c * q, (c + 1) * q)
        y = _layer_norm(alpha * x_ref[0, rows, :] + acc[rows, :], g_ref[...], b_ref[...])
        y_ref[0, rows, :] = _dot_2way_lhs(unperm2, y)

    @pl.when(t == nt - 1)
    def _():
        for m, r in enumerate(tail_rows):
            tail_ref[0, m:m + 1, :] = gb_s[tm - q + r:tm - q + r + 1, :]


def _ffn_seg(x, hist0, state, cb_lm, xe_lm, wts, tm, steps):
    nb, seq, d_model = x.shape
    nseq = 2 * state.shape[0]
    d_ssd = xe_lm.shape[1] // 2
    assert cb_lm.shape[0] == steps * nseq and nseq % SUBLANES == 0
    d_ff = wts['wg'].shape[1]
    taps = wts['fcw'].shape[0]
    wrap = (taps - 1) * SUBLANES
    fchunk = 1024
    nt = seq // tm
    npairs = state.shape[0]
    pps = npairs // (nb * nt)
    assert pps * nb * nt == npairs
    consts = [wts[k] for k in ['wg', 'wu', 'fcw', 'fcb', 'wd', 'l2g', 'l2b']]
    state_dims = dict(steps=steps, n_heads=wts['n_heads'], headdim=wts['headdim'], d_state=wts['d_state'])
    kern = functools.partial(_ffn_seg_kernel, tm=tm, alpha=wts['alpha'], fchunk=fchunk, state_dims=state_dims)
    pair_blk = lambda a: pl.BlockSpec((pps,) + tuple(a.shape[1:]),
                                      lambda b, t: (b * nt + t,) + (0,) * (len(a.shape) - 1))

    def rows_blk(l, width):
        return pl.BlockSpec((SUBLANES, width),
                            lambda b, t: (l * (nseq // SUBLANES) + ((b * nt + t) * 2 * pps) // SUBLANES, 0))

    def yoff_blk():
        return pl.BlockSpec((SUBLANES, d_ssd), lambda b, t: (((b * nt + t) * 2 * pps) // SUBLANES, 0))

    outs = pl.pallas_call(
        kern,
        grid=(nb, nt),
        in_specs=[pl.BlockSpec((1, tm, d_model), lambda b, t: (b, t, 0)),
                  pl.BlockSpec((1, taps - 1, d_ff), lambda b, t: (b, 0, 0)),
                  pair_blk(state)]
        + [rows_blk(l, cb_lm.shape[1]) for l in range(steps)]
        + [rows_blk(l, xe_lm.shape[1]) for l in range(steps)]
        + [_const_spec(c.shape) for c in consts],
        out_specs=(pl.BlockSpec((1, tm, d_model), lambda b, t: (b, t, 0)),
                   pl.BlockSpec((1, taps - 1, d_ff), lambda b, t: (b, 0, 0)),
                   pair_blk(state)) + tuple(yoff_blk() for _ in range(steps)),
        out_shape=(jax.ShapeDtypeStruct((nb, seq, d_model), F32),
                   jax.ShapeDtypeStruct((nb, taps - 1, d_ff), F32),
                   jax.ShapeDtypeStruct(state.shape, F32))
        + tuple(jax.ShapeDtypeStruct((nseq, d_ssd), F32) for _ in range(steps)),
        scratch_shapes=[pltpu.VMEM((tm, d_model), BF16),
                        pltpu.VMEM((tm, d_ff), F32),
                        pltpu.VMEM((tm // SSD_CHUNK * wrap, d_ff), F32),
                        pltpu.VMEM((wrap, d_ff), F32),
                        pltpu.VMEM((tm, fchunk), BF16),
                        pltpu.VMEM((pps, 2 * steps, cb_lm.shape[1]), F32),
                        pltpu.VMEM((pps, 2 * steps, xe_lm.shape[1]), F32),
                        pltpu.VMEM((pps, 2 * steps, d_ssd), F32)],
        compiler_params=pltpu.CompilerParams(
            dimension_semantics=("arbitrary", "arbitrary"),
            vmem_limit_bytes=VMEM_LIMIT_BYTES),
        name="conv_ffn_seg",
    )(x, hist0, state, *([cb_lm] * steps), *([xe_lm] * steps), *consts)
    return outs[0], outs[1], outs[2], outs[3:]


def _ffn(x, hist0, wts, tm, stride):
    nb, seq, d_model = x.shape
    d_ff = wts['wg'].shape[1]
    taps = wts['fcw'].shape[0]
    nh = (taps - 1) * stride
    hist = -(-nh // SUBLANES) * SUBLANES
    consts = [wts[k] for k in ['wg', 'wu', 'fcw', 'fcb', 'wd', 'l2g', 'l2b']]
    kern = functools.partial(_ffn_kernel, tm=tm, stride=stride, hist=hist, alpha=wts['alpha'], fchunk=1024)
    return pl.pallas_call(
        kern,
        grid=(nb, seq // tm),
        in_specs=[pl.BlockSpec((1, tm, d_model), lambda b, t: (b, t, 0)),
                  pl.BlockSpec((1, nh, d_ff), lambda b, t: (b, 0, 0))]
        + [_const_spec(c.shape) for c in consts],
        out_specs=(pl.BlockSpec((1, tm, d_model), lambda b, t: (b, t, 0)),
                   pl.BlockSpec((1, nh, d_ff), lambda b, t: (b, 0, 0))),
        out_shape=(jax.ShapeDtypeStruct((nb, seq, d_model), F32),
                   jax.ShapeDtypeStruct((nb, nh, d_ff), F32)),
        scratch_shapes=[pltpu.VMEM((tm, d_model), BF16),
                        pltpu.VMEM((hist + tm, d_ff), F32),
                        pltpu.VMEM((tm, 1024), BF16)],
        compiler_params=pltpu.CompilerParams(
            dimension_semantics=("arbitrary", "arbitrary"),
            vmem_limit_bytes=VMEM_LIMIT_BYTES),
        name="conv_ffn",
    )(x, hist0, *consts)


def _sample_ssd_kernel(
        x_ref, xall_ref, cssd_ref, wxbc_ref, wdt_ref, scw_ref, scb_ref, dtb_ref, aneg_ref, dexp_ref,
        e_ref, gsum_ref, e2_ref,
        pre_ref, cb_ref, xe_ref, ydg_ref,
        xbc_s, xs_s, bs_s, acs_s, dts_s,
        *, nseq, steps, n_heads, headdim, d_state):
    l = pl.program_id(0)
    d_ssd = n_heads * headdim
    gn = SSD_GROUPS * d_state
    hist = (scw_ref.shape[0] - 1) * nseq
    r0 = l * nseq

    def blk(i):
        return pl.ds(pl.multiple_of(i * nseq, nseq), nseq)

    def sblk(i):
        return slice(i * nseq, (i + 1) * nseq)

    @pl.when(l == 0)
    def _():
        xbc_s[0:hist, :] = cssd_ref[...]
        dts_s[...] = _softplus(_dot(xall_ref[...].astype(BF16), wdt_ref[...]) + dtb_ref[...])
        acc = jnp.zeros((nseq, LANES), F32)
        for s in range(steps):
            acc = acc + dts_s[sblk(s), :] * aneg_ref[...]
            acs_s[sblk(s), :] = acc

    xb = x_ref[...].astype(BF16)

    pre = _dot(xb, wxbc_ref[...])
    pre_ref[...] = pre
    xbc_s[pl.ds(pl.multiple_of(hist + r0, nseq), nseq), :] = pre
    cwid = 512
    for c0 in range(0, d_ssd, cwid):
        xs_s[blk(l), c0:c0 + cwid] = _silu_of_half(
            _conv_block(xbc_s, scw_ref, scb_ref, hist, nseq, r0, nseq, c0, cwid))
    b_l = _silu_of_half(_conv_block(xbc_s, scw_ref, scb_ref, hist, nseq, r0, nseq, d_ssd, gn))
    c_l = _silu_of_half(_conv_block(xbc_s, scw_ref, scb_ref, hist, nseq, r0, nseq, d_ssd + gn, gn))
    bs_s[blk(l), :] = b_l
    cb_ref[:, 0:gn] = c_l
    cb_ref[:, gn:2 * gn] = b_l

    a_cs = acs_s[blk(l), :]
    dt = dts_s[blk(l), :]
    a_end = acs_s[sblk(steps - 1), :]
    xe_ref[:, d_ssd:2 * d_ssd] = _dot_2way_rhs(jnp.exp(a_cs), e_ref[...])
    xe_ref[:, 0:d_ssd] = xs_s[blk(l), :] * _dot_2way_rhs(dt * jnp.exp(a_end - a_cs), e_ref[...])

    ydg_ref[...] = dexp_ref[...] * xs_s[blk(l), :]
    for s in range(steps):
        @pl.when(s <= l)
        def _(s=s):
            coef = _dot_2way_rhs(jnp.exp(a_cs - acs_s[sblk(s), :]) * dts_s[sblk(s), :], e_ref[...])
            cbx = _dot_2way_rhs(_dot_2way_rhs(bs_s[sblk(s), :] * c_l, gsum_ref[...]), e2_ref[...])
            ydg_ref[...] += cbx * coef * xs_s[sblk(s), :]


def _sample_lru_kernel(
        x_ref, clru_ref, slru_ref, wlx_ref, wly_ref, lcw_ref, lcb_ref, wax_ref, ba_ref, bx_ref,
        lam_ref, prelx_ref, ylru_ref, lst_ref, lx_s, hl_s, *, nseq, steps, start_pos):
    l = pl.program_id(0)
    hist = (lcw_ref.shape[0] - 1) * nseq
    d_lru = lx_s.shape[1]
    r0 = l * nseq

    @pl.when(l == 0)
    def _():
        lx_s[0:hist, :] = clru_ref[...]
        hl_s[...] = slru_ref[...]

    xb = x_ref[...].astype(BF16)
    prelx = _dot(xb, wlx_ref[...])
    prelx_ref[...] = prelx
    lx_s[pl.ds(pl.multiple_of(hist + r0, nseq), nseq), :] = prelx
    ly = _dot(xb, wly_ref[...])
    bw = d_lru // LRU_BLOCKS
    first = (l + start_pos) == 0
    for n in range(LRU_BLOCKS):
        cols = slice(n * bw, (n + 1) * bw)
        xr = _conv_block(lx_s, lcw_ref, lcb_ref, hist, nseq, r0, nseq, n * bw, bw)
        xrb = xr.astype(BF16)
        rg = _dot(xrb, wax_ref[n])
        r = _sigmoid(rg[:, 0:bw] + ba_ref[:, cols])
        gi = _sigmoid(rg[:, bw:2 * bw] + bx_ref[:, cols])
        log_a = (-LRU_C) * r * _softplus(-lam_ref[:, cols])
        a = jnp.exp(log_a)
        mult = jnp.where(first, 1.0, jnp.sqrt(1.0 - a * a))
        h = a * hl_s[:, cols] + mult * gi * xr
        hl_s[:, cols] = h
        ylru_ref[:, cols] = (h * _gelu(ly[:, cols])).astype(BF16)

    @pl.when(l == steps - 1)
    def _():
        lst_ref[...] = hl_s[...]


def _state_pair_group(g, st_ref, cb8_ref, xe8_ref, nst_ref, yoff8_ref,
                      *, steps, n_heads, headdim, d_state):
    nrow = 2 * steps
    hpg = n_heads // SSD_GROUPS
    gw = hpg * headdim
    gn = SSD_GROUPS * d_state
    d_ssd = n_heads * headdim
    assert 2 * headdim == LANES and d_state == LANES
    par = lax.broadcasted_iota(jnp.int32, (nrow, gw), 0) % 2
    low = lax.broadcasted_iota(jnp.int32, (nrow, LANES), 1) < headdim
    gcols = slice(g * gw, (g + 1) * gw)
    c8g = cb8_ref[:, g * d_state:(g + 1) * d_state].astype(BF16)
    b8g = cb8_ref[:, gn + g * d_state:gn + (g + 1) * d_state].astype(BF16)
    xw8 = xe8_ref[:, g * gw:(g + 1) * gw]
    ea8 = xe8_ref[:, d_ssd + g * gw:d_ssd + (g + 1) * gw]
    cds = []
    for hp in range(hpg // 2):
        pair = ea8[:, hp * LANES:(hp + 1) * LANES]
        swapped = pltpu.roll(pair, headdim, 1)
        cds.append(jnp.where(low, pair, swapped))
        cds.append(jnp.where(low, swapped, pair))
    yo = None
    for e in range(2):
        sg = st_ref[e, gcols, :]
        yo_e = _dot_nt(c8g, sg.astype(BF16))
        yo = yo_e if e == 0 else jnp.where(par == e, yo_e, yo)
        xw_e = jnp.where(par == e, xw8, 0.0).astype(BF16)
        upd = _dot_tn(xw_e, b8g)
        k_last = 2 * (steps - 1) + e
        for hh in range(hpg):
            cd = cds[hh][k_last:k_last + 1, :]
            hr = slice(hh * headdim, (hh + 1) * headdim)
            nst_ref[e, g * gw + hh * headdim:g * gw + (hh + 1) * headdim, :] = sg[hr, :] * cd + upd[hr, :]
    yoff8_ref[:, gcols] = yo * ea8


def _sample_post_kernel(x_ref, ydg_ref, yoff_ref, ylru_ref, wz_ref, ng_ref, wso_ref, wlo_ref,
                        wgs_ref, wgl_ref, bg_ref, wo_ref, l1g_ref, l1b_ref, x1_ref, *, alpha):
    d_model = x_ref.shape[1]
    xb = x_ref[...].astype(BF16)
    y = (ydg_ref[...] + yoff_ref[...]) * _silu_of_half(_dot(xb, wz_ref[...]))
    ms = jnp.mean(y * y, axis=-1, keepdims=True)
    ysb = (y * lax.rsqrt(ms + RMS_EPS) * ng_ref[...]).astype(BF16)
    g_ssd = _sigmoid(_dot(xb, wgs_ref[...]) + bg_ref[:, 0:d_model])
    g_lru = _sigmoid(_dot(xb, wgl_ref[...]) + bg_ref[:, d_model:2 * d_model])
    merged = g_ssd * _dot(ysb, wso_ref[...]) + g_lru * _dot(ylru_ref[...], wlo_ref[...])
    o = _dot(merged.astype(BF16), wo_ref[...])
    x1_ref[...] = _layer_norm(alpha * x_ref[...] + o, l1g_ref[...], l1b_ref[...])


def _sample_front(x_lm, cssd_lm, clru_lm, slru_lm, wts, nseq, steps, start_pos):
    n_heads, headdim, d_state = wts['n_heads'], wts['headdim'], wts['d_state']
    d_model = x_lm.shape[1]
    d_ssd = n_heads * headdim
    gn = SSD_GROUPS * d_state
    d_xbc = d_ssd + 2 * gn
    d_lru = wts['wlx'].shape[1]
    ntok = steps * nseq
    dims = dict(nseq=nseq, steps=steps, n_heads=n_heads, headdim=headdim, d_state=d_state)
    params = pltpu.CompilerParams(dimension_semantics=("arbitrary",), vmem_limit_bytes=VMEM_LIMIT_BYTES)
    step_blk = lambda w: pl.BlockSpec((nseq, w), lambda l: (l, 0))

    sds = jax.ShapeDtypeStruct
    ssd_names = ['wxbc', 'wdt', 'scw', 'scb', 'dtb', 'aneg', 'dexp', 'ee', 'gsum2', 'gexp2']
    ssd_consts = [x_lm, cssd_lm] + [wts[k] for k in ssd_names]
    pre, cb_lm, xe_lm, ydg_lm = pl.pallas_call(
        functools.partial(_sample_ssd_kernel, **dims),
        grid=(steps,),
        in_specs=[step_blk(d_model)] + [_const_spec(c.shape) for c in ssd_consts],
        out_specs=(step_blk(d_xbc), step_blk(2 * gn), step_blk(2 * d_ssd), step_blk(d_ssd)),
        out_shape=(sds((ntok, d_xbc), F32), sds((ntok, 2 * gn), F32), sds((ntok, 2 * d_ssd), F32),
                   sds((ntok, d_ssd), F32)),
        scratch_shapes=[
            pltpu.VMEM((cssd_lm.shape[0] + ntok, d_xbc), F32),
            pltpu.VMEM((ntok, d_ssd), F32),
            pltpu.VMEM((ntok, gn), F32),
            pltpu.VMEM((ntok, LANES), F32),
            pltpu.VMEM((ntok, LANES), F32),
        ],
        compiler_params=params,
        name="sample_ssd",
    )(x_lm, *ssd_consts)

    lru_names = ['wlx', 'wly', 'lcw', 'lcb', 'wax', 'ba', 'bx', 'lam']
    lru_consts = [clru_lm, slru_lm] + [wts[k] for k in lru_names]
    prelx, ylru_lm, lst = pl.pallas_call(
        functools.partial(_sample_lru_kernel, nseq=nseq, steps=steps, start_pos=start_pos),
        grid=(steps,),
        in_specs=[step_blk(d_model)] + [_const_spec(c.shape) for c in lru_consts],
        out_specs=(step_blk(d_lru), step_blk(d_lru), pl.BlockSpec((nseq, d_lru), lambda l: (0, 0))),
        out_shape=(sds((ntok, d_lru), F32), sds((ntok, d_lru), BF16), sds((nseq, d_lru), F32)),
        scratch_shapes=[
            pltpu.VMEM((clru_lm.shape[0] + ntok, d_lru), F32),
            pltpu.VMEM((nseq, d_lru), F32),
        ],
        compiler_params=params,
        name="sample_lru",
    )(x_lm, *lru_consts)

    return dict(pre=pre, cb=cb_lm, xe=xe_lm, ydg=ydg_lm, prelx=prelx, ylru=ylru_lm, lst=lst)


def _sample_back(x_lm, ydg_lm, yoff_lm, ylru_lm, wts, nseq, steps):
    d_model = x_lm.shape[1]
    d_ssd = wts['n_heads'] * wts['headdim']
    d_lru = wts['wlx'].shape[1]
    ntok = steps * nseq
    params = pltpu.CompilerParams(dimension_semantics=("arbitrary",), vmem_limit_bytes=VMEM_LIMIT_BYTES)
    step_blk = lambda w: pl.BlockSpec((nseq, w), lambda l: (l, 0))
    sds = jax.ShapeDtypeStruct
    post_names = ['wz', 'ng', 'wso', 'wlo', 'wgs', 'wgl', 'bg', 'wo', 'l1g', 'l1b']
    post_consts = [wts[k] for k in post_names]
    x1_lm = pl.pallas_call(
        functools.partial(_sample_post_kernel, alpha=wts['alpha']),
        grid=(steps,),
        in_specs=[step_blk(d_model), step_blk(d_ssd), step_blk(d_ssd), step_blk(d_lru)]
        + [_const_spec(c.shape) for c in post_consts],
        out_specs=step_blk(d_model),
        out_shape=sds((ntok, d_model), F32),
        compiler_params=params,
        name="sample_post",
    )(x_lm, ydg_lm, yoff_lm, ylru_lm, *post_consts)
    return x1_lm


def _prep_weights(w_in, b_gate, ssd_conv_w, ssd_conv_b, ssd_dt_bias, ssd_a_log, ssd_d, ssd_norm_g,
                  w_ssd_out, lru_conv_w, lru_conv_b, lru_wa, lru_ba, lru_wx, lru_bx, lru_lambda,
                  w_lru_out, w_o, ln1_g, ln1_b, ffn_w_gate, ffn_w_up, ffn_conv_w, ffn_conv_b,
                  ffn_w_down, ln2_g, ln2_b, n_heads, headdim, d_state):
    depth = w_in.shape[0]
    d_model = w_in.shape[1]
    d_ssd = n_heads * headdim
    d_xbc = d_ssd + 2 * SSD_GROUPS * d_state
    d_lru = lru_lambda.shape[1]
    sizes = (d_ssd, d_xbc, n_heads, d_lru, d_lru, d_model, d_model)
    cuts = np.cumsum((0,) + sizes)
    scales = [0.5 if i == 0 else 1.0 for i in range(len(sizes))]
    parts = _pack_w_in_t(jnp.swapaxes(w_in, 1, 2)[0].astype(F32), [int(c) for c in cuts], 2, scales)
    row = lambda v: v.reshape(1, -1).astype(F32)
    wax = jnp.concatenate([lru_wa[0], lru_wx[0]], axis=-1)
    wg, wu, wlo, wo, wax_p = _pack_weights([ffn_w_gate[0], ffn_w_up[0], w_lru_out[0], w_o[0],
                                            wax.reshape(-1, wax.shape[-1])])
    wax = wax_p.reshape(wax.shape[0], wax.shape[1] // 2, wax.shape[2])
    wso, = _pack_weights([w_ssd_out[0]])
    wd, = _pack_weights([ffn_w_down[0]])
    pad_heads = lambda v: jnp.pad(v.reshape(1, -1).astype(F32), ((0, 0), (0, LANES - n_heads)))
    head_of_col = np.arange(d_ssd) // headdim
    expand = (np.arange(LANES)[:, None] == head_of_col[None, :]).astype(np.float32)
    group_of_col = head_of_col // (n_heads // SSD_GROUPS)
    group_sum = (np.arange(SSD_GROUPS * d_state)[:, None] // d_state
                 == np.arange(LANES)[None, :])
    group_exp = np.arange(LANES)[:, None] == group_of_col[None, :]
    return dict(
        n_heads=n_heads, headdim=headdim, d_state=d_state,
        alpha=float((2.0 * depth) ** 0.25),
        wz=parts[0], wxbc=parts[1], wdt=parts[2], wlx=parts[3], wly=parts[4], wgs=parts[5], wgl=parts[6],
        scw=0.5 * ssd_conv_w[0].astype(F32), scb=0.5 * row(ssd_conv_b[0]),
        dtb=pad_heads(ssd_dt_bias[0]), aneg=pad_heads(-jnp.exp(ssd_a_log[0].astype(F32))),
        dexp=row(jnp.repeat(ssd_d[0], headdim)), ng=row(ssd_norm_g[0]),
        wso=wso,
        lcw=lru_conv_w[0].astype(F32), lcb=row(lru_conv_b[0]),
        ba=row(lru_ba[0]), bx=row(lru_bx[0]), wax=wax,
        lam=row(lru_lambda[0]), wlo=wlo,
        bg=row(b_gate[0]), wo=wo, l1g=row(ln1_g[0]), l1b=row(ln1_b[0]),
        ee=_pack_rows_01(np.concatenate([expand, expand], axis=0)),
        gsum2=_pack_rows_01(np.concatenate([group_sum, group_sum], axis=0)),
        gexp2=_pack_rows_01(np.concatenate([group_exp, group_exp], axis=0)),
        wg=wg, wu=wu,
        fcw=ffn_conv_w[0].astype(F32), fcb=row(ffn_conv_b[0]), wd=wd,
        l2g=row(ln2_g[0]), l2b=row(ln2_b[0]),
    )


def kernel(x_prompt, x_sample, state_ssd, cache_ssd_conv, state_lru, cache_lru_conv, cache_ffn_conv, w_in, b_gate, ssd_conv_w, ssd_conv_b, ssd_dt_bias, ssd_a_log, ssd_d, ssd_norm_g, w_ssd_out, lru_conv_w, lru_conv_b, lru_wa, lru_ba, lru_wx, lru_bx, lru_lambda, w_lru_out, w_o, ln1_g, ln1_b, ffn_w_gate, ffn_w_up, ffn_conv_w, ffn_conv_b, ffn_w_down, ln2_g, ln2_b):
    assert w_in.shape[0] == 1, "single-layer trunk"
    _, _, n_heads, headdim, d_state = state_ssd.shape
    wts = _prep_weights(w_in, b_gate, ssd_conv_w, ssd_conv_b, ssd_dt_bias, ssd_a_log, ssd_d, ssd_norm_g,
                        w_ssd_out, lru_conv_w, lru_conv_b, lru_wa, lru_ba, lru_wx, lru_bx, lru_lambda,
                        w_lru_out, w_o, ln1_g, ln1_b, ffn_w_gate, ffn_w_up, ffn_conv_w, ffn_conv_b,
                        ffn_w_down, ln2_g, ln2_b, n_heads, headdim, d_state)
    bp = x_prompt.shape[0]
    d_ff = ffn_w_gate.shape[2]

    nb_s, steps, _ = x_sample.shape
    half = nb_s // 2

    def to_lm(a):
        return jnp.swapaxes(a, 0, 1).reshape(a.shape[1] * nb_s, a.shape[2])

    def from_lm(a, k):
        return jnp.swapaxes(a.reshape(k, nb_s, a.shape[1]), 0, 1)

    d_ssd = n_heads * headdim
    x_lm = to_lm(x_sample)
    sf = _sample_front(x_lm, to_lm(cache_ssd_conv[0]), to_lm(cache_lru_conv[0]),
                       to_lm(state_lru[0][:, None, :]), wts, nb_s, steps, PAST_LEN)
    pre, lst, prelx = sf['pre'], sf['lst'], sf['prelx']

    x1_p, p_ssd, p_ssd_buf, p_lru, p_lru_buf = _prompt_mixer(x_prompt, wts, tl=MIXER_TILE)
    y_prompt, p_ffn_buf, new_state, yoff_steps = _ffn_seg(
        x1_p, jnp.zeros((bp, ffn_conv_w.shape[1] - 1, d_ff), F32),
        state_ssd[0].reshape(half, 2, d_ssd, d_state), sf['cb'], sf['xe'], wts, tm=FFN_TILE, steps=steps)
    p_ssd = p_ssd.reshape(1, bp, n_heads, headdim, d_state)

    x1_lm = _sample_back(x_lm, sf['ydg'], jnp.concatenate(yoff_steps, axis=0), sf['ylru'], wts, nb_s, steps)
    y_lm, tail = _ffn(x1_lm[None], to_lm(cache_ffn_conv[0])[None], wts, tm=steps * nb_s, stride=nb_s)
    k_ssd = ssd_conv_w.shape[1] - 1
    k_lru = lru_conv_w.shape[1] - 1
    k_ffn = ffn_conv_w.shape[1] - 1
    assert steps >= max(k_ssd, k_lru, k_ffn)
    return (y_prompt, from_lm(y_lm[0], steps), p_ssd, p_ssd_buf[None], p_lru.reshape(1, bp, -1), p_lru_buf[None],
            p_ffn_buf[None],
            new_state.reshape(1, nb_s, n_heads, headdim, d_state),
            from_lm(pre[(steps - k_ssd) * nb_s:], k_ssd)[None],
            from_lm(lst, 1).reshape(1, nb_s, -1),
            from_lm(prelx[(steps - k_lru) * nb_s:], k_lru)[None],
            from_lm(tail[0], k_ffn)[None])
```

```python
import functools

import numpy as np
import jax
import jax.numpy as jnp
from jax import lax
from jax.experimental import pallas as pl
from jax.experimental.pallas import tpu as pltpu

F32 = jnp.float32
BF16 = jnp.bfloat16

SSD_GROUPS = 4
SSD_CHUNK = 128
LRU_BLOCKS = 8
LRU_C = 8.0
LN_EPS = 1e-5
RMS_EPS = 1e-6
PAST_LEN = 16384

LANES = 128
SUBLANES = 8
VMEM_LIMIT_BYTES = 60 * 1024 * 1024

ROW_BLK = 32
MIXER_TILE = 2 * SSD_CHUNK
FFN_TILE = 4 * SSD_CHUNK


def _dot(a, b):
    if b.dtype == jnp.uint32:
        b = pltpu.bitcast(b, BF16)
    return jnp.dot(a, b, preferred_element_type=F32)


def _pack_kernel(*refs):
    n = len(refs) // 2
    for w_ref, o_ref in zip(refs[:n], refs[n:]):
        o_ref[...] = pltpu.bitcast(w_ref[...].astype(BF16), jnp.uint32)


def _pack_weights(ws):
    k = ws[0].shape[0]
    assert all(w.shape[0] == k for w in ws)
    bk = 512 if k >= 2048 else 256
    return pl.pallas_call(
        _pack_kernel,
        grid=(k // bk,),
        in_specs=[pl.BlockSpec((bk, w.shape[1]), lambda i: (i, 0)) for w in ws],
        out_specs=tuple(pl.BlockSpec((bk // 2, w.shape[1]), lambda i: (i, 0)) for w in ws),
        out_shape=tuple(jax.ShapeDtypeStruct((k // 2, w.shape[1]), jnp.uint32) for w in ws),
        compiler_params=pltpu.CompilerParams(dimension_semantics=("arbitrary",),
                                             vmem_limit_bytes=VMEM_LIMIT_BYTES),
        name="pack_weight",
    )(*[w.astype(F32) for w in ws])


def _pack_t_kernel(wt_ref, wdt_ref, *out_refs, parts, bn, dt_rows):
    j = pl.program_id(0)
    first = 0
    for o_ref, (nblk, scale) in zip(out_refs, parts):
        @pl.when(jnp.logical_and(j >= first, j < first + nblk))
        def _(o_ref=o_ref, scale=scale):
            v = wt_ref[...].T
            if scale != 1.0:
                v = v * scale
            o_ref[...] = pltpu.bitcast(v.astype(BF16), jnp.uint32)
        first += nblk

    @pl.when(j == 0)
    def _():
        rows = lax.broadcasted_iota(jnp.int32, wdt_ref.shape, 0)
        v = jnp.where(rows < dt_rows, wdt_ref[...], 0.0)
        out_refs[-1][...] = pltpu.bitcast(v.T.astype(BF16), jnp.uint32)


def _pack_w_in_t(wt, cuts, dt_index, scales, bn=1024):
    n, k = wt.shape
    parts, offs = [], []
    for i in range(len(cuts) - 1):
        if i == dt_index:
            continue
        nblk = (cuts[i + 1] - cuts[i]) // bn
        assert nblk * bn == cuts[i + 1] - cuts[i]
        parts.append((nblk, scales[i]))
        offs += [cuts[i] + b * bn for b in range(nblk)]
    offs = np.asarray(offs, np.int32)
    firsts = np.cumsum([0] + [p[0] for p in parts])

    def row_off(j):
        off = jnp.int32(int(offs[0]))
        for idx in range(1, len(offs)):
            off = jnp.where(j >= idx, jnp.int32(int(offs[idx])), off)
        return off

    def out_map(p):
        return lambda j: (0, jnp.clip(j - int(firsts[p]), 0, parts[p][0] - 1))

    dt_rows = cuts[dt_index + 1] - cuts[dt_index]
    outs = pl.pallas_call(
        functools.partial(_pack_t_kernel, parts=tuple(parts), bn=bn, dt_rows=dt_rows),
        grid=(len(offs),),
        in_specs=[pl.BlockSpec((pl.Element(bn), pl.Element(k)), lambda j: (pl.multiple_of(row_off(j), SUBLANES), 0)),
                  pl.BlockSpec((pl.Element(LANES), pl.Element(k)), lambda j: (cuts[dt_index], 0))],
        out_specs=tuple(pl.BlockSpec((k // 2, bn), out_map(p)) for p in range(len(parts)))
        + (pl.BlockSpec((k // 2, LANES), lambda j: (0, 0)),),
        out_shape=tuple(jax.ShapeDtypeStruct((k // 2, nb_ * bn), jnp.uint32) for nb_, _ in parts)
        + (jax.ShapeDtypeStruct((k // 2, LANES), jnp.uint32),),
        compiler_params=pltpu.CompilerParams(dimension_semantics=("arbitrary",),
                                             vmem_limit_bytes=VMEM_LIMIT_BYTES),
        name="pack_w_in",
    )(wt, wt)
    outs = list(outs)
    dt_part = outs.pop()
    outs.insert(dt_index, dt_part)
    return outs


def _pack_rows_01(m):
    bits = np.ascontiguousarray(m, np.float32).view(np.uint32) >> 16
    return jnp.asarray(bits[0::2] | (bits[1::2] << 16), jnp.uint32)


def _dot_nt(a, b):
    return lax.dot_general(a, b, (((1,), (1,)), ((), ())), preferred_element_type=F32)


def _dot_tn(a, b):
    return lax.dot_general(a, b, (((0,), (0,)), ((), ())), preferred_element_type=F32)


def _split3(v):
    hi = v.astype(BF16)
    r1 = v - hi.astype(F32)
    mid = r1.astype(BF16)
    lo = (r1 - mid.astype(F32)).astype(BF16)
    return hi, mid, lo


def _dot_exact_lhs(m, v):
    hi, mid, lo = _split3(v)
    return _dot(m, hi) + _dot(m, mid) + _dot(m, lo)


def _split2(v):
    hi = v.astype(BF16)
    lo = (v - hi.astype(F32)).astype(BF16)
    return hi, lo


def _dot_2way_rhs(v, m2):
    return _dot(jnp.concatenate(_split2(v), axis=1), m2)


def _dot_2way_lhs(m2, v):
    return _dot(m2, jnp.concatenate(_split2(v), axis=0))


def _softplus(x):
    return jnp.maximum(x, 0.0) + jnp.log1p(jnp.exp(-jnp.abs(x)))


def _sigmoid(x):
    return 0.5 * jnp.tanh(0.5 * x) + 0.5


def _silu_of_half(h):
    return h + h * jnp.tanh(h)


def _gelu(x):
    c = np.sqrt(2.0 / np.pi).astype(np.float32)
    return 0.5 * x * (1.0 + jnp.tanh(c * (x + 0.044715 * (x * x * x))))


def _layer_norm(v, g, b):
    mu = jnp.mean(v, axis=-1, keepdims=True)
    d = v - mu
    var = jnp.mean(d * d, axis=-1, keepdims=True)
    return d * lax.rsqrt(var + LN_EPS) * g + b


def _rows(i, n):
    if isinstance(i, int):
        return slice(i * n, (i + 1) * n)
    return pl.ds(pl.multiple_of(i * n, n), n)


def _conv_block(buf_ref, w_ref, b_ref, hist, stride, r0, rows, c0, cw):
    taps = w_ref.shape[0]
    acc = b_ref[:, c0:c0 + cw]
    for k in range(taps):
        off = hist + r0 - (taps - 1 - k) * stride
        if not isinstance(off, int):
            off = pl.multiple_of(off, SUBLANES)
        acc = acc + w_ref[k:k + 1, c0:c0 + cw] * buf_ref[pl.ds(off, rows), c0:c0 + cw]
    return acc


LOG2_SUBLANES = 3
NPOS = SSD_CHUNK // SUBLANES
LOG2_NPOS = 4
assert 1 << LOG2_SUBLANES == SUBLANES and 1 << LOG2_NPOS == NPOS


def _tok_of_row(r):
    return (r & (SUBLANES - 1)) * NPOS + lax.shift_right_logical(r, LOG2_SUBLANES)


def _row_of_tok(t):
    return (t & (NPOS - 1)) * SUBLANES + lax.shift_right_logical(t, LOG2_NPOS)


def _perm_matrix():
    q = SSD_CHUNK
    r = lax.broadcasted_iota(jnp.int32, (q, q), 0)
    c = lax.broadcasted_iota(jnp.int32, (q, q), 1)
    return jnp.where(c == _tok_of_row(r), 1.0, 0.0).astype(BF16)


def _unperm_matrix():
    q = SSD_CHUNK
    t = lax.broadcasted_iota(jnp.int32, (q, q), 0)
    r = lax.broadcasted_iota(jnp.int32, (q, q), 1)
    return jnp.where(r == _row_of_tok(t), 1.0, 0.0).astype(BF16)


def _fill_wrap(buf_ref, wrap_ref, c, wrap, tail_ref, cols=slice(None)):
    q = SSD_CHUNK
    for m in range(wrap // SUBLANES):
        r_cur = (c + 1) * q - wrap + m * SUBLANES
        cur = buf_ref[r_cur:r_cur + SUBLANES, cols]
        if c == 0:
            prv = tail_ref[m * SUBLANES:(m + 1) * SUBLANES, cols]
        else:
            prv = buf_ref[r_cur - q:r_cur - q + SUBLANES, cols]
        sub0 = lax.broadcasted_iota(jnp.int32, cur.shape, 0) == 0
        wrap_ref[c * wrap + m * SUBLANES:c * wrap + (m + 1) * SUBLANES, cols] = jnp.where(
            sub0, pltpu.roll(prv, 1, 0), pltpu.roll(cur, 1, 0))


def _conv_seg(buf_ref, wrap_ref, w_ref, b_ref, c, wrap, c0, cw):
    q = SSD_CHUNK
    taps = w_ref.shape[0]
    cols = slice(c0, c0 + cw)
    acc = b_ref[:, cols] + w_ref[taps - 1:taps, cols] * buf_ref[c * q:(c + 1) * q, cols]
    for k in range(taps - 1):
        back = (taps - 1 - k) * SUBLANES
        shifted = jnp.concatenate(
            [wrap_ref[(c + 1) * wrap - back:(c + 1) * wrap, cols], buf_ref[c * q:(c + 1) * q - back, cols]],
            axis=0)
        acc = acc + w_ref[k:k + 1, cols] * shifted
    return acc


def _seg_tail_rows(wrap):
    n = wrap // SUBLANES
    return [(NPOS - n + m) * SUBLANES + SUBLANES - 1 for m in range(n)]


def _prompt_mixer_kernel(
        x_ref, wz_ref, wxbc_ref, wdt_ref, wlx_ref, wly_ref, wgs_ref, wgl_ref,
        scw_ref, scb_ref, dtb_ref, aneg_ref, dexp_ref, ng_ref, wso_ref,
        lcw_ref, lcb_ref, wax_ref, ba_ref, bx_ref, lam_ref, wlo_ref,
        bg_ref, wo_ref, l1g_ref, l1b_ref, e_ref,
        x1_ref, st_ref, sconv_ref, lst_ref, lconv_ref,
        xb_s, xp_s, xbc_s, swrap_s, stail_s, xc_s, lx_s, lwrap_s, ltail_s, xr_s, ly_s, y_s, z_s, ysb_s, ylb_s,
        ht_s, hl_s, gs_s, gl_s, ys_s, yl_s, o_s, mb_s,
        *, tl, alpha, n_heads, headdim, d_state):
    t = pl.program_id(1)
    nt = pl.num_programs(1)
    d_ssd = n_heads * headdim
    gn = SSD_GROUPS * d_state
    hpg = n_heads // SSD_GROUPS
    gw = hpg * headdim
    d_lru = lx_s.shape[1]
    d_model = x_ref.shape[2]
    q = SSD_CHUNK

    nch = tl // q
    wrap_s = (scw_ref.shape[0] - 1) * SUBLANES
    wrap_l = (lcw_ref.shape[0] - 1) * SUBLANES

    @pl.when(t == 0)
    def _():
        stail_s[...] = jnp.zeros(stail_s.shape, F32)
        ltail_s[...] = jnp.zeros(ltail_s.shape, F32)
        ht_s[...] = jnp.zeros(ht_s.shape, F32)
        hl_s[...] = jnp.zeros(hl_s.shape, F32)

    perm = _perm_matrix()
    perm2 = jnp.concatenate([perm, perm], axis=1)
    for c in range(nch):
        rows = _rows(c, q)
        xp = _dot_2way_lhs(perm2, x_ref[0, rows, :])
        xp_s[rows, :] = xp
        xb_s[rows, :] = xp.astype(BF16)

    cwid = 512

    def proj(dst_ref, w_ref, c0, cw):
        dst_ref[:, c0:c0 + cw] = _dot(xb_s[...], w_ref[:, c0:c0 + cw])

    def lru_conv(c):
        _fill_wrap(lx_s, lwrap_s, c, wrap_l, ltail_s)
        for c0 in range(0, d_lru, cwid):
            xr_s[c * q:(c + 1) * q, c0:c0 + cwid] = _conv_seg(lx_s, lwrap_s, lcw_ref, lcb_ref, c, wrap_l, c0, cwid)
        if c == nch - 1:
            ltail_s[...] = lx_s[tl - wrap_l:tl, :]

    bw = d_lru // LRU_BLOCKS
    sub = lax.broadcasted_iota(jnp.int32, (SUBLANES, bw), 0)
    crow = lax.broadcasted_iota(jnp.int32, (q, bw), 0)

    def lru_block(n):
        cols = slice(n * bw, (n + 1) * bw)
        xr = xr_s[:, cols]
        xrb = xr.astype(BF16)
        rg = _dot(xrb, wax_ref[n])
        r = _sigmoid(rg[:, 0:bw] + ba_ref[:, cols])
        gi = _sigmoid(rg[:, bw:2 * bw] + bx_ref[:, cols])
        log_a = (-LRU_C) * r * _softplus(-lam_ref[:, cols])
        a_all = jnp.exp(log_a)
        mult_all = jnp.sqrt(1.0 - a_all * a_all)
        for c in range(nch):
            a = a_all[c * q:(c + 1) * q, :]
            mult = mult_all[c * q:(c + 1) * q, :]
            if c == 0:
                mult = jnp.where(jnp.logical_and(crow == 0, t == 0), 1.0, mult)
            u = mult * gi[c * q:(c + 1) * q, :] * xr[c * q:(c + 1) * q, :]
            a_p = [a[i * SUBLANES:(i + 1) * SUBLANES, :] for i in range(NPOS)]
            u_p = [u[i * SUBLANES:(i + 1) * SUBLANES, :] for i in range(NPOS)]
            h = u_p[0]
            g = a_p[0]
            for i in range(1, NPOS):
                h = a_p[i] * h + u_p[i]
                g = a_p[i] * g
            gs = jnp.where(sub == 0, 0.0, pltpu.roll(g, 1, 0))
            hs = jnp.where(sub == 0, hl_s[0:1, cols], pltpu.roll(h, 1, 0))
            d = 1
            while d < SUBLANES:
                keep = sub >= d
                hs = jnp.where(keep, gs * pltpu.roll(hs, d, 0) + hs, hs)
                gs = jnp.where(keep, gs * pltpu.roll(gs, d, 0), gs)
                d *= 2
            h = hs
            out = []
            for i in range(NPOS):
                h = a_p[i] * h + u_p[i]
                out.append(h)
            hl_s[0:1, cols] = h[SUBLANES - 1:SUBLANES, :]
            hseq = jnp.concatenate(out, axis=0)
            ylb_s[c * q:(c + 1) * q, cols] = (hseq * _gelu(ly_s[c * q:(c + 1) * q, cols])).astype(BF16)

    def ssd_conv(c0):
        cols = slice(c0, c0 + cwid)
        for c in range(nch):
            _fill_wrap(xbc_s, swrap_s, c, wrap_s, stail_s, cols)
        stail_s[:, cols] = xbc_s[tl - wrap_s:tl, cols]
        for c in range(nch):
            xc_s[c * q:(c + 1) * q, cols] = _silu_of_half(
                _conv_seg(xbc_s, swrap_s, scw_ref, scb_ref, c, wrap_s, c0, cwid))

    def merge_gate(dst_ref, c0, cw, b0):
        dst_ref[:, c0:c0 + cw] = _sigmoid(dst_ref[:, c0:c0 + cw] + bg_ref[:, b0 + c0:b0 + c0 + cw])

    tok_r = _tok_of_row(lax.broadcasted_iota(jnp.int32, (q, q), 0))
    tok_c = _tok_of_row(lax.broadcasted_iota(jnp.int32, (q, q), 1))
    causal = tok_r >= tok_c
    tri = jnp.where(causal, 1.0, 0.0).astype(BF16)
    lane_i = lax.broadcasted_iota(jnp.int32, (q, LANES), 1)
    left = lane_i < headdim

    def chunk_body(c, carry):
        rows = _rows(c, q)
        dt = _softplus(_dot(xb_s[rows, :], wdt_ref[...]) + dtb_ref[...])
        d_a = dt * aneg_ref[...]
        a_cs = _dot_exact_lhs(tri, d_a)
        a_last = a_cs[q - 1:q, :]
        wgt = dt * jnp.exp(a_last - a_cs)
        ea = jnp.exp(a_cs)
        w_exp = _dot_2way_rhs(wgt, e_ref[...])
        ea_exp = _dot_2way_rhs(ea, e_ref[...])
        a_cs_t = a_cs.T
        dt_t = dt.T
        for g in range(SSD_GROUPS):
            b_g = xc_s[rows, d_ssd + g * d_state:d_ssd + (g + 1) * d_state]
            c_g = xc_s[rows, d_ssd + gn + g * d_state:d_ssd + gn + (g + 1) * d_state]
            b_gb = b_g.astype(BF16)
            c_gb = c_g.astype(BF16)
            cb = _dot_nt(c_gb, b_gb)
            for hp in range(hpg // 2):
                c0 = g * gw + hp * 2 * headdim
                xs_pair = xc_s[rows, c0:c0 + 2 * headdim]
                lmats = []
                for j in range(2):
                    h = g * hpg + hp * 2 + j
                    seg = (jnp.broadcast_to(a_cs[:, h:h + 1], (q, q))
                           - jnp.broadcast_to(a_cs_t[h:h + 1, :], (q, q)))
                    dec = jnp.exp(jnp.where(causal, seg, -jnp.inf))
                    lmats.append((cb * dec * jnp.broadcast_to(dt_t[h:h + 1, :], (q, q))).astype(BF16))
                lpair = jnp.concatenate(lmats, axis=1)
                rhs = jnp.concatenate([jnp.where(left, xs_pair, 0.0),
                                       jnp.where(left, 0.0, xs_pair)], axis=0).astype(BF16)
                y_s[rows, c0:c0 + 2 * headdim] = _dot(lpair, rhs)
            gcols = slice(g * gw, (g + 1) * gw)
            h_g = ht_s[:, gcols]
            y_off = _dot(c_gb, h_g.astype(BF16)) * ea_exp[:, gcols]
            y_s[rows, gcols] = y_s[rows, gcols] + y_off
            xw = (xc_s[rows, gcols] * w_exp[:, gcols]).astype(BF16)
            ht_s[:, gcols] = h_g * ea_exp[q - 1:q, gcols] + _dot_tn(b_gb, xw)
            if carry is not None and g < len(carry):
                carry[g]()
        return carry

    def gate_body(i, carry):
        rows = _rows(i, ROW_BLK)
        y = y_s[rows, :] + dexp_ref[...] * xc_s[rows, 0:d_ssd]
        y = y * _silu_of_half(z_s[rows, :])
        ms = jnp.mean(y * y, axis=-1, keepdims=True)
        ysb_s[rows, :] = (y * lax.rsqrt(ms + RMS_EPS) * ng_ref[...]).astype(BF16)
        return carry

    assert nch == 2 and LRU_BLOCKS == 8 and d_lru == 2 * cwid and d_model == 2 * cwid
    d_xbc = xc_s.shape[1]
    nxb = d_xbc // cwid
    proj(lx_s, wlx_ref, 0, cwid)
    proj(lx_s, wlx_ref, cwid, cwid)
    proj(ly_s, wly_ref, 0, cwid); lru_conv(0)
    proj(ly_s, wly_ref, cwid, cwid); lru_conv(1)
    P = functools.partial
    mxu_a = ([P(proj, xbc_s, wxbc_ref, j * cwid, cwid) for j in range(nxb)]
             + [P(proj, gs_s, wgs_ref, j * cwid, cwid) for j in range(2)])
    for n in range(LRU_BLOCKS):
        mxu_a[n]()
        lru_block(n)
        if 1 <= n <= nxb:
            ssd_conv((n - 1) * cwid)
    gates_per_chunk = q // ROW_BLK
    nz = d_ssd // cwid
    chunk_body(0, [P(proj, z_s, wz_ref, j * cwid, cwid) for j in range(nz)])
    merge_gate(gs_s, 0, d_model, 0)
    for i in range(gates_per_chunk):
        gate_body(i, None)

    def wlo_piece(j):
        yl_s[:, j * cwid:(j + 1) * cwid] = _dot(ylb_s[...], wlo_ref[:, j * cwid:(j + 1) * cwid])

    chunk_body(1, [P(proj, gl_s, wgl_ref, 0, cwid), P(proj, gl_s, wgl_ref, cwid, cwid),
                   P(wlo_piece, 0), P(wlo_piece, 1)])
    merge_gate(gl_s, 0, d_model, d_model)
    for i in range(gates_per_chunk, 2 * gates_per_chunk):
        gate_body(i, None)

    ys_s[...] = _dot(ysb_s[...], wso_ref[...])
    for i in range(tl // ROW_BLK):
        rows = _rows(i, ROW_BLK)
        mb_s[rows, :] = (gs_s[rows, :] * ys_s[rows, :] + gl_s[rows, :] * yl_s[rows, :]).astype(BF16)
    o_s[...] = _dot(mb_s[...], wo_ref[...])
    for i in range(tl // ROW_BLK):
        rows = _rows(i, ROW_BLK)
        v = alpha * xp_s[rows, :] + o_s[rows, :]
        x1_ref[0, rows, :] = _layer_norm(v, l1g_ref[...], l1b_ref[...])

    @pl.when(t == nt - 1)
    def _():
        st_ref[0] = ht_s[...].T
        lst_ref[0] = hl_s[0:1, :]
        for m, r in enumerate(_seg_tail_rows(wrap_s)):
            sconv_ref[0, m:m + 1, :] = xbc_s[tl - q + r:tl - q + r + 1, :]
        for m, r in enumerate(_seg_tail_rows(wrap_l)):
            lconv_ref[0, m:m + 1, :] = lx_s[tl - q + r:tl - q + r + 1, :]


def _const_spec(shape):
    nd = len(shape)
    return pl.BlockSpec(shape, lambda *_: (0,) * nd, pipeline_mode=pl.Buffered(1))


def _prompt_mixer(x, wts, tl):
    nb, seq, d_model = x.shape
    n_heads, headdim, d_state = wts['n_heads'], wts['headdim'], wts['d_state']
    d_ssd = n_heads * headdim
    d_xbc = d_ssd + 2 * SSD_GROUPS * d_state
    d_lru = wts['wlx'].shape[1]
    names = ['wz', 'wxbc', 'wdt', 'wlx', 'wly', 'wgs', 'wgl', 'scw', 'scb', 'dtb', 'aneg', 'dexp', 'ng',
             'wso', 'lcw', 'lcb', 'wax', 'ba', 'bx', 'lam', 'wlo', 'bg', 'wo', 'l1g', 'l1b', 'ee']
    consts = [wts[k] for k in names]
    kern = functools.partial(_prompt_mixer_kernel, tl=tl, alpha=wts['alpha'], n_heads=n_heads,
                             headdim=headdim, d_state=d_state)
    out_shape = (
        jax.ShapeDtypeStruct((nb, seq, d_model), F32),
        jax.ShapeDtypeStruct((nb, d_ssd, d_state), F32),
        jax.ShapeDtypeStruct((nb, 3, d_xbc), F32),
        jax.ShapeDtypeStruct((nb, 1, d_lru), F32),
        jax.ShapeDtypeStruct((nb, 3, d_lru), F32),
    )
    out_specs = (
        pl.BlockSpec((1, tl, d_model), lambda b, t: (b, t, 0)),
        pl.BlockSpec((1, d_ssd, d_state), lambda b, t: (b, 0, 0)),
        pl.BlockSpec((1, 3, d_xbc), lambda b, t: (b, 0, 0)),
        pl.BlockSpec((1, 1, d_lru), lambda b, t: (b, 0, 0)),
        pl.BlockSpec((1, 3, d_lru), lambda b, t: (b, 0, 0)),
    )
    nch = tl // SSD_CHUNK
    wrap_s = (wts['scw'].shape[0] - 1) * SUBLANES
    wrap_l = (wts['lcw'].shape[0] - 1) * SUBLANES
    scratch = [
        pltpu.VMEM((tl, d_model), BF16),
        pltpu.VMEM((tl, d_model), F32),
        pltpu.VMEM((tl, d_xbc), F32),
        pltpu.VMEM((nch * wrap_s, d_xbc), F32),
        pltpu.VMEM((wrap_s, d_xbc), F32),
        pltpu.VMEM((tl, d_xbc), F32),
        pltpu.VMEM((tl, d_lru), F32),
        pltpu.VMEM((nch * wrap_l, d_lru), F32),
        pltpu.VMEM((wrap_l, d_lru), F32),
        pltpu.VMEM((tl, d_lru), F32),
        pltpu.VMEM((tl, d_lru), F32),
        pltpu.VMEM((tl, d_ssd), F32),
        pltpu.VMEM((tl, d_ssd), F32),
        pltpu.VMEM((tl, d_ssd), BF16),
        pltpu.VMEM((tl, d_lru), BF16),
        pltpu.VMEM((d_state, d_ssd), F32),
        pltpu.VMEM((SUBLANES, d_lru), F32),
        pltpu.VMEM((tl, d_model), F32),
        pltpu.VMEM((tl, d_model), F32),
        pltpu.VMEM((tl, d_model), F32),
        pltpu.VMEM((tl, d_model), F32),
        pltpu.VMEM((tl, d_model), F32),
        pltpu.VMEM((tl, d_model), BF16),
    ]
    return pl.pallas_call(
        kern,
        grid=(nb, seq // tl),
        in_specs=[pl.BlockSpec((1, tl, d_model), lambda b, t: (b, t, 0))]
        + [_const_spec(c.shape) for c in consts],
        out_specs=out_specs,
        out_shape=out_shape,
        scratch_shapes=scratch,
        compiler_params=pltpu.CompilerParams(
            dimension_semantics=("arbitrary", "arbitrary"),
            vmem_limit_bytes=VMEM_LIMIT_BYTES),
        name="prompt_mixer",
    )(x, *consts)


def _ffn_kernel(x_ref, h0_ref, wg_ref, wu_ref, cw_ref, cb_ref, wd_ref, g_ref, b_ref,
                y_ref, tail_ref, xb_s, gb_s, hb_s, *, tm, stride, hist, alpha, fchunk):
    t = pl.program_id(1)
    nt = pl.num_programs(1)
    taps = cw_ref.shape[0]
    nh = (taps - 1) * stride
    d_ff = gb_s.shape[1]

    @pl.when(t == 0)
    def _():
        gb_s[hist - nh:hist, :] = h0_ref[0]

    xb_s[...] = x_ref[0].astype(BF16)
    acc = None
    for c0 in range(0, d_ff, fchunk):
        cols = slice(c0, c0 + fchunk)
        gb_s[hist:hist + tm, cols] = _dot(xb_s[...], wg_ref[:, cols])
        up = _dot(xb_s[...], wu_ref[:, cols])
        gc = _conv_block(gb_s, cw_ref, cb_ref, hist, stride, 0, tm, c0, fchunk)
        hb_s[...] = (_gelu(gc) * up).astype(BF16)
        part = _dot(hb_s[...], wd_ref[c0 // 2:(c0 + fchunk) // 2, :])
        acc = part if acc is None else acc + part
    v = alpha * x_ref[0] + acc
    y_ref[0] = _layer_norm(v, g_ref[...], b_ref[...])
    gb_s[hist - nh:hist, :] = gb_s[hist + tm - nh:hist + tm, :]

    @pl.when(t == nt - 1)
    def _():
        tail_ref[0] = gb_s[hist - nh:hist, :]


def _ffn_seg_kernel(*refs, tm, alpha, fchunk, state_dims):
    steps = state_dims['steps']
    x_ref, h0_ref, st_ref = refs[0:3]
    cb_refs, xe_refs = refs[3:3 + steps], refs[3 + steps:3 + 2 * steps]
    wg_ref, wu_ref, cw_ref, cb_ref, wd_ref, g_ref, b_ref = refs[3 + 2 * steps:10 + 2 * steps]
    y_ref, tail_ref, nst_ref = refs[10 + 2 * steps:13 + 2 * steps]
    yoff_refs = refs[13 + 2 * steps:13 + 3 * steps]
    xb_s, gb_s, gwrap_s, gtail_s, hb_s, cb8_s, xe8_s, yo8_s = refs[13 + 3 * steps:]

    t = pl.program_id(1)
    pps = st_ref.shape[0]
    step_idx = pl.program_id(0) * pl.num_programs(1) + t
    row0 = lax.rem(step_idx * (2 * pps), SUBLANES)
    for p in range(pps):
        for l in range(steps):
            for e in range(2):
                src = pl.ds(row0 + 2 * p + e, 1)
                cb8_s[p, 2 * l + e:2 * l + e + 1, :] = cb_refs[l][src, :]
                xe8_s[p, 2 * l + e:2 * l + e + 1, :] = xe_refs[l][src, :]
    state_pieces = [
        functools.partial(_state_pair_group, g, st_ref.at[p], cb8_s.at[p], xe8_s.at[p],
                          nst_ref.at[p], yo8_s.at[p], **state_dims)
        for p in range(pps) for g in range(SSD_GROUPS)]

    def state_work(n):
        for _ in range(min(n, len(state_pieces))):
            state_pieces.pop(0)()
    nt = pl.num_programs(1)
    q = SSD_CHUNK
    nch = tm // q
    wrap = (cw_ref.shape[0] - 1) * SUBLANES
    d_ff = gb_s.shape[1]
    tail_rows = _seg_tail_rows(wrap)

    @pl.when(t == 0)
    def _():
        gtail_s[...] = jnp.zeros(gtail_s.shape, F32)
        for m in range(len(tail_rows)):
            r = m * SUBLANES + SUBLANES - 1
            gtail_s[r:r + 1, :] = h0_ref[0, m:m + 1, :]

    xb_s[...] = x_ref[0].astype(BF16)
    def gate_proj(c0):
        cols = slice(c0, c0 + fchunk)
        gb_s[:, cols] = _dot(xb_s[...], wg_ref[:, cols])
        for c in range(nch):
            _fill_wrap(gb_s, gwrap_s, c, wrap, gtail_s, cols)
        gtail_s[:, cols] = gb_s[tm - wrap:tm, cols]

    per_slot = -(-len(state_pieces) // (1 + 2 * (d_ff // fchunk)))
    gate_proj(0)
    state_work(per_slot)
    acc = None
    for c0 in range(0, d_ff, fchunk):
        cols = slice(c0, c0 + fchunk)
        up = _dot(xb_s[...], wu_ref[:, cols])
        if c0 + fchunk < d_ff:
            gate_proj(c0 + fchunk)
        state_work(per_slot)
        for c in range(nch):
            gc = _conv_seg(gb_s, gwrap_s, cw_ref, cb_ref, c, wrap, c0, fchunk)
            hb_s[c * q:(c + 1) * q, :] = (_gelu(gc) * up[c * q:(c + 1) * q, :]).astype(BF16)
        part = _dot(hb_s[...], wd_ref[c0 // 2:(c0 + fchunk) // 2, :])
        state_work(per_slot)
        acc = part if acc is None else acc + part
    state_work(len(state_pieces))
    for p in range(pps):
        for l in range(steps):
            for e in range(2):
                yoff_refs[l][pl.ds(row0 + 2 * p + e, 1), :] = yo8_s[p, 2 * l + e:2 * l + e + 1, :]

    unperm = _unperm_matrix()
    unperm2 = jnp.concatenate([unperm, unperm], axis=1)
    for c in range(nch):
        rows = slice(c * q, (c + 1) * q)
        y = _layer_norm(alpha * x_ref[0, rows, :] + acc[rows, :], g_ref[...], b_ref[...])
        y_ref[0, rows, :] = _dot_2way_lhs(unperm2, y)

    @pl.when(t == nt - 1)
    def _():
        for m, r in enumerate(tail_rows):
            tail_ref[0, m:m + 1, :] = gb_s[tm - q + r:tm - q + r + 1, :]


def _ffn_seg(x, hist0, state, cb_lm, xe_lm, wts, tm, steps):
    nb, seq, d_model = x.shape
    nseq = 2 * state.shape[0]
    d_ssd = xe_lm.shape[1] // 2
    assert cb_lm.shape[0] == steps * nseq and nseq % SUBLANES == 0
    d_ff = wts['wg'].shape[1]
    taps = wts['fcw'].shape[0]
    wrap = (taps - 1) * SUBLANES
    fchunk = 1024
    nt = seq // tm
    npairs = state.shape[0]
    pps = npairs // (nb * nt)
    assert pps * nb * nt == npairs
    consts = [wts[k] for k in ['wg', 'wu', 'fcw', 'fcb', 'wd', 'l2g', 'l2b']]
    state_dims = dict(steps=steps, n_heads=wts['n_heads'], headdim=wts['headdim'], d_state=wts['d_state'])
    kern = functools.partial(_ffn_seg_kernel, tm=tm, alpha=wts['alpha'], fchunk=fchunk, state_dims=state_dims)
    pair_blk = lambda a: pl.BlockSpec((pps,) + tuple(a.shape[1:]),
                                      lambda b, t: (b * nt + t,) + (0,) * (len(a.shape) - 1))

    def rows_blk(l, width):
        return pl.BlockSpec((SUBLANES, width),
                            lambda b, t: (l * (nseq // SUBLANES) + ((b * nt + t) * 2 * pps) // SUBLANES, 0))

    def yoff_blk():
        return pl.BlockSpec((SUBLANES, d_ssd), lambda b, t: (((b * nt + t) * 2 * pps) // SUBLANES, 0))

    outs = pl.pallas_call(
        kern,
        grid=(nb, nt),
        in_specs=[pl.BlockSpec((1, tm, d_model), lambda b, t: (b, t, 0)),
                  pl.BlockSpec((1, taps - 1, d_ff), lambda b, t: (b, 0, 0)),
                  pair_blk(state)]
        + [rows_blk(l, cb_lm.shape[1]) for l in range(steps)]
        + [rows_blk(l, xe_lm.shape[1]) for l in range(steps)]
        + [_const_spec(c.shape) for c in consts],
        out_specs=(pl.BlockSpec((1, tm, d_model), lambda b, t: (b, t, 0)),
                   pl.BlockSpec((1, taps - 1, d_ff), lambda b, t: (b, 0, 0)),
                   pair_blk(state)) + tuple(yoff_blk() for _ in range(steps)),
        out_shape=(jax.ShapeDtypeStruct((nb, seq, d_model), F32),
                   jax.ShapeDtypeStruct((nb, taps - 1, d_ff), F32),
                   jax.ShapeDtypeStruct(state.shape, F32))
        + tuple(jax.ShapeDtypeStruct((nseq, d_ssd), F32) for _ in range(steps)),
        scratch_shapes=[pltpu.VMEM((tm, d_model), BF16),
                        pltpu.VMEM((tm, d_ff), F32),
                        pltpu.VMEM((tm // SSD_CHUNK * wrap, d_ff), F32),
                        pltpu.VMEM((wrap, d_ff), F32),
                        pltpu.VMEM((tm, fchunk), BF16),
                        pltpu.VMEM((pps, 2 * steps, cb_lm.shape[1]), F32),
                        pltpu.VMEM((pps, 2 * steps, xe_lm.shape[1]), F32),
                        pltpu.VMEM((pps, 2 * steps, d_ssd), F32)],
        compiler_params=pltpu.CompilerParams(
            dimension_semantics=("arbitrary", "arbitrary"),
            vmem_limit_bytes=VMEM_LIMIT_BYTES),
        name="conv_ffn_seg",
    )(x, hist0, state, *([cb_lm] * steps), *([xe_lm] * steps), *consts)
    return outs[0], outs[1], outs[2], outs[3:]


def _ffn(x, hist0, wts, tm, stride):
    nb, seq, d_model = x.shape
    d_ff = wts['wg'].shape[1]
    taps = wts['fcw'].shape[0]
    nh = (taps - 1) * stride
    hist = -(-nh // SUBLANES) * SUBLANES
    consts = [wts[k] for k in ['wg', 'wu', 'fcw', 'fcb', 'wd', 'l2g', 'l2b']]
    kern = functools.partial(_ffn_kernel, tm=tm, stride=stride, hist=hist, alpha=wts['alpha'], fchunk=1024)
    return pl.pallas_call(
        kern,
        grid=(nb, seq // tm),
        in_specs=[pl.BlockSpec((1, tm, d_model), lambda b, t: (b, t, 0)),
                  pl.BlockSpec((1, nh, d_ff), lambda b, t: (b, 0, 0))]
        + [_const_spec(c.shape) for c in consts],
        out_specs=(pl.BlockSpec((1, tm, d_model), lambda b, t: (b, t, 0)),
                   pl.BlockSpec((1, nh, d_ff), lambda b, t: (b, 0, 0))),
        out_shape=(jax.ShapeDtypeStruct((nb, seq, d_model), F32),
                   jax.ShapeDtypeStruct((nb, nh, d_ff), F32)),
        scratch_shapes=[pltpu.VMEM((tm, d_model), BF16),
                        pltpu.VMEM((hist + tm, d_ff), F32),
                        pltpu.VMEM((tm, 1024), BF16)],
        compiler_params=pltpu.CompilerParams(
            dimension_semantics=("arbitrary", "arbitrary"),
            vmem_limit_bytes=VMEM_LIMIT_BYTES),
        name="conv_ffn",
    )(x, hist0, *consts)


def _sample_ssd_kernel(
        x_ref, xall_ref, cssd_ref, wxbc_ref, wdt_ref, scw_ref, scb_ref, dtb_ref, aneg_ref, dexp_ref,
        e_ref, gsum_ref, e2_ref,
        pre_ref, cb_ref, xe_ref, ydg_ref,
        xbc_s, xs_s, bs_s, acs_s, dts_s,
        *, nseq, steps, n_heads, headdim, d_state):
    l = pl.program_id(0)
    d_ssd = n_heads * headdim
    gn = SSD_GROUPS * d_state
    hist = (scw_ref.shape[0] - 1) * nseq
    r0 = l * nseq

    def blk(i):
        return pl.ds(pl.multiple_of(i * nseq, nseq), nseq)

    def sblk(i):
        return slice(i * nseq, (i + 1) * nseq)

    @pl.when(l == 0)
    def _():
        xbc_s[0:hist, :] = cssd_ref[...]
        dts_s[...] = _softplus(_dot(xall_ref[...].astype(BF16), wdt_ref[...]) + dtb_ref[...])
        acc = jnp.zeros((nseq, LANES), F32)
        for s in range(steps):
            acc = acc + dts_s[sblk(s), :] * aneg_ref[...]
            acs_s[sblk(s), :] = acc

    xb = x_ref[...].astype(BF16)

    pre = _dot(xb, wxbc_ref[...])
    pre_ref[...] = pre
    xbc_s[pl.ds(pl.multiple_of(hist + r0, nseq), nseq), :] = pre
    cwid = 512
    for c0 in range(0, d_ssd, cwid):
        xs_s[blk(l), c0:c0 + cwid] = _silu_of_half(
            _conv_block(xbc_s, scw_ref, scb_ref, hist, nseq, r0, nseq, c0, cwid))
    b_l = _silu_of_half(_conv_block(xbc_s, scw_ref, scb_ref, hist, nseq, r0, nseq, d_ssd, gn))
    c_l = _silu_of_half(_conv_block(xbc_s, scw_ref, scb_ref, hist, nseq, r0, nseq, d_ssd + gn, gn))
    bs_s[blk(l), :] = b_l
    cb_ref[:, 0:gn] = c_l
    cb_ref[:, gn:2 * gn] = b_l

    a_cs = acs_s[blk(l), :]
    dt = dts_s[blk(l), :]
    a_end = acs_s[sblk(steps - 1), :]
    xe_ref[:, d_ssd:2 * d_ssd] = _dot_2way_rhs(jnp.exp(a_cs), e_ref[...])
    xe_ref[:, 0:d_ssd] = xs_s[blk(l), :] * _dot_2way_rhs(dt * jnp.exp(a_end - a_cs), e_ref[...])

    ydg_ref[...] = dexp_ref[...] * xs_s[blk(l), :]
    for s in range(steps):
        @pl.when(s <= l)
        def _(s=s):
            coef = _dot_2way_rhs(jnp.exp(a_cs - acs_s[sblk(s), :]) * dts_s[sblk(s), :], e_ref[...])
            cbx = _dot_2way_rhs(_dot_2way_rhs(bs_s[sblk(s), :] * c_l, gsum_ref[...]), e2_ref[...])
            ydg_ref[...] += cbx * coef * xs_s[sblk(s), :]


def _sample_lru_kernel(
        x_ref, clru_ref, slru_ref, wlx_ref, wly_ref, lcw_ref, lcb_ref, wax_ref, ba_ref, bx_ref,
        lam_ref, prelx_ref, ylru_ref, lst_ref, lx_s, hl_s, *, nseq, steps, start_pos):
    l = pl.program_id(0)
    hist = (lcw_ref.shape[0] - 1) * nseq
    d_lru = lx_s.shape[1]
    r0 = l * nseq

    @pl.when(l == 0)
    def _():
        lx_s[0:hist, :] = clru_ref[...]
        hl_s[...] = slru_ref[...]

    xb = x_ref[...].astype(BF16)
    prelx = _dot(xb, wlx_ref[...])
    prelx_ref[...] = prelx
    lx_s[pl.ds(pl.multiple_of(hist + r0, nseq), nseq), :] = prelx
    ly = _dot(xb, wly_ref[...])
    bw = d_lru // LRU_BLOCKS
    first = (l + start_pos) == 0
    for n in range(LRU_BLOCKS):
        cols = slice(n * bw, (n + 1) * bw)
        xr = _conv_block(lx_s, lcw_ref, lcb_ref, hist, nseq, r0, nseq, n * bw, bw)
        xrb = xr.astype(BF16)
        rg = _dot(xrb, wax_ref[n])
        r = _sigmoid(rg[:, 0:bw] + ba_ref[:, cols])
        gi = _sigmoid(rg[:, bw:2 * bw] + bx_ref[:, cols])
        log_a = (-LRU_C) * r * _softplus(-lam_ref[:, cols])
        a = jnp.exp(log_a)
        mult = jnp.where(first, 1.0, jnp.sqrt(1.0 - a * a))
        h = a * hl_s[:, cols] + mult * gi * xr
        hl_s[:, cols] = h
        ylru_ref[:, cols] = (h * _gelu(ly[:, cols])).astype(BF16)

    @pl.when(l == steps - 1)
    def _():
        lst_ref[...] = hl_s[...]


def _state_pair_group(g, st_ref, cb8_ref, xe8_ref, nst_ref, yoff8_ref,
                      *, steps, n_heads, headdim, d_state):
    nrow = 2 * steps
    hpg = n_heads // SSD_GROUPS
    gw = hpg * headdim
    gn = SSD_GROUPS * d_state
    d_ssd = n_heads * headdim
    assert 2 * headdim == LANES and d_state == LANES
    par = lax.broadcasted_iota(jnp.int32, (nrow, gw), 0) % 2
    low = lax.broadcasted_iota(jnp.int32, (nrow, LANES), 1) < headdim
    gcols = slice(g * gw, (g + 1) * gw)
    c8g = cb8_ref[:, g * d_state:(g + 1) * d_state].astype(BF16)
    b8g = cb8_ref[:, gn + g * d_state:gn + (g + 1) * d_state].astype(BF16)
    xw8 = xe8_ref[:, g * gw:(g + 1) * gw]
    ea8 = xe8_ref[:, d_ssd + g * gw:d_ssd + (g + 1) * gw]
    cds = []
    for hp in range(hpg // 2):
        pair = ea8[:, hp * LANES:(hp + 1) * LANES]
        swapped = pltpu.roll(pair, headdim, 1)
        cds.append(jnp.where(low, pair, swapped))
        cds.append(jnp.where(low, swapped, pair))
    yo = None
    for e in range(2):
        sg = st_ref[e, gcols, :]
        yo_e = _dot_nt(c8g, sg.astype(BF16))
        yo = yo_e if e == 0 else jnp.where(par == e, yo_e, yo)
        xw_e = jnp.where(par == e, xw8, 0.0).astype(BF16)
        upd = _dot_tn(xw_e, b8g)
        k_last = 2 * (steps - 1) + e
        for hh in range(hpg):
            cd = cds[hh][k_last:k_last + 1, :]
            hr = slice(hh * headdim, (hh + 1) * headdim)
            nst_ref[e, g * gw + hh * headdim:g * gw + (hh + 1) * headdim, :] = sg[hr, :] * cd + upd[hr, :]
    yoff8_ref[:, gcols] = yo * ea8


def _sample_post_kernel(x_ref, ydg_ref, yoff_ref, ylru_ref, wz_ref, ng_ref, wso_ref, wlo_ref,
                        wgs_ref, wgl_ref, bg_ref, wo_ref, l1g_ref, l1b_ref, x1_ref, *, alpha):
    d_model = x_ref.shape[1]
    xb = x_ref[...].astype(BF16)
    y = (ydg_ref[...] + yoff_ref[...]) * _silu_of_half(_dot(xb, wz_ref[...]))
    ms = jnp.mean(y * y, axis=-1, keepdims=True)
    ysb = (y * lax.rsqrt(ms + RMS_EPS) * ng_ref[...]).astype(BF16)
    g_ssd = _sigmoid(_dot(xb, wgs_ref[...]) + bg_ref[:, 0:d_model])
    g_lru = _sigmoid(_dot(xb, wgl_ref[...]) + bg_ref[:, d_model:2 * d_model])
    merged = g_ssd * _dot(ysb, wso_ref[...]) + g_lru * _dot(ylru_ref[...], wlo_ref[...])
    o = _dot(merged.astype(BF16), wo_ref[...])
    x1_ref[...] = _layer_norm(alpha * x_ref[...] + o, l1g_ref[...], l1b_ref[...])


def _sample_front(x_lm, cssd_lm, clru_lm, slru_lm, wts, nseq, steps, start_pos):
    n_heads, headdim, d_state = wts['n_heads'], wts['headdim'], wts['d_state']
    d_model = x_lm.shape[1]
    d_ssd = n_heads * headdim
    gn = SSD_GROUPS * d_state
    d_xbc = d_ssd + 2 * gn
    d_lru = wts['wlx'].shape[1]
    ntok = steps * nseq
    dims = dict(nseq=nseq, steps=steps, n_heads=n_heads, headdim=headdim, d_state=d_state)
    params = pltpu.CompilerParams(dimension_semantics=("arbitrary",), vmem_limit_bytes=VMEM_LIMIT_BYTES)
    step_blk = lambda w: pl.BlockSpec((nseq, w), lambda l: (l, 0))

    sds = jax.ShapeDtypeStruct
    k_ssd, k_lru = cssd_lm.shape[0] // nseq, clru_lm.shape[0] // nseq
    assert steps >= max(k_ssd, k_lru)
    tail_blk = lambda w, k: pl.BlockSpec((nseq, w), lambda l: (jnp.maximum(l - (steps - k), 0), 0))
    ssd_names = ['wxbc', 'wdt', 'scw', 'scb', 'dtb', 'aneg', 'dexp', 'ee', 'gsum2', 'gexp2']
    ssd_consts = [x_lm, cssd_lm] + [wts[k] for k in ssd_names]
    pre, cb_lm, xe_lm, ydg_lm = pl.pallas_call(
        functools.partial(_sample_ssd_kernel, **dims),
        grid=(steps,),
        in_specs=[step_blk(d_model)] + [_const_spec(c.shape) for c in ssd_consts],
        out_specs=(tail_blk(d_xbc, k_ssd), step_blk(2 * gn), step_blk(2 * d_ssd), step_blk(d_ssd)),
        out_shape=(sds((k_ssd * nseq, d_xbc), F32), sds((ntok, 2 * gn), F32), sds((ntok, 2 * d_ssd), F32),
                   sds((ntok, d_ssd), F32)),
        scratch_shapes=[
            pltpu.VMEM((cssd_lm.shape[0] + ntok, d_xbc), F32),
            pltpu.VMEM((ntok, d_ssd), F32),
            pltpu.VMEM((ntok, gn), F32),
            pltpu.VMEM((ntok, LANES), F32),
            pltpu.VMEM((ntok, LANES), F32),
        ],
        compiler_params=params,
        name="sample_ssd",
    )(x_lm, *ssd_consts)

    lru_names = ['wlx', 'wly', 'lcw', 'lcb', 'wax', 'ba', 'bx', 'lam']
    lru_consts = [clru_lm, slru_lm] + [wts[k] for k in lru_names]
    prelx, ylru_lm, lst = pl.pallas_call(
        functools.partial(_sample_lru_kernel, nseq=nseq, steps=steps, start_pos=start_pos),
        grid=(steps,),
        in_specs=[step_blk(d_model)] + [_const_spec(c.shape) for c in lru_consts],
        out_specs=(tail_blk(d_lru, k_lru), step_blk(d_lru), pl.BlockSpec((nseq, d_lru), lambda l: (0, 0))),
        out_shape=(sds((k_lru * nseq, d_lru), F32), sds((ntok, d_lru), BF16), sds((nseq, d_lru), F32)),
        scratch_shapes=[
            pltpu.VMEM((clru_lm.shape[0] + ntok, d_lru), F32),
            pltpu.VMEM((nseq, d_lru), F32),
        ],
        compiler_params=params,
        name="sample_lru",
    )(x_lm, *lru_consts)

    return dict(pre=pre, cb=cb_lm, xe=xe_lm, ydg=ydg_lm, prelx=prelx, ylru=ylru_lm, lst=lst)


def _sample_back(x_lm, ydg_lm, yoff_lm, ylru_lm, wts, nseq, steps):
    d_model = x_lm.shape[1]
    d_ssd = wts['n_heads'] * wts['headdim']
    d_lru = wts['wlx'].shape[1]
    ntok = steps * nseq
    params = pltpu.CompilerParams(dimension_semantics=("arbitrary",), vmem_limit_bytes=VMEM_LIMIT_BYTES)
    step_blk = lambda w: pl.BlockSpec((nseq, w), lambda l: (l, 0))
    sds = jax.ShapeDtypeStruct
    post_names = ['wz', 'ng', 'wso', 'wlo', 'wgs', 'wgl', 'bg', 'wo', 'l1g', 'l1b']
    post_consts = [wts[k] for k in post_names]
    x1_lm = pl.pallas_call(
        functools.partial(_sample_post_kernel, alpha=wts['alpha']),
        grid=(steps,),
        in_specs=[step_blk(d_model), step_blk(d_ssd), step_blk(d_ssd), step_blk(d_lru)]
        + [_const_spec(c.shape) for c in post_consts],
        out_specs=step_blk(d_model),
        out_shape=sds((ntok, d_model), F32),
        compiler_params=params,
        name="sample_post",
    )(x_lm, ydg_lm, yoff_lm, ylru_lm, *post_consts)
    return x1_lm


def _prep_weights(w_in, b_gate, ssd_conv_w, ssd_conv_b, ssd_dt_bias, ssd_a_log, ssd_d, ssd_norm_g,
                  w_ssd_out, lru_conv_w, lru_conv_b, lru_wa, lru_ba, lru_wx, lru_bx, lru_lambda,
                  w_lru_out, w_o, ln1_g, ln1_b, ffn_w_gate, ffn_w_up, ffn_conv_w, ffn_conv_b,
                  ffn_w_down, ln2_g, ln2_b, n_heads, headdim, d_state):
    depth = w_in.shape[0]
    d_model = w_in.shape[1]
    d_ssd = n_heads * headdim
    d_xbc = d_ssd + 2 * SSD_GROUPS * d_state
    d_lru = lru_lambda.shape[1]
    sizes = (d_ssd, d_xbc, n_heads, d_lru, d_lru, d_model, d_model)
    cuts = np.cumsum((0,) + sizes)
    scales = [0.5 if i == 0 else 1.0 for i in range(len(sizes))]
    parts = _pack_w_in_t(jnp.swapaxes(w_in, 1, 2)[0].astype(F32), [int(c) for c in cuts], 2, scales)
    row = lambda v: v.reshape(1, -1).astype(F32)
    wax = jnp.concatenate([lru_wa[0], lru_wx[0]], axis=-1)
    wg, wu, wlo, wo, wax_p = _pack_weights([ffn_w_gate[0], ffn_w_up[0], w_lru_out[0], w_o[0],
                                            wax.reshape(-1, wax.shape[-1])])
    wax = wax_p.reshape(wax.shape[0], wax.shape[1] // 2, wax.shape[2])
    wso, = _pack_weights([w_ssd_out[0]])
    wd, = _pack_weights([ffn_w_down[0]])
    pad_heads = lambda v: jnp.pad(v.reshape(1, -1).astype(F32), ((0, 0), (0, LANES - n_heads)))
    head_of_col = np.arange(d_ssd) // headdim
    expand = (np.arange(LANES)[:, None] == head_of_col[None, :]).astype(np.float32)
    group_of_col = head_of_col // (n_heads // SSD_GROUPS)
    group_sum = (np.arange(SSD_GROUPS * d_state)[:, None] // d_state
                 == np.arange(LANES)[None, :])
    group_exp = np.arange(LANES)[:, None] == group_of_col[None, :]
    return dict(
        n_heads=n_heads, headdim=headdim, d_state=d_state,
        alpha=float((2.0 * depth) ** 0.25),
        wz=parts[0], wxbc=parts[1], wdt=parts[2], wlx=parts[3], wly=parts[4], wgs=parts[5], wgl=parts[6],
        scw=0.5 * ssd_conv_w[0].astype(F32), scb=0.5 * row(ssd_conv_b[0]),
        dtb=pad_heads(ssd_dt_bias[0]), aneg=pad_heads(-jnp.exp(ssd_a_log[0].astype(F32))),
        dexp=row(jnp.repeat(ssd_d[0], headdim)), ng=row(ssd_norm_g[0]),
        wso=wso,
        lcw=lru_conv_w[0].astype(F32), lcb=row(lru_conv_b[0]),
        ba=row(lru_ba[0]), bx=row(lru_bx[0]), wax=wax,
        lam=row(lru_lambda[0]), wlo=wlo,
        bg=row(b_gate[0]), wo=wo, l1g=row(ln1_g[0]), l1b=row(ln1_b[0]),
        ee=_pack_rows_01(np.concatenate([expand, expand], axis=0)),
        gsum2=_pack_rows_01(np.concatenate([group_sum, group_sum], axis=0)),
        gexp2=_pack_rows_01(np.concatenate([group_exp, group_exp], axis=0)),
        wg=wg, wu=wu,
        fcw=ffn_conv_w[0].astype(F32), fcb=row(ffn_conv_b[0]), wd=wd,
        l2g=row(ln2_g[0]), l2b=row(ln2_b[0]),
    )


def kernel(x_prompt, x_sample, state_ssd, cache_ssd_conv, state_lru, cache_lru_conv, cache_ffn_conv, w_in, b_gate, ssd_conv_w, ssd_conv_b, ssd_dt_bias, ssd_a_log, ssd_d, ssd_norm_g, w_ssd_out, lru_conv_w, lru_conv_b, lru_wa, lru_ba, lru_wx, lru_bx, lru_lambda, w_lru_out, w_o, ln1_g, ln1_b, ffn_w_gate, ffn_w_up, ffn_conv_w, ffn_conv_b, ffn_w_down, ln2_g, ln2_b):
    assert w_in.shape[0] == 1, "single-layer trunk"
    _, _, n_heads, headdim, d_state = state_ssd.shape
    wts = _prep_weights(w_in, b_gate, ssd_conv_w, ssd_conv_b, ssd_dt_bias, ssd_a_log, ssd_d, ssd_norm_g,
                        w_ssd_out, lru_conv_w, lru_conv_b, lru_wa, lru_ba, lru_wx, lru_bx, lru_lambda,
                        w_lru_out, w_o, ln1_g, ln1_b, ffn_w_gate, ffn_w_up, ffn_conv_w, ffn_conv_b,
                        ffn_w_down, ln2_g, ln2_b, n_heads, headdim, d_state)
    bp = x_prompt.shape[0]
    d_ff = ffn_w_gate.shape[2]

    nb_s, steps, _ = x_sample.shape
    half = nb_s // 2

    def to_lm(a):
        return jnp.swapaxes(a, 0, 1).reshape(a.shape[1] * nb_s, a.shape[2])

    def from_lm(a, k):
        return jnp.swapaxes(a.reshape(k, nb_s, a.shape[1]), 0, 1)

    d_ssd = n_heads * headdim
    x_lm = to_lm(x_sample)
    sf = _sample_front(x_lm, to_lm(cache_ssd_conv[0]), to_lm(cache_lru_conv[0]),
                       to_lm(state_lru[0][:, None, :]), wts, nb_s, steps, PAST_LEN)
    pre, lst, prelx = sf['pre'], sf['lst'], sf['prelx']

    x1_p, p_ssd, p_ssd_buf, p_lru, p_lru_buf = _prompt_mixer(x_prompt, wts, tl=MIXER_TILE)
    y_prompt, p_ffn_buf, new_state, yoff_steps = _ffn_seg(
        x1_p, jnp.zeros((bp, ffn_conv_w.shape[1] - 1, d_ff), F32),
        state_ssd[0].reshape(half, 2, d_ssd, d_state), sf['cb'], sf['xe'], wts, tm=FFN_TILE, steps=steps)
    p_ssd = p_ssd.reshape(1, bp, n_heads, headdim, d_state)

    x1_lm = _sample_back(x_lm, sf['ydg'], jnp.concatenate(yoff_steps, axis=0), sf['ylru'], wts, nb_s, steps)
    y_lm, tail = _ffn(x1_lm[None], to_lm(cache_ffn_conv[0])[None], wts, tm=steps * nb_s, stride=nb_s)
    k_ssd = ssd_conv_w.shape[1] - 1
    k_lru = lru_conv_w.shape[1] - 1
    k_ffn = ffn_conv_w.shape[1] - 1
    assert steps >= max(k_ssd, k_lru, k_ffn)
    return (y_prompt, from_lm(y_lm[0], steps), p_ssd, p_ssd_buf[None], p_lru.reshape(1, bp, -1), p_lru_buf[None],
            p_ffn_buf[None],
            new_state.reshape(1, nb_s, n_heads, headdim, d_state),
            from_lm(pre, k_ssd)[None],
            from_lm(lst, 1).reshape(1, nb_s, -1),
            from_lm(prelx, k_lru)[None],
            from_lm(tail[0], k_ffn)[None])
```

```python
import functools

import numpy as np
import jax
import jax.numpy as jnp
from jax import lax
from jax.experimental import pallas as pl
from jax.experimental.pallas import tpu as pltpu

F32 = jnp.float32
BF16 = jnp.bfloat16

SSD_GROUPS = 4
SSD_CHUNK = 128
LRU_BLOCKS = 8
LRU_C = 8.0
LN_EPS = 1e-5
RMS_EPS = 1e-6
PAST_LEN = 16384

LANES = 128
SUBLANES = 8
VMEM_LIMIT_BYTES = 60 * 1024 * 1024

ROW_BLK = 32
MIXER_TILE = 2 * SSD_CHUNK
FFN_TILE = 4 * SSD_CHUNK


def _dot(a, b):
    if b.dtype == jnp.uint32:
        b = pltpu.bitcast(b, BF16)
    return jnp.dot(a, b, preferred_element_type=F32)


def _pack_kernel(*refs):
    n = len(refs) // 2
    for w_ref, o_ref in zip(refs[:n], refs[n:]):
        o_ref[...] = pltpu.bitcast(w_ref[...].astype(BF16), jnp.uint32)


def _pack_weights(ws):
    k = ws[0].shape[0]
    assert all(w.shape[0] == k for w in ws)
    bk = 512 if k >= 2048 else 256
    return pl.pallas_call(
        _pack_kernel,
        grid=(k // bk,),
        in_specs=[pl.BlockSpec((bk, w.shape[1]), lambda i: (i, 0)) for w in ws],
        out_specs=tuple(pl.BlockSpec((bk // 2, w.shape[1]), lambda i: (i, 0)) for w in ws),
        out_shape=tuple(jax.ShapeDtypeStruct((k // 2, w.shape[1]), jnp.uint32) for w in ws),
        compiler_params=pltpu.CompilerParams(dimension_semantics=("arbitrary",),
                                             vmem_limit_bytes=VMEM_LIMIT_BYTES),
        name="pack_weight",
    )(*[w.astype(F32) for w in ws])


def _pack_t_kernel(wt_ref, wdt_ref, *out_refs, parts, bn, dt_rows):
    j = pl.program_id(0)
    first = 0
    for o_ref, (nblk, scale) in zip(out_refs, parts):
        @pl.when(jnp.logical_and(j >= first, j < first + nblk))
        def _(o_ref=o_ref, scale=scale):
            v = wt_ref[...].T
            if scale != 1.0:
                v = v * scale
            o_ref[...] = pltpu.bitcast(v.astype(BF16), jnp.uint32)
        first += nblk

    @pl.when(j == 0)
    def _():
        rows = lax.broadcasted_iota(jnp.int32, wdt_ref.shape, 0)
        v = jnp.where(rows < dt_rows, wdt_ref[...], 0.0)
        out_refs[-1][...] = pltpu.bitcast(v.T.astype(BF16), jnp.uint32)


def _pack_w_in_t(wt, cuts, dt_index, scales, bn=1024):
    n, k = wt.shape
    parts, offs = [], []
    for i in range(len(cuts) - 1):
        if i == dt_index:
            continue
        nblk = (cuts[i + 1] - cuts[i]) // bn
        assert nblk * bn == cuts[i + 1] - cuts[i]
        parts.append((nblk, scales[i]))
        offs += [cuts[i] + b * bn for b in range(nblk)]
    offs = np.asarray(offs, np.int32)
    firsts = np.cumsum([0] + [p[0] for p in parts])

    def row_off(j):
        off = jnp.int32(int(offs[0]))
        for idx in range(1, len(offs)):
            off = jnp.where(j >= idx, jnp.int32(int(offs[idx])), off)
        return off

    def out_map(p):
        return lambda j: (0, jnp.clip(j - int(firsts[p]), 0, parts[p][0] - 1))

    dt_rows = cuts[dt_index + 1] - cuts[dt_index]
    outs = pl.pallas_call(
        functools.partial(_pack_t_kernel, parts=tuple(parts), bn=bn, dt_rows=dt_rows),
        grid=(len(offs),),
        in_specs=[pl.BlockSpec((pl.Element(bn), pl.Element(k)), lambda j: (pl.multiple_of(row_off(j), SUBLANES), 0)),
                  pl.BlockSpec((pl.Element(LANES), pl.Element(k)), lambda j: (cuts[dt_index], 0))],
        out_specs=tuple(pl.BlockSpec((k // 2, bn), out_map(p)) for p in range(len(parts)))
        + (pl.BlockSpec((k // 2, LANES), lambda j: (0, 0)),),
        out_shape=tuple(jax.ShapeDtypeStruct((k // 2, nb_ * bn), jnp.uint32) for nb_, _ in parts)
        + (jax.ShapeDtypeStruct((k // 2, LANES), jnp.uint32),),
        compiler_params=pltpu.CompilerParams(dimension_semantics=("arbitrary",),
                                             vmem_limit_bytes=VMEM_LIMIT_BYTES),
        name="pack_w_in",
    )(wt, wt)
    outs = list(outs)
    dt_part = outs.pop()
    outs.insert(dt_index, dt_part)
    return outs


def _pack_rows_01(m):
    bits = np.ascontiguousarray(m, np.float32).view(np.uint32) >> 16
    return jnp.asarray(bits[0::2] | (bits[1::2] << 16), jnp.uint32)


def _dot_nt(a, b):
    return lax.dot_general(a, b, (((1,), (1,)), ((), ())), preferred_element_type=F32)


def _dot_tn(a, b):
    return lax.dot_general(a, b, (((0,), (0,)), ((), ())), preferred_element_type=F32)


def _split3(v):
    hi = v.astype(BF16)
    r1 = v - hi.astype(F32)
    mid = r1.astype(BF16)
    lo = (r1 - mid.astype(F32)).astype(BF16)
    return hi, mid, lo


def _dot_exact_lhs(m, v):
    hi, mid, lo = _split3(v)
    return _dot(m, hi) + _dot(m, mid) + _dot(m, lo)


def _split2(v):
    hi = v.astype(BF16)
    lo = (v - hi.astype(F32)).astype(BF16)
    return hi, lo


def _dot_2way_rhs(v, m2):
    return _dot(jnp.concatenate(_split2(v), axis=1), m2)


def _dot_2way_lhs(m2, v):
    return _dot(m2, jnp.concatenate(_split2(v), axis=0))


def _softplus(x):
    return jnp.maximum(x, 0.0) + jnp.log1p(jnp.exp(-jnp.abs(x)))


def _sigmoid(x):
    return 0.5 * jnp.tanh(0.5 * x) + 0.5


def _silu_of_half(h):
    return h + h * jnp.tanh(h)


def _gelu(x):
    c = np.sqrt(2.0 / np.pi).astype(np.float32)
    return 0.5 * x * (1.0 + jnp.tanh(c * (x + 0.044715 * (x * x * x))))


def _layer_norm(v, g, b):
    mu = jnp.mean(v, axis=-1, keepdims=True)
    d = v - mu
    var = jnp.mean(d * d, axis=-1, keepdims=True)
    return d * lax.rsqrt(var + LN_EPS) * g + b


def _rows(i, n):
    if isinstance(i, int):
        return slice(i * n, (i + 1) * n)
    return pl.ds(pl.multiple_of(i * n, n), n)


def _conv_block(buf_ref, w_ref, b_ref, hist, stride, r0, rows, c0, cw):
    taps = w_ref.shape[0]
    acc = b_ref[:, c0:c0 + cw]
    for k in range(taps):
        off = hist + r0 - (taps - 1 - k) * stride
        if not isinstance(off, int):
            off = pl.multiple_of(off, SUBLANES)
        acc = acc + w_ref[k:k + 1, c0:c0 + cw] * buf_ref[pl.ds(off, rows), c0:c0 + cw]
    return acc


LOG2_SUBLANES = 3
NPOS = SSD_CHUNK // SUBLANES
LOG2_NPOS = 4
assert 1 << LOG2_SUBLANES == SUBLANES and 1 << LOG2_NPOS == NPOS


def _tok_of_row(r):
    return (r & (SUBLANES - 1)) * NPOS + lax.shift_right_logical(r, LOG2_SUBLANES)


def _row_of_tok(t):
    return (t & (NPOS - 1)) * SUBLANES + lax.shift_right_logical(t, LOG2_NPOS)


def _perm_matrix():
    q = SSD_CHUNK
    r = lax.broadcasted_iota(jnp.int32, (q, q), 0)
    c = lax.broadcasted_iota(jnp.int32, (q, q), 1)
    return jnp.where(c == _tok_of_row(r), 1.0, 0.0).astype(BF16)


def _unperm_matrix():
    q = SSD_CHUNK
    t = lax.broadcasted_iota(jnp.int32, (q, q), 0)
    r = lax.broadcasted_iota(jnp.int32, (q, q), 1)
    return jnp.where(r == _row_of_tok(t), 1.0, 0.0).astype(BF16)


def _fill_wrap(buf_ref, wrap_ref, c, wrap, tail_ref, cols=slice(None)):
    q = SSD_CHUNK
    for m in range(wrap // SUBLANES):
        r_cur = (c + 1) * q - wrap + m * SUBLANES
        cur = buf_ref[r_cur:r_cur + SUBLANES, cols]
        if c == 0:
            prv = tail_ref[m * SUBLANES:(m + 1) * SUBLANES, cols]
        else:
            prv = buf_ref[r_cur - q:r_cur - q + SUBLANES, cols]
        sub0 = lax.broadcasted_iota(jnp.int32, cur.shape, 0) == 0
        wrap_ref[c * wrap + m * SUBLANES:c * wrap + (m + 1) * SUBLANES, cols] = jnp.where(
            sub0, pltpu.roll(prv, 1, 0), pltpu.roll(cur, 1, 0))


def _conv_seg(buf_ref, wrap_ref, w_ref, b_ref, c, wrap, c0, cw):
    q = SSD_CHUNK
    taps = w_ref.shape[0]
    cols = slice(c0, c0 + cw)
    acc = b_ref[:, cols] + w_ref[taps - 1:taps, cols] * buf_ref[c * q:(c + 1) * q, cols]
    for k in range(taps - 1):
        back = (taps - 1 - k) * SUBLANES
        shifted = jnp.concatenate(
            [wrap_ref[(c + 1) * wrap - back:(c + 1) * wrap, cols], buf_ref[c * q:(c + 1) * q - back, cols]],
            axis=0)
        acc = acc + w_ref[k:k + 1, cols] * shifted
    return acc


def _seg_tail_rows(wrap):
    n = wrap // SUBLANES
    return [(NPOS - n + m) * SUBLANES + SUBLANES - 1 for m in range(n)]


def _prompt_mixer_kernel(
        x_ref, wz_ref, wxbc_ref, wdt_ref, wlx_ref, wly_ref, wgs_ref, wgl_ref,
        scw_ref, scb_ref, dtb_ref, aneg_ref, dexp_ref, ng_ref, wso_ref,
        lcw_ref, lcb_ref, wax_ref, ba_ref, bx_ref, lam_ref, wlo_ref,
        bg_ref, wo_ref, l1g_ref, l1b_ref, e_ref,
        x1_ref, st_ref, sconv_ref, lst_ref, lconv_ref,
        xb_s, xp_s, xbc_s, swrap_s, stail_s, xc_s, lx_s, lwrap_s, ltail_s, xr_s, ly_s, y_s, z_s, ysb_s, ylb_s,
        ht_s, hl_s, gs_s, gl_s, ys_s, yl_s, o_s, mb_s,
        *, tl, alpha, n_heads, headdim, d_state):
    t = pl.program_id(1)
    nt = pl.num_programs(1)
    d_ssd = n_heads * headdim
    gn = SSD_GROUPS * d_state
    hpg = n_heads // SSD_GROUPS
    gw = hpg * headdim
    d_lru = lx_s.shape[1]
    d_model = x_ref.shape[2]
    q = SSD_CHUNK

    nch = tl // q
    wrap_s = (scw_ref.shape[0] - 1) * SUBLANES
    wrap_l = (lcw_ref.shape[0] - 1) * SUBLANES

    @pl.when(t == 0)
    def _():
        stail_s[...] = jnp.zeros(stail_s.shape, F32)
        ltail_s[...] = jnp.zeros(ltail_s.shape, F32)
        ht_s[...] = jnp.zeros(ht_s.shape, F32)
        hl_s[...] = jnp.zeros(hl_s.shape, F32)

    perm = _perm_matrix()
    perm2 = jnp.concatenate([perm, perm], axis=1)
    for c in range(nch):
        rows = _rows(c, q)
        xp = _dot_2way_lhs(perm2, x_ref[0, rows, :])
        xp_s[rows, :] = xp
        xb_s[rows, :] = xp.astype(BF16)

    cwid = 512

    def proj(dst_ref, w_ref, c0, cw):
        dst_ref[:, c0:c0 + cw] = _dot(xb_s[...], w_ref[:, c0:c0 + cw])

    def lru_conv(c):
        _fill_wrap(lx_s, lwrap_s, c, wrap_l, ltail_s)
        for c0 in range(0, d_lru, cwid):
            xr_s[c * q:(c + 1) * q, c0:c0 + cwid] = _conv_seg(lx_s, lwrap_s, lcw_ref, lcb_ref, c, wrap_l, c0, cwid)
        if c == nch - 1:
            ltail_s[...] = lx_s[tl - wrap_l:tl, :]

    bw = d_lru // LRU_BLOCKS
    sub = lax.broadcasted_iota(jnp.int32, (SUBLANES, bw), 0)
    crow = lax.broadcasted_iota(jnp.int32, (q, bw), 0)

    def lru_block(n):
        cols = slice(n * bw, (n + 1) * bw)
        xr = xr_s[:, cols]
        xrb = xr.astype(BF16)
        rg = _dot(xrb, wax_ref[n])
        r = _sigmoid(rg[:, 0:bw] + ba_ref[:, cols])
        gi = _sigmoid(rg[:, bw:2 * bw] + bx_ref[:, cols])
        log_a = (-LRU_C) * r * _softplus(-lam_ref[:, cols])
        a_all = jnp.exp(log_a)
        mult_all = jnp.sqrt(1.0 - a_all * a_all)
        for c in range(nch):
            a = a_all[c * q:(c + 1) * q, :]
            mult = mult_all[c * q:(c + 1) * q, :]
            if c == 0:
                mult = jnp.where(jnp.logical_and(crow == 0, t == 0), 1.0, mult)
            u = mult * gi[c * q:(c + 1) * q, :] * xr[c * q:(c + 1) * q, :]
            a_p = [a[i * SUBLANES:(i + 1) * SUBLANES, :] for i in range(NPOS)]
            u_p = [u[i * SUBLANES:(i + 1) * SUBLANES, :] for i in range(NPOS)]
            h = u_p[0]
            g = a_p[0]
            for i in range(1, NPOS):
                h = a_p[i] * h + u_p[i]
                g = a_p[i] * g
            gs = jnp.where(sub == 0, 0.0, pltpu.roll(g, 1, 0))
            hs = jnp.where(sub == 0, hl_s[0:1, cols], pltpu.roll(h, 1, 0))
            d = 1
            while d < SUBLANES:
                keep = sub >= d
                hs = jnp.where(keep, gs * pltpu.roll(hs, d, 0) + hs, hs)
                gs = jnp.where(keep, gs * pltpu.roll(gs, d, 0), gs)
                d *= 2
            h = hs
            out = []
            for i in range(NPOS):
                h = a_p[i] * h + u_p[i]
                out.append(h)
            hl_s[0:1, cols] = h[SUBLANES - 1:SUBLANES, :]
            hseq = jnp.concatenate(out, axis=0)
            ylb_s[c * q:(c + 1) * q, cols] = (hseq * _gelu(ly_s[c * q:(c + 1) * q, cols])).astype(BF16)

    def ssd_conv(c0):
        cols = slice(c0, c0 + cwid)
        for c in range(nch):
            _fill_wrap(xbc_s, swrap_s, c, wrap_s, stail_s, cols)
        stail_s[:, cols] = xbc_s[tl - wrap_s:tl, cols]
        for c in range(nch):
            xc_s[c * q:(c + 1) * q, cols] = _silu_of_half(
                _conv_seg(xbc_s, swrap_s, scw_ref, scb_ref, c, wrap_s, c0, cwid))

    def merge_gate(dst_ref, c0, cw, b0):
        dst_ref[:, c0:c0 + cw] = _sigmoid(dst_ref[:, c0:c0 + cw] + bg_ref[:, b0 + c0:b0 + c0 + cw])

    tok_r = _tok_of_row(lax.broadcasted_iota(jnp.int32, (q, q), 0))
    tok_c = _tok_of_row(lax.broadcasted_iota(jnp.int32, (q, q), 1))
    causal = tok_r >= tok_c
    tri = jnp.where(causal, 1.0, 0.0).astype(BF16)
    lane_i = lax.broadcasted_iota(jnp.int32, (q, LANES), 1)
    left = lane_i < headdim

    def chunk_body(c, carry):
        rows = _rows(c, q)
        dt = _softplus(_dot(xb_s[rows, :], wdt_ref[...]) + dtb_ref[...])
        d_a = dt * aneg_ref[...]
        a_cs = _dot_exact_lhs(tri, d_a)
        a_last = a_cs[q - 1:q, :]
        wgt = dt * jnp.exp(a_last - a_cs)
        ea = jnp.exp(a_cs)
        w_exp = _dot_2way_rhs(wgt, e_ref[...])
        ea_exp = _dot_2way_rhs(ea, e_ref[...])
        a_cs_t = a_cs.T
        dt_t = dt.T
        for g in range(SSD_GROUPS):
            b_g = xc_s[rows, d_ssd + g * d_state:d_ssd + (g + 1) * d_state]
            c_g = xc_s[rows, d_ssd + gn + g * d_state:d_ssd + gn + (g + 1) * d_state]
            b_gb = b_g.astype(BF16)
            c_gb = c_g.astype(BF16)
            cb = _dot_nt(c_gb, b_gb)
            for hp in range(hpg // 2):
                c0 = g * gw + hp * 2 * headdim
                xs_pair = xc_s[rows, c0:c0 + 2 * headdim]
                lmats = []
                for j in range(2):
                    h = g * hpg + hp * 2 + j
                    seg = (jnp.broadcast_to(a_cs[:, h:h + 1], (q, q))
                           - jnp.broadcast_to(a_cs_t[h:h + 1, :], (q, q)))
                    dec = jnp.exp(jnp.where(causal, seg, -jnp.inf))
                    lmats.append((cb * dec * jnp.broadcast_to(dt_t[h:h + 1, :], (q, q))).astype(BF16))
                lpair = jnp.concatenate(lmats, axis=1)
                rhs = jnp.concatenate([jnp.where(left, xs_pair, 0.0),
                                       jnp.where(left, 0.0, xs_pair)], axis=0).astype(BF16)
                y_s[rows, c0:c0 + 2 * headdim] = _dot(lpair, rhs)
            gcols = slice(g * gw, (g + 1) * gw)
            h_g = ht_s[:, gcols]
            y_off = _dot(c_gb, h_g.astype(BF16)) * ea_exp[:, gcols]
            y_s[rows, gcols] = y_s[rows, gcols] + y_off
            xw = (xc_s[rows, gcols] * w_exp[:, gcols]).astype(BF16)
            ht_s[:, gcols] = h_g * ea_exp[q - 1:q, gcols] + _dot_tn(b_gb, xw)
            if carry is not None and g < len(carry):
                carry[g]()
        return carry

    def gate_body(i, carry):
        rows = _rows(i, ROW_BLK)
        y = y_s[rows, :] + dexp_ref[...] * xc_s[rows, 0:d_ssd]
        y = y * _silu_of_half(z_s[rows, :])
        ms = jnp.mean(y * y, axis=-1, keepdims=True)
        ysb_s[rows, :] = (y * lax.rsqrt(ms + RMS_EPS) * ng_ref[...]).astype(BF16)
        return carry

    assert nch == 2 and LRU_BLOCKS == 8 and d_lru == 2 * cwid and d_model == 2 * cwid
    d_xbc = xc_s.shape[1]
    nxb = d_xbc // cwid
    proj(lx_s, wlx_ref, 0, cwid)
    proj(lx_s, wlx_ref, cwid, cwid)
    proj(ly_s, wly_ref, 0, cwid); lru_conv(0)
    proj(ly_s, wly_ref, cwid, cwid); lru_conv(1)
    P = functools.partial
    mxu_a = ([P(proj, xbc_s, wxbc_ref, j * cwid, cwid) for j in range(nxb)]
             + [P(proj, gs_s, wgs_ref, j * cwid, cwid) for j in range(2)])
    for n in range(LRU_BLOCKS):
        mxu_a[n]()
        lru_block(n)
        if 1 <= n <= nxb:
            ssd_conv((n - 1) * cwid)
    gates_per_chunk = q // ROW_BLK
    nz = d_ssd // cwid
    chunk_body(0, [P(proj, z_s, wz_ref, j * cwid, cwid) for j in range(nz)])
    merge_gate(gs_s, 0, d_model, 0)
    for i in range(gates_per_chunk):
        gate_body(i, None)

    def wlo_piece(j):
        yl_s[:, j * cwid:(j + 1) * cwid] = _dot(ylb_s[...], wlo_ref[:, j * cwid:(j + 1) * cwid])

    chunk_body(1, [P(proj, gl_s, wgl_ref, 0, cwid), P(proj, gl_s, wgl_ref, cwid, cwid),
                   P(wlo_piece, 0), P(wlo_piece, 1)])
    merge_gate(gl_s, 0, d_model, d_model)
    for i in range(gates_per_chunk, 2 * gates_per_chunk):
        gate_body(i, None)

    ys_s[...] = _dot(ysb_s[...], wso_ref[...])
    for i in range(tl // ROW_BLK):
        rows = _rows(i, ROW_BLK)
        mb_s[rows, :] = (gs_s[rows, :] * ys_s[rows, :] + gl_s[rows, :] * yl_s[rows, :]).astype(BF16)
    o_s[...] = _dot(mb_s[...], wo_ref[...])
    for i in range(tl // ROW_BLK):
        rows = _rows(i, ROW_BLK)
        v = alpha * xp_s[rows, :] + o_s[rows, :]
        x1_ref[0, rows, :] = _layer_norm(v, l1g_ref[...], l1b_ref[...])

    @pl.when(t == nt - 1)
    def _():
        st_ref[0] = ht_s[...].T
        lst_ref[0] = hl_s[0:1, :]
        for m, r in enumerate(_seg_tail_rows(wrap_s)):
            sconv_ref[0, m:m + 1, :] = xbc_s[tl - q + r:tl - q + r + 1, :]
        for m, r in enumerate(_seg_tail_rows(wrap_l)):
            lconv_ref[0, m:m + 1, :] = lx_s[tl - q + r:tl - q + r + 1, :]


def _const_spec(shape):
    nd = len(shape)
    return pl.BlockSpec(shape, lambda *_: (0,) * nd, pipeline_mode=pl.Buffered(1))


def _prompt_mixer(x, wts, tl):
    nb, seq, d_model = x.shape
    n_heads, headdim, d_state = wts['n_heads'], wts['headdim'], wts['d_state']
    d_ssd = n_heads * headdim
    d_xbc = d_ssd + 2 * SSD_GROUPS * d_state
    d_lru = wts['wlx'].shape[1]
    names = ['wz', 'wxbc', 'wdt', 'wlx', 'wly', 'wgs', 'wgl', 'scw', 'scb', 'dtb', 'aneg', 'dexp', 'ng',
             'wso', 'lcw', 'lcb', 'wax', 'ba', 'bx', 'lam', 'wlo', 'bg', 'wo', 'l1g', 'l1b', 'ee']
    consts = [wts[k] for k in names]
    kern = functools.partial(_prompt_mixer_kernel, tl=tl, alpha=wts['alpha'], n_heads=n_heads,
                             headdim=headdim, d_state=d_state)
    out_shape = (
        jax.ShapeDtypeStruct((nb, seq, d_model), F32),
        jax.ShapeDtypeStruct((nb, d_ssd, d_state), F32),
        jax.ShapeDtypeStruct((nb, 3, d_xbc), F32),
        jax.ShapeDtypeStruct((nb, 1, d_lru), F32),
        jax.ShapeDtypeStruct((nb, 3, d_lru), F32),
    )
    out_specs = (
        pl.BlockSpec((1, tl, d_model), lambda b, t: (b, t, 0)),
        pl.BlockSpec((1, d_ssd, d_state), lambda b, t: (b, 0, 0)),
        pl.BlockSpec((1, 3, d_xbc), lambda b, t: (b, 0, 0)),
        pl.BlockSpec((1, 1, d_lru), lambda b, t: (b, 0, 0)),
        pl.BlockSpec((1, 3, d_lru), lambda b, t: (b, 0, 0)),
    )
    nch = tl // SSD_CHUNK
    wrap_s = (wts['scw'].shape[0] - 1) * SUBLANES
    wrap_l = (wts['lcw'].shape[0] - 1) * SUBLANES
    scratch = [
        pltpu.VMEM((tl, d_model), BF16),
        pltpu.VMEM((tl, d_model), F32),
        pltpu.VMEM((tl, d_xbc), F32),
        pltpu.VMEM((nch * wrap_s, d_xbc), F32),
        pltpu.VMEM((wrap_s, d_xbc), F32),
        pltpu.VMEM((tl, d_xbc), F32),
        pltpu.VMEM((tl, d_lru), F32),
        pltpu.VMEM((nch * wrap_l, d_lru), F32),
        pltpu.VMEM((wrap_l, d_lru), F32),
        pltpu.VMEM((tl, d_lru), F32),
        pltpu.VMEM((tl, d_lru), F32),
        pltpu.VMEM((tl, d_ssd), F32),
        pltpu.VMEM((tl, d_ssd), F32),
        pltpu.VMEM((tl, d_ssd), BF16),
        pltpu.VMEM((tl, d_lru), BF16),
        pltpu.VMEM((d_state, d_ssd), F32),
        pltpu.VMEM((SUBLANES, d_lru), F32),
        pltpu.VMEM((tl, d_model), F32),
        pltpu.VMEM((tl, d_model), F32),
        pltpu.VMEM((tl, d_model), F32),
        pltpu.VMEM((tl, d_model), F32),
        pltpu.VMEM((tl, d_model), F32),
        pltpu.VMEM((tl, d_model), BF16),
    ]
    return pl.pallas_call(
        kern,
        grid=(nb, seq // tl),
        in_specs=[pl.BlockSpec((1, tl, d_model), lambda b, t: (b, t, 0))]
        + [_const_spec(c.shape) for c in consts],
        out_specs=out_specs,
        out_shape=out_shape,
        scratch_shapes=scratch,
        compiler_params=pltpu.CompilerParams(
            dimension_semantics=("arbitrary", "arbitrary"),
            vmem_limit_bytes=VMEM_LIMIT_BYTES),
        name="prompt_mixer",
    )(x, *consts)


def _ffn_kernel(x_ref, h0_ref, wg_ref, wu_ref, cw_ref, cb_ref, wd_ref, g_ref, b_ref,
                y_ref, tail_ref, xb_s, gb_s, hb_s, *, tm, stride, hist, alpha, fchunk):
    t = pl.program_id(1)
    nt = pl.num_programs(1)
    taps = cw_ref.shape[0]
    nh = (taps - 1) * stride
    d_ff = gb_s.shape[1]

    @pl.when(t == 0)
    def _():
        gb_s[hist - nh:hist, :] = h0_ref[0]

    xb_s[...] = x_ref[0].astype(BF16)
    acc = None
    for c0 in range(0, d_ff, fchunk):
        cols = slice(c0, c0 + fchunk)
        gb_s[hist:hist + tm, cols] = _dot(xb_s[...], wg_ref[:, cols])
        up = _dot(xb_s[...], wu_ref[:, cols])
        gc = _conv_block(gb_s, cw_ref, cb_ref, hist, stride, 0, tm, c0, fchunk)
        hb_s[...] = (_gelu(gc) * up).astype(BF16)
        part = _dot(hb_s[...], wd_ref[c0 // 2:(c0 + fchunk) // 2, :])
        acc = part if acc is None else acc + part
    v = alpha * x_ref[0] + acc
    y_ref[0] = _layer_norm(v, g_ref[...], b_ref[...])
    gb_s[hist - nh:hist, :] = gb_s[hist + tm - nh:hist + tm, :]

    @pl.when(t == nt - 1)
    def _():
        tail_ref[0] = gb_s[hist - nh:hist, :]


def _ffn_seg_kernel(*refs, tm, alpha, fchunk, state_dims):
    steps = state_dims['steps']
    x_ref, h0_ref, st_ref = refs[0:3]
    cb_refs, xe_refs = refs[3:3 + steps], refs[3 + steps:3 + 2 * steps]
    wg_ref, wu_ref, cw_ref, cb_ref, wd_ref, g_ref, b_ref = refs[3 + 2 * steps:10 + 2 * steps]
    y_ref, tail_ref, nst_ref = refs[10 + 2 * steps:13 + 2 * steps]
    yoff_refs = refs[13 + 2 * steps:13 + 3 * steps]
    xb_s, gb_s, gwrap_s, gtail_s, hb_s, cb8_s, xe8_s, yo8_s = refs[13 + 3 * steps:]

    t = pl.program_id(1)
    pps = st_ref.shape[0]
    step_idx = pl.program_id(0) * pl.num_programs(1) + t
    row0 = lax.rem(step_idx * (2 * pps), SUBLANES)
    for p in range(pps):
        for l in range(steps):
            for e in range(2):
                src = pl.ds(row0 + 2 * p + e, 1)
                cb8_s[p, 2 * l + e:2 * l + e + 1, :] = cb_refs[l][src, :]
                xe8_s[p, 2 * l + e:2 * l + e + 1, :] = xe_refs[l][src, :]
    state_pieces = [
        functools.partial(_state_pair_group, g, st_ref.at[p], cb8_s.at[p], xe8_s.at[p],
                          nst_ref.at[p], yo8_s.at[p], **state_dims)
        for p in range(pps) for g in range(SSD_GROUPS)]

    def state_work(n):
        for _ in range(min(n, len(state_pieces))):
            state_pieces.pop(0)()
    nt = pl.num_programs(1)
    q = SSD_CHUNK
    nch = tm // q
    wrap = (cw_ref.shape[0] - 1) * SUBLANES
    d_ff = gb_s.shape[1]
    tail_rows = _seg_tail_rows(wrap)

    @pl.when(t == 0)
    def _():
        gtail_s[...] = jnp.zeros(gtail_s.shape, F32)
        for m in range(len(tail_rows)):
            r = m * SUBLANES + SUBLANES - 1
            gtail_s[r:r + 1, :] = h0_ref[0, m:m + 1, :]

    xb_s[...] = x_ref[0].astype(BF16)
    def gate_proj(c0):
        cols = slice(c0, c0 + fchunk)
        gb_s[:, cols] = _dot(xb_s[...], wg_ref[:, cols])
        for c in range(nch):
            _fill_wrap(gb_s, gwrap_s, c, wrap, gtail_s, cols)
        gtail_s[:, cols] = gb_s[tm - wrap:tm, cols]

    per_slot = -(-len(state_pieces) // (1 + 2 * (d_ff // fchunk)))
    gate_proj(0)
    state_work(per_slot)
    acc = None
    for c0 in range(0, d_ff, fchunk):
        cols = slice(c0, c0 + fchunk)
        up = _dot(xb_s[...], wu_ref[:, cols])
        if c0 + fchunk < d_ff:
            gate_proj(c0 + fchunk)
        state_work(per_slot)
        for c in range(nch):
            gc = _conv_seg(gb_s, gwrap_s, cw_ref, cb_ref, c, wrap, c0, fchunk)
            hb_s[c * q:(c + 1) * q, cols] = (_gelu(gc) * up[c * q:(c + 1) * q, :]).astype(BF16)
        state_work(per_slot)
    acc = _dot(hb_s[...], wd_ref[...])
    state_work(len(state_pieces))
    for p in range(pps):
        for l in range(steps):
            for e in range(2):
                yoff_refs[l][pl.ds(row0 + 2 * p + e, 1), :] = yo8_s[p, 2 * l + e:2 * l + e + 1, :]

    unperm = _unperm_matrix()
    unperm2 = jnp.concatenate([unperm, unperm], axis=1)
    for c in range(nch):
        rows = slice(c * q, (c + 1) * q)
        y = _layer_norm(alpha * x_ref[0, rows, :] + acc[rows, :], g_ref[...], b_ref[...])
        y_ref[0, rows, :] = _dot_2way_lhs(unperm2, y)

    @pl.when(t == nt - 1)
    def _():
        for m, r in enumerate(tail_rows):
            tail_ref[0, m:m + 1, :] = gb_s[tm - q + r:tm - q + r + 1, :]


def _ffn_seg(x, hist0, state, cb_lm, xe_lm, wts, tm, steps):
    nb, seq, d_model = x.shape
    nseq = 2 * state.shape[0]
    d_ssd = xe_lm.shape[1] // 2
    assert cb_lm.shape[0] == steps * nseq and nseq % SUBLANES == 0
    d_ff = wts['wg'].shape[1]
    taps = wts['fcw'].shape[0]
    wrap = (taps - 1) * SUBLANES
    fchunk = 1024
    nt = seq // tm
    npairs = state.shape[0]
    pps = npairs // (nb * nt)
    assert pps * nb * nt == npairs
    consts = [wts[k] for k in ['wg', 'wu', 'fcw', 'fcb', 'wd', 'l2g', 'l2b']]
    state_dims = dict(steps=steps, n_heads=wts['n_heads'], headdim=wts['headdim'], d_state=wts['d_state'])
    kern = functools.partial(_ffn_seg_kernel, tm=tm, alpha=wts['alpha'], fchunk=fchunk, state_dims=state_dims)
    pair_blk = lambda a: pl.BlockSpec((pps,) + tuple(a.shape[1:]),
                                      lambda b, t: (b * nt + t,) + (0,) * (len(a.shape) - 1))

    def rows_blk(l, width):
        return pl.BlockSpec((SUBLANES, width),
                            lambda b, t: (l * (nseq // SUBLANES) + ((b * nt + t) * 2 * pps) // SUBLANES, 0))

    def yoff_blk():
        return pl.BlockSpec((SUBLANES, d_ssd), lambda b, t: (((b * nt + t) * 2 * pps) // SUBLANES, 0))

    outs = pl.pallas_call(
        kern,
        grid=(nb, nt),
        in_specs=[pl.BlockSpec((1, tm, d_model), lambda b, t: (b, t, 0)),
                  pl.BlockSpec((1, taps - 1, d_ff), lambda b, t: (b, 0, 0)),
                  pair_blk(state)]
        + [rows_blk(l, cb_lm.shape[1]) for l in range(steps)]
        + [rows_blk(l, xe_lm.shape[1]) for l in range(steps)]
        + [_const_spec(c.shape) for c in consts],
        out_specs=(pl.BlockSpec((1, tm, d_model), lambda b, t: (b, t, 0)),
                   pl.BlockSpec((1, taps - 1, d_ff), lambda b, t: (b, 0, 0)),
                   pair_blk(state)) + tuple(yoff_blk() for _ in range(steps)),
        out_shape=(jax.ShapeDtypeStruct((nb, seq, d_model), F32),
                   jax.ShapeDtypeStruct((nb, taps - 1, d_ff), F32),
                   jax.ShapeDtypeStruct(state.shape, F32))
        + tuple(jax.ShapeDtypeStruct((nseq, d_ssd), F32) for _ in range(steps)),
        scratch_shapes=[pltpu.VMEM((tm, d_model), BF16),
                        pltpu.VMEM((tm, d_ff), F32),
                        pltpu.VMEM((tm // SSD_CHUNK * wrap, d_ff), F32),
                        pltpu.VMEM((wrap, d_ff), F32),
                        pltpu.VMEM((tm, d_ff), BF16),
                        pltpu.VMEM((pps, 2 * steps, cb_lm.shape[1]), F32),
                        pltpu.VMEM((pps, 2 * steps, xe_lm.shape[1]), F32),
                        pltpu.VMEM((pps, 2 * steps, d_ssd), F32)],
        compiler_params=pltpu.CompilerParams(
            dimension_semantics=("arbitrary", "arbitrary"),
            vmem_limit_bytes=VMEM_LIMIT_BYTES),
        name="conv_ffn_seg",
    )(x, hist0, state, *([cb_lm] * steps), *([xe_lm] * steps), *consts)
    return outs[0], outs[1], outs[2], outs[3:]


def _ffn(x, hist0, wts, tm, stride):
    nb, seq, d_model = x.shape
    d_ff = wts['wg'].shape[1]
    taps = wts['fcw'].shape[0]
    nh = (taps - 1) * stride
    hist = -(-nh // SUBLANES) * SUBLANES
    consts = [wts[k] for k in ['wg', 'wu', 'fcw', 'fcb', 'wd', 'l2g', 'l2b']]
    kern = functools.partial(_ffn_kernel, tm=tm, stride=stride, hist=hist, alpha=wts['alpha'], fchunk=1024)
    return pl.pallas_call(
        kern,
        grid=(nb, seq // tm),
        in_specs=[pl.BlockSpec((1, tm, d_model), lambda b, t: (b, t, 0)),
                  pl.BlockSpec((1, nh, d_ff), lambda b, t: (b, 0, 0))]
        + [_const_spec(c.shape) for c in consts],
        out_specs=(pl.BlockSpec((1, tm, d_model), lambda b, t: (b, t, 0)),
                   pl.BlockSpec((1, nh, d_ff), lambda b, t: (b, 0, 0))),
        out_shape=(jax.ShapeDtypeStruct((nb, seq, d_model), F32),
                   jax.ShapeDtypeStruct((nb, nh, d_ff), F32)),
        scratch_shapes=[pltpu.VMEM((tm, d_model), BF16),
                        pltpu.VMEM((hist + tm, d_ff), F32),
                        pltpu.VMEM((tm, 1024), BF16)],
        compiler_params=pltpu.CompilerParams(
            dimension_semantics=("arbitrary", "arbitrary"),
            vmem_limit_bytes=VMEM_LIMIT_BYTES),
        name="conv_ffn",
    )(x, hist0, *consts)


def _sample_ssd_kernel(
        x_ref, xall_ref, cssd_ref, wxbc_ref, wdt_ref, scw_ref, scb_ref, dtb_ref, aneg_ref, dexp_ref,
        e_ref, gsum_ref, e2_ref,
        pre_ref, cb_ref, xe_ref, ydg_ref,
        xbc_s, xs_s, bs_s, acs_s, dts_s,
        *, nseq, steps, n_heads, headdim, d_state):
    l = pl.program_id(0)
    d_ssd = n_heads * headdim
    gn = SSD_GROUPS * d_state
    hist = (scw_ref.shape[0] - 1) * nseq
    r0 = l * nseq

    def blk(i):
        return pl.ds(pl.multiple_of(i * nseq, nseq), nseq)

    def sblk(i):
        return slice(i * nseq, (i + 1) * nseq)

    @pl.when(l == 0)
    def _():
        xbc_s[0:hist, :] = cssd_ref[...]
        dts_s[...] = _softplus(_dot(xall_ref[...].astype(BF16), wdt_ref[...]) + dtb_ref[...])
        acc = jnp.zeros((nseq, LANES), F32)
        for s in range(steps):
            acc = acc + dts_s[sblk(s), :] * aneg_ref[...]
            acs_s[sblk(s), :] = acc

    xb = x_ref[...].astype(BF16)

    pre = _dot(xb, wxbc_ref[...])
    pre_ref[...] = pre
    xbc_s[pl.ds(pl.multiple_of(hist + r0, nseq), nseq), :] = pre
    cwid = 512
    for c0 in range(0, d_ssd, cwid):
        xs_s[blk(l), c0:c0 + cwid] = _silu_of_half(
            _conv_block(xbc_s, scw_ref, scb_ref, hist, nseq, r0, nseq, c0, cwid))
    b_l = _silu_of_half(_conv_block(xbc_s, scw_ref, scb_ref, hist, nseq, r0, nseq, d_ssd, gn))
    c_l = _silu_of_half(_conv_block(xbc_s, scw_ref, scb_ref, hist, nseq, r0, nseq, d_ssd + gn, gn))
    bs_s[blk(l), :] = b_l
    cb_ref[:, 0:gn] = c_l
    cb_ref[:, gn:2 * gn] = b_l

    a_cs = acs_s[blk(l), :]
    dt = dts_s[blk(l), :]
    a_end = acs_s[sblk(steps - 1), :]
    xe_ref[:, d_ssd:2 * d_ssd] = _dot_2way_rhs(jnp.exp(a_cs), e_ref[...])
    xe_ref[:, 0:d_ssd] = xs_s[blk(l), :] * _dot_2way_rhs(dt * jnp.exp(a_end - a_cs), e_ref[...])

    ydg_ref[...] = dexp_ref[...] * xs_s[blk(l), :]
    for s in range(steps):
        @pl.when(s <= l)
        def _(s=s):
            coef = _dot_2way_rhs(jnp.exp(a_cs - acs_s[sblk(s), :]) * dts_s[sblk(s), :], e_ref[...])
            cbx = _dot_2way_rhs(_dot_2way_rhs(bs_s[sblk(s), :] * c_l, gsum_ref[...]), e2_ref[...])
            ydg_ref[...] += cbx * coef * xs_s[sblk(s), :]


def _sample_lru_kernel(
        x_ref, clru_ref, slru_ref, wlx_ref, wly_ref, lcw_ref, lcb_ref, wax_ref, ba_ref, bx_ref,
        lam_ref, prelx_ref, ylru_ref, lst_ref, lx_s, hl_s, *, nseq, steps, start_pos):
    l = pl.program_id(0)
    hist = (lcw_ref.shape[0] - 1) * nseq
    d_lru = lx_s.shape[1]
    r0 = l * nseq

    @pl.when(l == 0)
    def _():
        lx_s[0:hist, :] = clru_ref[...]
        hl_s[...] = slru_ref[...]

    xb = x_ref[...].astype(BF16)
    prelx = _dot(xb, wlx_ref[...])
    prelx_ref[...] = prelx
    lx_s[pl.ds(pl.multiple_of(hist + r0, nseq), nseq), :] = prelx
    ly = _dot(xb, wly_ref[...])
    bw = d_lru // LRU_BLOCKS
    first = (l + start_pos) == 0
    for n in range(LRU_BLOCKS):
        cols = slice(n * bw, (n + 1) * bw)
        xr = _conv_block(lx_s, lcw_ref, lcb_ref, hist, nseq, r0, nseq, n * bw, bw)
        xrb = xr.astype(BF16)
        rg = _dot(xrb, wax_ref[n])
        r = _sigmoid(rg[:, 0:bw] + ba_ref[:, cols])
        gi = _sigmoid(rg[:, bw:2 * bw] + bx_ref[:, cols])
        log_a = (-LRU_C) * r * _softplus(-lam_ref[:, cols])
        a = jnp.exp(log_a)
        mult = jnp.where(first, 1.0, jnp.sqrt(1.0 - a * a))
        h = a * hl_s[:, cols] + mult * gi * xr
        hl_s[:, cols] = h
        ylru_ref[:, cols] = (h * _gelu(ly[:, cols])).astype(BF16)

    @pl.when(l == steps - 1)
    def _():
        lst_ref[...] = hl_s[...]


def _state_pair_group(g, st_ref, cb8_ref, xe8_ref, nst_ref, yoff8_ref,
                      *, steps, n_heads, headdim, d_state):
    nrow = 2 * steps
    hpg = n_heads // SSD_GROUPS
    gw = hpg * headdim
    gn = SSD_GROUPS * d_state
    d_ssd = n_heads * headdim
    assert 2 * headdim == LANES and d_state == LANES
    par = lax.broadcasted_iota(jnp.int32, (nrow, gw), 0) % 2
    low = lax.broadcasted_iota(jnp.int32, (nrow, LANES), 1) < headdim
    gcols = slice(g * gw, (g + 1) * gw)
    c8g = cb8_ref[:, g * d_state:(g + 1) * d_state].astype(BF16)
    b8g = cb8_ref[:, gn + g * d_state:gn + (g + 1) * d_state].astype(BF16)
    xw8 = xe8_ref[:, g * gw:(g + 1) * gw]
    ea8 = xe8_ref[:, d_ssd + g * gw:d_ssd + (g + 1) * gw]
    cds = []
    for hp in range(hpg // 2):
        pair = ea8[:, hp * LANES:(hp + 1) * LANES]
        swapped = pltpu.roll(pair, headdim, 1)
        cds.append(jnp.where(low, pair, swapped))
        cds.append(jnp.where(low, swapped, pair))
    yo = None
    for e in range(2):
        sg = st_ref[e, gcols, :]
        yo_e = _dot_nt(c8g, sg.astype(BF16))
        yo = yo_e if e == 0 else jnp.where(par == e, yo_e, yo)
        xw_e = jnp.where(par == e, xw8, 0.0).astype(BF16)
        upd = _dot_tn(xw_e, b8g)
        k_last = 2 * (steps - 1) + e
        for hh in range(hpg):
            cd = cds[hh][k_last:k_last + 1, :]
            hr = slice(hh * headdim, (hh + 1) * headdim)
            nst_ref[e, g * gw + hh * headdim:g * gw + (hh + 1) * headdim, :] = sg[hr, :] * cd + upd[hr, :]
    yoff8_ref[:, gcols] = yo * ea8


def _sample_post_kernel(x_ref, ydg_ref, yoff_ref, ylru_ref, wz_ref, ng_ref, wso_ref, wlo_ref,
                        wgs_ref, wgl_ref, bg_ref, wo_ref, l1g_ref, l1b_ref, x1_ref, *, alpha):
    d_model = x_ref.shape[1]
    xb = x_ref[...].astype(BF16)
    y = (ydg_ref[...] + yoff_ref[...]) * _silu_of_half(_dot(xb, wz_ref[...]))
    ms = jnp.mean(y * y, axis=-1, keepdims=True)
    ysb = (y * lax.rsqrt(ms + RMS_EPS) * ng_ref[...]).astype(BF16)
    g_ssd = _sigmoid(_dot(xb, wgs_ref[...]) + bg_ref[:, 0:d_model])
    g_lru = _sigmoid(_dot(xb, wgl_ref[...]) + bg_ref[:, d_model:2 * d_model])
    merged = g_ssd * _dot(ysb, wso_ref[...]) + g_lru * _dot(ylru_ref[...], wlo_ref[...])
    o = _dot(merged.astype(BF16), wo_ref[...])
    x1_ref[...] = _layer_norm(alpha * x_ref[...] + o, l1g_ref[...], l1b_ref[...])


def _sample_front(x_lm, cssd_lm, clru_lm, slru_lm, wts, nseq, steps, start_pos):
    n_heads, headdim, d_state = wts['n_heads'], wts['headdim'], wts['d_state']
    d_model = x_lm.shape[1]
    d_ssd = n_heads * headdim
    gn = SSD_GROUPS * d_state
    d_xbc = d_ssd + 2 * gn
    d_lru = wts['wlx'].shape[1]
    ntok = steps * nseq
    dims = dict(nseq=nseq, steps=steps, n_heads=n_heads, headdim=headdim, d_state=d_state)
    params = pltpu.CompilerParams(dimension_semantics=("arbitrary",), vmem_limit_bytes=VMEM_LIMIT_BYTES)
    step_blk = lambda w: pl.BlockSpec((nseq, w), lambda l: (l, 0))

    sds = jax.ShapeDtypeStruct
    ssd_names = ['wxbc', 'wdt', 'scw', 'scb', 'dtb', 'aneg', 'dexp', 'ee', 'gsum2', 'gexp2']
    ssd_consts = [x_lm, cssd_lm] + [wts[k] for k in ssd_names]
    pre, cb_lm, xe_lm, ydg_lm = pl.pallas_call(
        functools.partial(_sample_ssd_kernel, **dims),
        grid=(steps,),
        in_specs=[step_blk(d_model)] + [_const_spec(c.shape) for c in ssd_consts],
        out_specs=(step_blk(d_xbc), step_blk(2 * gn), step_blk(2 * d_ssd), step_blk(d_ssd)),
        out_shape=(sds((ntok, d_xbc), F32), sds((ntok, 2 * gn), F32), sds((ntok, 2 * d_ssd), F32),
                   sds((ntok, d_ssd), F32)),
        scratch_shapes=[
            pltpu.VMEM((cssd_lm.shape[0] + ntok, d_xbc), F32),
            pltpu.VMEM((ntok, d_ssd), F32),
            pltpu.VMEM((ntok, gn), F32),
            pltpu.VMEM((ntok, LANES), F32),
            pltpu.VMEM((ntok, LANES), F32),
        ],
        compiler_params=params,
        name="sample_ssd",
    )(x_lm, *ssd_consts)

    lru_names = ['wlx', 'wly', 'lcw', 'lcb', 'wax', 'ba', 'bx', 'lam']
    lru_consts = [clru_lm, slru_lm] + [wts[k] for k in lru_names]
    prelx, ylru_lm, lst = pl.pallas_call(
        functools.partial(_sample_lru_kernel, nseq=nseq, steps=steps, start_pos=start_pos),
        grid=(steps,),
        in_specs=[step_blk(d_model)] + [_const_spec(c.shape) for c in lru_consts],
        out_specs=(step_blk(d_lru), step_blk(d_lru), pl.BlockSpec((nseq, d_lru), lambda l: (0, 0))),
        out_shape=(sds((ntok, d_lru), F32), sds((ntok, d_lru), BF16), sds((nseq, d_lru), F32)),
        scratch_shapes=[
            pltpu.VMEM((clru_lm.shape[0] + ntok, d_lru), F32),
            pltpu.VMEM((nseq, d_lru), F32),
        ],
        compiler_params=params,
        name="sample_lru",
    )(x_lm, *lru_consts)

    return dict(pre=pre, cb=cb_lm, xe=xe_lm, ydg=ydg_lm, prelx=prelx, ylru=ylru_lm, lst=lst)


def _sample_back(x_lm, ydg_lm, yoff_lm, ylru_lm, wts, nseq, steps):
    d_model = x_lm.shape[1]
    d_ssd = wts['n_heads'] * wts['headdim']
    d_lru = wts['wlx'].shape[1]
    ntok = steps * nseq
    params = pltpu.CompilerParams(dimension_semantics=("arbitrary",), vmem_limit_bytes=VMEM_LIMIT_BYTES)
    step_blk = lambda w: pl.BlockSpec((nseq, w), lambda l: (l, 0))
    sds = jax.ShapeDtypeStruct
    post_names = ['wz', 'ng', 'wso', 'wlo', 'wgs', 'wgl', 'bg', 'wo', 'l1g', 'l1b']
    post_consts = [wts[k] for k in post_names]
    x1_lm = pl.pallas_call(
        functools.partial(_sample_post_kernel, alpha=wts['alpha']),
        grid=(steps,),
        in_specs=[step_blk(d_model), step_blk(d_ssd), step_blk(d_ssd), step_blk(d_lru)]
        + [_const_spec(c.shape) for c in post_consts],
        out_specs=step_blk(d_model),
        out_shape=sds((ntok, d_model), F32),
        compiler_params=params,
        name="sample_post",
    )(x_lm, ydg_lm, yoff_lm, ylru_lm, *post_consts)
    return x1_lm


def _prep_weights(w_in, b_gate, ssd_conv_w, ssd_conv_b, ssd_dt_bias, ssd_a_log, ssd_d, ssd_norm_g,
                  w_ssd_out, lru_conv_w, lru_conv_b, lru_wa, lru_ba, lru_wx, lru_bx, lru_lambda,
                  w_lru_out, w_o, ln1_g, ln1_b, ffn_w_gate, ffn_w_up, ffn_conv_w, ffn_conv_b,
                  ffn_w_down, ln2_g, ln2_b, n_heads, headdim, d_state):
    depth = w_in.shape[0]
    d_model = w_in.shape[1]
    d_ssd = n_heads * headdim
    d_xbc = d_ssd + 2 * SSD_GROUPS * d_state
    d_lru = lru_lambda.shape[1]
    sizes = (d_ssd, d_xbc, n_heads, d_lru, d_lru, d_model, d_model)
    cuts = np.cumsum((0,) + sizes)
    scales = [0.5 if i == 0 else 1.0 for i in range(len(sizes))]
    parts = _pack_w_in_t(jnp.swapaxes(w_in, 1, 2)[0].astype(F32), [int(c) for c in cuts], 2, scales)
    row = lambda v: v.reshape(1, -1).astype(F32)
    wax = jnp.concatenate([lru_wa[0], lru_wx[0]], axis=-1)
    wg, wu, wlo, wo, wax_p = _pack_weights([ffn_w_gate[0], ffn_w_up[0], w_lru_out[0], w_o[0],
                                            wax.reshape(-1, wax.shape[-1])])
    wax = wax_p.reshape(wax.shape[0], wax.shape[1] // 2, wax.shape[2])
    wso, = _pack_weights([w_ssd_out[0]])
    wd, = _pack_weights([ffn_w_down[0]])
    pad_heads = lambda v: jnp.pad(v.reshape(1, -1).astype(F32), ((0, 0), (0, LANES - n_heads)))
    head_of_col = np.arange(d_ssd) // headdim
    expand = (np.arange(LANES)[:, None] == head_of_col[None, :]).astype(np.float32)
    group_of_col = head_of_col // (n_heads // SSD_GROUPS)
    group_sum = (np.arange(SSD_GROUPS * d_state)[:, None] // d_state
                 == np.arange(LANES)[None, :])
    group_exp = np.arange(LANES)[:, None] == group_of_col[None, :]
    return dict(
        n_heads=n_heads, headdim=headdim, d_state=d_state,
        alpha=float((2.0 * depth) ** 0.25),
        wz=parts[0], wxbc=parts[1], wdt=parts[2], wlx=parts[3], wly=parts[4], wgs=parts[5], wgl=parts[6],
        scw=0.5 * ssd_conv_w[0].astype(F32), scb=0.5 * row(ssd_conv_b[0]),
        dtb=pad_heads(ssd_dt_bias[0]), aneg=pad_heads(-jnp.exp(ssd_a_log[0].astype(F32))),
        dexp=row(jnp.repeat(ssd_d[0], headdim)), ng=row(ssd_norm_g[0]),
        wso=wso,
        lcw=lru_conv_w[0].astype(F32), lcb=row(lru_conv_b[0]),
        ba=row(lru_ba[0]), bx=row(lru_bx[0]), wax=wax,
        lam=row(lru_lambda[0]), wlo=wlo,
        bg=row(b_gate[0]), wo=wo, l1g=row(ln1_g[0]), l1b=row(ln1_b[0]),
        ee=_pack_rows_01(np.concatenate([expand, expand], axis=0)),
        gsum2=_pack_rows_01(np.concatenate([group_sum, group_sum], axis=0)),
        gexp2=_pack_rows_01(np.concatenate([group_exp, group_exp], axis=0)),
        wg=wg, wu=wu,
        fcw=ffn_conv_w[0].astype(F32), fcb=row(ffn_conv_b[0]), wd=wd,
        l2g=row(ln2_g[0]), l2b=row(ln2_b[0]),
    )


def kernel(x_prompt, x_sample, state_ssd, cache_ssd_conv, state_lru, cache_lru_conv, cache_ffn_conv, w_in, b_gate, ssd_conv_w, ssd_conv_b, ssd_dt_bias, ssd_a_log, ssd_d, ssd_norm_g, w_ssd_out, lru_conv_w, lru_conv_b, lru_wa, lru_ba, lru_wx, lru_bx, lru_lambda, w_lru_out, w_o, ln1_g, ln1_b, ffn_w_gate, ffn_w_up, ffn_conv_w, ffn_conv_b, ffn_w_down, ln2_g, ln2_b):
    assert w_in.shape[0] == 1, "single-layer trunk"
    _, _, n_heads, headdim, d_state = state_ssd.shape
    wts = _prep_weights(w_in, b_gate, ssd_conv_w, ssd_conv_b, ssd_dt_bias, ssd_a_log, ssd_d, ssd_norm_g,
                        w_ssd_out, lru_conv_w, lru_conv_b, lru_wa, lru_ba, lru_wx, lru_bx, lru_lambda,
                        w_lru_out, w_o, ln1_g, ln1_b, ffn_w_gate, ffn_w_up, ffn_conv_w, ffn_conv_b,
                        ffn_w_down, ln2_g, ln2_b, n_heads, headdim, d_state)
    bp = x_prompt.shape[0]
    d_ff = ffn_w_gate.shape[2]

    nb_s, steps, _ = x_sample.shape
    half = nb_s // 2

    def to_lm(a):
        return jnp.swapaxes(a, 0, 1).reshape(a.shape[1] * nb_s, a.shape[2])

    def from_lm(a, k):
        return jnp.swapaxes(a.reshape(k, nb_s, a.shape[1]), 0, 1)

    d_ssd = n_heads * headdim
    x_lm = to_lm(x_sample)
    sf = _sample_front(x_lm, to_lm(cache_ssd_conv[0]), to_lm(cache_lru_conv[0]),
                       to_lm(state_lru[0][:, None, :]), wts, nb_s, steps, PAST_LEN)
    pre, lst, prelx = sf['pre'], sf['lst'], sf['prelx']

    x1_p, p_ssd, p_ssd_buf, p_lru, p_lru_buf = _prompt_mixer(x_prompt, wts, tl=MIXER_TILE)
    y_prompt, p_ffn_buf, new_state, yoff_steps = _ffn_seg(
        x1_p, jnp.zeros((bp, ffn_conv_w.shape[1] - 1, d_ff), F32),
        state_ssd[0].reshape(half, 2, d_ssd, d_state), sf['cb'], sf['xe'], wts, tm=FFN_TILE, steps=steps)
    p_ssd = p_ssd.reshape(1, bp, n_heads, headdim, d_state)

    x1_lm = _sample_back(x_lm, sf['ydg'], jnp.concatenate(yoff_steps, axis=0), sf['ylru'], wts, nb_s, steps)
    y_lm, tail = _ffn(x1_lm[None], to_lm(cache_ffn_conv[0])[None], wts, tm=steps * nb_s, stride=nb_s)
    k_ssd = ssd_conv_w.shape[1] - 1
    k_lru = lru_conv_w.shape[1] - 1
    k_ffn = ffn_conv_w.shape[1] - 1
    assert steps >= max(k_ssd, k_lru, k_ffn)
    return (y_prompt, from_lm(y_lm[0], steps), p_ssd, p_ssd_buf[None], p_lru.reshape(1, bp, -1), p_lru_buf[None],
            p_ffn_buf[None],
            new_state.reshape(1, nb_s, n_heads, headdim, d_state),
            from_lm(pre[(steps - k_ssd) * nb_s:], k_ssd)[None],
            from_lm(lst, 1).reshape(1, nb_s, -1),
            from_lm(prelx[(steps - k_lru) * nb_s:], k_lru)[None],
            from_lm(tail[0], k_ffn)[None])
```
